```python
import jax
import jax.numpy as jnp
from jax import lax
import numpy as np

D_MODEL = 1024
BATCH = 16
SEQ = 4096
DEPTH = 1
DEC_BATCH = 16
DEC_SEQ = 64
PAST_LEN = 4096

CHUNK = 64
N_META = 16
MLA_HEADS = 8
MLA_V = D_MODEL // (2 * MLA_HEADS)
MLA_NOPE = MLA_V
MLA_ROPE = MLA_NOPE // 2
MLA_QK = MLA_NOPE + MLA_ROPE
MLA_Q_LORA = D_MODEL // 4
MLA_KV_LORA = D_MODEL // 8
MLA_SCALE = MLA_QK ** -0.5
ROPE_BASE = 10000.0
Q_BLOCK = 128
M_HEADS = 4
M_DV = D_MODEL // (2 * M_HEADS)
M_DK = M_DV // 2
PROJ_NAMES = ('cq', 'ckv', 'kr', 'mq', 'mk', 'mv', 'mi', 'mf', 'mo')
PROJ_WIDTHS = (MLA_Q_LORA, MLA_KV_LORA, MLA_ROPE, M_HEADS * M_DK, M_HEADS * M_DK, M_HEADS * M_DV, M_HEADS, M_HEADS, M_HEADS * M_DV)
PROJ_DIM = MLA_Q_LORA + MLA_KV_LORA + MLA_ROPE + 2 * M_HEADS * M_DK + 2 * M_HEADS * M_DV + 2 * M_HEADS
MIX_DIM = MLA_HEADS * MLA_V + M_HEADS * M_DV
N_GROUPS = 4
E_PER_GROUP = 8
N_EXPERTS = N_GROUPS * E_PER_GROUP
TOP_K = 2
EXPERT_FF = D_MODEL // 2
MOE_BLOCK = 128
EPS = 1e-6

kernel_name = 'hymba_mla_mlstm_hmoe_stream_step'


def rms_norm(x, g):
    xf = x.astype(jnp.float32)
    y = xf * lax.rsqrt(jnp.mean(xf * xf, axis=-1, keepdims=True) + EPS)
    return (y * g.astype(jnp.float32)).astype(x.dtype)


def rope(x, pos):
    half = x.shape[-1] // 2
    inv = ROPE_BASE ** (-jnp.arange(half, dtype=jnp.float32) / half)
    ang = pos.astype(jnp.float32)[:, None] * inv[None, :]
    ang = ang.reshape((1, ang.shape[0]) + (1,) * (x.ndim - 3) + (half,))
    cos, sin = jnp.cos(ang), jnp.sin(ang)
    x1 = x[..., :half].astype(jnp.float32)
    x2 = x[..., half:].astype(jnp.float32)
    return jnp.concatenate([x1 * cos - x2 * sin, x1 * sin + x2 * cos], axis=-1).astype(x.dtype)


def project(x, lp):
    z = jnp.einsum('bld,dp->blp', rms_norm(x, lp['g_attn']), lp['w_in'])
    out, off = {}, 0
    for name, w in zip(PROJ_NAMES, PROJ_WIDTHS):
        out[name] = z[..., off:off + w]
        off += w
    return out


def latent_rows(pr, pos, lp):
    return rms_norm(pr['ckv'], lp['g_ckv']), rope(pr['kr'], pos)


def mla_queries(cq, pos, lp):
    q = jnp.einsum('blr,rhe->blhe', rms_norm(cq, lp['g_cq']), lp['w_uq'])
    q = jnp.concatenate([q[..., :MLA_NOPE], rope(q[..., MLA_NOPE:], pos)], axis=-1)
    return rms_norm(q, lp['g_q'])


def mla_keys_values(ckv, kr, lp):
    kv = jnp.einsum('blc,che->blhe', ckv, lp['w_ukv'])
    kr_h = jnp.broadcast_to(kr[:, :, None, :], kv.shape[:3] + (MLA_ROPE,))
    k = rms_norm(jnp.concatenate([kv[..., :MLA_NOPE], kr_h], axis=-1), lp['g_k'])
    return k, kv[..., MLA_NOPE:]


def attend_block(q, k, v, q_cid, k_cid):
    s = jnp.einsum('bqhe,bkhe->bhqk', q, k).astype(jnp.float32) * MLA_SCALE
    visible = k_cid[None, :] <= q_cid[:, None]
    s = jnp.where(visible[None, None], s, -jnp.inf)
    p = jax.nn.softmax(s, axis=-1).astype(v.dtype)
    return jnp.einsum('bhqk,bkhe->bqhe', p, v)


def attend(q, k, v, q_cid, k_cid):
    b, lq = q.shape[:2]
    if lq <= Q_BLOCK:
        return attend_block(q, k, v, q_cid, k_cid)
    nb = lq // Q_BLOCK
    qb = q.reshape((b, nb, Q_BLOCK) + q.shape[2:]).swapaxes(0, 1)
    cb = q_cid.reshape(nb, Q_BLOCK)
    out = lax.map(lambda a: attend_block(a[0], k, v, a[1], k_cid), (qb, cb))
    return out.swapaxes(0, 1).reshape((b, lq) + out.shape[3:])


def mlstm_zero_state(b, dtype):
    return (jnp.zeros((b, M_HEADS, M_DV, M_DK), dtype), jnp.zeros((b, M_HEADS, M_DK), dtype), jnp.zeros((b, M_HEADS), dtype))


def mlstm_gates(pr, lp):
    b, l = pr['mk'].shape[:2]
    k = pr['mk'].reshape(b, l, M_HEADS, M_DK) * (M_DK ** -0.5)
    v = pr['mv'].reshape(b, l, M_HEADS, M_DV)
    ig = pr['mi'].astype(jnp.float32) + lp['b_igate'].astype(jnp.float32)
    logf = jax.nn.log_sigmoid(pr['mf'].astype(jnp.float32) + lp['b_fgate'].astype(jnp.float32))
    return k, v, ig, logf


def mlstm_state_update(state, k, v, ig, logf):
    c, n, m = state
    k, v = k.astype(jnp.float32), v.astype(jnp.float32)
    mf = m.astype(jnp.float32)
    b = jnp.cumsum(logf, axis=1)
    b_last = b[:, -1]
    log_w = b_last[:, None] - b + ig
    m_new = jnp.maximum(b_last + mf, jnp.max(log_w, axis=1))
    w = jnp.exp(log_w - m_new[:, None])
    decay = jnp.exp(b_last + mf - m_new)
    c_new = decay[..., None, None] * c.astype(jnp.float32) + jnp.einsum('bsh,bshv,bshd->bhvd', w, v, k)
    n_new = decay[..., None] * n.astype(jnp.float32) + jnp.einsum('bsh,bshd->bhd', w, k)
    return (c_new.astype(c.dtype), n_new.astype(n.dtype), m_new.astype(m.dtype))


def mlstm_block(state, q, k, v, ig, logf):
    c, n, m = state
    qf, kf, vf = q.astype(jnp.float32), k.astype(jnp.float32), v.astype(jnp.float32)
    l = q.shape[1]
    b = jnp.cumsum(logf, axis=1)
    causal = jnp.tril(jnp.ones((l, l), dtype=bool))
    log_d = b[:, :, None, :] - b[:, None, :, :] + ig[:, None, :, :]
    log_d = jnp.where(causal[None, :, :, None], log_d, -jnp.inf)
    log_inter = b + m.astype(jnp.float32)[:, None, :]
    m_t = jnp.maximum(log_inter, jnp.max(log_d, axis=2))
    w = jnp.exp(log_d - m_t[:, :, None, :]) * jnp.einsum('bthd,bshd->btsh', qf, kf)
    inter = jnp.exp(log_inter - m_t)
    num = jnp.einsum('btsh,bshv->bthv', w, vf) + inter[..., None] * jnp.einsum('bhvd,bthd->bthv', c.astype(jnp.float32), qf)
    den = jnp.sum(w, axis=2) + inter * jnp.einsum('bhd,bthd->bth', n.astype(jnp.float32), qf)
    h = num / jnp.maximum(jnp.abs(den), jnp.exp(-m_t))[..., None]
    return h, mlstm_state_update(state, k, v, ig, logf)


def mlstm_run(state, q, k, v, ig, logf):
    b, l = q.shape[:2]
    if l <= CHUNK:
        h, state = mlstm_block(state, q, k, v, ig, logf)
        return h.astype(q.dtype), state
    nc = l // CHUNK

    def split(a):
        return a.reshape((b, nc, CHUNK) + a.shape[2:]).swapaxes(0, 1)

    def step(st, xs):
        h, st = mlstm_block(st, *xs)
        return st, h.astype(q.dtype)

    state, hs = lax.scan(step, state, (split(q), split(k), split(v), split(ig), split(logf)))
    return hs.swapaxes(0, 1).reshape((b, l) + hs.shape[3:]), state


def hier_route(xf, lp):
    t = xf.shape[0]
    rows = jnp.arange(t)
    g_logits = (xf @ lp['w_group']).astype(jnp.float32) + lp['b_group'].astype(jnp.float32)
    g_prob = jax.nn.softmax(g_logits, axis=-1)
    g_idx = lax.top_k(g_logits, 1)[1][:, 0]
    e_logits = ((xf @ lp['w_erouter']).astype(jnp.float32) + lp['b_erouter'].astype(jnp.float32)).reshape(t, N_GROUPS, E_PER_GROUP)
    e_sel = e_logits[rows, g_idx]
    e_val, e_idx = lax.top_k(e_sel, TOP_K)
    gate = g_prob[rows, g_idx][:, None] * jax.nn.softmax(e_val, axis=-1)
    return g_idx[:, None] * E_PER_GROUP + e_idx, gate


def moe(h, lp):
    b, l, d = h.shape
    t = b * l
    a = t * TOP_K
    xf = h.reshape(t, d)
    expert, gate = hier_route(xf, lp)
    e_flat = expert.reshape(a)
    order = jnp.argsort(e_flat)
    e_sorted = e_flat[order]
    tok_sorted = (jnp.arange(a) // TOP_K)[order]
    g_sorted = gate.reshape(a)[order].astype(h.dtype)
    counts = jnp.bincount(e_flat, length=N_EXPERTS)
    start = jnp.cumsum(counts) - counts
    padded = (counts + MOE_BLOCK - 1) // MOE_BLOCK * MOE_BLOCK
    pend = jnp.cumsum(padded)
    pstart = pend - padded
    dest = pstart[e_sorted] + jnp.arange(a) - start[e_sorted]
    nb = (a + N_EXPERTS * (MOE_BLOCK - 1)) // MOE_BLOCK + 1
    xs = jnp.zeros((nb * MOE_BLOCK, d), h.dtype).at[dest].set(xf[tok_sorted])
    blk_e = jnp.minimum(jnp.searchsorted(pend, jnp.arange(nb) * MOE_BLOCK, side='right'), N_EXPERTS - 1)

    def expert_block(args):
        xb, e = args
        return (jax.nn.silu(xb @ lp['w1'][e]) * (xb @ lp['w3'][e])) @ lp['w2'][e]

    ys = lax.map(expert_block, (xs.reshape(nb, MOE_BLOCK, d), blk_e)).reshape(nb * MOE_BLOCK, d)
    contrib = ys[dest] * g_sorted[:, None]
    return jnp.zeros((t, d), h.dtype).at[tok_sorted].add(contrib).reshape(b, l, d)


def frame_layer(x, pos, cid, ctx_ckv, ctx_kr, ctx_cid, mstate, lp):
    b, l = x.shape[:2]
    pr = project(x, lp)
    ckv, kr = latent_rows(pr, pos, lp)
    q = mla_queries(pr['cq'], pos, lp)
    k, v = mla_keys_values(jnp.concatenate([ctx_ckv, ckv], axis=1), jnp.concatenate([ctx_kr, kr], axis=1), lp)
    att = attend(q, k, v, cid, jnp.concatenate([ctx_cid, cid]))
    mk, mv, ig, logf = mlstm_gates(pr, lp)
    mq = pr['mq'].reshape(b, l, M_HEADS, M_DK)
    hm, mstate_new = mlstm_run(mstate, mq, mk, mv, ig, logf)
    hm = rms_norm(hm, lp['g_mh']) * jax.nn.sigmoid(pr['mo']).reshape(b, l, M_HEADS, M_DV)
    mix = jnp.concatenate([att.reshape(b, l, MLA_HEADS * MLA_V), hm.reshape(b, l, M_HEADS * M_DV)], axis=-1)
    x = x + jnp.einsum('blm,md->bld', mix, lp['w_out'])
    x = x + moe(rms_norm(x, lp['g_ffn']), lp)
    return x, ckv, kr, mstate_new


def meta_rows_state(meta_h, meta_pos, lp):
    pr = project(meta_h, lp)
    ckv, kr = latent_rows(pr, meta_pos, lp)
    mk, mv, ig, logf = mlstm_gates(pr, lp)
    return ckv, kr, mlstm_state_update(mlstm_zero_state(1, meta_h.dtype), mk, mv, ig, logf)


def setup_inputs(seed: int = 0) -> dict:
    key = jax.random.key(seed)
    ks = jax.random.split(key, 32)

    def nrm(k, shape, scale):
        return jax.random.normal(k, shape, jnp.float32) * scale

    def gain(k, shape):
        return 1.0 + 0.05 * jax.random.normal(k, shape, jnp.float32)

    f_bias = jnp.linspace(3.0, 6.0, M_HEADS, dtype=jnp.float32)[None, :] + nrm(ks[14], (DEPTH, M_HEADS), 0.1)
    return {
        'x_prompt': nrm(ks[0], (BATCH, SEQ, D_MODEL), 1.0),
        'x_sample': nrm(ks[1], (DEC_BATCH, DEC_SEQ, D_MODEL), 1.0),
        'cache_ckv': nrm(ks[2], (DEPTH, DEC_BATCH, PAST_LEN, MLA_KV_LORA), 1.0),
        'cache_krope': nrm(ks[3], (DEPTH, DEC_BATCH, PAST_LEN, MLA_ROPE), 1.0),
        'state_mlstm_c': nrm(ks[4], (DEPTH, DEC_BATCH, M_HEADS, M_DV, M_DK), 0.1),
        'state_mlstm_n': nrm(ks[5], (DEPTH, DEC_BATCH, M_HEADS, M_DK), 0.1),
        'state_mlstm_m': nrm(ks[6], (DEPTH, DEC_BATCH, M_HEADS), 1.0),
        'meta_tokens': nrm(ks[7], (N_META, D_MODEL), 1.0),
        'g_attn': gain(ks[8], (DEPTH, D_MODEL)),
        'w_in': nrm(ks[9], (DEPTH, D_MODEL, PROJ_DIM), D_MODEL ** -0.5),
        'g_cq': gain(ks[10], (DEPTH, MLA_Q_LORA)),
        'w_uq': nrm(ks[11], (DEPTH, MLA_Q_LORA, MLA_HEADS, MLA_QK), MLA_Q_LORA ** -0.5),
        'g_ckv': gain(ks[12], (DEPTH, MLA_KV_LORA)),
        'w_ukv': nrm(ks[13], (DEPTH, MLA_KV_LORA, MLA_HEADS, MLA_NOPE + MLA_V), MLA_KV_LORA ** -0.5),
        'g_q': gain(ks[15], (DEPTH, MLA_QK)),
        'g_k': gain(ks[16], (DEPTH, MLA_QK)),
        'b_igate': nrm(ks[17], (DEPTH, M_HEADS), 0.1),
        'b_fgate': f_bias,
        'g_mh': gain(ks[18], (DEPTH, M_HEADS, M_DV)),
        'w_out': nrm(ks[19], (DEPTH, MIX_DIM, D_MODEL), MIX_DIM ** -0.5),
        'g_ffn': gain(ks[20], (DEPTH, D_MODEL)),
        'w_group': nrm(ks[21], (DEPTH, D_MODEL, N_GROUPS), D_MODEL ** -0.5),
        'b_group': nrm(ks[22], (DEPTH, N_GROUPS), 0.01),
        'w_erouter': nrm(ks[23], (DEPTH, D_MODEL, N_EXPERTS), D_MODEL ** -0.5),
        'b_erouter': nrm(ks[24], (DEPTH, N_EXPERTS), 0.01),
        'w1': nrm(ks[25], (DEPTH, N_EXPERTS, D_MODEL, EXPERT_FF), D_MODEL ** -0.5),
        'w3': nrm(ks[26], (DEPTH, N_EXPERTS, D_MODEL, EXPERT_FF), D_MODEL ** -0.5),
        'w2': nrm(ks[27], (DEPTH, N_EXPERTS, EXPERT_FF, D_MODEL), EXPERT_FF ** -0.5),
    }


def reference(x_prompt, x_sample, cache_ckv, cache_krope, state_mlstm_c, state_mlstm_n, state_mlstm_m,
              meta_tokens, g_attn, w_in, g_cq, w_uq, g_ckv, w_ukv, g_q, g_k, b_igate, b_fgate, g_mh,
              w_out, g_ffn, w_group, b_group, w_erouter, b_erouter, w1, w3, w2):
    bp, seq = x_prompt.shape[:2]
    bs, dec_seq = x_sample.shape[:2]
    past = cache_ckv.shape[2]
    meta_pos = jnp.arange(N_META) - N_META
    meta_cid = jnp.full((N_META,), -1, dtype=jnp.int32)
    p_pos = jnp.arange(seq)
    p_cid = (p_pos // CHUNK).astype(jnp.int32)
    s_pos = past + jnp.arange(dec_seq)
    s_cid = (s_pos // CHUNK).astype(jnp.int32)
    c_cid = (jnp.arange(past) // CHUNK).astype(jnp.int32)

    def bcast(a, b):
        return jnp.broadcast_to(a, (b,) + a.shape[1:])

    meta_h = meta_tokens[None]
    xp, xs = x_prompt, x_sample
    ckv_p, kr_p, c_p, n_p, m_p = [], [], [], [], []
    ckv_s, kr_s, c_s, n_s, m_s = [], [], [], [], []
    for layer in range(DEPTH):
        lp = {'g_attn': g_attn[layer], 'w_in': w_in[layer], 'g_cq': g_cq[layer], 'w_uq': w_uq[layer],
              'g_ckv': g_ckv[layer], 'w_ukv': w_ukv[layer], 'g_q': g_q[layer], 'g_k': g_k[layer],
              'b_igate': b_igate[layer], 'b_fgate': b_fgate[layer], 'g_mh': g_mh[layer], 'w_out': w_out[layer],
              'g_ffn': g_ffn[layer], 'w_group': w_group[layer], 'b_group': b_group[layer],
              'w_erouter': w_erouter[layer], 'b_erouter': b_erouter[layer],
              'w1': w1[layer], 'w3': w3[layer], 'w2': w2[layer]}
        if layer + 1 < DEPTH:
            dt = meta_h.dtype
            meta_h, m_ckv, m_kr, m_state = frame_layer(
                meta_h, meta_pos, meta_cid, jnp.zeros((1, 0, MLA_KV_LORA), dt), jnp.zeros((1, 0, MLA_ROPE), dt),
                jnp.zeros((0,), jnp.int32), mlstm_zero_state(1, dt), lp)
        else:
            m_ckv, m_kr, m_state = meta_rows_state(meta_h, meta_pos, lp)
        xp, pc, pk, pst = frame_layer(xp, p_pos, p_cid, bcast(m_ckv, bp), bcast(m_kr, bp), meta_cid,
                                      (bcast(m_state[0], bp), bcast(m_state[1], bp), bcast(m_state[2], bp)), lp)
        xs, sc, sk, sst = frame_layer(xs, s_pos, s_cid,
                                      jnp.concatenate([bcast(m_ckv, bs), cache_ckv[layer]], axis=1),
                                      jnp.concatenate([bcast(m_kr, bs), cache_krope[layer]], axis=1),
                                      jnp.concatenate([meta_cid, c_cid]),
                                      (state_mlstm_c[layer], state_mlstm_n[layer], state_mlstm_m[layer]), lp)
        ckv_p.append(jnp.concatenate([bcast(m_ckv, bp), pc], axis=1))
        kr_p.append(jnp.concatenate([bcast(m_kr, bp), pk], axis=1))
        c_p.append(pst[0])
        n_p.append(pst[1])
        m_p.append(pst[2])
        ckv_s.append(sc)
        kr_s.append(sk)
        c_s.append(sst[0])
        n_s.append(sst[1])
        m_s.append(sst[2])
    new_ckv_prompt = jnp.stack(ckv_p)
    new_krope_prompt = jnp.stack(kr_p)
    new_c_prompt = jnp.stack(c_p)
    new_n_prompt = jnp.stack(n_p)
    new_m_prompt = jnp.stack(m_p)
    new_ckv_sample = jnp.stack(ckv_s)
    new_krope_sample = jnp.stack(kr_s)
    new_c_sample = jnp.stack(c_s)
    new_n_sample = jnp.stack(n_s)
    new_m_sample = jnp.stack(m_s)
    return (xp, xs, new_ckv_prompt, new_krope_prompt, new_c_prompt, new_n_prompt, new_m_prompt,
            new_ckv_sample, new_krope_sample, new_c_sample, new_n_sample, new_m_sample)
```

```python
import functools

import numpy as np
import jax
import jax.numpy as jnp
from jax import lax
from jax.experimental import pallas as pl
from jax.experimental.pallas import tpu as pltpu

F32 = jnp.float32
BF16 = jnp.bfloat16
I32 = jnp.int32
U32 = jnp.uint32

D_MODEL = 1024
CHUNK = 64
N_META = 16
MLA_HEADS = 8
MLA_V = 64
MLA_NOPE = 64
MLA_ROPE = 32
MLA_QK = MLA_NOPE + MLA_ROPE
MLA_Q_LORA = 256
MLA_KV_LORA = 128
MLA_SCALE = MLA_QK ** -0.5
ROPE_BASE = 10000.0
M_HEADS = 4
M_DV = 128
M_DK = 64
N_GROUPS = 4
E_PER_GROUP = 8
N_EXPERTS = 32
EXPERT_FF = 512
EPS = 1e-6

LANES = 128
HEAD_PAD = 128
PROJ_PAD = 2048
META_PAD = 128
NEG_BIG = -1e30
VMEM_LIMIT = 56 * 1024 * 1024

_O_CQ, _O_CKV, _O_MQ, _O_MK, _O_MV, _O_MO, _O_MISC = 0, 256, 384, 640, 896, 1408, 1920
_L_IG, _L_FG = 64, 68


def _cparams(sem):
    return pltpu.CompilerParams(dimension_semantics=sem, vmem_limit_bytes=VMEM_LIMIT)


def _dot(a, b):
    return jnp.dot(a, b, preferred_element_type=F32)


def _dot_nt(a, b):
    return lax.dot_general(a, b, (((1,), (1,)), ((), ())), preferred_element_type=F32)


def _split3(x):
    x1 = x.astype(BF16)
    r1 = x - x1.astype(F32)
    x2 = r1.astype(BF16)
    x3 = (r1 - x2.astype(F32)).astype(BF16)
    return x1, x2, x3


def _proj_kernel(x_ref, tab_ref, w_ref, g_ref, gcq_ref, wq_ref, gckv_ref, gq_ref, bias_ref,
                 q_ref, ckv_ref, misc_ref, mq_ref, mk_ref, mv_ref, og_ref):
    x = x_ref[...]
    xn = x * lax.rsqrt(jnp.mean(x * x, axis=-1, keepdims=True) + EPS) * g_ref[...]
    z = _dot(xn.astype(BF16), w_ref[...])
    tab = tab_ref[...]
    lane = lax.broadcasted_iota(I32, tab.shape, 1)

    cq = z[:, _O_CQ:_O_CQ + MLA_Q_LORA]
    cqn = cq * lax.rsqrt(jnp.mean(cq * cq, axis=-1, keepdims=True) + EPS) * gcq_ref[...]
    qz = _dot(cqn.astype(BF16), wq_ref[...])
    gq = gq_ref[...]
    for h in range(MLA_HEADS):
        zh = qz[:, h * HEAD_PAD:(h + 1) * HEAD_PAD]
        y = zh * tab
        rot = y + pltpu.roll(y, LANES - MLA_ROPE, 1)
        qh = jnp.where(lane < MLA_NOPE, zh, jnp.where(lane < MLA_QK, rot, 0.0))
        ms = jnp.sum(qh * qh, axis=-1, keepdims=True) * (1.0 / MLA_QK)
        q_ref[h] = (qh * lax.rsqrt(ms + EPS) * gq).astype(BF16)

    ckv = z[:, _O_CKV:_O_CKV + MLA_KV_LORA]
    ckv_ref[...] = ckv * lax.rsqrt(jnp.mean(ckv * ckv, axis=-1, keepdims=True) + EPS) * gckv_ref[...]

    zm = z[:, _O_MISC:_O_MISC + LANES]
    y = zm * tab
    rot = y + pltpu.roll(y, LANES - MLA_ROPE, 1)
    gate = zm + bias_ref[...]
    logf = jnp.minimum(gate, 0.0) - jnp.log1p(jnp.exp(-jnp.abs(gate)))
    misc = jnp.where(lane < MLA_ROPE, rot,
                     jnp.where((lane >= _L_IG) & (lane < _L_FG), gate,
                               jnp.where((lane >= _L_FG) & (lane < _L_FG + M_HEADS), logf, 0.0)))
    misc_ref[...] = misc

    mq_ref[...] = z[:, _O_MQ:_O_MQ + M_HEADS * M_DK].astype(BF16)
    mk_ref[...] = (z[:, _O_MK:_O_MK + M_HEADS * M_DK] * (M_DK ** -0.5)).astype(BF16)
    mv_ref[...] = z[:, _O_MV:_O_MV + M_HEADS * M_DV].astype(BF16)
    og_ref[...] = jax.nn.sigmoid(z[:, _O_MO:_O_MO + M_HEADS * M_DV]).astype(BF16)


def _project(x2d, tab, pw, tm):
    t = x2d.shape[0]
    nt = t // tm
    ntab = tab.shape[0] // tm
    row = lambda i: (i, 0)
    full = lambda i: (0, 0)
    return pl.pallas_call(
        _proj_kernel,
        grid=(nt,),
        in_specs=[
            pl.BlockSpec((tm, D_MODEL), row),
            pl.BlockSpec((tm, LANES), lambda i: (i % ntab, 0)),
            pl.BlockSpec((D_MODEL, PROJ_PAD), full),
            pl.BlockSpec((1, D_MODEL), full),
            pl.BlockSpec((1, MLA_Q_LORA), full),
            pl.BlockSpec((MLA_Q_LORA, MLA_HEADS * HEAD_PAD), full),
            pl.BlockSpec((1, MLA_KV_LORA), full),
            pl.BlockSpec((1, HEAD_PAD), full),
            pl.BlockSpec((1, LANES), full),
        ],
        out_specs=[
            pl.BlockSpec((MLA_HEADS, tm, HEAD_PAD), lambda i: (0, i, 0)),
            pl.BlockSpec((tm, MLA_KV_LORA), row),
            pl.BlockSpec((tm, LANES), row),
            pl.BlockSpec((tm, M_HEADS * M_DK), row),
            pl.BlockSpec((tm, M_HEADS * M_DK), row),
            pl.BlockSpec((tm, M_HEADS * M_DV), row),
            pl.BlockSpec((tm, M_HEADS * M_DV), row),
        ],
        out_shape=[
            jax.ShapeDtypeStruct((MLA_HEADS, t, HEAD_PAD), BF16),
            jax.ShapeDtypeStruct((t, MLA_KV_LORA), F32),
            jax.ShapeDtypeStruct((t, LANES), F32),
            jax.ShapeDtypeStruct((t, M_HEADS * M_DK), BF16),
            jax.ShapeDtypeStruct((t, M_HEADS * M_DK), BF16),
            jax.ShapeDtypeStruct((t, M_HEADS * M_DV), BF16),
            jax.ShapeDtypeStruct((t, M_HEADS * M_DV), BF16),
        ],
        compiler_params=_cparams(("arbitrary",)),
        name="projection",
    )(x2d, tab, pw["w_in"], pw["g_attn"], pw["g_cq"], pw["w_q"], pw["g_ckv"], pw["g_q"], pw["gate_bias"])


def _attn_kernel(q_ref, ckv_ref, misc_ref, ckvm_ref, miscm_ref, wk_ref, wv_ref, gk_ref, o_ref,
                 k_scr, v_scr, *, lf, tq, tk, bt, causal):
    i = pl.program_id(1)
    gk = gk_ref[...]

    def build(ckv_rows, misc_rows, dst):
        n = ckv_rows.shape[0]
        lane = lax.broadcasted_iota(I32, (n, LANES), 1)
        cb = ckv_rows.astype(BF16)
        krp = jnp.where((lane >= MLA_NOPE) & (lane < MLA_QK), pltpu.roll(misc_rows, MLA_NOPE, 1), 0.0)
        for h in range(MLA_HEADS):
            kk = _dot(cb, wk_ref[h]) + krp
            ms = jnp.sum(kk * kk, axis=-1, keepdims=True) * (1.0 / MLA_QK)
            k_scr[h, pl.ds(dst, n), :] = (kk * lax.rsqrt(ms + EPS) * gk).astype(BF16)
            v_scr[h, pl.ds(dst, n), :] = _dot(cb, wv_ref[h]).astype(BF16)

    @pl.when(i == 0)
    def _():
        nfull_b = lf // bt

        def body(r, carry):
            r0 = pl.multiple_of(r * bt, bt)
            build(ckv_ref[pl.ds(r0, bt), :], misc_ref[pl.ds(r0, bt), :], r0)
            return carry

        lax.fori_loop(0, nfull_b, body, 0)
        rem = lf - nfull_b * bt
        if rem:
            build(ckv_ref[nfull_b * bt:lf, :], misc_ref[nfull_b * bt:lf, :], nfull_b * bt)
        build(ckvm_ref[...], miscm_ref[...], lf)

    meta_mask = lax.broadcasted_iota(I32, (tq, META_PAD), 1) < N_META
    if causal:
        rr = lax.broadcasted_iota(I32, (tq, tq), 0) // CHUNK
        cc = lax.broadcasted_iota(I32, (tq, tq), 1) // CHUNK
        diag_mask = cc <= rr
        nfull = i * (tq // tk)
    else:
        nfull = lf // tk

    for h in range(MLA_HEADS):
        qh = q_ref[h]

        def step(carry, krows, vrows, mask):
            m, l, acc = carry
            s = _dot_nt(qh, krows) * MLA_SCALE
            if mask is not None:
                s = jnp.where(mask, s, -jnp.inf)
            m_new = jnp.maximum(m, jnp.max(s, axis=-1, keepdims=True))
            alpha = jnp.exp(m - m_new)
            p = jnp.exp(s - m_new)
            l = alpha * l + jnp.sum(p, axis=-1, keepdims=True)
            acc = alpha * acc + _dot(p.astype(BF16), vrows)
            return m_new, l, acc

        carry = (jnp.full((tq, 1), -jnp.inf, F32), jnp.zeros((tq, 1), F32), jnp.zeros((tq, MLA_V), F32))
        carry = step(carry, k_scr[h, lf:lf + META_PAD, :], v_scr[h, lf:lf + META_PAD, :], meta_mask)

        def body(j, c):
            r0 = pl.multiple_of(j * tk, tk)
            return step(c, k_scr[h, pl.ds(r0, tk), :], v_scr[h, pl.ds(r0, tk), :], None)

        carry = lax.fori_loop(0, nfull, body, carry)
        if causal:
            r0 = pl.multiple_of(i * tq, tq)
            carry = step(carry, k_scr[h, pl.ds(r0, tq), :], v_scr[h, pl.ds(r0, tq), :], diag_mask)
        else:
            rem = lf - (lf // tk) * tk
            if rem:
                carry = step(carry, k_scr[h, lf - rem:lf, :], v_scr[h, lf - rem:lf, :], None)
        m, l, acc = carry
        o_ref[:, h * MLA_V:(h + 1) * MLA_V] = (acc / l).astype(BF16)


def _attention(q, ckv_f, misc_f, ckv_m, misc_m, pw, *, nb, lq, lf, tq, tk, causal):
    nq = lq // tq
    bt = min(512, lf)
    kern = functools.partial(_attn_kernel, lf=lf, tq=tq, tk=tk, bt=bt, causal=causal)
    full2 = lambda b, i: (0, 0)
    full3 = lambda b, i: (0, 0, 0)
    return pl.pallas_call(
        kern,
        grid=(nb, nq),
        in_specs=[
            pl.BlockSpec((MLA_HEADS, tq, HEAD_PAD), lambda b, i: (0, b * nq + i, 0)),
            pl.BlockSpec((lf, MLA_KV_LORA), lambda b, i: (b, 0)),
            pl.BlockSpec((lf, LANES), lambda b, i: (b, 0)),
            pl.BlockSpec((META_PAD, MLA_KV_LORA), full2),
            pl.BlockSpec((META_PAD, LANES), full2),
            pl.BlockSpec((MLA_HEADS, MLA_KV_LORA, HEAD_PAD), full3),
            pl.BlockSpec((MLA_HEADS, MLA_KV_LORA, MLA_V), full3),
            pl.BlockSpec((1, HEAD_PAD), full2),
        ],
        out_specs=pl.BlockSpec((tq, MLA_HEADS * MLA_V), lambda b, i: (b * nq + i, 0)),
        out_shape=jax.ShapeDtypeStruct((nb * lq, MLA_HEADS * MLA_V), BF16),
        scratch_shapes=[
            pltpu.VMEM((MLA_HEADS, lf + META_PAD, HEAD_PAD), BF16),
            pltpu.VMEM((MLA_HEADS, lf + META_PAD, MLA_V), BF16),
        ],
        compiler_params=_cparams(("arbitrary", "arbitrary")),
        name="attention",
    )(q, ckv_f, misc_f, ckv_m, misc_m, pw["w_k"], pw["w_v"], pw["g_k"])


S_W = 2 * M_DV


def _mlstm_kernel(q_ref, k_ref, v_ref, og_ref, misc_ref, s0_ref, m0_ref, gmh_ref,
                  h_ref, sout_ref, mout_ref, s_scr, m_scr, *, lc, n_valid):
    c = pl.program_id(1)
    nc = pl.num_programs(1)

    @pl.when(c == 0)
    def _():
        s_scr[...] = s0_ref[0]
        m_scr[...] = m0_ref[0]

    g = misc_ref[...]
    if n_valid is not None:
        row = lax.broadcasted_iota(I32, g.shape, 0) + c * lc
        lane = lax.broadcasted_iota(I32, g.shape, 1)
        pad_val = jnp.where((lane >= _L_IG) & (lane < _L_FG), NEG_BIG, 0.0)
        g = jnp.where(row < n_valid, g, pad_val)
    gt = g.T
    gt8 = gt[_L_IG:_L_IG + 8, :]
    rr = lax.broadcasted_iota(I32, (lc, lc), 0)
    cc = lax.broadcasted_iota(I32, (lc, lc), 1)
    causal = cc <= rr
    tri = causal.astype(BF16)
    trit = (rr <= cc).astype(BF16)
    g1, g2, g3 = _split3(g)
    bcol_all = _dot(tri, g1) + _dot(tri, g2) + _dot(tri, g3)
    t1, t2, t3 = _split3(gt8)
    brow_all = _dot(t1, trit) + _dot(t2, trit) + _dot(t3, trit)

    kt = k_ref[...].astype(F32).T
    onecol = (lax.broadcasted_iota(I32, (lc, M_DV), 1) == 0).astype(BF16)

    for h in range(M_HEADS):
        bcol = bcol_all[:, _L_FG + h:_L_FG + h + 1]
        brow = brow_all[M_HEADS + h:M_HEADS + h + 1, :]
        igrow = gt8[h:h + 1, :]
        m_prev = m_scr[h, 0:1, 0:1]
        qh = q_ref[:, h * M_DK:(h + 1) * M_DK]
        kh = k_ref[:, h * M_DK:(h + 1) * M_DK]
        vext = jnp.concatenate([v_ref[:, h * M_DV:(h + 1) * M_DV], onecol], axis=1)

        log_d = jnp.where(causal, bcol - brow + igrow, -jnp.inf)
        log_inter = bcol + m_prev
        m_t = jnp.maximum(log_inter, jnp.max(log_d, axis=-1, keepdims=True))
        w = jnp.exp(log_d - m_t) * _dot_nt(qh, kh)
        inter = jnp.exp(log_inter - m_t)
        s_h = s_scr[h]
        r = _dot(w.astype(BF16), vext) + inter * _dot(qh, s_h.astype(BF16))
        num = r[:, :M_DV]
        den = r[:, M_DV:M_DV + 1]
        hh = num / jnp.maximum(jnp.abs(den), jnp.exp(-m_t))
        hn = hh * lax.rsqrt(jnp.mean(hh * hh, axis=-1, keepdims=True) + EPS) * gmh_ref[h:h + 1, :]
        h_ref[:, h * M_DV:(h + 1) * M_DV] = (hn * og_ref[:, h * M_DV:(h + 1) * M_DV].astype(F32)).astype(BF16)

        blast = bcol[lc - 1:lc, :]
        logw = blast - brow + igrow
        m_new = jnp.maximum(blast + m_prev, jnp.max(logw, axis=-1, keepdims=True))
        wrow = jnp.exp(logw - m_new)
        decay = jnp.exp(blast + m_prev - m_new)
        ktw = (kt[h * M_DK:(h + 1) * M_DK, :] * wrow).astype(BF16)
        s_scr[h] = decay * s_h + _dot(ktw, vext)
        m_scr[h] = jnp.broadcast_to(m_new, (8, LANES))

    @pl.when(c == nc - 1)
    def _():
        sout_ref[0] = s_scr[...]
        mout_ref[0] = m_scr[...]


def _mlstm(mq, mk, mv, og, misc, s0, m0, gmh, *, nb, l, lc, n_valid=None):
    nc = l // lc
    shared = s0.shape[0] == 1
    st = (lambda b, c: (0, 0, 0, 0)) if shared else (lambda b, c: (b, 0, 0, 0))
    row = lambda b, c: (b * nc + c, 0)
    kern = functools.partial(_mlstm_kernel, lc=lc, n_valid=n_valid)
    return pl.pallas_call(
        kern,
        grid=(nb, nc),
        in_specs=[
            pl.BlockSpec((lc, M_HEADS * M_DK), row),
            pl.BlockSpec((lc, M_HEADS * M_DK), row),
            pl.BlockSpec((lc, M_HEADS * M_DV), row),
            pl.BlockSpec((lc, M_HEADS * M_DV), row),
            pl.BlockSpec((lc, LANES), row),
            pl.BlockSpec((1, M_HEADS, M_DK, S_W), st),
            pl.BlockSpec((1, M_HEADS, 8, LANES), st),
            pl.BlockSpec((M_HEADS, M_DV), lambda b, c: (0, 0)),
        ],
        out_specs=[
            pl.BlockSpec((lc, M_HEADS * M_DV), row),
            pl.BlockSpec((1, M_HEADS, M_DK, S_W), lambda b, c: (b, 0, 0, 0)),
            pl.BlockSpec((1, M_HEADS, 8, LANES), lambda b, c: (b, 0, 0, 0)),
        ],
        out_shape=[
            jax.ShapeDtypeStruct((nb * l, M_HEADS * M_DV), BF16),
            jax.ShapeDtypeStruct((nb, M_HEADS, M_DK, S_W), F32),
            jax.ShapeDtypeStruct((nb, M_HEADS, 8, LANES), F32),
        ],
        scratch_shapes=[
            pltpu.VMEM((M_HEADS, M_DK, S_W), F32),
            pltpu.VMEM((M_HEADS, 8, LANES), F32),
        ],
        compiler_params=_cparams(("arbitrary", "arbitrary")),
        name="mlstm",
    )(mq, mk, mv, og, misc, s0, m0, gmh)


R_ROWS = 40


def _route_kernel(x_ref, att_ref, hm_ref, wo_ref, g_ref, wr_ref, br_ref,
                  x1_ref, hp_ref, ri_ref, rg_ref, cnt_ref, cnt_scr):
    i = pl.program_id(0)
    tm = x_ref.shape[0]

    @pl.when(i == 0)
    def _():
        cnt_scr[...] = jnp.zeros_like(cnt_scr)

    mix = jnp.concatenate([att_ref[...], hm_ref[...]], axis=1)
    x1 = x_ref[...] + _dot(mix, wo_ref[...])
    x1_ref[...] = x1
    hn = x1 * lax.rsqrt(jnp.mean(x1 * x1, axis=-1, keepdims=True) + EPS) * g_ref[...]
    hb = hn.astype(BF16)
    half = D_MODEL // 2
    hi = lax.bitcast_convert_type(hb[:, :half].astype(F32), U32)
    lo = lax.bitcast_convert_type(hb[:, half:].astype(F32), U32)
    hp_ref[...] = (hi & jnp.uint32(0xFFFF0000)) | (lo >> 16)

    logits = _dot_nt(wr_ref[...], hb) + br_ref[:, 0:1]
    e_log = logits[0:N_EXPERTS, :]
    g_log = logits[N_EXPERTS:N_EXPERTS + N_GROUPS, :]
    gmax = jnp.max(g_log, axis=0, keepdims=True)
    gsum = jnp.sum(jnp.exp(g_log - gmax), axis=0, keepdims=True)
    gi = lax.broadcasted_iota(I32, g_log.shape, 0)
    g_idx = jnp.min(jnp.where(g_log == gmax, gi, N_GROUPS), axis=0, keepdims=True)
    e_sel = jnp.zeros((E_PER_GROUP, tm), F32)
    for gg in range(N_GROUPS):
        e_sel = jnp.where(g_idx == gg, e_log[gg * E_PER_GROUP:(gg + 1) * E_PER_GROUP, :], e_sel)
    ei = lax.broadcasted_iota(I32, e_sel.shape, 0)
    m1 = jnp.max(e_sel, axis=0, keepdims=True)
    i1 = jnp.min(jnp.where(e_sel == m1, ei, E_PER_GROUP), axis=0, keepdims=True)
    e2 = jnp.where(ei == i1, -jnp.inf, e_sel)
    m2 = jnp.max(e2, axis=0, keepdims=True)
    i2 = jnp.min(jnp.where(e2 == m2, ei, E_PER_GROUP), axis=0, keepdims=True)
    ex = jnp.exp(m2 - m1)
    gp = 1.0 / gsum
    p1 = 1.0 / (1.0 + ex)
    gate1 = gp * p1
    gate2 = gp * (ex * p1)
    id1 = g_idx * E_PER_GROUP + i1
    id2 = g_idx * E_PER_GROUP + i2

    xi = lax.broadcasted_iota(I32, (N_EXPERTS, tm), 0)
    oh1 = xi == id1
    oh2 = xi == id2
    e_cnt = (oh1 | oh2).astype(F32)
    rr = lax.broadcasted_iota(I32, (tm, tm), 0)
    cc = lax.broadcasted_iota(I32, (tm, tm), 1)
    upper = (rr < cc).astype(BF16)
    pref = _dot(e_cnt.astype(BF16), upper) + cnt_scr[:, 0:1]
    rank1 = jnp.sum(jnp.where(oh1, pref, 0.0), axis=0, keepdims=True)
    rank2 = jnp.sum(jnp.where(oh2, pref, 0.0), axis=0, keepdims=True)
    cnt_new = cnt_scr[...] + jnp.sum(e_cnt, axis=1, keepdims=True)
    cnt_scr[...] = cnt_new
    cnt_ref[...] = cnt_new.astype(I32)

    zi = jnp.zeros((1, tm), I32)
    ri_ref[...] = jnp.concatenate([id1, id2, rank1.astype(I32), rank2.astype(I32), zi, zi, zi, zi], axis=0)
    zf = jnp.zeros((1, tm), F32)
    rg_ref[...] = jnp.concatenate([gate1, gate2, zf, zf, zf, zf, zf, zf], axis=0)


def _route(x2d, att, hm, pw, tm):
    t = x2d.shape[0]
    row = lambda i: (i, 0)
    col = lambda i: (0, i)
    full = lambda i: (0, 0)
    return pl.pallas_call(
        _route_kernel,
        grid=(t // tm,),
        in_specs=[
            pl.BlockSpec((tm, D_MODEL), row),
            pl.BlockSpec((tm, D_MODEL // 2), row),
            pl.BlockSpec((tm, D_MODEL // 2), row),
            pl.BlockSpec((D_MODEL, D_MODEL), full),
            pl.BlockSpec((1, D_MODEL), full),
            pl.BlockSpec((R_ROWS, D_MODEL), full),
            pl.BlockSpec((R_ROWS, LANES), full),
        ],
        out_specs=[
            pl.BlockSpec((tm, D_MODEL), row),
            pl.BlockSpec((tm, D_MODEL // 2), row),
            pl.BlockSpec((8, tm), col),
            pl.BlockSpec((8, tm), col),
            pl.BlockSpec((N_EXPERTS, LANES), full),
        ],
        out_shape=[
            jax.ShapeDtypeStruct((t, D_MODEL), F32),
            jax.ShapeDtypeStruct((t, D_MODEL // 2), U32),
            jax.ShapeDtypeStruct((8, t), I32),
            jax.ShapeDtypeStruct((8, t), F32),
            jax.ShapeDtypeStruct((N_EXPERTS, LANES), I32),
        ],
        scratch_shapes=[pltpu.VMEM((N_EXPERTS, LANES), F32)],
        compiler_params=_cparams(("arbitrary",)),
        name="route",
    )(x2d, att, hm, pw["w_out"], pw["g_ffn"], pw["w_r"], pw["b_r"])


def _row_copy(src, s, dst, d, sem):
    return pltpu.make_async_copy(src.at[pl.ds(s, 1)], dst.at[pl.ds(d, 1)], sem)


def _dispatch_kernel(dest_ref, hp_ref, xs_in_ref, xs_ref, sem, *, tm):
    del xs_in_ref
    base = pl.program_id(0) * tm

    def body(t, carry):
        _row_copy(hp_ref, base + t, xs_ref, dest_ref[0, t], sem).start()
        _row_copy(hp_ref, base + t, xs_ref, dest_ref[1, t], sem).start()
        return carry

    lax.fori_loop(0, tm, body, 0)
    pltpu.make_async_copy(xs_ref.at[pl.ds(0, 2 * tm)], xs_ref.at[pl.ds(0, 2 * tm)], sem).wait()


def _dispatch(dest, hp, n_slots, tm):
    t = hp.shape[0]
    xs0 = jnp.zeros((n_slots, D_MODEL // 2), U32)
    return pl.pallas_call(
        functools.partial(_dispatch_kernel, tm=tm),
        grid=(t // tm,),
        in_specs=[
            pl.BlockSpec((8, tm), lambda i: (0, i), memory_space=pltpu.SMEM),
            pl.BlockSpec(memory_space=pl.ANY),
            pl.BlockSpec(memory_space=pl.ANY),
        ],
        out_specs=pl.BlockSpec(memory_space=pl.ANY),
        out_shape=jax.ShapeDtypeStruct((n_slots, D_MODEL // 2), U32),
        scratch_shapes=[pltpu.SemaphoreType.DMA],
        input_output_aliases={2: 0},
        compiler_params=_cparams(("arbitrary",)),
        name="dispatch",
    )(dest, hp, xs0)


def _expert_kernel(be_ref, nu_ref, xs_ref, w1_ref, w3_ref, w2_ref, ys_ref):
    del be_ref

    @pl.when(pl.program_id(0) < nu_ref[0])
    def _():
        xw = xs_ref[...]
        xa = lax.bitcast_convert_type(xw & jnp.uint32(0xFFFF0000), F32).astype(BF16)
        xb = lax.bitcast_convert_type(xw << 16, F32).astype(BF16)
        half = D_MODEL // 2
        h1 = _dot(xa, w1_ref[0, :half, :]) + _dot(xb, w1_ref[0, half:, :])
        h3 = _dot(xa, w3_ref[0, :half, :]) + _dot(xb, w3_ref[0, half:, :])
        a = (h1 * jax.nn.sigmoid(h1)) * h3
        ys_ref[...] = _dot(a.astype(BF16), w2_ref[0])

    @pl.when(pl.program_id(0) >= nu_ref[0])
    def _():
        ys_ref[...] = jnp.zeros_like(ys_ref)


def _experts(blk_e, n_used, xs, pw, bm):
    n_slots = xs.shape[0]
    nblk = n_slots // bm
    blk = lambda i, be, nu: (jnp.minimum(i, nu[0] - 1), 0)
    oblk = lambda i, be, nu: (i, 0)
    wsel = lambda i, be, nu: (be[i], 0, 0)
    return pl.pallas_call(
        _expert_kernel,
        grid_spec=pltpu.PrefetchScalarGridSpec(
            num_scalar_prefetch=2,
            grid=(nblk,),
            in_specs=[
                pl.BlockSpec((bm, D_MODEL // 2), blk),
                pl.BlockSpec((1, D_MODEL, EXPERT_FF), wsel),
                pl.BlockSpec((1, D_MODEL, EXPERT_FF), wsel),
                pl.BlockSpec((1, EXPERT_FF, D_MODEL), wsel),
            ],
            out_specs=pl.BlockSpec((bm, D_MODEL), oblk),
        ),
        out_shape=jax.ShapeDtypeStruct((n_slots, D_MODEL), F32),
        compiler_params=_cparams(("arbitrary",)),
        name="experts",
    )(blk_e, n_used, xs, pw["w1"], pw["w3"], pw["w2"])


def _combine_kernel(dest_ref, x1_ref, rg_ref, ys_ref, y_ref, r0_scr, r1_scr, sem, *, tm):
    def body(t, carry):
        _row_copy(ys_ref, dest_ref[0, t], r0_scr, t, sem).start()
        _row_copy(ys_ref, dest_ref[1, t], r1_scr, t, sem).start()
        return carry

    lax.fori_loop(0, tm, body, 0)
    gt = jnp.concatenate([rg_ref[...], jnp.zeros((LANES - 8, tm), F32)], axis=0).T
    pltpu.make_async_copy(ys_ref.at[pl.ds(0, tm)], r0_scr, sem).wait()
    pltpu.make_async_copy(ys_ref.at[pl.ds(0, tm)], r1_scr, sem).wait()
    y_ref[...] = x1_ref[...] + gt[:, 0:1] * r0_scr[...] + gt[:, 1:2] * r1_scr[...]


def _combine(dest, x1, rg, ys, tm):
    t = x1.shape[0]
    row = lambda i: (i, 0)
    col = lambda i: (0, i)
    return pl.pallas_call(
        functools.partial(_combine_kernel, tm=tm),
        grid=(t // tm,),
        in_specs=[
            pl.BlockSpec((8, tm), col, memory_space=pltpu.SMEM),
            pl.BlockSpec((tm, D_MODEL), row),
            pl.BlockSpec((8, tm), col),
            pl.BlockSpec(memory_space=pl.ANY),
        ],
        out_specs=pl.BlockSpec((tm, D_MODEL), row),
        out_shape=jax.ShapeDtypeStruct((t, D_MODEL), F32),
        scratch_shapes=[
            pltpu.VMEM((tm, D_MODEL), F32),
            pltpu.VMEM((tm, D_MODEL), F32),
            pltpu.SemaphoreType.DMA,
        ],
        compiler_params=_cparams(("arbitrary",)),
        name="combine",
    )(dest, x1, rg, ys)


def _prep_weights(g_attn, w_in, g_cq, w_uq, g_ckv, w_ukv, g_q, g_k, b_igate, b_fgate, g_mh,
                  w_out, g_ffn, w_group, b_group, w_erouter, b_erouter, w1, w3, w2):
    def cols(a, b):
        return w_in[:, a:b]

    o_cq, o_ckv, o_kr = 0, 256, 384
    o_mq, o_mk, o_mv, o_mi, o_mf, o_mo = 416, 672, 928, 1440, 1444, 1448
    hr = MLA_ROPE // 2
    misc = jnp.concatenate([
        cols(o_kr, o_kr + MLA_ROPE), -cols(o_kr + hr, o_kr + MLA_ROPE), cols(o_kr, o_kr + hr),
        cols(o_mi, o_mi + M_HEADS), cols(o_mf, o_mf + M_HEADS),
        jnp.zeros((D_MODEL, LANES - 2 * MLA_ROPE - 2 * M_HEADS), F32)], axis=1)
    w_p = jnp.concatenate([
        cols(o_cq, o_cq + 256), cols(o_ckv, o_ckv + 128), cols(o_mq, o_mq + 256), cols(o_mk, o_mk + 256),
        cols(o_mv, o_mv + 512), cols(o_mo, o_mo + 512), misc], axis=1).astype(BF16)
    nope, r1, r2 = w_uq[..., :MLA_NOPE], w_uq[..., MLA_NOPE:MLA_NOPE + hr], w_uq[..., MLA_NOPE + hr:]
    w_q = jnp.concatenate([nope, r1, r2, -r2, r1], axis=-1).reshape(MLA_Q_LORA, MLA_HEADS * HEAD_PAD).astype(BF16)
    pad_g = jnp.zeros((HEAD_PAD - MLA_QK,), F32)
    w_k = jnp.concatenate([w_ukv[..., :MLA_NOPE], jnp.zeros((MLA_KV_LORA, MLA_HEADS, HEAD_PAD - MLA_NOPE), F32)],
                          axis=-1).transpose(1, 0, 2).astype(BF16)
    w_v = w_ukv[..., MLA_NOPE:].transpose(1, 0, 2).astype(BF16)
    gate_bias = jnp.concatenate([jnp.zeros((_L_IG,), F32), b_igate, b_fgate,
                                 jnp.zeros((LANES - _L_FG - M_HEADS,), F32)])[None]
    w_r = jnp.concatenate([w_erouter.T, w_group.T, jnp.zeros((R_ROWS - N_EXPERTS - N_GROUPS, D_MODEL), F32)],
                          axis=0).astype(BF16)
    b_r = jnp.concatenate([b_erouter, b_group, jnp.zeros((R_ROWS - N_EXPERTS - N_GROUPS,), F32)])
    return {
        "w_in": w_p, "g_attn": g_attn[None], "g_cq": g_cq[None], "w_q": w_q, "g_ckv": g_ckv[None],
        "g_q": jnp.concatenate([g_q, pad_g])[None], "g_k": jnp.concatenate([g_k, pad_g])[None],
        "gate_bias": gate_bias, "w_k": w_k, "w_v": w_v, "g_mh": g_mh,
        "w_out": w_out.astype(BF16), "g_ffn": g_ffn[None], "w_r": w_r,
        "b_r": jnp.broadcast_to(b_r[:, None], (R_ROWS, LANES)),
        "w1": w1.astype(BF16), "w3": w3.astype(BF16), "w2": w2.astype(BF16),
    }


def _rope_table(pos):
    half = MLA_ROPE // 2
    inv = ROPE_BASE ** (-np.arange(half, dtype=np.float64) / half)
    ang = np.asarray(pos, np.float64)[:, None] * inv[None, :]
    cos = np.cos(ang)
    sin = np.sin(ang)
    c2 = np.concatenate([cos, cos], axis=1)
    s2 = np.concatenate([sin, sin], axis=1)
    return jnp.asarray(np.concatenate([c2, s2, c2, s2], axis=1), F32)


def _pick(n, pref):
    return pref if n % pref == 0 else n


def _moe_layer(x2d, att, hm, pw, *, tm_route, bm, tm_disp, tm_comb):
    t = x2d.shape[0]
    x1, hp, ri, rg, cnt = _route(x2d, att, hm, pw, tm_route)
    counts = cnt[:, 0]
    padded = (counts + bm - 1) // bm * bm
    pend = jnp.cumsum(padded)
    pstart = pend - padded
    ids = ri[0:2]
    sel = ids[:, None, :] == jnp.arange(N_EXPERTS, dtype=I32)[None, :, None]
    dest2 = jnp.sum(jnp.where(sel, pstart[None, :, None], 0), axis=1) + ri[2:4]
    dest = jnp.concatenate([dest2, jnp.zeros((6, t), I32)], axis=0)
    n_slots = (2 * t // bm + N_EXPERTS) * bm
    nblk = n_slots // bm
    n_used = (pend[-1] // bm).astype(I32)
    blk_e = jnp.searchsorted(pend, jnp.arange(nblk, dtype=I32) * bm, side="right").astype(I32)
    blk_last = jnp.max(jnp.where(counts > 0, jnp.arange(N_EXPERTS, dtype=I32), 0))
    blk_e = jnp.minimum(blk_e, blk_last)
    xs = _dispatch(dest, hp, n_slots, tm_disp)
    ys = _experts(blk_e, n_used[None], xs, pw, bm)
    return _combine(dest, x1, rg, ys, tm_comb)


def _state_pack(c, n):
    ct = jnp.swapaxes(c, -1, -2)
    return jnp.concatenate([ct, n[..., None], jnp.zeros(ct.shape[:-1] + (S_W - M_DV - 1,), F32)], axis=-1)


def _state_unpack(s, m):
    return jnp.swapaxes(s[..., :M_DV], -1, -2), s[..., M_DV], m[:, :, 0, 0]


def kernel(x_prompt, x_sample, cache_ckv, cache_krope, state_mlstm_c, state_mlstm_n, state_mlstm_m,
           meta_tokens, g_attn, w_in, g_cq, w_uq, g_ckv, w_ukv, g_q, g_k, b_igate, b_fgate, g_mh,
           w_out, g_ffn, w_group, b_group, w_erouter, b_erouter, w1, w3, w2):
    bp, seq = x_prompt.shape[:2]
    bs, dec = x_sample.shape[:2]
    past = cache_ckv.shape[2]
    layer = 0
    pw = _prep_weights(g_attn[layer], w_in[layer], g_cq[layer], w_uq[layer], g_ckv[layer], w_ukv[layer],
                       g_q[layer], g_k[layer], b_igate[layer], b_fgate[layer], g_mh[layer], w_out[layer],
                       g_ffn[layer], w_group[layer], b_group[layer], w_erouter[layer], b_erouter[layer],
                       w1[layer], w3[layer], w2[layer])

    xm = jnp.concatenate([meta_tokens, jnp.zeros((META_PAD - N_META, D_MODEL), F32)], axis=0)
    tab_m = _rope_table(np.arange(META_PAD) - N_META)
    _, ckv_m, misc_m, mq_m, mk_m, mv_m, og_m = _project(xm, tab_m, pw, META_PAD)
    zero_s = jnp.zeros((1, M_HEADS, M_DK, S_W), F32)
    zero_m = jnp.zeros((1, M_HEADS, 8, LANES), F32)
    _, s_meta, m_meta = _mlstm(mq_m, mk_m, mv_m, og_m, misc_m, zero_s, zero_m, pw["g_mh"],
                               nb=1, l=META_PAD, lc=META_PAD, n_valid=N_META)

    tp = bp * seq
    xp2 = x_prompt.reshape(tp, D_MODEL)
    tm_p = _pick(seq, 512)
    q_p, ckv_p, misc_p, mq_p, mk_p, mv_p, og_p = _project(xp2, _rope_table(np.arange(seq)), pw, tm_p)
    tq = _pick(seq, 256)
    att_p = _attention(q_p, ckv_p, misc_p, ckv_m, misc_m, pw, nb=bp, lq=seq, lf=seq, tq=tq, tk=tq, causal=True)
    hm_p, s_p, m_p = _mlstm(mq_p, mk_p, mv_p, og_p, misc_p, s_meta, m_meta, pw["g_mh"],
                            nb=bp, l=seq, lc=_pick(seq, 128))
    y_p = _moe_layer(xp2, att_p, hm_p, pw, tm_route=tm_p, bm=512, tm_disp=tm_p, tm_comb=_pick(seq, 256))

    ts = bs * dec
    xs2 = x_sample.reshape(ts, D_MODEL)
    q_s, ckv_s, misc_s, mq_s, mk_s, mv_s, og_s = _project(xs2, _rope_table(past + np.arange(dec)), pw, dec)
    lf_s = past + dec
    kr_cache = jnp.concatenate([cache_krope[layer], jnp.zeros((bs, past, LANES - MLA_ROPE), F32)], axis=-1)
    ckv_fs = jnp.concatenate([cache_ckv[layer], ckv_s.reshape(bs, dec, MLA_KV_LORA)], axis=1)
    misc_fs = jnp.concatenate([kr_cache, misc_s.reshape(bs, dec, LANES)], axis=1)
    att_s = _attention(q_s, ckv_fs.reshape(bs * lf_s, MLA_KV_LORA), misc_fs.reshape(bs * lf_s, LANES),
                       ckv_m, misc_m, pw, nb=bs, lq=dec, lf=lf_s, tq=dec, tk=_pick(past, 256), causal=False)
    s0 = _state_pack(state_mlstm_c[layer], state_mlstm_n[layer])
    m0 = jnp.broadcast_to(state_mlstm_m[layer][:, :, None, None], (bs, M_HEADS, 8, LANES))
    hm_s, s_s, m_s = _mlstm(mq_s, mk_s, mv_s, og_s, misc_s, s0, m0, pw["g_mh"], nb=bs, l=dec, lc=dec)
    y_s = _moe_layer(xs2, att_s, hm_s, pw, tm_route=_pick(ts, 512), bm=128, tm_disp=_pick(ts, 512),
                     tm_comb=_pick(ts, 256))

    m_ckv = ckv_m[:N_META]
    m_kr = misc_m[:N_META, :MLA_ROPE]
    new_ckv_p = jnp.concatenate([jnp.broadcast_to(m_ckv[None], (bp, N_META, MLA_KV_LORA)),
                                 ckv_p.reshape(bp, seq, MLA_KV_LORA)], axis=1)[None]
    new_kr_p = jnp.concatenate([jnp.broadcast_to(m_kr[None], (bp, N_META, MLA_ROPE)),
                                misc_p[:, :MLA_ROPE].reshape(bp, seq, MLA_ROPE)], axis=1)[None]
    c_p, n_p, mm_p = _state_unpack(s_p, m_p)
    c_s, n_s, mm_s = _state_unpack(s_s, m_s)
    return (y_p.reshape(bp, seq, D_MODEL), y_s.reshape(bs, dec, D_MODEL),
            new_ckv_p, new_kr_p, c_p[None], n_p[None], mm_p[None],
            ckv_s.reshape(bs, dec, MLA_KV_LORA)[None], misc_s[:, :MLA_ROPE].reshape(bs, dec, MLA_ROPE)[None],
            c_s[None], n_s[None], mm_s[None])
```

```python
import functools

import numpy as np
import jax
import jax.numpy as jnp
from jax import lax
from jax.experimental import pallas as pl
from jax.experimental.pallas import tpu as pltpu

F32 = jnp.float32
BF16 = jnp.bfloat16
I32 = jnp.int32
U32 = jnp.uint32

D_MODEL = 1024
CHUNK = 64
N_META = 16
MLA_HEADS = 8
MLA_V = 64
MLA_NOPE = 64
MLA_ROPE = 32
MLA_QK = MLA_NOPE + MLA_ROPE
MLA_Q_LORA = 256
MLA_KV_LORA = 128
MLA_SCALE = MLA_QK ** -0.5
ROPE_BASE = 10000.0
M_HEADS = 4
M_DV = 128
M_DK = 64
N_GROUPS = 4
E_PER_GROUP = 8
N_EXPERTS = 32
EXPERT_FF = 512
EPS = 1e-6

LANES = 128
HEAD_PAD = 128
PROJ_PAD = 2048
META_PAD = 128
NEG_BIG = -1e30
VMEM_LIMIT = 56 * 1024 * 1024

_O_CQ, _O_CKV, _O_MQ, _O_MK, _O_MV, _O_MO, _O_MISC = 0, 256, 384, 640, 896, 1408, 1920
_L_IG, _L_FG = 64, 68


def _cparams(sem):
    return pltpu.CompilerParams(dimension_semantics=sem, vmem_limit_bytes=VMEM_LIMIT)


def _dot(a, b):
    return jnp.dot(a, b, preferred_element_type=F32)


def _dot_nt(a, b):
    return lax.dot_general(a, b, (((1,), (1,)), ((), ())), preferred_element_type=F32)


def _split3(x):
    x1 = x.astype(BF16)
    r1 = x - x1.astype(F32)
    x2 = r1.astype(BF16)
    x3 = (r1 - x2.astype(F32)).astype(BF16)
    return x1, x2, x3


def _proj_kernel(x_ref, tab_ref, w_ref, g_ref, gcq_ref, wq_ref, gckv_ref, gq_ref, bias_ref,
                 q_ref, ckv_ref, misc_ref, mq_ref, mk_ref, mv_ref, og_ref):
    x = x_ref[...]
    xn = x * lax.rsqrt(jnp.mean(x * x, axis=-1, keepdims=True) + EPS) * g_ref[...]
    z = _dot(xn.astype(BF16), w_ref[...])
    tab = tab_ref[...]
    lane = lax.broadcasted_iota(I32, tab.shape, 1)

    cq = z[:, _O_CQ:_O_CQ + MLA_Q_LORA]
    cqn = cq * lax.rsqrt(jnp.mean(cq * cq, axis=-1, keepdims=True) + EPS) * gcq_ref[...]
    qz = _dot(cqn.astype(BF16), wq_ref[...])
    gq = gq_ref[...]
    for h in range(MLA_HEADS):
        zh = qz[:, h * HEAD_PAD:(h + 1) * HEAD_PAD]
        y = zh * tab
        rot = y + pltpu.roll(y, LANES - MLA_ROPE, 1)
        qh = jnp.where(lane < MLA_NOPE, zh, jnp.where(lane < MLA_QK, rot, 0.0))
        ms = jnp.sum(qh * qh, axis=-1, keepdims=True) * (1.0 / MLA_QK)
        q_ref[h] = (qh * lax.rsqrt(ms + EPS) * gq).astype(BF16)

    ckv = z[:, _O_CKV:_O_CKV + MLA_KV_LORA]
    ckv_ref[...] = ckv * lax.rsqrt(jnp.mean(ckv * ckv, axis=-1, keepdims=True) + EPS) * gckv_ref[...]

    zm = z[:, _O_MISC:_O_MISC + LANES]
    y = zm * tab
    rot = y + pltpu.roll(y, LANES - MLA_ROPE, 1)
    gate = zm + bias_ref[...]
    logf = jnp.minimum(gate, 0.0) - jnp.log1p(jnp.exp(-jnp.abs(gate)))
    misc = jnp.where(lane < MLA_ROPE, rot,
                     jnp.where((lane >= _L_IG) & (lane < _L_FG), gate,
                               jnp.where((lane >= _L_FG) & (lane < _L_FG + M_HEADS), logf, 0.0)))
    misc_ref[...] = misc

    mq_ref[...] = z[:, _O_MQ:_O_MQ + M_HEADS * M_DK].astype(BF16)
    mk_ref[...] = (z[:, _O_MK:_O_MK + M_HEADS * M_DK] * (M_DK ** -0.5)).astype(BF16)
    mv_ref[...] = z[:, _O_MV:_O_MV + M_HEADS * M_DV].astype(BF16)
    og_ref[...] = jax.nn.sigmoid(z[:, _O_MO:_O_MO + M_HEADS * M_DV]).astype(BF16)


def _project(x2d, tab, pw, tm):
    t = x2d.shape[0]
    nt = t // tm
    ntab = tab.shape[0] // tm
    row = lambda i: (i, 0)
    full = lambda i: (0, 0)
    return pl.pallas_call(
        _proj_kernel,
        grid=(nt,),
        in_specs=[
            pl.BlockSpec((tm, D_MODEL), row),
            pl.BlockSpec((tm, LANES), lambda i: (i % ntab, 0)),
            pl.BlockSpec((D_MODEL, PROJ_PAD), full),
            pl.BlockSpec((1, D_MODEL), full),
            pl.BlockSpec((1, MLA_Q_LORA), full),
            pl.BlockSpec((MLA_Q_LORA, MLA_HEADS * HEAD_PAD), full),
            pl.BlockSpec((1, MLA_KV_LORA), full),
            pl.BlockSpec((1, HEAD_PAD), full),
            pl.BlockSpec((1, LANES), full),
        ],
        out_specs=[
            pl.BlockSpec((MLA_HEADS, tm, HEAD_PAD), lambda i: (0, i, 0)),
            pl.BlockSpec((tm, MLA_KV_LORA), row),
            pl.BlockSpec((tm, LANES), row),
            pl.BlockSpec((tm, M_HEADS * M_DK), row),
            pl.BlockSpec((tm, M_HEADS * M_DK), row),
            pl.BlockSpec((tm, M_HEADS * M_DV), row),
            pl.BlockSpec((tm, M_HEADS * M_DV), row),
        ],
        out_shape=[
            jax.ShapeDtypeStruct((MLA_HEADS, t, HEAD_PAD), BF16),
            jax.ShapeDtypeStruct((t, MLA_KV_LORA), F32),
            jax.ShapeDtypeStruct((t, LANES), F32),
            jax.ShapeDtypeStruct((t, M_HEADS * M_DK), BF16),
            jax.ShapeDtypeStruct((t, M_HEADS * M_DK), BF16),
            jax.ShapeDtypeStruct((t, M_HEADS * M_DV), BF16),
            jax.ShapeDtypeStruct((t, M_HEADS * M_DV), BF16),
        ],
        compiler_params=_cparams(("arbitrary",)),
        name="projection",
    )(x2d, tab, pw["w_in"], pw["g_attn"], pw["g_cq"], pw["w_q"], pw["g_ckv"], pw["g_q"], pw["gate_bias"])


SAFE_BOUND = 40.0
LOG2E = 1.4426950408889634


def _attn_kernel(q_ref, ckv_ref, misc_ref, ckvt_ref, misct_ref, wk_ref, wv_ref, gk_ref, o_ref,
                 kt_scr, v_scr, kmax_scr, bnd_scr, acc_scr, *, lf, tq, tk, bt, causal, n_tail):
    i = pl.program_id(1)
    gk = gk_ref[...]

    def build(ckv_rows, misc_rows, dst):
        n = ckv_rows.shape[0]
        lane = lax.broadcasted_iota(I32, (n, LANES), 1)
        cb = ckv_rows.astype(BF16)
        krp = jnp.where((lane >= MLA_NOPE) & (lane < MLA_QK), pltpu.roll(misc_rows, MLA_NOPE, 1), 0.0)
        onecol = (lane == MLA_V).astype(F32)
        for h in range(MLA_HEADS):
            kk = _dot(cb, wk_ref[h]) + krp
            ms = jnp.sum(kk * kk, axis=-1, keepdims=True) * (1.0 / MLA_QK)
            kf = (kk * lax.rsqrt(ms + EPS) * gk).astype(BF16).astype(F32)
            kn = jnp.max(jnp.sum(kf * kf, axis=-1, keepdims=True), axis=0, keepdims=True)
            kmax_scr[h] = jnp.maximum(kmax_scr[h], jnp.broadcast_to(kn, (8, LANES)))
            kt_scr[h, :, pl.ds(dst, n)] = kf.T.astype(BF16)
            v_scr[h, pl.ds(dst, n), :] = (_dot(cb, wv_ref[h]) + onecol).astype(BF16)

    @pl.when(i == 0)
    def _():
        kmax_scr[...] = jnp.zeros_like(kmax_scr)

        def body(r, carry):
            r0 = pl.multiple_of(r * bt, bt)
            build(ckv_ref[pl.ds(r0, bt), :], misc_ref[pl.ds(r0, bt), :], r0)
            return carry

        lax.fori_loop(0, lf // bt, body, 0)
        build(ckvt_ref[0], misct_ref[0], lf)

    tail_mask = lax.broadcasted_iota(I32, (tq, META_PAD), 1) < n_tail
    if causal:
        rr = lax.broadcasted_iota(I32, (tq, tq), 0) // CHUNK
        cc = lax.broadcasted_iota(I32, (tq, tq), 1) // CHUNK
        diag_mask = cc <= rr
        nfull = i * (tq // tk)
    else:
        nfull = lf // tk

    bmax = jnp.zeros((1, 1), F32)
    for h in range(MLA_HEADS):
        qf = q_ref[h].astype(F32)
        qn = jnp.sqrt(jnp.sum(qf * qf, axis=-1, keepdims=True))
        bnd = qn * (jnp.sqrt(kmax_scr[h, 0:1, 0:1]) * MLA_SCALE)
        bnd_scr[h] = jnp.broadcast_to(bnd * LOG2E, (tq, LANES))
        bmax = jnp.maximum(bmax, jnp.max(bnd, axis=0, keepdims=True))
    safe = bmax[0, 0] <= SAFE_BOUND

    def finish(h, acc):
        o_ref[:, h * MLA_V:(h + 1) * MLA_V] = (acc[:, :MLA_V] / acc[:, MLA_V:MLA_V + 1]).astype(BF16)

    @pl.when(safe)
    def _():
        def contrib(h, kt, vrows, mask):
            w = kt.shape[1]
            s = _dot(q_ref[h], kt) * (MLA_SCALE * LOG2E) - jnp.tile(bnd_scr[h], (1, w // LANES))
            if mask is not None:
                s = jnp.where(mask, s, -jnp.inf)
            return _dot(jnp.exp2(s).astype(BF16), vrows)

        for h in range(MLA_HEADS):
            acc_scr[h] = contrib(h, kt_scr[h, :, lf:lf + META_PAD], v_scr[h, lf:lf + META_PAD, :], tail_mask)

        def body(j, carry):
            r0 = pl.multiple_of(j * tk, tk)
            for h in range(MLA_HEADS):
                acc_scr[h] += contrib(h, kt_scr[h, :, pl.ds(r0, tk)], v_scr[h, pl.ds(r0, tk), :], None)
            return carry

        lax.fori_loop(0, nfull, body, 0)
        for h in range(MLA_HEADS):
            acc = acc_scr[h]
            if causal:
                r0 = pl.multiple_of(i * tq, tq)
                acc = acc + contrib(h, kt_scr[h, :, pl.ds(r0, tq)], v_scr[h, pl.ds(r0, tq), :], diag_mask)
            finish(h, acc)

    @pl.when(jnp.logical_not(safe))
    def _():
        for h in range(MLA_HEADS):
            qh = q_ref[h]

            def step(carry, kt, vrows, mask):
                m, acc = carry
                s = _dot(qh, kt) * MLA_SCALE
                if mask is not None:
                    s = jnp.where(mask, s, -jnp.inf)
                m_new = jnp.maximum(m, jnp.max(s, axis=-1, keepdims=True))
                acc = jnp.exp(m - m_new) * acc + _dot(jnp.exp(s - m_new).astype(BF16), vrows)
                return m_new, acc

            carry = (jnp.full((tq, 1), -jnp.inf, F32), jnp.zeros((tq, LANES), F32))
            carry = step(carry, kt_scr[h, :, lf:lf + META_PAD], v_scr[h, lf:lf + META_PAD, :], tail_mask)

            def body(j, c):
                r0 = pl.multiple_of(j * tk, tk)
                return step(c, kt_scr[h, :, pl.ds(r0, tk)], v_scr[h, pl.ds(r0, tk), :], None)

            carry = lax.fori_loop(0, nfull, body, carry)
            if causal:
                r0 = pl.multiple_of(i * tq, tq)
                carry = step(carry, kt_scr[h, :, pl.ds(r0, tq)], v_scr[h, pl.ds(r0, tq), :], diag_mask)
            finish(h, carry[1])


def _attention(q, ckv_f, misc_f, ckv_t, misc_t, pw, *, nb, lq, lf, tq, tk, causal, n_tail):
    nq = lq // tq
    bt = min(512, lf)
    assert lf % bt == 0 and lf % tk == 0 and lq % tq == 0 and (not causal or tq % tk == 0)
    kern = functools.partial(_attn_kernel, lf=lf, tq=tq, tk=tk, bt=bt, causal=causal, n_tail=n_tail)
    full2 = lambda b, i: (0, 0)
    full3 = lambda b, i: (0, 0, 0)
    tail = full3 if ckv_t.shape[0] == 1 else (lambda b, i: (b, 0, 0))
    lk = lf + META_PAD
    return pl.pallas_call(
        kern,
        grid=(nb, nq),
        in_specs=[
            pl.BlockSpec((MLA_HEADS, tq, HEAD_PAD), lambda b, i: (0, b * nq + i, 0)),
            pl.BlockSpec((lf, MLA_KV_LORA), lambda b, i: (b, 0)),
            pl.BlockSpec((lf, LANES), lambda b, i: (b, 0)),
            pl.BlockSpec((1, META_PAD, MLA_KV_LORA), tail),
            pl.BlockSpec((1, META_PAD, LANES), tail),
            pl.BlockSpec((MLA_HEADS, MLA_KV_LORA, HEAD_PAD), full3),
            pl.BlockSpec((MLA_HEADS, MLA_KV_LORA, LANES), full3),
            pl.BlockSpec((1, HEAD_PAD), full2),
        ],
        out_specs=pl.BlockSpec((tq, MLA_HEADS * MLA_V), lambda b, i: (b * nq + i, 0)),
        out_shape=jax.ShapeDtypeStruct((nb * lq, MLA_HEADS * MLA_V), BF16),
        scratch_shapes=[
            pltpu.VMEM((MLA_HEADS, HEAD_PAD, lk), BF16),
            pltpu.VMEM((MLA_HEADS, lk, LANES), BF16),
            pltpu.VMEM((MLA_HEADS, 8, LANES), F32),
            pltpu.VMEM((MLA_HEADS, tq, LANES), F32),
            pltpu.VMEM((MLA_HEADS, tq, LANES), F32),
        ],
        compiler_params=_cparams(("arbitrary", "arbitrary")),
        name="attention",
    )(q, ckv_f, misc_f, ckv_t, misc_t, pw["w_k"], pw["w_v"], pw["g_k"])


S_W = 2 * M_DV


def _mlstm_kernel(q_ref, k_ref, v_ref, og_ref, misc_ref, s0_ref, m0_ref, gmh_ref,
                  h_ref, sout_ref, mout_ref, s_scr, m_scr, *, lc, n_valid):
    c = pl.program_id(1)
    nc = pl.num_programs(1)

    @pl.when(c == 0)
    def _():
        s_scr[...] = s0_ref[0]
        m_scr[...] = m0_ref[0]

    g = misc_ref[...]
    if n_valid is not None:
        row = lax.broadcasted_iota(I32, g.shape, 0) + c * lc
        lane = lax.broadcasted_iota(I32, g.shape, 1)
        pad_val = jnp.where((lane >= _L_IG) & (lane < _L_FG), NEG_BIG, 0.0)
        g = jnp.where(row < n_valid, g, pad_val)
    gt = g.T
    gt8 = gt[_L_IG:_L_IG + 8, :]
    rr = lax.broadcasted_iota(I32, (lc, lc), 0)
    cc = lax.broadcasted_iota(I32, (lc, lc), 1)
    causal = cc <= rr
    tri = causal.astype(BF16)
    trit = (rr <= cc).astype(BF16)
    g1, g2, g3 = _split3(g)
    bcol_all = _dot(tri, g1) + _dot(tri, g2) + _dot(tri, g3)
    t1, t2, t3 = _split3(gt8)
    brow_all = _dot(t1, trit) + _dot(t2, trit) + _dot(t3, trit)

    kt = k_ref[...].astype(F32).T
    onecol = (lax.broadcasted_iota(I32, (lc, M_DV), 1) == 0).astype(BF16)

    for h in range(M_HEADS):
        bcol = bcol_all[:, _L_FG + h:_L_FG + h + 1]
        brow = brow_all[M_HEADS + h:M_HEADS + h + 1, :]
        igrow = gt8[h:h + 1, :]
        m_prev = m_scr[h, 0:1, 0:1]
        qh = q_ref[:, h * M_DK:(h + 1) * M_DK]
        kh = k_ref[:, h * M_DK:(h + 1) * M_DK]
        vext = jnp.concatenate([v_ref[:, h * M_DV:(h + 1) * M_DV], onecol], axis=1)

        log_d = jnp.where(causal, bcol - brow + igrow, -jnp.inf)
        log_inter = bcol + m_prev
        m_t = jnp.maximum(log_inter, jnp.max(log_d, axis=-1, keepdims=True))
        w = jnp.exp(log_d - m_t) * _dot_nt(qh, kh)
        inter = jnp.exp(log_inter - m_t)
        s_h = s_scr[h]
        r = _dot(w.astype(BF16), vext) + inter * _dot(qh, s_h.astype(BF16))
        num = r[:, :M_DV]
        den = r[:, M_DV:M_DV + 1]
        hh = num / jnp.maximum(jnp.abs(den), jnp.exp(-m_t))
        hn = hh * lax.rsqrt(jnp.mean(hh * hh, axis=-1, keepdims=True) + EPS) * gmh_ref[h:h + 1, :]
        h_ref[:, h * M_DV:(h + 1) * M_DV] = (hn * og_ref[:, h * M_DV:(h + 1) * M_DV].astype(F32)).astype(BF16)

        blast = bcol[lc - 1:lc, :]
        logw = blast - brow + igrow
        m_new = jnp.maximum(blast + m_prev, jnp.max(logw, axis=-1, keepdims=True))
        wrow = jnp.exp(logw - m_new)
        decay = jnp.exp(blast + m_prev - m_new)
        ktw = (kt[h * M_DK:(h + 1) * M_DK, :] * wrow).astype(BF16)
        s_scr[h] = decay * s_h + _dot(ktw, vext)
        m_scr[h] = jnp.broadcast_to(m_new, (8, LANES))

    @pl.when(c == nc - 1)
    def _():
        sout_ref[0] = s_scr[...]
        mout_ref[0] = m_scr[...]


def _mlstm(mq, mk, mv, og, misc, s0, m0, gmh, *, nb, l, lc, n_valid=None):
    nc = l // lc
    shared = s0.shape[0] == 1
    st = (lambda b, c: (0, 0, 0, 0)) if shared else (lambda b, c: (b, 0, 0, 0))
    row = lambda b, c: (b * nc + c, 0)
    kern = functools.partial(_mlstm_kernel, lc=lc, n_valid=n_valid)
    return pl.pallas_call(
        kern,
        grid=(nb, nc),
        in_specs=[
            pl.BlockSpec((lc, M_HEADS * M_DK), row),
            pl.BlockSpec((lc, M_HEADS * M_DK), row),
            pl.BlockSpec((lc, M_HEADS * M_DV), row),
            pl.BlockSpec((lc, M_HEADS * M_DV), row),
            pl.BlockSpec((lc, LANES), row),
            pl.BlockSpec((1, M_HEADS, M_DK, S_W), st),
            pl.BlockSpec((1, M_HEADS, 8, LANES), st),
            pl.BlockSpec((M_HEADS, M_DV), lambda b, c: (0, 0)),
        ],
        out_specs=[
            pl.BlockSpec((lc, M_HEADS * M_DV), row),
            pl.BlockSpec((1, M_HEADS, M_DK, S_W), lambda b, c: (b, 0, 0, 0)),
            pl.BlockSpec((1, M_HEADS, 8, LANES), lambda b, c: (b, 0, 0, 0)),
        ],
        out_shape=[
            jax.ShapeDtypeStruct((nb * l, M_HEADS * M_DV), BF16),
            jax.ShapeDtypeStruct((nb, M_HEADS, M_DK, S_W), F32),
            jax.ShapeDtypeStruct((nb, M_HEADS, 8, LANES), F32),
        ],
        scratch_shapes=[
            pltpu.VMEM((M_HEADS, M_DK, S_W), F32),
            pltpu.VMEM((M_HEADS, 8, LANES), F32),
        ],
        compiler_params=_cparams(("arbitrary", "arbitrary")),
        name="mlstm",
    )(mq, mk, mv, og, misc, s0, m0, gmh)


R_ROWS = 40


def _route_kernel(x_ref, att_ref, hm_ref, wo_ref, g_ref, wr_ref, br_ref,
                  x1_ref, hp_ref, ri_ref, rg_ref, cnt_ref, cnt_scr):
    i = pl.program_id(0)
    tm = x_ref.shape[0]

    @pl.when(i == 0)
    def _():
        cnt_scr[...] = jnp.zeros_like(cnt_scr)

    mix = jnp.concatenate([att_ref[...], hm_ref[...]], axis=1)
    x1 = x_ref[...] + _dot(mix, wo_ref[...])
    x1_ref[...] = x1
    hn = x1 * lax.rsqrt(jnp.mean(x1 * x1, axis=-1, keepdims=True) + EPS) * g_ref[...]
    hb = hn.astype(BF16)
    half = D_MODEL // 2
    hi = lax.bitcast_convert_type(hb[:, :half].astype(F32), U32)
    lo = lax.bitcast_convert_type(hb[:, half:].astype(F32), U32)
    hp_ref[...] = (hi & jnp.uint32(0xFFFF0000)) | (lo >> 16)

    logits = _dot_nt(wr_ref[...], hb) + br_ref[:, 0:1]
    e_log = logits[0:N_EXPERTS, :]
    g_log = logits[N_EXPERTS:N_EXPERTS + N_GROUPS, :]
    gmax = jnp.max(g_log, axis=0, keepdims=True)
    gsum = jnp.sum(jnp.exp(g_log - gmax), axis=0, keepdims=True)
    gi = lax.broadcasted_iota(I32, g_log.shape, 0)
    g_idx = jnp.min(jnp.where(g_log == gmax, gi, N_GROUPS), axis=0, keepdims=True)
    e_sel = jnp.zeros((E_PER_GROUP, tm), F32)
    for gg in range(N_GROUPS):
        e_sel = jnp.where(g_idx == gg, e_log[gg * E_PER_GROUP:(gg + 1) * E_PER_GROUP, :], e_sel)
    ei = lax.broadcasted_iota(I32, e_sel.shape, 0)
    m1 = jnp.max(e_sel, axis=0, keepdims=True)
    i1 = jnp.min(jnp.where(e_sel == m1, ei, E_PER_GROUP), axis=0, keepdims=True)
    e2 = jnp.where(ei == i1, -jnp.inf, e_sel)
    m2 = jnp.max(e2, axis=0, keepdims=True)
    i2 = jnp.min(jnp.where(e2 == m2, ei, E_PER_GROUP), axis=0, keepdims=True)
    ex = jnp.exp(m2 - m1)
    gp = 1.0 / gsum
    p1 = 1.0 / (1.0 + ex)
    gate1 = gp * p1
    gate2 = gp * (ex * p1)
    id1 = g_idx * E_PER_GROUP + i1
    id2 = g_idx * E_PER_GROUP + i2

    xi = lax.broadcasted_iota(I32, (N_EXPERTS, tm), 0)
    oh1 = xi == id1
    oh2 = xi == id2
    e_cnt = (oh1 | oh2).astype(F32)
    rr = lax.broadcasted_iota(I32, (tm, tm), 0)
    cc = lax.broadcasted_iota(I32, (tm, tm), 1)
    upper = (rr < cc).astype(BF16)
    pref = _dot(e_cnt.astype(BF16), upper) + cnt_scr[:, 0:1]
    rank1 = jnp.sum(jnp.where(oh1, pref, 0.0), axis=0, keepdims=True)
    rank2 = jnp.sum(jnp.where(oh2, pref, 0.0), axis=0, keepdims=True)
    cnt_new = cnt_scr[...] + jnp.sum(e_cnt, axis=1, keepdims=True)
    cnt_scr[...] = cnt_new
    cnt_ref[...] = cnt_new.astype(I32)

    zi = jnp.zeros((1, tm), I32)
    ri_ref[...] = jnp.concatenate([id1, id2, rank1.astype(I32), rank2.astype(I32), zi, zi, zi, zi], axis=0)
    zf = jnp.zeros((1, tm), F32)
    rg_ref[...] = jnp.concatenate([gate1, gate2, zf, zf, zf, zf, zf, zf], axis=0)


def _route(x2d, att, hm, pw, tm):
    t = x2d.shape[0]
    row = lambda i: (i, 0)
    col = lambda i: (0, i)
    full = lambda i: (0, 0)
    return pl.pallas_call(
        _route_kernel,
        grid=(t // tm,),
        in_specs=[
            pl.BlockSpec((tm, D_MODEL), row),
            pl.BlockSpec((tm, D_MODEL // 2), row),
            pl.BlockSpec((tm, D_MODEL // 2), row),
            pl.BlockSpec((D_MODEL, D_MODEL), full),
            pl.BlockSpec((1, D_MODEL), full),
            pl.BlockSpec((R_ROWS, D_MODEL), full),
            pl.BlockSpec((R_ROWS, LANES), full),
        ],
        out_specs=[
            pl.BlockSpec((tm, D_MODEL), row),
            pl.BlockSpec((tm, D_MODEL // 2), row),
            pl.BlockSpec((8, tm), col),
            pl.BlockSpec((8, tm), col),
            pl.BlockSpec((N_EXPERTS, LANES), full),
        ],
        out_shape=[
            jax.ShapeDtypeStruct((t, D_MODEL), F32),
            jax.ShapeDtypeStruct((t, D_MODEL // 2), U32),
            jax.ShapeDtypeStruct((8, t), I32),
            jax.ShapeDtypeStruct((8, t), F32),
            jax.ShapeDtypeStruct((N_EXPERTS, LANES), I32),
        ],
        scratch_shapes=[pltpu.VMEM((N_EXPERTS, LANES), F32)],
        compiler_params=_cparams(("arbitrary",)),
        name="route",
    )(x2d, att, hm, pw["w_out"], pw["g_ffn"], pw["w_r"], pw["b_r"])


def _row_copy(src, s, dst, d, sem):
    return pltpu.make_async_copy(src.at[pl.ds(s, 1)], dst.at[pl.ds(d, 1)], sem)


DMA_UNROLL = 8


def _dispatch_kernel(dest_ref, hp_ref, xs_in_ref, xs_ref, sem, *, tm):
    del xs_in_ref

    def body(g, carry):
        for u in range(DMA_UNROLL):
            t = g * DMA_UNROLL + u
            _row_copy(hp_ref, t, xs_ref, dest_ref[0, t], sem).start(priority=0)
            _row_copy(hp_ref, t, xs_ref, dest_ref[1, t], sem).start(priority=1)
        return carry

    lax.fori_loop(0, tm // DMA_UNROLL, body, 0)
    pltpu.make_async_copy(xs_ref.at[pl.ds(0, 2 * tm)], xs_ref.at[pl.ds(0, 2 * tm)], sem).wait()


def _dispatch(dest, hp, n_slots, tm):
    t = hp.shape[0]
    xs0 = jnp.zeros((n_slots, D_MODEL // 2), U32)
    return pl.pallas_call(
        functools.partial(_dispatch_kernel, tm=tm),
        grid=(t // tm,),
        in_specs=[
            pl.BlockSpec((8, tm), lambda i: (0, i), memory_space=pltpu.SMEM),
            pl.BlockSpec((tm, D_MODEL // 2), lambda i: (i, 0)),
            pl.BlockSpec(memory_space=pl.ANY),
        ],
        out_specs=pl.BlockSpec(memory_space=pl.ANY),
        out_shape=jax.ShapeDtypeStruct((n_slots, D_MODEL // 2), U32),
        scratch_shapes=[pltpu.SemaphoreType.DMA],
        input_output_aliases={2: 0},
        compiler_params=_cparams(("arbitrary",)),
        name="dispatch",
    )(dest, hp, xs0)


def _expert_kernel(be_ref, nu_ref, xs_ref, w1_ref, w3_ref, w2_ref, ys_ref):
    del be_ref

    @pl.when(pl.program_id(0) < nu_ref[0])
    def _():
        xw = xs_ref[...]
        xa = lax.bitcast_convert_type(xw & jnp.uint32(0xFFFF0000), F32).astype(BF16)
        xb = lax.bitcast_convert_type(xw << 16, F32).astype(BF16)
        half = D_MODEL // 2
        h1 = _dot(xa, w1_ref[0, :half, :]) + _dot(xb, w1_ref[0, half:, :])
        h3 = _dot(xa, w3_ref[0, :half, :]) + _dot(xb, w3_ref[0, half:, :])
        a = (h1 * jax.nn.sigmoid(h1)) * h3
        ys_ref[...] = _dot(a.astype(BF16), w2_ref[0])

    @pl.when(pl.program_id(0) >= nu_ref[0])
    def _():
        ys_ref[...] = jnp.zeros_like(ys_ref)


def _experts(blk_e, n_used, xs, pw, bm):
    n_slots = xs.shape[0]
    nblk = n_slots // bm
    blk = lambda i, be, nu: (jnp.minimum(i, nu[0] - 1), 0)
    oblk = lambda i, be, nu: (i, 0)
    wsel = lambda i, be, nu: (be[i], 0, 0)
    return pl.pallas_call(
        _expert_kernel,
        grid_spec=pltpu.PrefetchScalarGridSpec(
            num_scalar_prefetch=2,
            grid=(nblk,),
            in_specs=[
                pl.BlockSpec((bm, D_MODEL // 2), blk),
                pl.BlockSpec((1, D_MODEL, EXPERT_FF), wsel),
                pl.BlockSpec((1, D_MODEL, EXPERT_FF), wsel),
                pl.BlockSpec((1, EXPERT_FF, D_MODEL), wsel),
            ],
            out_specs=pl.BlockSpec((bm, D_MODEL), oblk),
        ),
        out_shape=jax.ShapeDtypeStruct((n_slots, D_MODEL), F32),
        compiler_params=_cparams(("arbitrary",)),
        name="experts",
    )(blk_e, n_used, xs, pw["w1"], pw["w3"], pw["w2"])


def _combine_kernel(dest_ref, x1_ref, rg_ref, ys_ref, y_ref, r0_scr, r1_scr, sem, *, tm):
    def body(g, carry):
        for u in range(DMA_UNROLL):
            t = g * DMA_UNROLL + u
            _row_copy(ys_ref, dest_ref[0, t], r0_scr, t, sem).start(priority=0)
            _row_copy(ys_ref, dest_ref[1, t], r1_scr, t, sem).start(priority=1)
        return carry

    lax.fori_loop(0, tm // DMA_UNROLL, body, 0)
    gt =jnp.concatenate([rg_ref[...], jnp.zeros((LANES - 8, tm), F32)], axis=0).T
    pltpu.make_async_copy(ys_ref.at[pl.ds(0, tm)], r0_scr, sem).wait()
    pltpu.make_async_copy(ys_ref.at[pl.ds(0, tm)], r1_scr, sem).wait()
    y_ref[...] = x1_ref[...] + gt[:, 0:1] * r0_scr[...] + gt[:, 1:2] * r1_scr[...]


def _combine(dest, x1, rg, ys, tm):
    t = x1.shape[0]
    row = lambda i: (i, 0)
    col = lambda i: (0, i)
    return pl.pallas_call(
        functools.partial(_combine_kernel, tm=tm),
        grid=(t // tm,),
        in_specs=[
            pl.BlockSpec((8, tm), col, memory_space=pltpu.SMEM),
            pl.BlockSpec((tm, D_MODEL), row),
            pl.BlockSpec((8, tm), col),
            pl.BlockSpec(memory_space=pl.ANY),
        ],
        out_specs=pl.BlockSpec((tm, D_MODEL), row),
        out_shape=jax.ShapeDtypeStruct((t, D_MODEL), F32),
        scratch_shapes=[
            pltpu.VMEM((tm, D_MODEL), F32),
            pltpu.VMEM((tm, D_MODEL), F32),
            pltpu.SemaphoreType.DMA,
        ],
        compiler_params=_cparams(("arbitrary",)),
        name="combine",
    )(dest, x1, rg, ys)


def _prep_weights(g_attn, w_in, g_cq, w_uq, g_ckv, w_ukv, g_q, g_k, b_igate, b_fgate, g_mh,
                  w_out, g_ffn, w_group, b_group, w_erouter, b_erouter, w1, w3, w2):
    def cols(a, b):
        return w_in[:, a:b]

    o_cq, o_ckv, o_kr = 0, 256, 384
    o_mq, o_mk, o_mv, o_mi, o_mf, o_mo = 416, 672, 928, 1440, 1444, 1448
    hr = MLA_ROPE // 2
    misc = jnp.concatenate([
        cols(o_kr, o_kr + MLA_ROPE), -cols(o_kr + hr, o_kr + MLA_ROPE), cols(o_kr, o_kr + hr),
        cols(o_mi, o_mi + M_HEADS), cols(o_mf, o_mf + M_HEADS),
        jnp.zeros((D_MODEL, LANES - 2 * MLA_ROPE - 2 * M_HEADS), F32)], axis=1)
    w_p = jnp.concatenate([
        cols(o_cq, o_cq + 256), cols(o_ckv, o_ckv + 128), cols(o_mq, o_mq + 256), cols(o_mk, o_mk + 256),
        cols(o_mv, o_mv + 512), cols(o_mo, o_mo + 512), misc], axis=1).astype(BF16)
    nope, r1, r2 = w_uq[..., :MLA_NOPE], w_uq[..., MLA_NOPE:MLA_NOPE + hr], w_uq[..., MLA_NOPE + hr:]
    w_q = jnp.concatenate([nope, r1, r2, -r2, r1], axis=-1).reshape(MLA_Q_LORA, MLA_HEADS * HEAD_PAD).astype(BF16)
    pad_g = jnp.zeros((HEAD_PAD - MLA_QK,), F32)
    w_k = jnp.concatenate([w_ukv[..., :MLA_NOPE], jnp.zeros((MLA_KV_LORA, MLA_HEADS, HEAD_PAD - MLA_NOPE), F32)],
                          axis=-1).transpose(1, 0, 2).astype(BF16)
    w_v = jnp.concatenate([w_ukv[..., MLA_NOPE:], jnp.zeros((MLA_KV_LORA, MLA_HEADS, LANES - MLA_V), F32)],
                          axis=-1).transpose(1, 0, 2).astype(BF16)
    gate_bias = jnp.concatenate([jnp.zeros((_L_IG,), F32), b_igate, b_fgate,
                                 jnp.zeros((LANES - _L_FG - M_HEADS,), F32)])[None]
    w_r = jnp.concatenate([w_erouter.T, w_group.T, jnp.zeros((R_ROWS - N_EXPERTS - N_GROUPS, D_MODEL), F32)],
                          axis=0).astype(BF16)
    b_r = jnp.concatenate([b_erouter, b_group, jnp.zeros((R_ROWS - N_EXPERTS - N_GROUPS,), F32)])
    return {
        "w_in": w_p, "g_attn": g_attn[None], "g_cq": g_cq[None], "w_q": w_q, "g_ckv": g_ckv[None],
        "g_q": jnp.concatenate([g_q, pad_g])[None], "g_k": jnp.concatenate([g_k, pad_g])[None],
        "gate_bias": gate_bias, "w_k": w_k, "w_v": w_v, "g_mh": g_mh,
        "w_out": w_out.astype(BF16), "g_ffn": g_ffn[None], "w_r": w_r,
        "b_r": jnp.broadcast_to(b_r[:, None], (R_ROWS, LANES)),
        "w1": w1.astype(BF16), "w3": w3.astype(BF16), "w2": w2.astype(BF16),
    }


def _rope_table(pos):
    half = MLA_ROPE // 2
    inv = ROPE_BASE ** (-np.arange(half, dtype=np.float64) / half)
    ang = np.asarray(pos, np.float64)[:, None] * inv[None, :]
    cos = np.cos(ang)
    sin = np.sin(ang)
    c2 = np.concatenate([cos, cos], axis=1)
    s2 = np.concatenate([sin, sin], axis=1)
    return jnp.asarray(np.concatenate([c2, s2, c2, s2], axis=1), F32)


def _pick(n, pref):
    return pref if n % pref == 0 else n


def _moe_layer(x2d, att, hm, pw, *, tm_route, bm, tm_disp, tm_comb):
    t = x2d.shape[0]
    x1, hp, ri, rg, cnt = _route(x2d, att, hm, pw, tm_route)
    counts = cnt[:, 0]
    padded = (counts + bm - 1) // bm * bm
    pend = jnp.cumsum(padded)
    pstart = pend - padded
    ids = ri[0:2]
    sel = ids[:, None, :] == jnp.arange(N_EXPERTS, dtype=I32)[None, :, None]
    dest2 = jnp.sum(jnp.where(sel, pstart[None, :, None], 0), axis=1) + ri[2:4]
    dest = jnp.concatenate([dest2, jnp.zeros((6, t), I32)], axis=0)
    n_slots = (2 * t // bm + N_EXPERTS) * bm
    nblk = n_slots // bm
    n_used = (pend[-1] // bm).astype(I32)
    blk_e = jnp.searchsorted(pend, jnp.arange(nblk, dtype=I32) * bm, side="right").astype(I32)
    blk_last = jnp.max(jnp.where(counts > 0, jnp.arange(N_EXPERTS, dtype=I32), 0))
    blk_e = jnp.minimum(blk_e, blk_last)
    xs = _dispatch(dest, hp, n_slots, tm_disp)
    ys = _experts(blk_e, n_used[None], xs, pw, bm)
    return _combine(dest, x1, rg, ys, tm_comb)


def _state_pack(c, n):
    ct = jnp.swapaxes(c, -1, -2)
    return jnp.concatenate([ct, n[..., None], jnp.zeros(ct.shape[:-1] + (S_W - M_DV - 1,), F32)], axis=-1)


def _state_unpack(s, m):
    return jnp.swapaxes(s[..., :M_DV], -1, -2), s[..., M_DV], m[:, :, 0, 0]


def kernel(x_prompt, x_sample, cache_ckv, cache_krope, state_mlstm_c, state_mlstm_n, state_mlstm_m,
           meta_tokens, g_attn, w_in, g_cq, w_uq, g_ckv, w_ukv, g_q, g_k, b_igate, b_fgate, g_mh,
           w_out, g_ffn, w_group, b_group, w_erouter, b_erouter, w1, w3, w2):
    bp, seq = x_prompt.shape[:2]
    bs, dec = x_sample.shape[:2]
    past = cache_ckv.shape[2]
    layer = 0
    pw = _prep_weights(g_attn[layer], w_in[layer], g_cq[layer], w_uq[layer], g_ckv[layer], w_ukv[layer],
                       g_q[layer], g_k[layer], b_igate[layer], b_fgate[layer], g_mh[layer], w_out[layer],
                       g_ffn[layer], w_group[layer], b_group[layer], w_erouter[layer], b_erouter[layer],
                       w1[layer], w3[layer], w2[layer])

    xm = jnp.concatenate([meta_tokens, jnp.zeros((META_PAD - N_META, D_MODEL), F32)], axis=0)
    tab_m = _rope_table(np.arange(META_PAD) - N_META)
    _, ckv_m, misc_m, mq_m, mk_m, mv_m, og_m = _project(xm, tab_m, pw, META_PAD)
    zero_s = jnp.zeros((1, M_HEADS, M_DK, S_W), F32)
    zero_m = jnp.zeros((1, M_HEADS, 8, LANES), F32)
    _, s_meta, m_meta = _mlstm(mq_m, mk_m, mv_m, og_m, misc_m, zero_s, zero_m, pw["g_mh"],
                               nb=1, l=META_PAD, lc=META_PAD, n_valid=N_META)

    tp = bp * seq
    xp2 = x_prompt.reshape(tp, D_MODEL)
    tm_p = _pick(seq, 512)
    q_p, ckv_p, misc_p, mq_p, mk_p, mv_p, og_p = _project(xp2, _rope_table(np.arange(seq)), pw, tm_p)
    tq = _pick(seq, 256)
    att_p = _attention(q_p, ckv_p, misc_p, ckv_m[None], misc_m[None], pw, nb=bp, lq=seq, lf=seq, tq=tq, tk=tq,
                       causal=True, n_tail=N_META)
    hm_p, s_p, m_p = _mlstm(mq_p, mk_p, mv_p, og_p, misc_p, s_meta, m_meta, pw["g_mh"],
                            nb=bp, l=seq, lc=_pick(seq, 128))
    y_p = _moe_layer(xp2, att_p, hm_p, pw, tm_route=tm_p, bm=512, tm_disp=tm_p, tm_comb=_pick(seq, 256))

    ts = bs * dec
    xs2 = x_sample.reshape(ts, D_MODEL)
    q_s, ckv_s, misc_s, mq_s, mk_s, mv_s, og_s = _project(xs2, _rope_table(past + np.arange(dec)), pw, dec)
    kr_cache = jnp.concatenate([cache_krope[layer], jnp.zeros((bs, past, LANES - MLA_ROPE), F32)], axis=-1)
    n_tail = dec + N_META
    assert n_tail <= META_PAD

    def tail_rows(own, meta):
        w = own.shape[-1]
        return jnp.concatenate([own.reshape(bs, dec, w), jnp.broadcast_to(meta[None, :N_META], (bs, N_META, w)),
                                jnp.zeros((bs, META_PAD - n_tail, w), F32)], axis=1)

    att_s = _attention(q_s, cache_ckv[layer].reshape(bs * past, MLA_KV_LORA), kr_cache.reshape(bs * past, LANES),
                       tail_rows(ckv_s, ckv_m), tail_rows(misc_s, misc_m), pw, nb=bs, lq=dec, lf=past, tq=dec,
                       tk=_pick(past, 256), causal=False, n_tail=n_tail)
    s0 = _state_pack(state_mlstm_c[layer], state_mlstm_n[layer])
    m0 = jnp.broadcast_to(state_mlstm_m[layer][:, :, None, None], (bs, M_HEADS, 8, LANES))
    hm_s, s_s, m_s = _mlstm(mq_s, mk_s, mv_s, og_s, misc_s, s0, m0, pw["g_mh"], nb=bs, l=dec, lc=dec)
    y_s = _moe_layer(xs2, att_s, hm_s, pw, tm_route=_pick(ts, 512), bm=128, tm_disp=_pick(ts, 512),
                     tm_comb=_pick(ts, 256))

    m_ckv = ckv_m[:N_META]
    m_kr = misc_m[:N_META, :MLA_ROPE]
    new_ckv_p = jnp.concatenate([jnp.broadcast_to(m_ckv[None], (bp, N_META, MLA_KV_LORA)),
                                 ckv_p.reshape(bp, seq, MLA_KV_LORA)], axis=1)[None]
    new_kr_p = jnp.concatenate([jnp.broadcast_to(m_kr[None], (bp, N_META, MLA_ROPE)),
                                misc_p[:, :MLA_ROPE].reshape(bp, seq, MLA_ROPE)], axis=1)[None]
    c_p, n_p, mm_p = _state_unpack(s_p, m_p)
    c_s, n_s, mm_s = _state_unpack(s_s, m_s)
    return (y_p.reshape(bp, seq, D_MODEL), y_s.reshape(bs, dec, D_MODEL),
            new_ckv_p, new_kr_p, c_p[None], n_p[None], mm_p[None],
            ckv_s.reshape(bs, dec, MLA_KV_LORA)[None], misc_s[:, :MLA_ROPE].reshape(bs, dec, MLA_ROPE)[None],
            c_s[None], n_s[None], mm_s[None])
```

```python
import functools

import numpy as np
import jax
import jax.numpy as jnp
from jax import lax
from jax.experimental import pallas as pl
from jax.experimental.pallas import tpu as pltpu

F32 = jnp.float32
BF16 = jnp.bfloat16
I32 = jnp.int32
U32 = jnp.uint32

D_MODEL = 1024
CHUNK = 64
N_META = 16
MLA_HEADS = 8
MLA_V = 64
MLA_NOPE = 64
MLA_ROPE = 32
MLA_QK = MLA_NOPE + MLA_ROPE
MLA_Q_LORA = 256
MLA_KV_LORA = 128
MLA_SCALE = MLA_QK ** -0.5
ROPE_BASE = 10000.0
M_HEADS = 4
M_DV = 128
M_DK = 64
N_GROUPS = 4
E_PER_GROUP = 8
N_EXPERTS = 32
EXPERT_FF = 512
EPS = 1e-6

LANES = 128
HEAD_PAD = 128
PROJ_PAD = 2048
META_PAD = 128
NEG_BIG = -1e30
VMEM_LIMIT = 56 * 1024 * 1024

_O_CQ, _O_CKV, _O_MQ, _O_MK, _O_MV, _O_MO, _O_MISC = 0, 256, 384, 640, 896, 1408, 1920
_L_IG, _L_FG = 64, 68


def _cparams(sem):
    return pltpu.CompilerParams(dimension_semantics=sem, vmem_limit_bytes=VMEM_LIMIT)


def _dot(a, b):
    return jnp.dot(a, b, preferred_element_type=F32)


def _dot_nt(a, b):
    return lax.dot_general(a, b, (((1,), (1,)), ((), ())), preferred_element_type=F32)


def _split3(x):
    x1 = x.astype(BF16)
    r1 = x - x1.astype(F32)
    x2 = r1.astype(BF16)
    x3 = (r1 - x2.astype(F32)).astype(BF16)
    return x1, x2, x3


def _proj_kernel(x_ref, tab_ref, w_ref, g_ref, gcq_ref, wq_ref, gckv_ref, gq_ref, bias_ref,
                 q_ref, ckv_ref, misc_ref, mq_ref, mk_ref, mv_ref, og_ref):
    x = x_ref[...]
    xn = x * lax.rsqrt(jnp.mean(x * x, axis=-1, keepdims=True) + EPS) * g_ref[...]
    z = _dot(xn.astype(BF16), w_ref[...])
    tab = tab_ref[...]
    lane = lax.broadcasted_iota(I32, tab.shape, 1)

    cq = z[:, _O_CQ:_O_CQ + MLA_Q_LORA]
    cqn = cq * lax.rsqrt(jnp.mean(cq * cq, axis=-1, keepdims=True) + EPS) * gcq_ref[...]
    qz = _dot(cqn.astype(BF16), wq_ref[...])
    gq = gq_ref[...]
    for h in range(MLA_HEADS):
        zh = qz[:, h * HEAD_PAD:(h + 1) * HEAD_PAD]
        y = zh * tab
        rot = y + pltpu.roll(y, LANES - MLA_ROPE, 1)
        qh = jnp.where(lane < MLA_NOPE, zh, jnp.where(lane < MLA_QK, rot, 0.0))
        ms = jnp.sum(qh * qh, axis=-1, keepdims=True) * (1.0 / MLA_QK)
        q_ref[h] = (qh * lax.rsqrt(ms + EPS) * gq).astype(BF16)

    ckv = z[:, _O_CKV:_O_CKV + MLA_KV_LORA]
    ckv_ref[...] = ckv * lax.rsqrt(jnp.mean(ckv * ckv, axis=-1, keepdims=True) + EPS) * gckv_ref[...]

    zm = z[:, _O_MISC:_O_MISC + LANES]
    y = zm * tab
    rot = y + pltpu.roll(y, LANES - MLA_ROPE, 1)
    gate = zm + bias_ref[...]
    logf = jnp.minimum(gate, 0.0) - jnp.log1p(jnp.exp(-jnp.abs(gate)))
    misc = jnp.where(lane < MLA_ROPE, rot,
                     jnp.where((lane >= _L_IG) & (lane < _L_FG), gate,
                               jnp.where((lane >= _L_FG) & (lane < _L_FG + M_HEADS), logf, 0.0)))
    misc_ref[...] = misc

    mq_ref[...] = z[:, _O_MQ:_O_MQ + M_HEADS * M_DK].astype(BF16)
    mk_ref[...] = (z[:, _O_MK:_O_MK + M_HEADS * M_DK] * (M_DK ** -0.5)).astype(BF16)
    mv_ref[...] = z[:, _O_MV:_O_MV + M_HEADS * M_DV].astype(BF16)
    og_ref[...] = jax.nn.sigmoid(z[:, _O_MO:_O_MO + M_HEADS * M_DV]).astype(BF16)


def _project(x2d, tab, pw, tm):
    t = x2d.shape[0]
    nt = t // tm
    ntab = tab.shape[0] // tm
    row = lambda i: (i, 0)
    full = lambda i: (0, 0)
    return pl.pallas_call(
        _proj_kernel,
        grid=(nt,),
        in_specs=[
            pl.BlockSpec((tm, D_MODEL), row),
            pl.BlockSpec((tm, LANES), lambda i: (i % ntab, 0)),
            pl.BlockSpec((D_MODEL, PROJ_PAD), full),
            pl.BlockSpec((1, D_MODEL), full),
            pl.BlockSpec((1, MLA_Q_LORA), full),
            pl.BlockSpec((MLA_Q_LORA, MLA_HEADS * HEAD_PAD), full),
            pl.BlockSpec((1, MLA_KV_LORA), full),
            pl.BlockSpec((1, HEAD_PAD), full),
            pl.BlockSpec((1, LANES), full),
        ],
        out_specs=[
            pl.BlockSpec((MLA_HEADS, tm, HEAD_PAD), lambda i: (0, i, 0)),
            pl.BlockSpec((tm, MLA_KV_LORA), row),
            pl.BlockSpec((tm, LANES), row),
            pl.BlockSpec((tm, M_HEADS * M_DK), row),
            pl.BlockSpec((tm, M_HEADS * M_DK), row),
            pl.BlockSpec((tm, M_HEADS * M_DV), row),
            pl.BlockSpec((tm, M_HEADS * M_DV), row),
        ],
        out_shape=[
            jax.ShapeDtypeStruct((MLA_HEADS, t, HEAD_PAD), BF16),
            jax.ShapeDtypeStruct((t, MLA_KV_LORA), F32),
            jax.ShapeDtypeStruct((t, LANES), F32),
            jax.ShapeDtypeStruct((t, M_HEADS * M_DK), BF16),
            jax.ShapeDtypeStruct((t, M_HEADS * M_DK), BF16),
            jax.ShapeDtypeStruct((t, M_HEADS * M_DV), BF16),
            jax.ShapeDtypeStruct((t, M_HEADS * M_DV), BF16),
        ],
        compiler_params=_cparams(("arbitrary",)),
        name="projection",
    )(x2d, tab, pw["w_in"], pw["g_attn"], pw["g_cq"], pw["w_q"], pw["g_ckv"], pw["g_q"], pw["gate_bias"])


SAFE_BOUND = 40.0
LOG2E = 1.4426950408889634


VT_ROWS = 80


def _attn_kernel(q_ref, ckv_ref, misc_ref, ckvt_ref, misct_ref, wk_ref, wv_ref, gk_ref, o_ref,
                 k_scr, vt_scr, kmax_scr, qt_scr, bnd_scr, acc_scr, ot_scr, p_scr, *, lf, tq, tk, bt, causal,
                 n_tail):
    i = pl.program_id(1)
    gk = gk_ref[...]

    def build(ckv_rows, misc_rows, dst):
        n = ckv_rows.shape[0]
        lane = lax.broadcasted_iota(I32, (n, LANES), 1)
        cb = ckv_rows.astype(BF16)
        krp = jnp.where((lane >= MLA_NOPE) & (lane < MLA_QK), pltpu.roll(misc_rows, MLA_NOPE, 1), 0.0)
        onecol = (lane == MLA_V).astype(F32)
        for h in range(MLA_HEADS):
            kk = _dot(cb, wk_ref[h]) + krp
            ms = jnp.sum(kk * kk, axis=-1, keepdims=True) * (1.0 / MLA_QK)
            kb = (kk * lax.rsqrt(ms + EPS) * gk).astype(BF16)
            kf = kb.astype(F32)
            kn = jnp.max(jnp.sum(kf * kf, axis=-1, keepdims=True), axis=0, keepdims=True)
            kmax_scr[h] = jnp.maximum(kmax_scr[h], jnp.broadcast_to(kn, (8, LANES)))
            k_scr[h, pl.ds(dst, n), :] = kb
            vext_t = (_dot(cb, wv_ref[h]) + onecol).T
            vt_scr[h, :, pl.ds(dst, n)] = vext_t[0:VT_ROWS, :].astype(BF16)

    @pl.when(i == 0)
    def _():
        kmax_scr[...] = jnp.zeros_like(kmax_scr)

        def body(r, carry):
            r0 = pl.multiple_of(r * bt, bt)
            build(ckv_ref[pl.ds(r0, bt), :], misc_ref[pl.ds(r0, bt), :], r0)
            return carry

        lax.fori_loop(0, lf // bt, body, 0)
        build(ckvt_ref[0], misct_ref[0], lf)

    tail_mask = lax.broadcasted_iota(I32, (META_PAD, tq), 0) < n_tail
    if causal:
        nfull = i * (tq // tk)
        key_c = lax.broadcasted_iota(I32, (tk, tq), 0)
        qry_c = lax.broadcasted_iota(I32, (tk, tq), 1) // CHUNK
        diag_masks = [(key_c + d * tk) // CHUNK <= qry_c for d in range(tq // tk)]
    else:
        nfull = lf // tk
        diag_masks = []

    bmax = jnp.zeros((1, 1), F32)
    for h in range(MLA_HEADS):
        qt = q_ref[h].astype(F32).T
        qt_scr[h] = qt.astype(BF16)
        qn = jnp.sqrt(jnp.sum(qt * qt, axis=0, keepdims=True))
        bnd = qn * (jnp.sqrt(kmax_scr[h, 0:1, 0:1]) * MLA_SCALE)
        bnd_scr[h] = jnp.broadcast_to(bnd * LOG2E, (8, tq))
        bmax = jnp.maximum(bmax, jnp.max(bnd, axis=1, keepdims=True))
    safe = bmax[0, 0] <= SAFE_BOUND

    def kv_tail(h):
        return k_scr[h, lf:lf + META_PAD, :], vt_scr[h, :, lf:lf + META_PAD]

    def kv_tile(h, r0):
        return k_scr[h, pl.ds(r0, tk), :], vt_scr[h, :, pl.ds(r0, tk)]

    def finish(h, acc):
        ot_scr[h * MLA_V:(h + 1) * MLA_V, :] = acc[:MLA_V, :] / acc[MLA_V:MLA_V + 1, :]

    @pl.when(safe)
    def _():
        def scores(h, krows):
            return _dot(krows, qt_scr[h]) * (MLA_SCALE * LOG2E) - bnd_scr[h, 0:1, :]

        def probs(s, mask):
            if mask is not None:
                s = jnp.where(mask, s, -jnp.inf)
            return jnp.exp2(s).astype(BF16)

        def k_tile(h, j):
            return k_scr[h, pl.ds(pl.multiple_of(j * tk, tk), tk), :]

        def vt_tile(h, j):
            return vt_scr[h, :, pl.ds(pl.multiple_of(j * tk, tk), tk)]

        def pipe_step(j, mask):
            ss = [scores(h, k_tile(h, j)) for h in range(MLA_HEADS)]
            for h in range(MLA_HEADS):
                acc_scr[h] += _dot(vt_tile(h, j - 1), p_scr[(j - 1) % 2, h])
            for h in range(MLA_HEADS):
                p_scr[j % 2, h] = probs(ss[h], mask)

        for h in range(MLA_HEADS):
            krows, vt = kv_tail(h)
            acc_scr[h] = _dot(vt, probs(scores(h, krows), tail_mask))
        if causal:
            lift = jnp.where(nfull > 0, tk, 0)
            first_mask = key_c // CHUNK <= qry_c + lift
        else:
            first_mask = None
        for h in range(MLA_HEADS):
            p_scr[0, h] = probs(scores(h, k_tile(h, 0)), first_mask)

        def body(j, carry):
            pipe_step(j, None)
            return carry

        lax.fori_loop(1, nfull, body, 0)
        if causal:
            @pl.when(nfull >= 1)
            def _():
                pipe_step(nfull, diag_masks[0])
        last = nfull if causal else nfull - 1
        for h in range(MLA_HEADS):
            finish(h, acc_scr[h] + _dot(vt_tile(h, last), p_scr[last % 2, h]))

    @pl.when(jnp.logical_not(safe))
    def _():
        for h in range(MLA_HEADS):
            qth = qt_scr[h]

            def step(carry, kv, mask):
                m, acc = carry
                krows, vt = kv
                s = _dot(krows, qth) * MLA_SCALE
                if mask is not None:
                    s = jnp.where(mask, s, -jnp.inf)
                m_new = jnp.maximum(m, jnp.max(s, axis=0, keepdims=True))
                acc = jnp.exp(m - m_new) * acc + _dot(vt, jnp.exp(s - m_new).astype(BF16))
                return m_new, acc

            carry = (jnp.full((1, tq), -jnp.inf, F32), jnp.zeros((VT_ROWS, tq), F32))
            carry = step(carry, kv_tail(h), tail_mask)
            carry = lax.fori_loop(
                0, nfull, lambda j, c: step(c, kv_tile(h, pl.multiple_of(j * tk, tk)), None), carry)
            for d, mask in enumerate(diag_masks):
                carry = step(carry, kv_tile(h, pl.multiple_of(i * tq + d * tk, tk)), mask)
            finish(h, carry[1])

    o_ref[...] = ot_scr[...].T.astype(BF16)


def _attention(q, ckv_f, misc_f, ckv_t, misc_t, pw, *, nb, lq, lf, tq, tk, causal, n_tail):
    nq = lq // tq
    bt = min(512, lf)
    assert lf % bt == 0 and lf % tk == 0 and lq % tq == 0 and (not causal or tq == tk)
    kern = functools.partial(_attn_kernel, lf=lf, tq=tq, tk=tk, bt=bt, causal=causal, n_tail=n_tail)
    full2 = lambda b, i: (0, 0)
    full3 = lambda b, i: (0, 0, 0)
    tail = full3 if ckv_t.shape[0] == 1 else (lambda b, i: (b, 0, 0))
    lk = lf + META_PAD
    return pl.pallas_call(
        kern,
        grid=(nb, nq),
        in_specs=[
            pl.BlockSpec((MLA_HEADS, tq, HEAD_PAD), lambda b, i: (0, b * nq + i, 0)),
            pl.BlockSpec((lf, MLA_KV_LORA), lambda b, i: (b, 0)),
            pl.BlockSpec((lf, LANES), lambda b, i: (b, 0)),
            pl.BlockSpec((1, META_PAD, MLA_KV_LORA), tail),
            pl.BlockSpec((1, META_PAD, LANES), tail),
            pl.BlockSpec((MLA_HEADS, MLA_KV_LORA, HEAD_PAD), full3),
            pl.BlockSpec((MLA_HEADS, MLA_KV_LORA, LANES), full3),
            pl.BlockSpec((1, HEAD_PAD), full2),
        ],
        out_specs=pl.BlockSpec((tq, MLA_HEADS * MLA_V), lambda b, i: (b * nq + i, 0)),
        out_shape=jax.ShapeDtypeStruct((nb * lq, MLA_HEADS * MLA_V), BF16),
        scratch_shapes=[
            pltpu.VMEM((MLA_HEADS, lk, HEAD_PAD), BF16),
            pltpu.VMEM((MLA_HEADS, VT_ROWS, lk), BF16),
            pltpu.VMEM((MLA_HEADS, 8, LANES), F32),
            pltpu.VMEM((MLA_HEADS, HEAD_PAD, tq), BF16),
            pltpu.VMEM((MLA_HEADS, 8, tq), F32),
            pltpu.VMEM((MLA_HEADS, VT_ROWS, tq), F32),
            pltpu.VMEM((MLA_HEADS * MLA_V, tq), F32),
            pltpu.VMEM((2, MLA_HEADS, tk, tq), BF16),
        ],
        compiler_params=_cparams(("arbitrary", "arbitrary")),
        name="attention",
    )(q, ckv_f, misc_f, ckv_t, misc_t, pw["w_k"], pw["w_v"], pw["g_k"])


S_W = 2 * M_DV


def _mlstm_kernel(q_ref, k_ref, v_ref, og_ref, misc_ref, s0_ref, m0_ref, gmh_ref,
                  h_ref, sout_ref, mout_ref, s_scr, m_scr, *, lc, n_valid):
    c = pl.program_id(1)
    nc = pl.num_programs(1)

    @pl.when(c == 0)
    def _():
        s_scr[...] = s0_ref[0]
        m_scr[...] = m0_ref[0]

    g = misc_ref[...]
    if n_valid is not None:
        row = lax.broadcasted_iota(I32, g.shape, 0) + c * lc
        lane = lax.broadcasted_iota(I32, g.shape, 1)
        pad_val = jnp.where((lane >= _L_IG) & (lane < _L_FG), NEG_BIG, 0.0)
        g = jnp.where(row < n_valid, g, pad_val)
    gt = g.T
    gt8 = gt[_L_IG:_L_IG + 8, :]
    rr = lax.broadcasted_iota(I32, (lc, lc), 0)
    cc = lax.broadcasted_iota(I32, (lc, lc), 1)
    causal = cc <= rr
    tri = causal.astype(BF16)
    trit = (rr <= cc).astype(BF16)
    g1, g2, g3 = _split3(g)
    bcol_all = _dot(tri, g1) + _dot(tri, g2) + _dot(tri, g3)
    t1, t2, t3 = _split3(gt8)
    brow_all = _dot(t1, trit) + _dot(t2, trit) + _dot(t3, trit)

    kt = k_ref[...].astype(F32).T
    onecol = (lax.broadcasted_iota(I32, (lc, M_DV), 1) == 0).astype(BF16)

    for h in range(M_HEADS):
        bcol = bcol_all[:, _L_FG + h:_L_FG + h + 1]
        brow = brow_all[M_HEADS + h:M_HEADS + h + 1, :]
        igrow = gt8[h:h + 1, :]
        m_prev = m_scr[h, 0:1, 0:1]
        qh = q_ref[:, h * M_DK:(h + 1) * M_DK]
        kh = k_ref[:, h * M_DK:(h + 1) * M_DK]
        vext = jnp.concatenate([v_ref[:, h * M_DV:(h + 1) * M_DV], onecol], axis=1)

        log_d = jnp.where(causal, bcol - brow + igrow, -jnp.inf)
        log_inter = bcol + m_prev
        m_t = jnp.maximum(log_inter, jnp.max(log_d, axis=-1, keepdims=True))
        w = jnp.exp(log_d - m_t) * _dot_nt(qh, kh)
        inter = jnp.exp(log_inter - m_t)
        s_h = s_scr[h]
        r = _dot(w.astype(BF16), vext) + inter * _dot(qh, s_h.astype(BF16))
        num = r[:, :M_DV]
        den = r[:, M_DV:M_DV + 1]
        hh = num / jnp.maximum(jnp.abs(den), jnp.exp(-m_t))
        hn = hh * lax.rsqrt(jnp.mean(hh * hh, axis=-1, keepdims=True) + EPS) * gmh_ref[h:h + 1, :]
        h_ref[:, h * M_DV:(h + 1) * M_DV] = (hn * og_ref[:, h * M_DV:(h + 1) * M_DV].astype(F32)).astype(BF16)

        blast = bcol[lc - 1:lc, :]
        logw = blast - brow + igrow
        m_new = jnp.maximum(blast + m_prev, jnp.max(logw, axis=-1, keepdims=True))
        wrow = jnp.exp(logw - m_new)
        decay = jnp.exp(blast + m_prev - m_new)
        ktw = (kt[h * M_DK:(h + 1) * M_DK, :] * wrow).astype(BF16)
        s_scr[h] = decay * s_h + _dot(ktw, vext)
        m_scr[h] = jnp.broadcast_to(m_new, (8, LANES))

    @pl.when(c == nc - 1)
    def _():
        sout_ref[0] = s_scr[...]
        mout_ref[0] = m_scr[...]


def _mlstm(mq, mk, mv, og, misc, s0, m0, gmh, *, nb, l, lc, n_valid=None):
    nc = l // lc
    shared = s0.shape[0] == 1
    st = (lambda b, c: (0, 0, 0, 0)) if shared else (lambda b, c: (b, 0, 0, 0))
    row = lambda b, c: (b * nc + c, 0)
    kern = functools.partial(_mlstm_kernel, lc=lc, n_valid=n_valid)
    return pl.pallas_call(
        kern,
        grid=(nb, nc),
        in_specs=[
            pl.BlockSpec((lc, M_HEADS * M_DK), row),
            pl.BlockSpec((lc, M_HEADS * M_DK), row),
            pl.BlockSpec((lc, M_HEADS * M_DV), row),
            pl.BlockSpec((lc, M_HEADS * M_DV), row),
            pl.BlockSpec((lc, LANES), row),
            pl.BlockSpec((1, M_HEADS, M_DK, S_W), st),
            pl.BlockSpec((1, M_HEADS, 8, LANES), st),
            pl.BlockSpec((M_HEADS, M_DV), lambda b, c: (0, 0)),
        ],
        out_specs=[
            pl.BlockSpec((lc, M_HEADS * M_DV), row),
            pl.BlockSpec((1, M_HEADS, M_DK, S_W), lambda b, c: (b, 0, 0, 0)),
            pl.BlockSpec((1, M_HEADS, 8, LANES), lambda b, c: (b, 0, 0, 0)),
        ],
        out_shape=[
            jax.ShapeDtypeStruct((nb * l, M_HEADS * M_DV), BF16),
            jax.ShapeDtypeStruct((nb, M_HEADS, M_DK, S_W), F32),
            jax.ShapeDtypeStruct((nb, M_HEADS, 8, LANES), F32),
        ],
        scratch_shapes=[
            pltpu.VMEM((M_HEADS, M_DK, S_W), F32),
            pltpu.VMEM((M_HEADS, 8, LANES), F32),
        ],
        compiler_params=_cparams(("arbitrary", "arbitrary")),
        name="mlstm",
    )(mq, mk, mv, og, misc, s0, m0, gmh)


R_ROWS = 40


def _route_kernel(x_ref, att_ref, hm_ref, wo_ref, g_ref, wr_ref, br_ref,
                  x1_ref, hp_ref, ri_ref, rg_ref, cnt_ref, cnt_scr):
    i = pl.program_id(0)
    tm = x_ref.shape[0]

    @pl.when(i == 0)
    def _():
        cnt_scr[...] = jnp.zeros_like(cnt_scr)

    mix = jnp.concatenate([att_ref[...], hm_ref[...]], axis=1)
    x1 = x_ref[...] + _dot(mix, wo_ref[...])
    x1_ref[...] = x1
    hn = x1 * lax.rsqrt(jnp.mean(x1 * x1, axis=-1, keepdims=True) + EPS) * g_ref[...]
    hb = hn.astype(BF16)
    half = D_MODEL // 2
    hi = lax.bitcast_convert_type(hb[:, :half].astype(F32), U32)
    lo = lax.bitcast_convert_type(hb[:, half:].astype(F32), U32)
    hp_ref[...] = (hi & jnp.uint32(0xFFFF0000)) | (lo >> 16)

    logits = _dot_nt(wr_ref[...], hb) + br_ref[:, 0:1]
    e_log = logits[0:N_EXPERTS, :]
    g_log = logits[N_EXPERTS:N_EXPERTS + N_GROUPS, :]
    gmax = jnp.max(g_log, axis=0, keepdims=True)
    gsum = jnp.sum(jnp.exp(g_log - gmax), axis=0, keepdims=True)
    gi = lax.broadcasted_iota(I32, g_log.shape, 0)
    g_idx = jnp.min(jnp.where(g_log == gmax, gi, N_GROUPS), axis=0, keepdims=True)
    e_sel = jnp.zeros((E_PER_GROUP, tm), F32)
    for gg in range(N_GROUPS):
        e_sel = jnp.where(g_idx == gg, e_log[gg * E_PER_GROUP:(gg + 1) * E_PER_GROUP, :], e_sel)
    ei = lax.broadcasted_iota(I32, e_sel.shape, 0)
    m1 = jnp.max(e_sel, axis=0, keepdims=True)
    i1 = jnp.min(jnp.where(e_sel == m1, ei, E_PER_GROUP), axis=0, keepdims=True)
    e2 = jnp.where(ei == i1, -jnp.inf, e_sel)
    m2 = jnp.max(e2, axis=0, keepdims=True)
    i2 = jnp.min(jnp.where(e2 == m2, ei, E_PER_GROUP), axis=0, keepdims=True)
    ex = jnp.exp(m2 - m1)
    gp = 1.0 / gsum
    p1 = 1.0 / (1.0 + ex)
    gate1 = gp * p1
    gate2 = gp * (ex * p1)
    id1 = g_idx * E_PER_GROUP + i1
    id2 = g_idx * E_PER_GROUP + i2

    xi = lax.broadcasted_iota(I32, (N_EXPERTS, tm), 0)
    oh1 = xi == id1
    oh2 = xi == id2
    e_cnt = (oh1 | oh2).astype(F32)
    rr = lax.broadcasted_iota(I32, (tm, tm), 0)
    cc = lax.broadcasted_iota(I32, (tm, tm), 1)
    upper = (rr < cc).astype(BF16)
    pref = _dot(e_cnt.astype(BF16), upper) + cnt_scr[:, 0:1]
    rank1 = jnp.sum(jnp.where(oh1, pref, 0.0), axis=0, keepdims=True)
    rank2 = jnp.sum(jnp.where(oh2, pref, 0.0), axis=0, keepdims=True)
    cnt_new = cnt_scr[...] + jnp.sum(e_cnt, axis=1, keepdims=True)
    cnt_scr[...] = cnt_new
    cnt_ref[...] = cnt_new.astype(I32)

    zi = jnp.zeros((1, tm), I32)
    ri_ref[...] = jnp.concatenate([id1, id2, rank1.astype(I32), rank2.astype(I32), zi, zi, zi, zi], axis=0)
    zf = jnp.zeros((1, tm), F32)
    rg_ref[...] = jnp.concatenate([gate1, gate2, zf, zf, zf, zf, zf, zf], axis=0)


def _route(x2d, att, hm, pw, tm):
    t = x2d.shape[0]
    row = lambda i: (i, 0)
    col = lambda i: (0, i)
    full = lambda i: (0, 0)
    return pl.pallas_call(
        _route_kernel,
        grid=(t // tm,),
        in_specs=[
            pl.BlockSpec((tm, D_MODEL), row),
            pl.BlockSpec((tm, D_MODEL // 2), row),
            pl.BlockSpec((tm, D_MODEL // 2), row),
            pl.BlockSpec((D_MODEL, D_MODEL), full),
            pl.BlockSpec((1, D_MODEL), full),
            pl.BlockSpec((R_ROWS, D_MODEL), full),
            pl.BlockSpec((R_ROWS, LANES), full),
        ],
        out_specs=[
            pl.BlockSpec((tm, D_MODEL), row),
            pl.BlockSpec((tm, D_MODEL // 2), row),
            pl.BlockSpec((8, tm), col),
            pl.BlockSpec((8, tm), col),
            pl.BlockSpec((N_EXPERTS, LANES), full),
        ],
        out_shape=[
            jax.ShapeDtypeStruct((t, D_MODEL), F32),
            jax.ShapeDtypeStruct((t, D_MODEL // 2), U32),
            jax.ShapeDtypeStruct((8, t), I32),
            jax.ShapeDtypeStruct((8, t), F32),
            jax.ShapeDtypeStruct((N_EXPERTS, LANES), I32),
        ],
        scratch_shapes=[pltpu.VMEM((N_EXPERTS, LANES), F32)],
        compiler_params=_cparams(("arbitrary",)),
        name="route",
    )(x2d, att, hm, pw["w_out"], pw["g_ffn"], pw["w_r"], pw["b_r"])


def _row_copy(src, s, dst, d, sem):
    return pltpu.make_async_copy(src.at[pl.ds(s, 1)], dst.at[pl.ds(d, 1)], sem)


DMA_UNROLL = 8


N_ZERO_BLOCKS = 2 * N_EXPERTS


def _dispatch_kernel(zb_ref, dest_ref, hp_ref, xs_ref, zero_scr, sem, zsem, *, tm, bm):
    @pl.when(pl.program_id(0) == 0)
    def _():
        zero_scr[...] = jnp.zeros_like(zero_scr)

        def zcopy(k):
            return pltpu.make_async_copy(zero_scr, xs_ref.at[pl.ds(pl.multiple_of(zb_ref[k] * bm, bm), bm)], zsem)

        def zstart(k, carry):
            @pl.when(zb_ref[k] >= 0)
            def _():
                zcopy(k).start()
            return carry

        def zwait(k, carry):
            @pl.when(zb_ref[k] >= 0)
            def _():
                zcopy(k).wait()
            return carry

        lax.fori_loop(0, N_ZERO_BLOCKS, zstart, 0)
        lax.fori_loop(0, N_ZERO_BLOCKS, zwait, 0)

    def body(g, carry):
        for u in range(DMA_UNROLL):
            t = g * DMA_UNROLL + u
            _row_copy(hp_ref, t, xs_ref, dest_ref[0, t], sem).start(priority=0)
            _row_copy(hp_ref, t, xs_ref, dest_ref[1, t], sem).start(priority=1)
        return carry

    lax.fori_loop(0, tm // DMA_UNROLL, body, 0)
    pltpu.make_async_copy(xs_ref.at[pl.ds(0, 2 * tm)], xs_ref.at[pl.ds(0, 2 * tm)], sem).wait()


def _dispatch(zero_blocks, dest, hp, n_slots, tm, bm):
    t = hp.shape[0]
    return pl.pallas_call(
        functools.partial(_dispatch_kernel, tm=tm, bm=bm),
        grid_spec=pltpu.PrefetchScalarGridSpec(
            num_scalar_prefetch=1,
            grid=(t // tm,),
            in_specs=[
                pl.BlockSpec((8, tm), lambda i, zb: (0, i), memory_space=pltpu.SMEM),
                pl.BlockSpec((tm, D_MODEL // 2), lambda i, zb: (i, 0)),
            ],
            out_specs=pl.BlockSpec(memory_space=pl.ANY),
            scratch_shapes=[
                pltpu.VMEM((bm, D_MODEL // 2), U32),
                pltpu.SemaphoreType.DMA,
                pltpu.SemaphoreType.DMA,
            ],
        ),
        out_shape=jax.ShapeDtypeStruct((n_slots, D_MODEL // 2), U32),
        compiler_params=_cparams(("arbitrary",)),
        name="dispatch",
    )(zero_blocks, dest, hp)


def _expert_kernel(be_ref, nu_ref, xs_ref, w1_ref, w3_ref, w2_ref, ys_ref, w1_scr, w3_scr, w2_scr):
    i = pl.program_id(0)
    new_expert = jnp.logical_or(i == 0, be_ref[i] != be_ref[jnp.maximum(i - 1, 0)])

    @pl.when(jnp.logical_and(new_expert, i < nu_ref[0]))
    def _():
        w1_scr[...] = w1_ref[0].astype(BF16)
        w3_scr[...] = w3_ref[0].astype(BF16)
        w2_scr[...] = w2_ref[0].astype(BF16)

    @pl.when(i < nu_ref[0])
    def _():
        xw = xs_ref[...]
        xa = lax.bitcast_convert_type(xw & jnp.uint32(0xFFFF0000), F32).astype(BF16)
        xb = lax.bitcast_convert_type(xw << 16, F32).astype(BF16)
        half = D_MODEL // 2
        h1 = _dot(xa, w1_scr[:half, :]) + _dot(xb, w1_scr[half:, :])
        h3 = _dot(xa, w3_scr[:half, :]) + _dot(xb, w3_scr[half:, :])
        a = (h1 * jax.nn.sigmoid(h1)) * h3
        ys_ref[...] = _dot(a.astype(BF16), w2_scr[...])

    @pl.when(pl.program_id(0) >= nu_ref[0])
    def _():
        ys_ref[...] = jnp.zeros_like(ys_ref)


def _experts(blk_e, n_used, xs, pw, bm):
    n_slots = xs.shape[0]
    nblk = n_slots // bm
    blk = lambda i, be, nu: (jnp.minimum(i, nu[0] - 1), 0)
    oblk = lambda i, be, nu: (i, 0)
    wsel = lambda i, be, nu: (be[i], 0, 0)
    return pl.pallas_call(
        _expert_kernel,
        grid_spec=pltpu.PrefetchScalarGridSpec(
            num_scalar_prefetch=2,
            grid=(nblk,),
            in_specs=[
                pl.BlockSpec((bm, D_MODEL // 2), blk),
                pl.BlockSpec((1, D_MODEL, EXPERT_FF), wsel),
                pl.BlockSpec((1, D_MODEL, EXPERT_FF), wsel),
                pl.BlockSpec((1, EXPERT_FF, D_MODEL), wsel),
            ],
            out_specs=pl.BlockSpec((bm, D_MODEL), oblk),
            scratch_shapes=[
                pltpu.VMEM((D_MODEL, EXPERT_FF), BF16),
                pltpu.VMEM((D_MODEL, EXPERT_FF), BF16),
                pltpu.VMEM((EXPERT_FF, D_MODEL), BF16),
            ],
        ),
        out_shape=jax.ShapeDtypeStruct((n_slots, D_MODEL), F32),
        compiler_params=_cparams(("arbitrary",)),
        name="experts",
    )(blk_e, n_used, xs, pw["w1"], pw["w3"], pw["w2"])


def _combine_kernel(dest_ref, x1_ref, rg_ref, ys_ref, y_ref, r0_scr, r1_scr, sem, *, tm):
    def body(g, carry):
        for u in range(DMA_UNROLL):
            t = g * DMA_UNROLL + u
            _row_copy(ys_ref, dest_ref[0, t], r0_scr, t, sem).start(priority=0)
            _row_copy(ys_ref, dest_ref[1, t], r1_scr, t, sem).start(priority=1)
        return carry

    lax.fori_loop(0, tm // DMA_UNROLL, body, 0)
    gt =jnp.concatenate([rg_ref[...], jnp.zeros((LANES - 8, tm), F32)], axis=0).T
    pltpu.make_async_copy(ys_ref.at[pl.ds(0, tm)], r0_scr, sem).wait()
    pltpu.make_async_copy(ys_ref.at[pl.ds(0, tm)], r1_scr, sem).wait()
    y_ref[...] = x1_ref[...] + gt[:, 0:1] * r0_scr[...] + gt[:, 1:2] * r1_scr[...]


def _combine(dest, x1, rg, ys, tm):
    t = x1.shape[0]
    row = lambda i: (i, 0)
    col = lambda i: (0, i)
    return pl.pallas_call(
        functools.partial(_combine_kernel, tm=tm),
        grid=(t // tm,),
        in_specs=[
            pl.BlockSpec((8, tm), col, memory_space=pltpu.SMEM),
            pl.BlockSpec((tm, D_MODEL), row),
            pl.BlockSpec((8, tm), col),
            pl.BlockSpec(memory_space=pl.ANY),
        ],
        out_specs=pl.BlockSpec((tm, D_MODEL), row),
        out_shape=jax.ShapeDtypeStruct((t, D_MODEL), F32),
        scratch_shapes=[
            pltpu.VMEM((tm, D_MODEL), F32),
            pltpu.VMEM((tm, D_MODEL), F32),
            pltpu.SemaphoreType.DMA,
        ],
        compiler_params=_cparams(("arbitrary",)),
        name="combine",
    )(dest, x1, rg, ys)


def _prep_weights(g_attn, w_in, g_cq, w_uq, g_ckv, w_ukv, g_q, g_k, b_igate, b_fgate, g_mh,
                  w_out, g_ffn, w_group, b_group, w_erouter, b_erouter, w1, w3, w2):
    def cols(a, b):
        return w_in[:, a:b]

    o_cq, o_ckv, o_kr = 0, 256, 384
    o_mq, o_mk, o_mv, o_mi, o_mf, o_mo = 416, 672, 928, 1440, 1444, 1448
    hr = MLA_ROPE // 2
    misc = jnp.concatenate([
        cols(o_kr, o_kr + MLA_ROPE), -cols(o_kr + hr, o_kr + MLA_ROPE), cols(o_kr, o_kr + hr),
        cols(o_mi, o_mi + M_HEADS), cols(o_mf, o_mf + M_HEADS),
        jnp.zeros((D_MODEL, LANES - 2 * MLA_ROPE - 2 * M_HEADS), F32)], axis=1)
    w_p = jnp.concatenate([
        cols(o_cq, o_cq + 256), cols(o_ckv, o_ckv + 128), cols(o_mq, o_mq + 256), cols(o_mk, o_mk + 256),
        cols(o_mv, o_mv + 512), cols(o_mo, o_mo + 512), misc], axis=1).astype(BF16)
    nope, r1, r2 = w_uq[..., :MLA_NOPE], w_uq[..., MLA_NOPE:MLA_NOPE + hr], w_uq[..., MLA_NOPE + hr:]
    w_q = jnp.concatenate([nope, r1, r2, -r2, r1], axis=-1).reshape(MLA_Q_LORA, MLA_HEADS * HEAD_PAD).astype(BF16)
    pad_g = jnp.zeros((HEAD_PAD - MLA_QK,), F32)
    w_k = jnp.concatenate([w_ukv[..., :MLA_NOPE], jnp.zeros((MLA_KV_LORA, MLA_HEADS, HEAD_PAD - MLA_NOPE), F32)],
                          axis=-1).transpose(1, 0, 2).astype(BF16)
    w_v = jnp.concatenate([w_ukv[..., MLA_NOPE:], jnp.zeros((MLA_KV_LORA, MLA_HEADS, LANES - MLA_V), F32)],
                          axis=-1).transpose(1, 0, 2).astype(BF16)
    gate_bias = jnp.concatenate([jnp.zeros((_L_IG,), F32), b_igate, b_fgate,
                                 jnp.zeros((LANES - _L_FG - M_HEADS,), F32)])[None]
    w_r = jnp.concatenate([w_erouter.T, w_group.T, jnp.zeros((R_ROWS - N_EXPERTS - N_GROUPS, D_MODEL), F32)],
                          axis=0).astype(BF16)
    b_r = jnp.concatenate([b_erouter, b_group, jnp.zeros((R_ROWS - N_EXPERTS - N_GROUPS,), F32)])
    return {
        "w_in": w_p, "g_attn": g_attn[None], "g_cq": g_cq[None], "w_q": w_q, "g_ckv": g_ckv[None],
        "g_q": jnp.concatenate([g_q, pad_g])[None], "g_k": jnp.concatenate([g_k, pad_g])[None],
        "gate_bias": gate_bias, "w_k": w_k, "w_v": w_v, "g_mh": g_mh,
        "w_out": w_out.astype(BF16), "g_ffn": g_ffn[None], "w_r": w_r,
        "b_r": jnp.broadcast_to(b_r[:, None], (R_ROWS, LANES)),
        "w1": w1, "w3": w3, "w2": w2,
    }


def _rope_table(pos):
    half = MLA_ROPE // 2
    inv = ROPE_BASE ** (-np.arange(half, dtype=np.float64) / half)
    ang = np.asarray(pos, np.float64)[:, None] * inv[None, :]
    cos = np.cos(ang)
    sin = np.sin(ang)
    c2 = np.concatenate([cos, cos], axis=1)
    s2 = np.concatenate([sin, sin], axis=1)
    return jnp.asarray(np.concatenate([c2, s2, c2, s2], axis=1), F32)


def _pick(n, pref):
    return pref if n % pref == 0 else n


def _slots_kernel(ri_ref, ps_ref, d_ref):
    tm = ri_ref.shape[1]
    xi = lax.broadcasted_iota(I32, (N_EXPERTS, tm), 0)
    ps = ps_ref[:, 0:1]
    rows = []
    for k in range(2):
        start = jnp.sum(jnp.where(xi == ri_ref[k:k + 1, :], ps, 0), axis=0, keepdims=True)
        rows.append(start + ri_ref[2 + k:3 + k, :])
    d_ref[...] = jnp.concatenate(rows + [jnp.zeros((6, tm), I32)], axis=0)


def _slots(ri, pstart_b, tm):
    t = ri.shape[1]
    return pl.pallas_call(
        _slots_kernel,
        grid=(t // tm,),
        in_specs=[pl.BlockSpec((8, tm), lambda i: (0, i)), pl.BlockSpec((N_EXPERTS, LANES), lambda i: (0, 0))],
        out_specs=pl.BlockSpec((8, tm), lambda i: (0, i)),
        out_shape=jax.ShapeDtypeStruct((8, t), I32),
        compiler_params=_cparams(("arbitrary",)),
        name="slots",
    )(ri, pstart_b)


def _moe_layer(x2d, att, hm, pw, *, tm_route, bm, tm_disp, tm_comb):
    t = x2d.shape[0]
    x1, hp, ri, rg, cnt = _route(x2d, att, hm, pw, tm_route)
    counts = cnt[:, 0]
    padded = (counts + bm - 1) // bm * bm
    pend = jnp.cumsum(padded)
    pstart = pend - padded
    dest = _slots(ri, jnp.broadcast_to(pstart[:, None], (N_EXPERTS, LANES)), _pick(t, 2048))
    n_slots = (2 * t // bm + N_EXPERTS) * bm
    nblk = n_slots // bm
    n_used = (pend[-1] // bm).astype(I32)
    experts = jnp.arange(N_EXPERTS, dtype=I32)
    blk_first = jnp.arange(nblk, dtype=I32) * bm
    blk_e = jnp.sum((pend[None, :] <= blk_first[:, None]).astype(I32), axis=1)
    blk_e = jnp.minimum(blk_e, jnp.max(jnp.where(counts > 0, experts, 0)))
    zero_blocks = jnp.concatenate([jnp.where(counts > 0, pend // bm - 1, -1),
                                   jnp.where(n_used + experts < nblk, n_used + experts, -1)]).astype(I32)
    xs = _dispatch(zero_blocks, dest, hp, n_slots, tm_disp, bm)
    ys = _experts(blk_e, n_used[None], xs, pw, bm)
    return _combine(dest, x1, rg, ys, tm_comb)


def _state_pack(c, n):
    ct = jnp.swapaxes(c, -1, -2)
    return jnp.concatenate([ct, n[..., None], jnp.zeros(ct.shape[:-1] + (S_W - M_DV - 1,), F32)], axis=-1)


def _state_unpack(s, m):
    return jnp.swapaxes(s[..., :M_DV], -1, -2), s[..., M_DV], m[:, :, 0, 0]


def kernel(x_prompt, x_sample, cache_ckv, cache_krope, state_mlstm_c, state_mlstm_n, state_mlstm_m,
           meta_tokens, g_attn, w_in, g_cq, w_uq, g_ckv, w_ukv, g_q, g_k, b_igate, b_fgate, g_mh,
           w_out, g_ffn, w_group, b_group, w_erouter, b_erouter, w1, w3, w2):
    bp, seq = x_prompt.shape[:2]
    bs, dec = x_sample.shape[:2]
    past = cache_ckv.shape[2]
    layer = 0
    pw = _prep_weights(g_attn[layer], w_in[layer], g_cq[layer], w_uq[layer], g_ckv[layer], w_ukv[layer],
                       g_q[layer], g_k[layer], b_igate[layer], b_fgate[layer], g_mh[layer], w_out[layer],
                       g_ffn[layer], w_group[layer], b_group[layer], w_erouter[layer], b_erouter[layer],
                       w1[layer], w3[layer], w2[layer])

    xm = jnp.concatenate([meta_tokens, jnp.zeros((META_PAD - N_META, D_MODEL), F32)], axis=0)
    tab_m = _rope_table(np.arange(META_PAD) - N_META)
    _, ckv_m, misc_m, mq_m, mk_m, mv_m, og_m = _project(xm, tab_m, pw, META_PAD)
    zero_s = jnp.zeros((1, M_HEADS, M_DK, S_W), F32)
    zero_m = jnp.zeros((1, M_HEADS, 8, LANES), F32)
    _, s_meta, m_meta = _mlstm(mq_m, mk_m, mv_m, og_m, misc_m, zero_s, zero_m, pw["g_mh"],
                               nb=1, l=META_PAD, lc=META_PAD, n_valid=N_META)

    tp = bp * seq
    xp2 = x_prompt.reshape(tp, D_MODEL)
    tm_p = _pick(seq, 512)
    q_p, ckv_p, misc_p, mq_p, mk_p, mv_p, og_p = _project(xp2, _rope_table(np.arange(seq)), pw, tm_p)
    tq = _pick(seq, 512)
    att_p = _attention(q_p, ckv_p, misc_p, ckv_m[None], misc_m[None], pw, nb=bp, lq=seq, lf=seq, tq=tq, tk=tq,
                       causal=True, n_tail=N_META)
    hm_p, s_p, m_p = _mlstm(mq_p, mk_p, mv_p, og_p, misc_p, s_meta, m_meta, pw["g_mh"],
                            nb=bp, l=seq, lc=_pick(seq, 128))
    y_p = _moe_layer(xp2, att_p, hm_p, pw, tm_route=tm_p, bm=512, tm_disp=tm_p, tm_comb=_pick(seq, 256))

    ts = bs * dec
    xs2 = x_sample.reshape(ts, D_MODEL)
    q_s, ckv_s, misc_s, mq_s, mk_s, mv_s, og_s = _project(xs2, _rope_table(past + np.arange(dec)), pw, dec)
    kr_cache = jnp.concatenate([cache_krope[layer], jnp.zeros((bs, past, LANES - MLA_ROPE), F32)], axis=-1)
    n_tail = dec + N_META
    assert n_tail <= META_PAD

    def tail_rows(own, meta):
        w = own.shape[-1]
        return jnp.concatenate([own.reshape(bs, dec, w), jnp.broadcast_to(meta[None, :N_META], (bs, N_META, w)),
                                jnp.zeros((bs, META_PAD - n_tail, w), F32)], axis=1)

    att_s = _attention(q_s, cache_ckv[layer].reshape(bs * past, MLA_KV_LORA), kr_cache.reshape(bs * past, LANES),
                       tail_rows(ckv_s, ckv_m), tail_rows(misc_s, misc_m), pw, nb=bs, lq=dec, lf=past, tq=dec,
                       tk=_pick(past, 256), causal=False, n_tail=n_tail)
    s0 = _state_pack(state_mlstm_c[layer], state_mlstm_n[layer])
    m0 = jnp.broadcast_to(state_mlstm_m[layer][:, :, None, None], (bs, M_HEADS, 8, LANES))
    hm_s, s_s, m_s = _mlstm(mq_s, mk_s, mv_s, og_s, misc_s, s0, m0, pw["g_mh"], nb=bs, l=dec, lc=dec)
    y_s = _moe_layer(xs2, att_s, hm_s, pw, tm_route=_pick(ts, 512), bm=128, tm_disp=_pick(ts, 512),
                     tm_comb=_pick(ts, 256))

    m_ckv = ckv_m[:N_META]
    m_kr = misc_m[:N_META, :MLA_ROPE]
    new_ckv_p = jnp.concatenate([jnp.broadcast_to(m_ckv[None], (bp, N_META, MLA_KV_LORA)),
                                 ckv_p.reshape(bp, seq, MLA_KV_LORA)], axis=1)[None]
    new_kr_p = jnp.concatenate([jnp.broadcast_to(m_kr[None], (bp, N_META, MLA_ROPE)),
                                misc_p[:, :MLA_ROPE].reshape(bp, seq, MLA_ROPE)], axis=1)[None]
    c_p, n_p, mm_p = _state_unpack(s_p, m_p)
    c_s, n_s, mm_s = _state_unpack(s_s, m_s)
    return (y_p.reshape(bp, seq, D_MODEL), y_s.reshape(bs, dec, D_MODEL),
            new_ckv_p, new_kr_p, c_p[None], n_p[None], mm_p[None],
            ckv_s.reshape(bs, dec, MLA_KV_LORA)[None], misc_s[:, :MLA_ROPE].reshape(bs, dec, MLA_ROPE)[None],
            c_s[None], n_s[None], mm_s[None])
```

```python
import functools

import numpy as np
import jax
import jax.numpy as jnp
from jax import lax
from jax.experimental import pallas as pl
from jax.experimental.pallas import tpu as pltpu

F32 = jnp.float32
BF16 = jnp.bfloat16
I32 = jnp.int32
U32 = jnp.uint32

D_MODEL = 1024
CHUNK = 64
N_META = 16
MLA_HEADS = 8
MLA_V = 64
MLA_NOPE = 64
MLA_ROPE = 32
MLA_QK = MLA_NOPE + MLA_ROPE
MLA_Q_LORA = 256
MLA_KV_LORA = 128
MLA_SCALE = MLA_QK ** -0.5
ROPE_BASE = 10000.0
M_HEADS = 4
M_DV = 128
M_DK = 64
N_GROUPS = 4
E_PER_GROUP = 8
N_EXPERTS = 32
EXPERT_FF = 512
EPS = 1e-6

LANES = 128
HEAD_PAD = 128
PROJ_PAD = 2048
META_PAD = 128
NEG_BIG = -1e30
VMEM_LIMIT = 56 * 1024 * 1024

_O_CQ, _O_CKV, _O_MQ, _O_MK, _O_MV, _O_MO, _O_MISC = 0, 256, 384, 640, 896, 1408, 1920
_L_IG, _L_FG = 64, 68


def _cparams(sem):
    return pltpu.CompilerParams(dimension_semantics=sem, vmem_limit_bytes=VMEM_LIMIT)


def _dot(a, b):
    return jnp.dot(a, b, preferred_element_type=F32)


def _dot_nt(a, b):
    return lax.dot_general(a, b, (((1,), (1,)), ((), ())), preferred_element_type=F32)


def _split3(x):
    x1 = x.astype(BF16)
    r1 = x - x1.astype(F32)
    x2 = r1.astype(BF16)
    x3 = (r1 - x2.astype(F32)).astype(BF16)
    return x1, x2, x3


def _proj_kernel(x_ref, tab_ref, w_ref, g_ref, gcq_ref, wq_ref, gckv_ref, gq_ref, bias_ref,
                 q_ref, ckv_ref, misc_ref, mq_ref, mk_ref, mv_ref, og_ref):
    x = x_ref[...]
    xn = x * lax.rsqrt(jnp.mean(x * x, axis=-1, keepdims=True) + EPS) * g_ref[...]
    z = _dot(xn.astype(BF16), w_ref[...])
    tab = tab_ref[...]
    lane = lax.broadcasted_iota(I32, tab.shape, 1)

    cq = z[:, _O_CQ:_O_CQ + MLA_Q_LORA]
    cqn = cq * lax.rsqrt(jnp.mean(cq * cq, axis=-1, keepdims=True) + EPS) * gcq_ref[...]
    qz = _dot(cqn.astype(BF16), wq_ref[...])
    gq = gq_ref[...]
    for h in range(MLA_HEADS):
        zh = qz[:, h * HEAD_PAD:(h + 1) * HEAD_PAD]
        y = zh * tab
        rot = y + pltpu.roll(y, LANES - MLA_ROPE, 1)
        qh = jnp.where(lane < MLA_NOPE, zh, jnp.where(lane < MLA_QK, rot, 0.0))
        ms = jnp.sum(qh * qh, axis=-1, keepdims=True) * (1.0 / MLA_QK)
        q_ref[h] = (qh * lax.rsqrt(ms + EPS) * gq).astype(BF16)

    ckv = z[:, _O_CKV:_O_CKV + MLA_KV_LORA]
    ckv_ref[...] = ckv * lax.rsqrt(jnp.mean(ckv * ckv, axis=-1, keepdims=True) + EPS) * gckv_ref[...]

    zm = z[:, _O_MISC:_O_MISC + LANES]
    y = zm * tab
    rot = y + pltpu.roll(y, LANES - MLA_ROPE, 1)
    gate = zm + bias_ref[...]
    logf = jnp.minimum(gate, 0.0) - jnp.log1p(jnp.exp(-jnp.abs(gate)))
    misc = jnp.where(lane < MLA_ROPE, rot,
                     jnp.where((lane >= _L_IG) & (lane < _L_FG), gate,
                               jnp.where((lane >= _L_FG) & (lane < _L_FG + M_HEADS), logf, 0.0)))
    misc_ref[...] = misc

    mq_ref[...] = z[:, _O_MQ:_O_MQ + M_HEADS * M_DK].astype(BF16)
    mk_ref[...] = (z[:, _O_MK:_O_MK + M_HEADS * M_DK] * (M_DK ** -0.5)).astype(BF16)
    mv_ref[...] = z[:, _O_MV:_O_MV + M_HEADS * M_DV].astype(BF16)
    og_ref[...] = jax.nn.sigmoid(z[:, _O_MO:_O_MO + M_HEADS * M_DV]).astype(BF16)


def _project(x2d, tab, pw, tm):
    t = x2d.shape[0]
    nt = t // tm
    ntab = tab.shape[0] // tm
    row = lambda i: (i, 0)
    full = lambda i: (0, 0)
    return pl.pallas_call(
        _proj_kernel,
        grid=(nt,),
        in_specs=[
            pl.BlockSpec((tm, D_MODEL), row),
            pl.BlockSpec((tm, LANES), lambda i: (i % ntab, 0)),
            pl.BlockSpec((D_MODEL, PROJ_PAD), full),
            pl.BlockSpec((1, D_MODEL), full),
            pl.BlockSpec((1, MLA_Q_LORA), full),
            pl.BlockSpec((MLA_Q_LORA, MLA_HEADS * HEAD_PAD), full),
            pl.BlockSpec((1, MLA_KV_LORA), full),
            pl.BlockSpec((1, HEAD_PAD), full),
            pl.BlockSpec((1, LANES), full),
        ],
        out_specs=[
            pl.BlockSpec((MLA_HEADS, tm, HEAD_PAD), lambda i: (0, i, 0)),
            pl.BlockSpec((tm, MLA_KV_LORA), row),
            pl.BlockSpec((tm, LANES), row),
            pl.BlockSpec((tm, M_HEADS * M_DK), row),
            pl.BlockSpec((tm, M_HEADS * M_DK), row),
            pl.BlockSpec((tm, M_HEADS * M_DV), row),
            pl.BlockSpec((tm, M_HEADS * M_DV), row),
        ],
        out_shape=[
            jax.ShapeDtypeStruct((MLA_HEADS, t, HEAD_PAD), BF16),
            jax.ShapeDtypeStruct((t, MLA_KV_LORA), F32),
            jax.ShapeDtypeStruct((t, LANES), F32),
            jax.ShapeDtypeStruct((t, M_HEADS * M_DK), BF16),
            jax.ShapeDtypeStruct((t, M_HEADS * M_DK), BF16),
            jax.ShapeDtypeStruct((t, M_HEADS * M_DV), BF16),
            jax.ShapeDtypeStruct((t, M_HEADS * M_DV), BF16),
        ],
        compiler_params=_cparams(("arbitrary",)),
        name="projection",
    )(x2d, tab, pw["w_in"], pw["g_attn"], pw["g_cq"], pw["w_q"], pw["g_ckv"], pw["g_q"], pw["gate_bias"])


SAFE_BOUND = 40.0
LOG2E = 1.4426950408889634


VT_ROWS = 80


def _attn_kernel(q_ref, ckv_ref, misc_ref, ckvt_ref, misct_ref, wk_ref, wv_ref, gk_ref, o_ref,
                 k_scr, vt_scr, kmax_scr, qt_scr, bnd_scr, acc_scr, ot_scr, p_scr, *, lf, tq, tk, bt, causal,
                 n_tail):
    i = pl.program_id(1)
    gk = gk_ref[...]

    def build(ckv_rows, misc_rows, dst):
        n = ckv_rows.shape[0]
        lane = lax.broadcasted_iota(I32, (n, LANES), 1)
        cb = ckv_rows.astype(BF16)
        krp = jnp.where((lane >= MLA_NOPE) & (lane < MLA_QK), pltpu.roll(misc_rows, MLA_NOPE, 1), 0.0)
        onecol = (lane == MLA_V).astype(F32)
        for h in range(MLA_HEADS):
            kk = _dot(cb, wk_ref[h]) + krp
            ms = jnp.sum(kk * kk, axis=-1, keepdims=True) * (1.0 / MLA_QK)
            kb = (kk * lax.rsqrt(ms + EPS) * gk).astype(BF16)
            kf = kb.astype(F32)
            kn = jnp.max(jnp.sum(kf * kf, axis=-1, keepdims=True), axis=0, keepdims=True)
            kmax_scr[h] = jnp.maximum(kmax_scr[h], jnp.broadcast_to(kn, (8, LANES)))
            k_scr[h, pl.ds(dst, n), :] = kb
            vext_t = (_dot(cb, wv_ref[h]) + onecol).T
            vt_scr[h, :, pl.ds(dst, n)] = vext_t[0:VT_ROWS, :].astype(BF16)

    @pl.when(i == 0)
    def _():
        kmax_scr[...] = jnp.zeros_like(kmax_scr)

        def body(r, carry):
            r0 = pl.multiple_of(r * bt, bt)
            build(ckv_ref[pl.ds(r0, bt), :], misc_ref[pl.ds(r0, bt), :], r0)
            return carry

        lax.fori_loop(0, lf // bt, body, 0)
        build(ckvt_ref[0], misct_ref[0], lf)

    tail_mask = lax.broadcasted_iota(I32, (META_PAD, tq), 0) < n_tail
    if causal:
        nfull = i * (tq // tk)
        key_c = lax.broadcasted_iota(I32, (tk, tq), 0)
        qry_c = lax.broadcasted_iota(I32, (tk, tq), 1) // CHUNK
        diag_masks = [(key_c + d * tk) // CHUNK <= qry_c for d in range(tq // tk)]
    else:
        nfull = lf // tk
        diag_masks = []

    bmax = jnp.zeros((1, 1), F32)
    for h in range(MLA_HEADS):
        qt = q_ref[h].astype(F32).T
        qt_scr[h] = qt.astype(BF16)
        qn = jnp.sqrt(jnp.sum(qt * qt, axis=0, keepdims=True))
        bnd = qn * (jnp.sqrt(kmax_scr[h, 0:1, 0:1]) * MLA_SCALE)
        bnd_scr[h] = jnp.broadcast_to(bnd * LOG2E, (8, tq))
        bmax = jnp.maximum(bmax, jnp.max(bnd, axis=1, keepdims=True))
    safe = bmax[0, 0] <= SAFE_BOUND

    def kv_tail(h):
        return k_scr[h, lf:lf + META_PAD, :], vt_scr[h, :, lf:lf + META_PAD]

    def kv_tile(h, r0):
        return k_scr[h, pl.ds(r0, tk), :], vt_scr[h, :, pl.ds(r0, tk)]

    def finish(h, acc):
        ot_scr[h * MLA_V:(h + 1) * MLA_V, :] = acc[:MLA_V, :] / acc[MLA_V:MLA_V + 1, :]

    @pl.when(safe)
    def _():
        def scores(h, krows):
            return _dot(krows, qt_scr[h]) * (MLA_SCALE * LOG2E) - bnd_scr[h, 0:1, :]

        def probs(s, mask):
            if mask is not None:
                s = jnp.where(mask, s, -jnp.inf)
            return jnp.exp2(s).astype(BF16)

        def k_tile(h, j):
            return k_scr[h, pl.ds(pl.multiple_of(j * tk, tk), tk), :]

        def vt_tile(h, j):
            return vt_scr[h, :, pl.ds(pl.multiple_of(j * tk, tk), tk)]

        def pipe_step(j, mask):
            ss = [scores(h, k_tile(h, j)) for h in range(MLA_HEADS)]
            for h in range(MLA_HEADS):
                acc_scr[h] += _dot(vt_tile(h, j - 1), p_scr[(j - 1) % 2, h])
            for h in range(MLA_HEADS):
                p_scr[j % 2, h] = probs(ss[h], mask)

        for h in range(MLA_HEADS):
            krows, vt = kv_tail(h)
            acc_scr[h] = _dot(vt, probs(scores(h, krows), tail_mask))
        if causal:
            lift = jnp.where(nfull > 0, tk, 0)
            first_mask = key_c // CHUNK <= qry_c + lift
        else:
            first_mask = None
        for h in range(MLA_HEADS):
            p_scr[0, h] = probs(scores(h, k_tile(h, 0)), first_mask)

        def body(j, carry):
            pipe_step(j, None)
            return carry

        lax.fori_loop(1, nfull, body, 0)
        if causal:
            @pl.when(nfull >= 1)
            def _():
                pipe_step(nfull, diag_masks[0])
        last = nfull if causal else nfull - 1
        for h in range(MLA_HEADS):
            finish(h, acc_scr[h] + _dot(vt_tile(h, last), p_scr[last % 2, h]))

    @pl.when(jnp.logical_not(safe))
    def _():
        for h in range(MLA_HEADS):
            qth = qt_scr[h]

            def step(carry, kv, mask):
                m, acc = carry
                krows, vt = kv
                s = _dot(krows, qth) * MLA_SCALE
                if mask is not None:
                    s = jnp.where(mask, s, -jnp.inf)
                m_new = jnp.maximum(m, jnp.max(s, axis=0, keepdims=True))
                acc = jnp.exp(m - m_new) * acc + _dot(vt, jnp.exp(s - m_new).astype(BF16))
                return m_new, acc

            carry = (jnp.full((1, tq), -jnp.inf, F32), jnp.zeros((VT_ROWS, tq), F32))
            carry = step(carry, kv_tail(h), tail_mask)
            carry = lax.fori_loop(
                0, nfull, lambda j, c: step(c, kv_tile(h, pl.multiple_of(j * tk, tk)), None), carry)
            for d, mask in enumerate(diag_masks):
                carry = step(carry, kv_tile(h, pl.multiple_of(i * tq + d * tk, tk)), mask)
            finish(h, carry[1])

    o_ref[...] = ot_scr[...].T.astype(BF16)


def _attention(q, ckv_f, misc_f, ckv_t, misc_t, pw, *, nb, lq, lf, tq, tk, causal, n_tail):
    nq = lq // tq
    bt = min(512, lf)
    assert lf % bt == 0 and lf % tk == 0 and lq % tq == 0 and (not causal or tq == tk)
    kern = functools.partial(_attn_kernel, lf=lf, tq=tq, tk=tk, bt=bt, causal=causal, n_tail=n_tail)
    full2 = lambda b, i: (0, 0)
    full3 = lambda b, i: (0, 0, 0)
    tail = full3 if ckv_t.shape[0] == 1 else (lambda b, i: (b, 0, 0))
    lk = lf + META_PAD
    return pl.pallas_call(
        kern,
        grid=(nb, nq),
        in_specs=[
            pl.BlockSpec((MLA_HEADS, tq, HEAD_PAD), lambda b, i: (0, b * nq + i, 0)),
            pl.BlockSpec((lf, MLA_KV_LORA), lambda b, i: (b, 0)),
            pl.BlockSpec((lf, LANES), lambda b, i: (b, 0)),
            pl.BlockSpec((1, META_PAD, MLA_KV_LORA), tail),
            pl.BlockSpec((1, META_PAD, LANES), tail),
            pl.BlockSpec((MLA_HEADS, MLA_KV_LORA, HEAD_PAD), full3),
            pl.BlockSpec((MLA_HEADS, MLA_KV_LORA, LANES), full3),
            pl.BlockSpec((1, HEAD_PAD), full2),
        ],
        out_specs=pl.BlockSpec((tq, MLA_HEADS * MLA_V), lambda b, i: (b * nq + i, 0)),
        out_shape=jax.ShapeDtypeStruct((nb * lq, MLA_HEADS * MLA_V), BF16),
        scratch_shapes=[
            pltpu.VMEM((MLA_HEADS, lk, HEAD_PAD), BF16),
            pltpu.VMEM((MLA_HEADS, VT_ROWS, lk), BF16),
            pltpu.VMEM((MLA_HEADS, 8, LANES), F32),
            pltpu.VMEM((MLA_HEADS, HEAD_PAD, tq), BF16),
            pltpu.VMEM((MLA_HEADS, 8, tq), F32),
            pltpu.VMEM((MLA_HEADS, VT_ROWS, tq), F32),
            pltpu.VMEM((MLA_HEADS * MLA_V, tq), F32),
            pltpu.VMEM((2, MLA_HEADS, tk, tq), BF16),
        ],
        compiler_params=_cparams(("arbitrary", "arbitrary")),
        name="attention",
    )(q, ckv_f, misc_f, ckv_t, misc_t, pw["w_k"], pw["w_v"], pw["g_k"])


S_W = 2 * M_DV
S_SHAPE = (M_HEADS // 2, 2 * M_DK, S_W)


MLSTM_CHUNK = LANES


def _mlstm_kernel(q_ref, k_ref, v_ref, og_ref, misc_ref, s0_ref, m0_ref, gmh_ref,
                  h_ref, sout_ref, mout_ref, s_scr, m_scr, *, n_valid):
    c = pl.program_id(1)
    nc = pl.num_programs(1)
    lb = q_ref.shape[0]
    lc = MLSTM_CHUNK

    @pl.when(c == 0)
    def _():
        s_scr[...] = s0_ref[0]
        m_scr[...] = m0_ref[0]

    def rows(x):
        if lb == lc:
            return x
        return jnp.concatenate([x, jnp.zeros((lc - lb,) + x.shape[1:], x.dtype)], axis=0)

    g = rows(misc_ref[...])
    limit = lb if n_valid is None else n_valid - c * lb
    row = lax.broadcasted_iota(I32, g.shape, 0)
    lane = lax.broadcasted_iota(I32, g.shape, 1)
    g = jnp.where(row < limit, g, jnp.where((lane >= _L_IG) & (lane < _L_FG), NEG_BIG, 0.0))
    gt = g.T
    gt8 = gt[_L_IG:_L_IG + 2 * M_HEADS, :]
    rr = lax.broadcasted_iota(I32, (lc, lc), 0)
    cc = lax.broadcasted_iota(I32, (lc, lc), 1)
    causal = cc <= rr
    g1, g2, g3 = _split3(g)
    tri = causal.astype(BF16)
    b_cols = _dot(tri, g1) + _dot(tri, g2) + _dot(tri, g3)
    t1, t2, t3 = _split3(gt8)
    trit = (rr <= cc).astype(BF16)
    b_rows = (_dot(t1, trit) + _dot(t2, trit) + _dot(t3, trit))[M_HEADS:2 * M_HEADS, :]
    a = gt8[0:M_HEADS, :] - b_rows
    m_prev = m_scr[0:M_HEADS, :]
    u_b = [-jnp.maximum(m_prev[h:h + 1, :],
                        jnp.broadcast_to(jnp.max(jnp.where(causal, a[h:h + 1, :], -jnp.inf), axis=-1, keepdims=True),
                                         (lc, LANES))) for h in range(M_HEADS)]
    negm_b = [u_b[h] - jnp.broadcast_to(b_cols[:, _L_FG + h:_L_FG + h + 1], (lc, LANES)) for h in range(M_HEADS)]
    u_last = jnp.concatenate([u_b[h][lc - 1:lc, 0:1] for h in range(M_HEADS)], axis=0)
    m_new = jnp.broadcast_to(b_rows[:, lc - 1:lc] - u_last, (M_HEADS, LANES))
    w_state = jnp.exp(a + u_last)
    decay = jnp.exp(m_prev[:, 0:1] + u_last)

    q = rows(q_ref[...])
    k = rows(k_ref[...])
    v = rows(v_ref[...])
    kt = k.astype(F32).T
    ones_blk = jnp.ones((lc, M_DV), BF16)
    low_half = lax.broadcasted_iota(I32, (lc, LANES), 1) < M_DK

    heads = range(M_HEADS)
    mine = [jnp.logical_not(low_half) if h % 2 else low_half for h in heads]
    q_pair = [q[:, (h // 2) * LANES:(h // 2 + 1) * LANES] for h in heads]
    kh = [jnp.where(mine[h], k[:, (h // 2) * LANES:(h // 2 + 1) * LANES], jnp.zeros_like(q_pair[h])) for h in heads]
    qh = [jnp.where(mine[h], q_pair[h], jnp.zeros_like(q_pair[h])) for h in heads]
    vext = [jnp.concatenate([v[:, h * M_DV:(h + 1) * M_DV], ones_blk], axis=1) for h in heads]
    s_old = [s_scr[j] for j in range(M_HEADS // 2)]
    s_bf = [s.astype(BF16) for s in s_old]

    qk = [_dot_nt(q_pair[h], kh[h]) for h in heads]
    qs = [_dot(qh[h], s_bf[h // 2]) for h in heads]
    ktw = [(kt[h * M_DK:(h + 1) * M_DK, :] * w_state[h:h + 1, :]).astype(BF16) for h in heads]
    upd = [_dot(ktw[h], vext[h]) for h in heads]
    w =[(jnp.where(causal, jnp.exp(a[h:h + 1, :] + u_b[h]), 0.0) * qk[h]).astype(BF16) for h in heads]
    inter = [jnp.exp(m_prev[h:h + 1, :] + u_b[h]) for h in heads]
    r = [_dot(w[h], vext[h]) + jnp.tile(inter[h], (1, 2)) * qs[h] for h in heads]
    num = [r[h][:, :M_DV] for h in heads]
    sq = [num[h] * num[h] for h in heads]
    sq_hi = [s.astype(BF16) for s in sq]
    sq_lo = [(sq[h] - sq_hi[h].astype(F32)).astype(BF16) for h in heads]
    msn = [(_dot(sq_hi[h], ones_blk) + _dot(sq_lo[h], ones_blk)) * (1.0 / M_DV) for h in heads]
    for h in heads:
        rden = 1.0 / jnp.maximum(jnp.abs(r[h][:, M_DV:]), jnp.exp(negm_b[h]))
        scale = rden * lax.rsqrt(rden * rden * msn[h] + EPS)
        hn = num[h] * scale * gmh_ref[h:h + 1, :]
        out = (hn[:lb] * og_ref[:, h * M_DV:(h + 1) * M_DV].astype(F32)).astype(BF16)
        h_ref[:, h * M_DV:(h + 1) * M_DV] = out
    for j in range(M_HEADS // 2):
        new = [decay[h:h + 1, :] * s_old[j][(h % 2) * M_DK:(h % 2 + 1) * M_DK, :] + upd[h] for h in (2 * j, 2 * j + 1)]
        s_scr[j] = jnp.concatenate(new, axis=0)
    m_scr[0:M_HEADS, :] = m_new

    @pl.when(c == nc - 1)
    def _():
        sout_ref[0] = s_scr[...]
        mout_ref[0] = m_scr[...]


def _mlstm(mq, mk, mv, og, misc, s0, m0, gmh, *, nb, l, n_valid=None):
    lc = min(l, MLSTM_CHUNK)
    assert l % lc == 0 and lc % 16 == 0
    nc = l // lc
    shared = s0.shape[0] == 1
    st = (lambda b, c: (0, 0, 0, 0)) if shared else (lambda b, c: (b, 0, 0, 0))
    mst = (lambda b, c: (0, 0, 0)) if shared else (lambda b, c: (b, 0, 0))
    row = lambda b, c: (b * nc + c, 0)
    kern = functools.partial(_mlstm_kernel, n_valid=n_valid)
    return pl.pallas_call(
        kern,
        grid=(nb, nc),
        in_specs=[
            pl.BlockSpec((lc, M_HEADS * M_DK), row),
            pl.BlockSpec((lc, M_HEADS * M_DK), row),
            pl.BlockSpec((lc, M_HEADS * M_DV), row),
            pl.BlockSpec((lc, M_HEADS * M_DV), row),
            pl.BlockSpec((lc, LANES), row),
            pl.BlockSpec((1,) + S_SHAPE, st),
            pl.BlockSpec((1, 8, LANES), mst),
            pl.BlockSpec((M_HEADS, M_DV), lambda b, c: (0, 0)),
        ],
        out_specs=[
            pl.BlockSpec((lc, M_HEADS * M_DV), row),
            pl.BlockSpec((1,) + S_SHAPE, lambda b, c: (b, 0, 0, 0)),
            pl.BlockSpec((1, 8, LANES), lambda b, c: (b, 0, 0)),
        ],
        out_shape=[
            jax.ShapeDtypeStruct((nb * l, M_HEADS * M_DV), BF16),
            jax.ShapeDtypeStruct((nb,) + S_SHAPE, F32),
            jax.ShapeDtypeStruct((nb, 8, LANES), F32),
        ],
        scratch_shapes=[
            pltpu.VMEM(S_SHAPE, F32),
            pltpu.VMEM((8, LANES), F32),
        ],
        compiler_params=_cparams(("arbitrary", "arbitrary")),
        name="mlstm",
    )(mq, mk, mv, og, misc, s0, m0, gmh)


R_ROWS = 40


def _route_kernel(x_ref, att_ref, hm_ref, wo_ref, g_ref, wr_ref, br_ref,
                  x1_ref, hp_ref, ri_ref, rg_ref, cnt_ref, cnt_scr):
    i = pl.program_id(0)
    tm = x_ref.shape[0]

    @pl.when(i == 0)
    def _():
        cnt_scr[...] = jnp.zeros_like(cnt_scr)

    mix = jnp.concatenate([att_ref[...], hm_ref[...]], axis=1)
    x1 = x_ref[...] + _dot(mix, wo_ref[...])
    x1_ref[...] = x1
    hn = x1 * lax.rsqrt(jnp.mean(x1 * x1, axis=-1, keepdims=True) + EPS) * g_ref[...]
    hb = hn.astype(BF16)
    half = D_MODEL // 2
    hi = lax.bitcast_convert_type(hb[:, :half].astype(F32), U32)
    lo = lax.bitcast_convert_type(hb[:, half:].astype(F32), U32)
    hp_ref[...] = (hi & jnp.uint32(0xFFFF0000)) | (lo >> 16)

    logits = _dot_nt(wr_ref[...], hb) + br_ref[:, 0:1]
    e_log = logits[0:N_EXPERTS, :]
    g_log = logits[N_EXPERTS:N_EXPERTS + N_GROUPS, :]
    gmax = jnp.max(g_log, axis=0, keepdims=True)
    gsum = jnp.sum(jnp.exp(g_log - gmax), axis=0, keepdims=True)
    gi = lax.broadcasted_iota(I32, g_log.shape, 0)
    g_idx = jnp.min(jnp.where(g_log == gmax, gi, N_GROUPS), axis=0, keepdims=True)
    e_sel = jnp.zeros((E_PER_GROUP, tm), F32)
    for gg in range(N_GROUPS):
        e_sel = jnp.where(g_idx == gg, e_log[gg * E_PER_GROUP:(gg + 1) * E_PER_GROUP, :], e_sel)
    ei = lax.broadcasted_iota(I32, e_sel.shape, 0)
    m1 = jnp.max(e_sel, axis=0, keepdims=True)
    i1 = jnp.min(jnp.where(e_sel == m1, ei, E_PER_GROUP), axis=0, keepdims=True)
    e2 = jnp.where(ei == i1, -jnp.inf, e_sel)
    m2 = jnp.max(e2, axis=0, keepdims=True)
    i2 = jnp.min(jnp.where(e2 == m2, ei, E_PER_GROUP), axis=0, keepdims=True)
    ex = jnp.exp(m2 - m1)
    gp = 1.0 / gsum
    p1 = 1.0 / (1.0 + ex)
    gate1 = gp * p1
    gate2 = gp * (ex * p1)
    id1 = g_idx * E_PER_GROUP + i1
    id2 = g_idx * E_PER_GROUP + i2

    xi = lax.broadcasted_iota(I32, (N_EXPERTS, tm), 0)
    oh1 = xi == id1
    oh2 = xi == id2
    e_cnt = (oh1 | oh2).astype(F32)
    rr = lax.broadcasted_iota(I32, (tm, tm), 0)
    cc = lax.broadcasted_iota(I32, (tm, tm), 1)
    upper = (rr < cc).astype(BF16)
    pref = _dot(e_cnt.astype(BF16), upper) + cnt_scr[:, 0:1]
    rank1 = jnp.sum(jnp.where(oh1, pref, 0.0), axis=0, keepdims=True)
    rank2 = jnp.sum(jnp.where(oh2, pref, 0.0), axis=0, keepdims=True)
    cnt_new = cnt_scr[...] + jnp.sum(e_cnt, axis=1, keepdims=True)
    cnt_scr[...] = cnt_new
    cnt_ref[...] = cnt_new.astype(I32)

    zi = jnp.zeros((1, tm), I32)
    ri_ref[...] = jnp.concatenate([id1, id2, rank1.astype(I32), rank2.astype(I32), zi, zi, zi, zi], axis=0)
    zf = jnp.zeros((1, tm), F32)
    rg_ref[...] = jnp.concatenate([gate1, gate2, zf, zf, zf, zf, zf, zf], axis=0)


def _route(x2d, att, hm, pw, tm):
    t = x2d.shape[0]
    row = lambda i: (i, 0)
    col = lambda i: (0, i)
    full = lambda i: (0, 0)
    return pl.pallas_call(
        _route_kernel,
        grid=(t // tm,),
        in_specs=[
            pl.BlockSpec((tm, D_MODEL), row),
            pl.BlockSpec((tm, D_MODEL // 2), row),
            pl.BlockSpec((tm, D_MODEL // 2), row),
            pl.BlockSpec((D_MODEL, D_MODEL), full),
            pl.BlockSpec((1, D_MODEL), full),
            pl.BlockSpec((R_ROWS, D_MODEL), full),
            pl.BlockSpec((R_ROWS, LANES), full),
        ],
        out_specs=[
            pl.BlockSpec((tm, D_MODEL), row),
            pl.BlockSpec((tm, D_MODEL // 2), row),
            pl.BlockSpec((8, tm), col),
            pl.BlockSpec((8, tm), col),
            pl.BlockSpec((N_EXPERTS, LANES), full),
        ],
        out_shape=[
            jax.ShapeDtypeStruct((t, D_MODEL), F32),
            jax.ShapeDtypeStruct((t, D_MODEL // 2), U32),
            jax.ShapeDtypeStruct((8, t), I32),
            jax.ShapeDtypeStruct((8, t), F32),
            jax.ShapeDtypeStruct((N_EXPERTS, LANES), I32),
        ],
        scratch_shapes=[pltpu.VMEM((N_EXPERTS, LANES), F32)],
        compiler_params=_cparams(("arbitrary",)),
        name="route",
    )(x2d, att, hm, pw["w_out"], pw["g_ffn"], pw["w_r"], pw["b_r"])


def _row_copy(src, s, dst, d, sem):
    return pltpu.make_async_copy(src.at[pl.ds(s, 1)], dst.at[pl.ds(d, 1)], sem)


DMA_UNROLL = 8


N_ZERO_BLOCKS = 2 * N_EXPERTS


def _dispatch_kernel(zb_ref, dest_ref, hp_ref, xs_ref, zero_scr, sem, zsem, *, tm, bm):
    @pl.when(pl.program_id(0) == 0)
    def _():
        zero_scr[...] = jnp.zeros_like(zero_scr)

        def zcopy(k):
            return pltpu.make_async_copy(zero_scr, xs_ref.at[pl.ds(pl.multiple_of(zb_ref[k] * bm, bm), bm)], zsem)

        def zstart(k, carry):
            @pl.when(zb_ref[k] >= 0)
            def _():
                zcopy(k).start()
            return carry

        def zwait(k, carry):
            @pl.when(zb_ref[k] >= 0)
            def _():
                zcopy(k).wait()
            return carry

        lax.fori_loop(0, N_ZERO_BLOCKS, zstart, 0)
        lax.fori_loop(0, N_ZERO_BLOCKS, zwait, 0)

    def body(g, carry):
        for u in range(DMA_UNROLL):
            t = g * DMA_UNROLL + u
            _row_copy(hp_ref, t, xs_ref, dest_ref[0, t], sem).start(priority=0)
            _row_copy(hp_ref, t, xs_ref, dest_ref[1, t], sem).start(priority=1)
        return carry

    lax.fori_loop(0, tm // DMA_UNROLL, body, 0)
    pltpu.make_async_copy(xs_ref.at[pl.ds(0, 2 * tm)], xs_ref.at[pl.ds(0, 2 * tm)], sem).wait()


def _dispatch(zero_blocks, dest, hp, n_slots, tm, bm):
    t = hp.shape[0]
    return pl.pallas_call(
        functools.partial(_dispatch_kernel, tm=tm, bm=bm),
        grid_spec=pltpu.PrefetchScalarGridSpec(
            num_scalar_prefetch=1,
            grid=(t // tm,),
            in_specs=[
                pl.BlockSpec((8, tm), lambda i, zb: (0, i), memory_space=pltpu.SMEM),
                pl.BlockSpec((tm, D_MODEL // 2), lambda i, zb: (i, 0)),
            ],
            out_specs=pl.BlockSpec(memory_space=pl.ANY),
            scratch_shapes=[
                pltpu.VMEM((bm, D_MODEL // 2), U32),
                pltpu.SemaphoreType.DMA,
                pltpu.SemaphoreType.DMA,
            ],
        ),
        out_shape=jax.ShapeDtypeStruct((n_slots, D_MODEL // 2), U32),
        compiler_params=_cparams(("arbitrary",)),
        name="dispatch",
    )(zero_blocks, dest, hp)


def _expert_kernel(be_ref, nu_ref, xs_ref, w1_ref, w3_ref, w2_ref, ys_ref, w1_scr, w3_scr, w2_scr):
    i = pl.program_id(0)
    new_expert = jnp.logical_or(i == 0, be_ref[i] != be_ref[jnp.maximum(i - 1, 0)])

    @pl.when(jnp.logical_and(new_expert, i < nu_ref[0]))
    def _():
        w1_scr[...] = w1_ref[0].astype(BF16)
        w3_scr[...] = w3_ref[0].astype(BF16)
        w2_scr[...] = w2_ref[0].astype(BF16)

    @pl.when(i < nu_ref[0])
    def _():
        xw = xs_ref[...]
        xa = lax.bitcast_convert_type(xw & jnp.uint32(0xFFFF0000), F32).astype(BF16)
        xb = lax.bitcast_convert_type(xw << 16, F32).astype(BF16)
        half = D_MODEL // 2
        h1 = _dot(xa, w1_scr[:half, :]) + _dot(xb, w1_scr[half:, :])
        h3 = _dot(xa, w3_scr[:half, :]) + _dot(xb, w3_scr[half:, :])
        a = (h1 * jax.nn.sigmoid(h1)) * h3
        ys_ref[...] = _dot(a.astype(BF16), w2_scr[...])

    @pl.when(pl.program_id(0) >= nu_ref[0])
    def _():
        ys_ref[...] = jnp.zeros_like(ys_ref)


def _experts(blk_e, n_used, xs, pw, bm):
    n_slots = xs.shape[0]
    nblk = n_slots // bm
    blk = lambda i, be, nu: (jnp.minimum(i, nu[0] - 1), 0)
    oblk = lambda i, be, nu: (i, 0)
    wsel = lambda i, be, nu: (be[i], 0, 0)
    return pl.pallas_call(
        _expert_kernel,
        grid_spec=pltpu.PrefetchScalarGridSpec(
            num_scalar_prefetch=2,
            grid=(nblk,),
            in_specs=[
                pl.BlockSpec((bm, D_MODEL // 2), blk),
                pl.BlockSpec((1, D_MODEL, EXPERT_FF), wsel),
                pl.BlockSpec((1, D_MODEL, EXPERT_FF), wsel),
                pl.BlockSpec((1, EXPERT_FF, D_MODEL), wsel),
            ],
            out_specs=pl.BlockSpec((bm, D_MODEL), oblk),
            scratch_shapes=[
                pltpu.VMEM((D_MODEL, EXPERT_FF), BF16),
                pltpu.VMEM((D_MODEL, EXPERT_FF), BF16),
                pltpu.VMEM((EXPERT_FF, D_MODEL), BF16),
            ],
        ),
        out_shape=jax.ShapeDtypeStruct((n_slots, D_MODEL), F32),
        compiler_params=_cparams(("arbitrary",)),
        name="experts",
    )(blk_e, n_used, xs, pw["w1"], pw["w3"], pw["w2"])


def _combine_kernel(dest_ref, x1_ref, rg_ref, ys_ref, y_ref, r0_scr, r1_scr, sem, *, tm):
    def body(g, carry):
        for u in range(DMA_UNROLL):
            t = g * DMA_UNROLL + u
            _row_copy(ys_ref, dest_ref[0, t], r0_scr, t, sem).start(priority=0)
            _row_copy(ys_ref, dest_ref[1, t], r1_scr, t, sem).start(priority=1)
        return carry

    lax.fori_loop(0, tm // DMA_UNROLL, body, 0)
    gt =jnp.concatenate([rg_ref[...], jnp.zeros((LANES - 8, tm), F32)], axis=0).T
    pltpu.make_async_copy(ys_ref.at[pl.ds(0, tm)], r0_scr, sem).wait()
    pltpu.make_async_copy(ys_ref.at[pl.ds(0, tm)], r1_scr, sem).wait()
    y_ref[...] = x1_ref[...] + gt[:, 0:1] * r0_scr[...] + gt[:, 1:2] * r1_scr[...]


def _combine(dest, x1, rg, ys, tm):
    t = x1.shape[0]
    row = lambda i: (i, 0)
    col = lambda i: (0, i)
    return pl.pallas_call(
        functools.partial(_combine_kernel, tm=tm),
        grid=(t // tm,),
        in_specs=[
            pl.BlockSpec((8, tm), col, memory_space=pltpu.SMEM),
            pl.BlockSpec((tm, D_MODEL), row),
            pl.BlockSpec((8, tm), col),
            pl.BlockSpec(memory_space=pl.ANY),
        ],
        out_specs=pl.BlockSpec((tm, D_MODEL), row),
        out_shape=jax.ShapeDtypeStruct((t, D_MODEL), F32),
        scratch_shapes=[
            pltpu.VMEM((tm, D_MODEL), F32),
            pltpu.VMEM((tm, D_MODEL), F32),
            pltpu.SemaphoreType.DMA,
        ],
        compiler_params=_cparams(("arbitrary",)),
        name="combine",
    )(dest, x1, rg, ys)


def _prep_weights(g_attn, w_in, g_cq, w_uq, g_ckv, w_ukv, g_q, g_k, b_igate, b_fgate, g_mh,
                  w_out, g_ffn, w_group, b_group, w_erouter, b_erouter, w1, w3, w2):
    def cols(a, b):
        return w_in[:, a:b]

    o_cq, o_ckv, o_kr = 0, 256, 384
    o_mq, o_mk, o_mv, o_mi, o_mf, o_mo = 416, 672, 928, 1440, 1444, 1448
    hr = MLA_ROPE // 2
    misc = jnp.concatenate([
        cols(o_kr, o_kr + MLA_ROPE), -cols(o_kr + hr, o_kr + MLA_ROPE), cols(o_kr, o_kr + hr),
        cols(o_mi, o_mi + M_HEADS), cols(o_mf, o_mf + M_HEADS),
        jnp.zeros((D_MODEL, LANES - 2 * MLA_ROPE - 2 * M_HEADS), F32)], axis=1)
    w_p = jnp.concatenate([
        cols(o_cq, o_cq + 256), cols(o_ckv, o_ckv + 128), cols(o_mq, o_mq + 256), cols(o_mk, o_mk + 256),
        cols(o_mv, o_mv + 512), cols(o_mo, o_mo + 512), misc], axis=1).astype(BF16)
    nope, r1, r2 = w_uq[..., :MLA_NOPE], w_uq[..., MLA_NOPE:MLA_NOPE + hr], w_uq[..., MLA_NOPE + hr:]
    w_q = jnp.concatenate([nope, r1, r2, -r2, r1], axis=-1).reshape(MLA_Q_LORA, MLA_HEADS * HEAD_PAD).astype(BF16)
    pad_g = jnp.zeros((HEAD_PAD - MLA_QK,), F32)
    w_k = jnp.concatenate([w_ukv[..., :MLA_NOPE], jnp.zeros((MLA_KV_LORA, MLA_HEADS, HEAD_PAD - MLA_NOPE), F32)],
                          axis=-1).transpose(1, 0, 2).astype(BF16)
    w_v = jnp.concatenate([w_ukv[..., MLA_NOPE:], jnp.zeros((MLA_KV_LORA, MLA_HEADS, LANES - MLA_V), F32)],
                          axis=-1).transpose(1, 0, 2).astype(BF16)
    gate_bias = jnp.concatenate([jnp.zeros((_L_IG,), F32), b_igate, b_fgate,
                                 jnp.zeros((LANES - _L_FG - M_HEADS,), F32)])[None]
    w_r = jnp.concatenate([w_erouter.T, w_group.T, jnp.zeros((R_ROWS - N_EXPERTS - N_GROUPS, D_MODEL), F32)],
                          axis=0).astype(BF16)
    b_r = jnp.concatenate([b_erouter, b_group, jnp.zeros((R_ROWS - N_EXPERTS - N_GROUPS,), F32)])
    return {
        "w_in": w_p, "g_attn": g_attn[None], "g_cq": g_cq[None], "w_q": w_q, "g_ckv": g_ckv[None],
        "g_q": jnp.concatenate([g_q, pad_g])[None], "g_k": jnp.concatenate([g_k, pad_g])[None],
        "gate_bias": gate_bias, "w_k": w_k, "w_v": w_v, "g_mh": g_mh,
        "w_out": w_out.astype(BF16), "g_ffn": g_ffn[None], "w_r": w_r,
        "b_r": jnp.broadcast_to(b_r[:, None], (R_ROWS, LANES)),
        "w1": w1, "w3": w3, "w2": w2,
    }


def _rope_table(pos):
    half = MLA_ROPE // 2
    inv = ROPE_BASE ** (-np.arange(half, dtype=np.float64) / half)
    ang = np.asarray(pos, np.float64)[:, None] * inv[None, :]
    cos = np.cos(ang)
    sin = np.sin(ang)
    c2 = np.concatenate([cos, cos], axis=1)
    s2 = np.concatenate([sin, sin], axis=1)
    return jnp.asarray(np.concatenate([c2, s2, c2, s2], axis=1), F32)


def _pick(n, pref):
    return pref if n % pref == 0 else n


def _slots_kernel(ri_ref, ps_ref, d_ref):
    tm = ri_ref.shape[1]
    xi = lax.broadcasted_iota(I32, (N_EXPERTS, tm), 0)
    ps = ps_ref[:, 0:1]
    rows = []
    for k in range(2):
        start = jnp.sum(jnp.where(xi == ri_ref[k:k + 1, :], ps, 0), axis=0, keepdims=True)
        rows.append(start + ri_ref[2 + k:3 + k, :])
    d_ref[...] = jnp.concatenate(rows + [jnp.zeros((6, tm), I32)], axis=0)


def _slots(ri, pstart_b, tm):
    t = ri.shape[1]
    return pl.pallas_call(
        _slots_kernel,
        grid=(t // tm,),
        in_specs=[pl.BlockSpec((8, tm), lambda i: (0, i)), pl.BlockSpec((N_EXPERTS, LANES), lambda i: (0, 0))],
        out_specs=pl.BlockSpec((8, tm), lambda i: (0, i)),
        out_shape=jax.ShapeDtypeStruct((8, t), I32),
        compiler_params=_cparams(("arbitrary",)),
        name="slots",
    )(ri, pstart_b)


def _moe_layer(x2d, att, hm, pw, *, tm_route, bm, tm_disp, tm_comb):
    t = x2d.shape[0]
    x1, hp, ri, rg, cnt = _route(x2d, att, hm, pw, tm_route)
    counts = cnt[:, 0]
    padded = (counts + bm - 1) // bm * bm
    pend = jnp.cumsum(padded)
    pstart = pend - padded
    dest = _slots(ri, jnp.broadcast_to(pstart[:, None], (N_EXPERTS, LANES)), _pick(t, 2048))
    n_slots = (2 * t // bm + N_EXPERTS) * bm
    nblk = n_slots // bm
    n_used = (pend[-1] // bm).astype(I32)
    experts = jnp.arange(N_EXPERTS, dtype=I32)
    blk_first = jnp.arange(nblk, dtype=I32) * bm
    blk_e = jnp.sum((pend[None, :] <= blk_first[:, None]).astype(I32), axis=1)
    blk_e = jnp.minimum(blk_e, jnp.max(jnp.where(counts > 0, experts, 0)))
    zero_blocks = jnp.concatenate([jnp.where(counts > 0, pend // bm - 1, -1),
                                   jnp.where(n_used + experts < nblk, n_used + experts, -1)]).astype(I32)
    xs = _dispatch(zero_blocks, dest, hp, n_slots, tm_disp, bm)
    ys = _experts(blk_e, n_used[None], xs, pw, bm)
    return _combine(dest, x1, rg, ys, tm_comb)


def _state_pack(c, n):
    ct = jnp.swapaxes(c, -1, -2)
    s = jnp.concatenate([ct, jnp.broadcast_to(n[..., None], ct.shape[:-1] + (S_W - M_DV,))], axis=-1)
    return s.reshape((s.shape[0],) + S_SHAPE)


def _state_unpack(s, m):
    s = s.reshape(s.shape[0], M_HEADS, M_DK, S_W)
    return jnp.swapaxes(s[..., :M_DV], -1, -2), s[..., M_DV], m[:, :M_HEADS, 0]


def kernel(x_prompt, x_sample, cache_ckv, cache_krope, state_mlstm_c, state_mlstm_n, state_mlstm_m,
           meta_tokens, g_attn, w_in, g_cq, w_uq, g_ckv, w_ukv, g_q, g_k, b_igate, b_fgate, g_mh,
           w_out, g_ffn, w_group, b_group, w_erouter, b_erouter, w1, w3, w2):
    bp, seq = x_prompt.shape[:2]
    bs, dec = x_sample.shape[:2]
    past = cache_ckv.shape[2]
    layer = 0
    pw = _prep_weights(g_attn[layer], w_in[layer], g_cq[layer], w_uq[layer], g_ckv[layer], w_ukv[layer],
                       g_q[layer], g_k[layer], b_igate[layer], b_fgate[layer], g_mh[layer], w_out[layer],
                       g_ffn[layer], w_group[layer], b_group[layer], w_erouter[layer], b_erouter[layer],
                       w1[layer], w3[layer], w2[layer])

    xm = jnp.concatenate([meta_tokens, jnp.zeros((META_PAD - N_META, D_MODEL), F32)], axis=0)
    tab_m = _rope_table(np.arange(META_PAD) - N_META)
    _, ckv_m, misc_m, mq_m, mk_m, mv_m, og_m = _project(xm, tab_m, pw, META_PAD)
    zero_s = jnp.zeros((1,) + S_SHAPE, F32)
    zero_m = jnp.zeros((1, 8, LANES), F32)
    _, s_meta, m_meta = _mlstm(mq_m, mk_m, mv_m, og_m, misc_m, zero_s, zero_m, pw["g_mh"],
                               nb=1, l=META_PAD, n_valid=N_META)

    tp = bp * seq
    xp2 = x_prompt.reshape(tp, D_MODEL)
    tm_p = _pick(seq, 512)
    q_p, ckv_p, misc_p, mq_p, mk_p, mv_p, og_p = _project(xp2, _rope_table(np.arange(seq)), pw, tm_p)
    tq = _pick(seq, 512)
    att_p = _attention(q_p, ckv_p, misc_p, ckv_m[None], misc_m[None], pw, nb=bp, lq=seq, lf=seq, tq=tq, tk=tq,
                       causal=True, n_tail=N_META)
    hm_p, s_p, m_p = _mlstm(mq_p, mk_p, mv_p, og_p, misc_p, s_meta, m_meta, pw["g_mh"],
                            nb=bp, l=seq)
    y_p = _moe_layer(xp2, att_p, hm_p, pw, tm_route=tm_p, bm=512, tm_disp=tm_p, tm_comb=_pick(seq, 256))

    ts = bs * dec
    xs2 = x_sample.reshape(ts, D_MODEL)
    q_s, ckv_s, misc_s, mq_s, mk_s, mv_s, og_s = _project(xs2, _rope_table(past + np.arange(dec)), pw, dec)
    kr_cache = jnp.concatenate([cache_krope[layer], jnp.zeros((bs, past, LANES - MLA_ROPE), F32)], axis=-1)
    n_tail = dec + N_META
    assert n_tail <= META_PAD

    def tail_rows(own, meta):
        w = own.shape[-1]
        return jnp.concatenate([own.reshape(bs, dec, w), jnp.broadcast_to(meta[None, :N_META], (bs, N_META, w)),
                                jnp.zeros((bs, META_PAD - n_tail, w), F32)], axis=1)

    att_s = _attention(q_s, cache_ckv[layer].reshape(bs * past, MLA_KV_LORA), kr_cache.reshape(bs * past, LANES),
                       tail_rows(ckv_s, ckv_m), tail_rows(misc_s, misc_m), pw, nb=bs, lq=dec, lf=past, tq=dec,
                       tk=_pick(past, 256), causal=False, n_tail=n_tail)
    s0 = _state_pack(state_mlstm_c[layer], state_mlstm_n[layer])
    m0 = jnp.concatenate([jnp.broadcast_to(state_mlstm_m[layer][:, :, None], (bs, M_HEADS, LANES)),
                          jnp.zeros((bs, 8 - M_HEADS, LANES), F32)], axis=1)
    hm_s, s_s, m_s = _mlstm(mq_s, mk_s, mv_s, og_s, misc_s, s0, m0, pw["g_mh"], nb=bs, l=dec)
    y_s = _moe_layer(xs2, att_s, hm_s, pw, tm_route=_pick(ts, 512), bm=128, tm_disp=_pick(ts, 512),
                     tm_comb=_pick(ts, 256))

    m_ckv = ckv_m[:N_META]
    m_kr = misc_m[:N_META, :MLA_ROPE]
    new_ckv_p = jnp.concatenate([jnp.broadcast_to(m_ckv[None], (bp, N_META, MLA_KV_LORA)),
                                 ckv_p.reshape(bp, seq, MLA_KV_LORA)], axis=1)[None]
    new_kr_p = jnp.concatenate([jnp.broadcast_to(m_kr[None], (bp, N_META, MLA_ROPE)),
                                misc_p[:, :MLA_ROPE].reshape(bp, seq, MLA_ROPE)], axis=1)[None]
    c_p, n_p, mm_p = _state_unpack(s_p, m_p)
    c_s, n_s, mm_s = _state_unpack(s_s, m_s)
    return (y_p.reshape(bp, seq, D_MODEL), y_s.reshape(bs, dec, D_MODEL),
            new_ckv_p, new_kr_p, c_p[None], n_p[None], mm_p[None],
            ckv_s.reshape(bs, dec, MLA_KV_LORA)[None], misc_s[:, :MLA_ROPE].reshape(bs, dec, MLA_ROPE)[None],
            c_s[None], n_s[None], mm_s[None])
```

```python
import functools

import numpy as np
import jax
import jax.numpy as jnp
from jax import lax
from jax.experimental import pallas as pl
from jax.experimental.pallas import tpu as pltpu

F32 = jnp.float32
BF16 = jnp.bfloat16
I32 = jnp.int32
U32 = jnp.uint32

D_MODEL = 1024
CHUNK = 64
N_META = 16
MLA_HEADS = 8
MLA_V = 64
MLA_NOPE = 64
MLA_ROPE = 32
MLA_QK = MLA_NOPE + MLA_ROPE
MLA_Q_LORA = 256
MLA_KV_LORA = 128
MLA_SCALE = MLA_QK ** -0.5
ROPE_BASE = 10000.0
M_HEADS = 4
M_DV = 128
M_DK = 64
N_GROUPS = 4
E_PER_GROUP = 8
N_EXPERTS = 32
EXPERT_FF = 512
EPS = 1e-6

LANES = 128
HEAD_PAD = 128
PROJ_PAD = 2048
META_PAD = 128
NEG_BIG = -1e30
VMEM_LIMIT = 56 * 1024 * 1024

_O_CQ, _O_CKV, _O_MQ, _O_MK, _O_MV, _O_MO, _O_MISC = 0, 256, 384, 640, 896, 1408, 1920
_L_IG, _L_FG = 64, 68


def _cparams(sem):
    return pltpu.CompilerParams(dimension_semantics=sem, vmem_limit_bytes=VMEM_LIMIT)


def _dot(a, b):
    return jnp.dot(a, b, preferred_element_type=F32)


def _dot_nt(a, b):
    return lax.dot_general(a, b, (((1,), (1,)), ((), ())), preferred_element_type=F32)


def _split3(x):
    x1 = x.astype(BF16)
    r1 = x - x1.astype(F32)
    x2 = r1.astype(BF16)
    x3 = (r1 - x2.astype(F32)).astype(BF16)
    return x1, x2, x3


def _proj_kernel(x_ref, tab_ref, w_ref, g_ref, gcq_ref, wq_ref, gckv_ref, gq_ref, bias_ref,
                 q_ref, ckv_ref, misc_ref, mq_ref, mk_ref, mv_ref, og_ref):
    x = x_ref[...]
    xn = x * lax.rsqrt(jnp.mean(x * x, axis=-1, keepdims=True) + EPS) * g_ref[...]
    z = _dot(xn.astype(BF16), w_ref[...])
    tab = tab_ref[...]
    lane = lax.broadcasted_iota(I32, tab.shape, 1)

    cq = z[:, _O_CQ:_O_CQ + MLA_Q_LORA]
    cqn = cq * lax.rsqrt(jnp.mean(cq * cq, axis=-1, keepdims=True) + EPS) * gcq_ref[...]
    qz = _dot(cqn.astype(BF16), wq_ref[...])
    gq = gq_ref[...]
    for h in range(MLA_HEADS):
        zh = qz[:, h * HEAD_PAD:(h + 1) * HEAD_PAD]
        y = zh * tab
        rot = y + pltpu.roll(y, LANES - MLA_ROPE, 1)
        qh = jnp.where(lane < MLA_NOPE, zh, jnp.where(lane < MLA_QK, rot, 0.0))
        ms = jnp.sum(qh * qh, axis=-1, keepdims=True) * (1.0 / MLA_QK)
        q_ref[h] = (qh * lax.rsqrt(ms + EPS) * gq).astype(BF16)

    ckv = z[:, _O_CKV:_O_CKV + MLA_KV_LORA]
    ckv_ref[...] = ckv * lax.rsqrt(jnp.mean(ckv * ckv, axis=-1, keepdims=True) + EPS) * gckv_ref[...]

    zm = z[:, _O_MISC:_O_MISC + LANES]
    y = zm * tab
    rot = y + pltpu.roll(y, LANES - MLA_ROPE, 1)
    gate = zm + bias_ref[...]
    logf = jnp.minimum(gate, 0.0) - jnp.log1p(jnp.exp(-jnp.abs(gate)))
    misc = jnp.where(lane < MLA_ROPE, rot,
                     jnp.where((lane >= _L_IG) & (lane < _L_FG), gate,
                               jnp.where((lane >= _L_FG) & (lane < _L_FG + M_HEADS), logf, 0.0)))
    misc_ref[...] = misc

    mq_ref[...] = z[:, _O_MQ:_O_MQ + M_HEADS * M_DK].astype(BF16)
    mk_ref[...] = (z[:, _O_MK:_O_MK + M_HEADS * M_DK] * (M_DK ** -0.5)).astype(BF16)
    mv_ref[...] = z[:, _O_MV:_O_MV + M_HEADS * M_DV].astype(BF16)
    og_ref[...] = jax.nn.sigmoid(z[:, _O_MO:_O_MO + M_HEADS * M_DV]).astype(BF16)


def _project(x2d, tab, pw, tm):
    t = x2d.shape[0]
    nt = t // tm
    ntab = tab.shape[0] // tm
    row = lambda i: (i, 0)
    full = lambda i: (0, 0)
    return pl.pallas_call(
        _proj_kernel,
        grid=(nt,),
        in_specs=[
            pl.BlockSpec((tm, D_MODEL), row),
            pl.BlockSpec((tm, LANES), lambda i: (i % ntab, 0)),
            pl.BlockSpec((D_MODEL, PROJ_PAD), full),
            pl.BlockSpec((1, D_MODEL), full),
            pl.BlockSpec((1, MLA_Q_LORA), full),
            pl.BlockSpec((MLA_Q_LORA, MLA_HEADS * HEAD_PAD), full),
            pl.BlockSpec((1, MLA_KV_LORA), full),
            pl.BlockSpec((1, HEAD_PAD), full),
            pl.BlockSpec((1, LANES), full),
        ],
        out_specs=[
            pl.BlockSpec((MLA_HEADS, tm, HEAD_PAD), lambda i: (0, i, 0)),
            pl.BlockSpec((tm, MLA_KV_LORA), row),
            pl.BlockSpec((tm, LANES), row),
            pl.BlockSpec((tm, M_HEADS * M_DK), row),
            pl.BlockSpec((tm, M_HEADS * M_DK), row),
            pl.BlockSpec((tm, M_HEADS * M_DV), row),
            pl.BlockSpec((tm, M_HEADS * M_DV), row),
        ],
        out_shape=[
            jax.ShapeDtypeStruct((MLA_HEADS, t, HEAD_PAD), BF16),
            jax.ShapeDtypeStruct((t, MLA_KV_LORA), F32),
            jax.ShapeDtypeStruct((t, LANES), F32),
            jax.ShapeDtypeStruct((t, M_HEADS * M_DK), BF16),
            jax.ShapeDtypeStruct((t, M_HEADS * M_DK), BF16),
            jax.ShapeDtypeStruct((t, M_HEADS * M_DV), BF16),
            jax.ShapeDtypeStruct((t, M_HEADS * M_DV), BF16),
        ],
        compiler_params=_cparams(("arbitrary",)),
        name="projection",
    )(x2d, tab, pw["w_in"], pw["g_attn"], pw["g_cq"], pw["w_q"], pw["g_ckv"], pw["g_q"], pw["gate_bias"])


SAFE_BOUND = 40.0
LOG2E = 1.4426950408889634


def _attn_kernel(sb_ref, q_ref, ckv_ref, misc_ref, ckvt_ref, misct_ref, wk_ref, wv_ref, gk_ref, o_ref,
                 kt_scr, v_scr, acc_scr, p_scr, *, lf, tq, tk, bt, causal, n_tail):
    i = pl.program_id(1)
    gk = gk_ref[...]

    def build(ckv_rows, misc_rows, dst):
        n = ckv_rows.shape[0]
        lane = lax.broadcasted_iota(I32, (n, LANES), 1)
        cb = ckv_rows.astype(BF16)
        krp = jnp.where((lane >= MLA_NOPE) & (lane < MLA_QK), pltpu.roll(misc_rows, MLA_NOPE, 1), 0.0)
        onecol = (lane == MLA_V).astype(F32)
        for h in range(MLA_HEADS):
            kk = _dot(cb, wk_ref[h]) + krp
            ms = jnp.sum(kk * kk, axis=-1, keepdims=True) * (1.0 / MLA_QK)
            kt_scr[h, :, pl.ds(dst, n)] = (kk * lax.rsqrt(ms + EPS) * gk).T.astype(BF16)
            v_scr[h, pl.ds(dst, n), :] = (_dot(cb, wv_ref[h]) + onecol).astype(BF16)

    @pl.when(i == 0)
    def _():
        def body(r, carry):
            r0 = pl.multiple_of(r * bt, bt)
            build(ckv_ref[pl.ds(r0, bt), :], misc_ref[pl.ds(r0, bt), :], r0)
            return carry

        lax.fori_loop(0, lf // bt, body, 0)
        build(ckvt_ref[0], misct_ref[0], lf)

    ndiag = tq // tk if causal else 0
    nfull = i * ndiag if causal else lf // tk
    tail_mask = lax.broadcasted_iota(I32, (tq, META_PAD), 1) < n_tail

    def kt_tile(h, j):
        return kt_scr[h, :, pl.ds(pl.multiple_of(j * tk, tk), tk)]

    def v_tile(h, j):
        return v_scr[h, pl.ds(pl.multiple_of(j * tk, tk), tk), :]

    def chunk_ids(lo):
        rows = lax.broadcasted_iota(I32, (tq - lo, tk), 0) // CHUNK
        cols = lax.broadcasted_iota(I32, (tq - lo, tk), 1) // CHUNK
        return rows, cols

    def finish(h):
        acc = acc_scr[h]
        o_ref[:, h * MLA_V:(h + 1) * MLA_V] = (acc[:, :MLA_V] / acc[:, MLA_V:MLA_V + 1]).astype(BF16)

    safe = sb_ref[0] <= SAFE_BOUND * LOG2E

    @pl.when(safe)
    def _():
        bound = sb_ref[0]

        def scores(h, lo, kt):
            return _dot(q_ref[h, lo:, :], kt) * (MLA_SCALE * LOG2E) - bound

        def probs(s, mask):
            if mask is not None:
                s = jnp.where(mask, s, -jnp.inf)
            return jnp.exp2(s).astype(BF16)

        def pipe_step(j, lo, lo_prev, mask):
            ss = [scores(h, lo, kt_tile(h, j)) for h in range(MLA_HEADS)]
            for h in range(MLA_HEADS):
                acc_scr[h, lo_prev:, :] += _dot(p_scr[(j - 1) % 2, h, lo_prev:, :], v_tile(h, j - 1))
            for h in range(MLA_HEADS):
                p_scr[j % 2, h, lo:, :] = probs(ss[h], mask)

        for h in range(MLA_HEADS):
            p_tail = probs(scores(h, 0, kt_scr[h, :, lf:lf + META_PAD]), tail_mask)
            acc_scr[h] = _dot(p_tail, v_scr[h, lf:lf + META_PAD, :])
        rows0, cols0 = chunk_ids(0)
        if causal:
            first_mask = cols0 <= rows0 + jnp.where(nfull > 0, tk, 0)
        else:
            first_mask = None
        for h in range(MLA_HEADS):
            p_scr[0, h] = probs(scores(h, 0, kt_tile(h, 0)), first_mask)

        def body(j, carry):
            pipe_step(j, 0, 0, None)
            return carry

        lax.fori_loop(1, nfull, body, 0)
        if causal:
            @pl.when(nfull >= 1)
            def _():
                pipe_step(nfull, 0, 0, cols0 <= rows0)
            for d in range(1, ndiag):
                rows_d, cols_d = chunk_ids(d * tk)
                pipe_step(nfull + d, d * tk, (d - 1) * tk, cols_d <= rows_d)
            last, lo_last = nfull + ndiag - 1, (ndiag - 1) * tk
        else:
            last, lo_last = nfull - 1, 0
        for h in range(MLA_HEADS):
            acc_scr[h, lo_last:, :] += _dot(p_scr[last % 2, h, lo_last:, :], v_tile(h, last))
            finish(h)

    @pl.when(jnp.logical_not(safe))
    def _():
        rows0, cols0 = chunk_ids(0)
        for h in range(MLA_HEADS):
            qh = q_ref[h]

            def step(carry, kt, vrows, mask):
                m, acc = carry
                s = _dot(qh, kt) * MLA_SCALE
                if mask is not None:
                    s = jnp.where(mask, s, -jnp.inf)
                m_new = jnp.maximum(m, jnp.max(s, axis=-1, keepdims=True))
                acc = jnp.exp(m - m_new) * acc + _dot(jnp.exp(s - m_new).astype(BF16), vrows)
                return m_new, acc

            def body(j, c):
                mask = (cols0 + j * (tk // CHUNK) <= rows0 + i * (tq // CHUNK)) if causal else None
                return step(c, kt_tile(h, j), v_tile(h, j), mask)

            carry = (jnp.full((tq, 1), -jnp.inf, F32), jnp.zeros((tq, LANES), F32))
            carry = step(carry, kt_scr[h, :, lf:lf + META_PAD], v_scr[h, lf:lf + META_PAD, :], tail_mask)
            carry = lax.fori_loop(0, nfull + ndiag, body, carry)
            acc_scr[h] = carry[1]
            finish(h)


def _attention(q, ckv_f, misc_f, ckv_t, misc_t, pw, *, nb, lq, lf, tq, tk, causal, n_tail):
    nq = lq // tq
    bt = min(512, lf)
    assert lf % bt == 0 and lf % tk == 0 and lq % tq == 0 and tk % CHUNK == 0 and (not causal or tq % tk == 0)
    kern = functools.partial(_attn_kernel, lf=lf, tq=tq, tk=tk, bt=bt, causal=causal, n_tail=n_tail)
    full2 = lambda b, i, sb: (0, 0)
    full3 = lambda b, i, sb: (0, 0, 0)
    tail = full3 if ckv_t.shape[0] == 1 else (lambda b, i, sb: (b, 0, 0))
    lk = lf + META_PAD
    return pl.pallas_call(
        kern,
        grid_spec=pltpu.PrefetchScalarGridSpec(
            num_scalar_prefetch=1,
            grid=(nb, nq),
            in_specs=[
                pl.BlockSpec((MLA_HEADS, tq, HEAD_PAD), lambda b, i, sb: (0, b * nq + i, 0)),
                pl.BlockSpec((lf, MLA_KV_LORA), lambda b, i, sb: (b, 0)),
                pl.BlockSpec((lf, LANES), lambda b, i, sb: (b, 0)),
                pl.BlockSpec((1, META_PAD, MLA_KV_LORA), tail),
                pl.BlockSpec((1, META_PAD, LANES), tail),
                pl.BlockSpec((MLA_HEADS, MLA_KV_LORA, HEAD_PAD), full3),
                pl.BlockSpec((MLA_HEADS, MLA_KV_LORA, LANES), full3),
                pl.BlockSpec((1, HEAD_PAD), full2),
            ],
            out_specs=pl.BlockSpec((tq, MLA_HEADS * MLA_V), lambda b, i, sb: (b * nq + i, 0)),
            scratch_shapes=[
                pltpu.VMEM((MLA_HEADS, HEAD_PAD, lk), BF16),
                pltpu.VMEM((MLA_HEADS, lk, LANES), BF16),
                pltpu.VMEM((MLA_HEADS, tq, LANES), F32),
                pltpu.VMEM((2, MLA_HEADS, tq, tk), BF16),
            ],
        ),
        out_shape=jax.ShapeDtypeStruct((nb * lq, MLA_HEADS * MLA_V), BF16),
        compiler_params=_cparams(("arbitrary", "arbitrary")),
        name="attention",
    )(pw["score_bound"], q, ckv_f, misc_f, ckv_t, misc_t, pw["w_k"], pw["w_v"], pw["g_k"])


S_W = 2 * M_DV
S_SHAPE = (M_HEADS // 2, 2 * M_DK, S_W)


MLSTM_CHUNK = LANES


def _mlstm_kernel(q_ref, k_ref, v_ref, og_ref, misc_ref, s0_ref, m0_ref, gmh_ref,
                  h_ref, sout_ref, mout_ref, s_scr, m_scr, *, n_valid):
    c = pl.program_id(1)
    nc = pl.num_programs(1)
    lb = q_ref.shape[0]
    lc = MLSTM_CHUNK

    @pl.when(c == 0)
    def _():
        s_scr[...] = s0_ref[0]
        m_scr[...] = m0_ref[0]

    def rows(x):
        if lb == lc:
            return x
        return jnp.concatenate([x, jnp.zeros((lc - lb,) + x.shape[1:], x.dtype)], axis=0)

    g = rows(misc_ref[...])
    limit = lb if n_valid is None else n_valid - c * lb
    row = lax.broadcasted_iota(I32, g.shape, 0)
    lane = lax.broadcasted_iota(I32, g.shape, 1)
    g = jnp.where(row < limit, g, jnp.where((lane >= _L_IG) & (lane < _L_FG), NEG_BIG, 0.0))
    gt = g.T
    gt8 = gt[_L_IG:_L_IG + 2 * M_HEADS, :]
    rr = lax.broadcasted_iota(I32, (lc, lc), 0)
    cc = lax.broadcasted_iota(I32, (lc, lc), 1)
    causal = cc <= rr
    g1, g2, g3 = _split3(g)
    tri = causal.astype(BF16)
    b_cols = _dot(tri, g1) + _dot(tri, g2) + _dot(tri, g3)
    t1, t2, t3 = _split3(gt8)
    trit = (rr <= cc).astype(BF16)
    b_rows = (_dot(t1, trit) + _dot(t2, trit) + _dot(t3, trit))[M_HEADS:2 * M_HEADS, :]
    a = gt8[0:M_HEADS, :] - b_rows
    m_prev = m_scr[0:M_HEADS, :]
    u_b = [-jnp.maximum(m_prev[h:h + 1, :],
                        jnp.broadcast_to(jnp.max(jnp.where(causal, a[h:h + 1, :], -jnp.inf), axis=-1, keepdims=True),
                                         (lc, LANES))) for h in range(M_HEADS)]
    negm_b = [u_b[h] - jnp.broadcast_to(b_cols[:, _L_FG + h:_L_FG + h + 1], (lc, LANES)) for h in range(M_HEADS)]
    u_last = jnp.concatenate([u_b[h][lc - 1:lc, 0:1] for h in range(M_HEADS)], axis=0)
    m_new = jnp.broadcast_to(b_rows[:, lc - 1:lc] - u_last, (M_HEADS, LANES))
    w_state = jnp.exp(a + u_last)
    decay = jnp.exp(m_prev[:, 0:1] + u_last)

    q = rows(q_ref[...])
    k = rows(k_ref[...])
    v = rows(v_ref[...])
    kt = k.astype(F32).T
    ones_blk = jnp.ones((lc, M_DV), BF16)
    low_half = lax.broadcasted_iota(I32, (lc, LANES), 1) < M_DK

    heads = range(M_HEADS)
    mine = [jnp.logical_not(low_half) if h % 2 else low_half for h in heads]
    q_pair = [q[:, (h // 2) * LANES:(h // 2 + 1) * LANES] for h in heads]
    kh = [jnp.where(mine[h], k[:, (h // 2) * LANES:(h // 2 + 1) * LANES], jnp.zeros_like(q_pair[h])) for h in heads]
    qh = [jnp.where(mine[h], q_pair[h], jnp.zeros_like(q_pair[h])) for h in heads]
    vext = [jnp.concatenate([v[:, h * M_DV:(h + 1) * M_DV], ones_blk], axis=1) for h in heads]
    s_old = [s_scr[j] for j in range(M_HEADS // 2)]
    s_bf = [s.astype(BF16) for s in s_old]

    qk = [_dot_nt(q_pair[h], kh[h]) for h in heads]
    qs = [_dot(qh[h], s_bf[h // 2]) for h in heads]
    ktw = [(kt[h * M_DK:(h + 1) * M_DK, :] * w_state[h:h + 1, :]).astype(BF16) for h in heads]
    upd = [_dot(ktw[h], vext[h]) for h in heads]
    w =[(jnp.where(causal, jnp.exp(a[h:h + 1, :] + u_b[h]), 0.0) * qk[h]).astype(BF16) for h in heads]
    inter = [jnp.exp(m_prev[h:h + 1, :] + u_b[h]) for h in heads]
    r = [_dot(w[h], vext[h]) + jnp.tile(inter[h], (1, 2)) * qs[h] for h in heads]
    num = [r[h][:, :M_DV] for h in heads]
    sq = [num[h] * num[h] for h in heads]
    sq_hi = [s.astype(BF16) for s in sq]
    sq_lo = [(sq[h] - sq_hi[h].astype(F32)).astype(BF16) for h in heads]
    msn = [(_dot(sq_hi[h], ones_blk) + _dot(sq_lo[h], ones_blk)) * (1.0 / M_DV) for h in heads]
    for h in heads:
        rden = 1.0 / jnp.maximum(jnp.abs(r[h][:, M_DV:]), jnp.exp(negm_b[h]))
        scale = rden * lax.rsqrt(rden * rden * msn[h] + EPS)
        hn = num[h] * scale * gmh_ref[h:h + 1, :]
        out = (hn[:lb] * og_ref[:, h * M_DV:(h + 1) * M_DV].astype(F32)).astype(BF16)
        h_ref[:, h * M_DV:(h + 1) * M_DV] = out
    for j in range(M_HEADS // 2):
        new = [decay[h:h + 1, :] * s_old[j][(h % 2) * M_DK:(h % 2 + 1) * M_DK, :] + upd[h] for h in (2 * j, 2 * j + 1)]
        s_scr[j] = jnp.concatenate(new, axis=0)
    m_scr[0:M_HEADS, :] = m_new

    @pl.when(c == nc - 1)
    def _():
        sout_ref[0] = s_scr[...]
        mout_ref[0] = m_scr[...]


def _mlstm(mq, mk, mv, og, misc, s0, m0, gmh, *, nb, l, n_valid=None):
    lc = min(l, MLSTM_CHUNK)
    assert l % lc == 0 and lc % 16 == 0
    nc = l // lc
    shared = s0.shape[0] == 1
    st = (lambda b, c: (0, 0, 0, 0)) if shared else (lambda b, c: (b, 0, 0, 0))
    mst = (lambda b, c: (0, 0, 0)) if shared else (lambda b, c: (b, 0, 0))
    row = lambda b, c: (b * nc + c, 0)
    kern = functools.partial(_mlstm_kernel, n_valid=n_valid)
    return pl.pallas_call(
        kern,
        grid=(nb, nc),
        in_specs=[
            pl.BlockSpec((lc, M_HEADS * M_DK), row),
            pl.BlockSpec((lc, M_HEADS * M_DK), row),
            pl.BlockSpec((lc, M_HEADS * M_DV), row),
            pl.BlockSpec((lc, M_HEADS * M_DV), row),
            pl.BlockSpec((lc, LANES), row),
            pl.BlockSpec((1,) + S_SHAPE, st),
            pl.BlockSpec((1, 8, LANES), mst),
            pl.BlockSpec((M_HEADS, M_DV), lambda b, c: (0, 0)),
        ],
        out_specs=[
            pl.BlockSpec((lc, M_HEADS * M_DV), row),
            pl.BlockSpec((1,) + S_SHAPE, lambda b, c: (b, 0, 0, 0)),
            pl.BlockSpec((1, 8, LANES), lambda b, c: (b, 0, 0)),
        ],
        out_shape=[
            jax.ShapeDtypeStruct((nb * l, M_HEADS * M_DV), BF16),
            jax.ShapeDtypeStruct((nb,) + S_SHAPE, F32),
            jax.ShapeDtypeStruct((nb, 8, LANES), F32),
        ],
        scratch_shapes=[
            pltpu.VMEM(S_SHAPE, F32),
            pltpu.VMEM((8, LANES), F32),
        ],
        compiler_params=_cparams(("arbitrary", "arbitrary")),
        name="mlstm",
    )(mq, mk, mv, og, misc, s0, m0, gmh)


R_ROWS = 40


def _route_kernel(x_ref, att_ref, hm_ref, wo_ref, g_ref, wr_ref, br_ref,
                  x1_ref, hp_ref, ri_ref, rg_ref, cnt_ref, cnt_scr):
    i = pl.program_id(0)
    tm = x_ref.shape[0]

    @pl.when(i == 0)
    def _():
        cnt_scr[...] = jnp.zeros_like(cnt_scr)

    mix = jnp.concatenate([att_ref[...], hm_ref[...]], axis=1)
    x1 = x_ref[...] + _dot(mix, wo_ref[...])
    x1_ref[...] = x1
    hn = x1 * lax.rsqrt(jnp.mean(x1 * x1, axis=-1, keepdims=True) + EPS) * g_ref[...]
    hb = hn.astype(BF16)
    half = D_MODEL // 2
    hi = lax.bitcast_convert_type(hb[:, :half].astype(F32), U32)
    lo = lax.bitcast_convert_type(hb[:, half:].astype(F32), U32)
    hp_ref[...] = (hi & jnp.uint32(0xFFFF0000)) | (lo >> 16)

    logits = _dot_nt(wr_ref[...], hb) + br_ref[:, 0:1]
    e_log = logits[0:N_EXPERTS, :]
    g_log = logits[N_EXPERTS:N_EXPERTS + N_GROUPS, :]
    gmax = jnp.max(g_log, axis=0, keepdims=True)
    gsum = jnp.sum(jnp.exp(g_log - gmax), axis=0, keepdims=True)
    gi = lax.broadcasted_iota(I32, g_log.shape, 0)
    g_idx = jnp.min(jnp.where(g_log == gmax, gi, N_GROUPS), axis=0, keepdims=True)
    e_sel = jnp.zeros((E_PER_GROUP, tm), F32)
    for gg in range(N_GROUPS):
        e_sel = jnp.where(g_idx == gg, e_log[gg * E_PER_GROUP:(gg + 1) * E_PER_GROUP, :], e_sel)
    ei = lax.broadcasted_iota(I32, e_sel.shape, 0)
    m1 = jnp.max(e_sel, axis=0, keepdims=True)
    i1 = jnp.min(jnp.where(e_sel == m1, ei, E_PER_GROUP), axis=0, keepdims=True)
    e2 = jnp.where(ei == i1, -jnp.inf, e_sel)
    m2 = jnp.max(e2, axis=0, keepdims=True)
    i2 = jnp.min(jnp.where(e2 == m2, ei, E_PER_GROUP), axis=0, keepdims=True)
    ex = jnp.exp(m2 - m1)
    gp = 1.0 / gsum
    p1 = 1.0 / (1.0 + ex)
    gate1 = gp * p1
    gate2 = gp * (ex * p1)
    id1 = g_idx * E_PER_GROUP + i1
    id2 = g_idx * E_PER_GROUP + i2

    xi = lax.broadcasted_iota(I32, (N_EXPERTS, tm), 0)
    oh1 = xi == id1
    oh2 = xi == id2
    e_cnt = (oh1 | oh2).astype(F32)
    rr = lax.broadcasted_iota(I32, (tm, tm), 0)
    cc = lax.broadcasted_iota(I32, (tm, tm), 1)
    upper = (rr < cc).astype(BF16)
    pref = _dot(e_cnt.astype(BF16), upper) + cnt_scr[:, 0:1]
    rank1 = jnp.sum(jnp.where(oh1, pref, 0.0), axis=0, keepdims=True)
    rank2 = jnp.sum(jnp.where(oh2, pref, 0.0), axis=0, keepdims=True)
    cnt_new = cnt_scr[...] + jnp.sum(e_cnt, axis=1, keepdims=True)
    cnt_scr[...] = cnt_new
    cnt_ref[...] = cnt_new.astype(I32)

    zi = jnp.zeros((1, tm), I32)
    ri_ref[...] = jnp.concatenate([id1, id2, rank1.astype(I32), rank2.astype(I32), zi, zi, zi, zi], axis=0)
    zf = jnp.zeros((1, tm), F32)
    rg_ref[...] = jnp.concatenate([gate1, gate2, zf, zf, zf, zf, zf, zf], axis=0)


def _route(x2d, att, hm, pw, tm):
    t = x2d.shape[0]
    row = lambda i: (i, 0)
    col = lambda i: (0, i)
    full = lambda i: (0, 0)
    return pl.pallas_call(
        _route_kernel,
        grid=(t // tm,),
        in_specs=[
            pl.BlockSpec((tm, D_MODEL), row),
            pl.BlockSpec((tm, D_MODEL // 2), row),
            pl.BlockSpec((tm, D_MODEL // 2), row),
            pl.BlockSpec((D_MODEL, D_MODEL), full),
            pl.BlockSpec((1, D_MODEL), full),
            pl.BlockSpec((R_ROWS, D_MODEL), full),
            pl.BlockSpec((R_ROWS, LANES), full),
        ],
        out_specs=[
            pl.BlockSpec((tm, D_MODEL), row),
            pl.BlockSpec((tm, D_MODEL // 2), row),
            pl.BlockSpec((8, tm), col),
            pl.BlockSpec((8, tm), col),
            pl.BlockSpec((N_EXPERTS, LANES), full),
        ],
        out_shape=[
            jax.ShapeDtypeStruct((t, D_MODEL), F32),
            jax.ShapeDtypeStruct((t, D_MODEL // 2), U32),
            jax.ShapeDtypeStruct((8, t), I32),
            jax.ShapeDtypeStruct((8, t), F32),
            jax.ShapeDtypeStruct((N_EXPERTS, LANES), I32),
        ],
        scratch_shapes=[pltpu.VMEM((N_EXPERTS, LANES), F32)],
        compiler_params=_cparams(("arbitrary",)),
        name="route",
    )(x2d, att, hm, pw["w_out"], pw["g_ffn"], pw["w_r"], pw["b_r"])


SUBLANES = 8


N_ZERO_BLOCKS = 2 * N_EXPERTS


def _dispatch_kernel(zb_ref, dest_ref, hp_ref, xs_ref, zero_scr, sem, zsem, *, tm, bm):
    @pl.when(pl.program_id(0) == 0)
    def _():
        zero_scr[...] = jnp.zeros_like(zero_scr)

        def zcopy(k):
            return pltpu.make_async_copy(zero_scr, xs_ref.at[pl.ds(pl.multiple_of(zb_ref[k] * bm, bm), bm)], zsem)

        def zstart(k, carry):
            @pl.when(zb_ref[k] >= 0)
            def _():
                zcopy(k).start()
            return carry

        def zwait(k, carry):
            @pl.when(zb_ref[k] >= 0)
            def _():
                zcopy(k).wait()
            return carry

        lax.fori_loop(0, N_ZERO_BLOCKS, zstart, 0)
        lax.fori_loop(0, N_ZERO_BLOCKS, zwait, 0)

    def body(g, carry):
        for u in range(SUBLANES):
            t = g * SUBLANES + u
            src = hp_ref.at[g, pl.ds(u, 1), :]
            pltpu.make_async_copy(src, xs_ref.at[pl.ds(dest_ref[t], 1)], sem).start(priority=0)
            pltpu.make_async_copy(src, xs_ref.at[pl.ds(dest_ref[tm + t], 1)], sem).start(priority=1)
        return carry

    lax.fori_loop(0, tm // SUBLANES, body, 0)
    pltpu.make_async_copy(xs_ref.at[pl.ds(0, 2 * tm)], xs_ref.at[pl.ds(0, 2 * tm)], sem).wait()


def _tile_slots(dest, tm):
    t = dest.shape[1]
    return dest[0:2].reshape(2, t // tm, tm).transpose(1, 0, 2).reshape(2 * t)


def _dispatch(zero_blocks, dest, hp, n_slots, tm, bm):
    t = hp.shape[0]
    return pl.pallas_call(
        functools.partial(_dispatch_kernel, tm=tm, bm=bm),
        grid_spec=pltpu.PrefetchScalarGridSpec(
            num_scalar_prefetch=1,
            grid=(t // tm,),
            in_specs=[
                pl.BlockSpec((2 * tm,), lambda i, zb: (i,), memory_space=pltpu.SMEM),
                pl.BlockSpec((tm // SUBLANES, SUBLANES, D_MODEL // 2), lambda i, zb: (i, 0, 0)),
            ],
            out_specs=pl.BlockSpec(memory_space=pl.ANY),
            scratch_shapes=[
                pltpu.VMEM((bm, D_MODEL // 2), U32),
                pltpu.SemaphoreType.DMA,
                pltpu.SemaphoreType.DMA,
            ],
        ),
        out_shape=jax.ShapeDtypeStruct((n_slots, D_MODEL // 2), U32),
        compiler_params=_cparams(("arbitrary",)),
        name="dispatch",
    )(zero_blocks, _tile_slots(dest, tm), hp.reshape(t // SUBLANES, SUBLANES, D_MODEL // 2))


def _expert_kernel(be_ref, nu_ref, xs_ref, w1_ref, w3_ref, w2_ref, ys_ref, w1_scr, w3_scr, w2_scr):
    i = pl.program_id(0)
    new_expert = jnp.logical_or(i == 0, be_ref[i] != be_ref[jnp.maximum(i - 1, 0)])

    @pl.when(jnp.logical_and(new_expert, i < nu_ref[0]))
    def _():
        w1_scr[...] = w1_ref[0].astype(BF16)
        w3_scr[...] = w3_ref[0].astype(BF16)
        w2_scr[...] = w2_ref[0].astype(BF16)

    @pl.when(i < nu_ref[0])
    def _():
        xw = xs_ref[...]
        xa = lax.bitcast_convert_type(xw & jnp.uint32(0xFFFF0000), F32).astype(BF16)
        xb = lax.bitcast_convert_type(xw << 16, F32).astype(BF16)
        half = D_MODEL // 2
        h1 = _dot(xa, w1_scr[:half, :]) + _dot(xb, w1_scr[half:, :])
        h3 = _dot(xa, w3_scr[:half, :]) + _dot(xb, w3_scr[half:, :])
        a = (h1 * jax.nn.sigmoid(h1)) * h3
        ys_ref[...] = _dot(a.astype(BF16), w2_scr[...])

    @pl.when(pl.program_id(0) >= nu_ref[0])
    def _():
        ys_ref[...] = jnp.zeros_like(ys_ref)


def _experts(blk_e, n_used, xs, pw, bm):
    n_slots = xs.shape[0]
    nblk = n_slots // bm
    blk = lambda i, be, nu: (jnp.minimum(i, nu[0] - 1), 0)
    oblk = lambda i, be, nu: (i, 0)
    wsel = lambda i, be, nu: (be[i], 0, 0)
    return pl.pallas_call(
        _expert_kernel,
        grid_spec=pltpu.PrefetchScalarGridSpec(
            num_scalar_prefetch=2,
            grid=(nblk,),
            in_specs=[
                pl.BlockSpec((bm, D_MODEL // 2), blk),
                pl.BlockSpec((1, D_MODEL, EXPERT_FF), wsel),
                pl.BlockSpec((1, D_MODEL, EXPERT_FF), wsel),
                pl.BlockSpec((1, EXPERT_FF, D_MODEL), wsel),
            ],
            out_specs=pl.BlockSpec((bm, D_MODEL), oblk),
            scratch_shapes=[
                pltpu.VMEM((D_MODEL, EXPERT_FF), BF16),
                pltpu.VMEM((D_MODEL, EXPERT_FF), BF16),
                pltpu.VMEM((EXPERT_FF, D_MODEL), BF16),
            ],
        ),
        out_shape=jax.ShapeDtypeStruct((n_slots, D_MODEL), F32),
        compiler_params=_cparams(("arbitrary",)),
        name="experts",
    )(blk_e, n_used, xs, pw["w1"], pw["w3"], pw["w2"])


def _combine_kernel(dest_ref, x1_ref, rg_ref, ys_ref, y_ref, r0_scr, r1_scr, sem, *, tm):
    def body(g, carry):
        for u in range(SUBLANES):
            t = g * SUBLANES + u
            pltpu.make_async_copy(ys_ref.at[pl.ds(dest_ref[t], 1)], r0_scr.at[g, pl.ds(u, 1), :], sem).start(priority=0)
            pltpu.make_async_copy(ys_ref.at[pl.ds(dest_ref[tm + t], 1)], r1_scr.at[g, pl.ds(u, 1), :],
                                  sem).start(priority=1)
        return carry

    lax.fori_loop(0, tm // SUBLANES, body, 0)
    gt = jnp.concatenate([rg_ref[...], jnp.zeros((LANES - 8, tm), F32)], axis=0).T
    slab = ys_ref.at[pl.ds(0, tm)]
    pltpu.make_async_copy(slab, slab, sem).wait()
    pltpu.make_async_copy(slab, slab, sem).wait()
    r0 = r0_scr[...].reshape(tm, D_MODEL)
    r1 = r1_scr[...].reshape(tm, D_MODEL)
    y_ref[...] = x1_ref[...] + gt[:, 0:1] * r0 + gt[:, 1:2] * r1


def _combine(dest, x1, rg, ys, tm):
    t = x1.shape[0]
    row = lambda i: (i, 0)
    col = lambda i: (0, i)
    return pl.pallas_call(
        functools.partial(_combine_kernel, tm=tm),
        grid=(t // tm,),
        in_specs=[
            pl.BlockSpec((2 * tm,), lambda i: (i,), memory_space=pltpu.SMEM),
            pl.BlockSpec((tm, D_MODEL), row),
            pl.BlockSpec((8, tm), col),
            pl.BlockSpec(memory_space=pl.ANY),
        ],
        out_specs=pl.BlockSpec((tm, D_MODEL), row),
        out_shape=jax.ShapeDtypeStruct((t, D_MODEL), F32),
        scratch_shapes=[
            pltpu.VMEM((tm // SUBLANES, SUBLANES, D_MODEL), F32),
            pltpu.VMEM((tm // SUBLANES, SUBLANES, D_MODEL), F32),
            pltpu.SemaphoreType.DMA,
        ],
        compiler_params=_cparams(("arbitrary",)),
        name="combine",
    )(_tile_slots(dest, tm), x1, rg, ys)


def _prep_weights(g_attn, w_in, g_cq, w_uq, g_ckv, w_ukv, g_q, g_k, b_igate, b_fgate, g_mh,
                  w_out, g_ffn, w_group, b_group, w_erouter, b_erouter, w1, w3, w2):
    def cols(a, b):
        return w_in[:, a:b]

    o_cq, o_ckv, o_kr = 0, 256, 384
    o_mq, o_mk, o_mv, o_mi, o_mf, o_mo = 416, 672, 928, 1440, 1444, 1448
    hr = MLA_ROPE // 2
    misc = jnp.concatenate([
        cols(o_kr, o_kr + MLA_ROPE), -cols(o_kr + hr, o_kr + MLA_ROPE), cols(o_kr, o_kr + hr),
        cols(o_mi, o_mi + M_HEADS), cols(o_mf, o_mf + M_HEADS),
        jnp.zeros((D_MODEL, LANES - 2 * MLA_ROPE - 2 * M_HEADS), F32)], axis=1)
    w_p = jnp.concatenate([
        cols(o_cq, o_cq + 256), cols(o_ckv, o_ckv + 128), cols(o_mq, o_mq + 256), cols(o_mk, o_mk + 256),
        cols(o_mv, o_mv + 512), cols(o_mo, o_mo + 512), misc], axis=1).astype(BF16)
    nope, r1, r2 = w_uq[..., :MLA_NOPE], w_uq[..., MLA_NOPE:MLA_NOPE + hr], w_uq[..., MLA_NOPE + hr:]
    w_q = jnp.concatenate([nope, r1, r2, -r2, r1], axis=-1).reshape(MLA_Q_LORA, MLA_HEADS * HEAD_PAD).astype(BF16)
    pad_g = jnp.zeros((HEAD_PAD - MLA_QK,), F32)
    w_k = jnp.concatenate([w_ukv[..., :MLA_NOPE], jnp.zeros((MLA_KV_LORA, MLA_HEADS, HEAD_PAD - MLA_NOPE), F32)],
                          axis=-1).transpose(1, 0, 2).astype(BF16)
    w_v = jnp.concatenate([w_ukv[..., MLA_NOPE:], jnp.zeros((MLA_KV_LORA, MLA_HEADS, LANES - MLA_V), F32)],
                          axis=-1).transpose(1, 0, 2).astype(BF16)
    gate_bias = jnp.concatenate([jnp.zeros((_L_IG,), F32), b_igate, b_fgate,
                                 jnp.zeros((LANES - _L_FG - M_HEADS,), F32)])[None]
    w_r = jnp.concatenate([w_erouter.T, w_group.T, jnp.zeros((R_ROWS - N_EXPERTS - N_GROUPS, D_MODEL), F32)],
                          axis=0).astype(BF16)
    b_r = jnp.concatenate([b_erouter, b_group, jnp.zeros((R_ROWS - N_EXPERTS - N_GROUPS,), F32)])
    return {
        "w_in": w_p, "g_attn": g_attn[None], "g_cq": g_cq[None], "w_q": w_q, "g_ckv": g_ckv[None],
        "g_q": jnp.concatenate([g_q, pad_g])[None], "g_k": jnp.concatenate([g_k, pad_g])[None],
        "gate_bias": gate_bias, "w_k": w_k, "w_v": w_v, "g_mh": g_mh,
        "w_out": w_out.astype(BF16), "g_ffn": g_ffn[None], "w_r": w_r,
        "b_r": jnp.broadcast_to(b_r[:, None], (R_ROWS, LANES)),
        "w1": w1, "w3": w3, "w2": w2,
        "score_bound": (MLA_QK * MLA_SCALE * LOG2E * 1.01 * jnp.max(jnp.abs(g_q)) * jnp.max(jnp.abs(g_k))).reshape(1),
    }


def _rope_table(pos):
    half = MLA_ROPE // 2
    inv = ROPE_BASE ** (-np.arange(half, dtype=np.float64) / half)
    ang = np.asarray(pos, np.float64)[:, None] * inv[None, :]
    cos = np.cos(ang)
    sin = np.sin(ang)
    c2 = np.concatenate([cos, cos], axis=1)
    s2 = np.concatenate([sin, sin], axis=1)
    return jnp.asarray(np.concatenate([c2, s2, c2, s2], axis=1), F32)


def _pick(n, pref):
    return pref if n % pref == 0 else n


def _slots_kernel(ri_ref, ps_ref, d_ref):
    tm = ri_ref.shape[1]
    xi = lax.broadcasted_iota(I32, (N_EXPERTS, tm), 0)
    ps = ps_ref[:, 0:1]
    rows = []
    for k in range(2):
        start = jnp.sum(jnp.where(xi == ri_ref[k:k + 1, :], ps, 0), axis=0, keepdims=True)
        rows.append(start + ri_ref[2 + k:3 + k, :])
    d_ref[...] = jnp.concatenate(rows + [jnp.zeros((6, tm), I32)], axis=0)


def _slots(ri, pstart_b, tm):
    t = ri.shape[1]
    return pl.pallas_call(
        _slots_kernel,
        grid=(t // tm,),
        in_specs=[pl.BlockSpec((8, tm), lambda i: (0, i)), pl.BlockSpec((N_EXPERTS, LANES), lambda i: (0, 0))],
        out_specs=pl.BlockSpec((8, tm), lambda i: (0, i)),
        out_shape=jax.ShapeDtypeStruct((8, t), I32),
        compiler_params=_cparams(("arbitrary",)),
        name="slots",
    )(ri, pstart_b)


def _moe_layer(x2d, att, hm, pw, *, tm_route, bm, tm_disp, tm_comb):
    t = x2d.shape[0]
    x1, hp, ri, rg, cnt = _route(x2d, att, hm, pw, tm_route)
    counts = cnt[:, 0]
    padded = (counts + bm - 1) // bm * bm
    pend = jnp.cumsum(padded)
    pstart = pend - padded
    dest = _slots(ri, jnp.broadcast_to(pstart[:, None], (N_EXPERTS, LANES)), _pick(t, 2048))
    n_slots = (2 * t // bm + N_EXPERTS) * bm
    nblk = n_slots // bm
    n_used = (pend[-1] // bm).astype(I32)
    experts = jnp.arange(N_EXPERTS, dtype=I32)
    blk_first = jnp.arange(nblk, dtype=I32) * bm
    blk_e = jnp.sum((pend[None, :] <= blk_first[:, None]).astype(I32), axis=1)
    blk_e = jnp.minimum(blk_e, jnp.max(jnp.where(counts > 0, experts, 0)))
    zero_blocks = jnp.concatenate([jnp.where(counts > 0, pend // bm - 1, -1),
                                   jnp.where(n_used + experts < nblk, n_used + experts, -1)]).astype(I32)
    xs = _dispatch(zero_blocks, dest, hp, n_slots, tm_disp, bm)
    ys = _experts(blk_e, n_used[None], xs, pw, bm)
    return _combine(dest, x1, rg, ys, tm_comb)


def _state_pack(c, n):
    ct = jnp.swapaxes(c, -1, -2)
    s = jnp.concatenate([ct, jnp.broadcast_to(n[..., None], ct.shape[:-1] + (S_W - M_DV,))], axis=-1)
    return s.reshape((s.shape[0],) + S_SHAPE)


def _state_unpack(s, m):
    s = s.reshape(s.shape[0], M_HEADS, M_DK, S_W)
    return jnp.swapaxes(s[..., :M_DV], -1, -2), s[..., M_DV], m[:, :M_HEADS, 0]


def kernel(x_prompt, x_sample, cache_ckv, cache_krope, state_mlstm_c, state_mlstm_n, state_mlstm_m,
           meta_tokens, g_attn, w_in, g_cq, w_uq, g_ckv, w_ukv, g_q, g_k, b_igate, b_fgate, g_mh,
           w_out, g_ffn, w_group, b_group, w_erouter, b_erouter, w1, w3, w2):
    bp, seq = x_prompt.shape[:2]
    bs, dec = x_sample.shape[:2]
    past = cache_ckv.shape[2]
    layer = 0
    pw = _prep_weights(g_attn[layer], w_in[layer], g_cq[layer], w_uq[layer], g_ckv[layer], w_ukv[layer],
                       g_q[layer], g_k[layer], b_igate[layer], b_fgate[layer], g_mh[layer], w_out[layer],
                       g_ffn[layer], w_group[layer], b_group[layer], w_erouter[layer], b_erouter[layer],
                       w1[layer], w3[layer], w2[layer])

    xm = jnp.concatenate([meta_tokens, jnp.zeros((META_PAD - N_META, D_MODEL), F32)], axis=0)
    tab_m = _rope_table(np.arange(META_PAD) - N_META)
    _, ckv_m, misc_m, mq_m, mk_m, mv_m, og_m = _project(xm, tab_m, pw, META_PAD)
    zero_s = jnp.zeros((1,) + S_SHAPE, F32)
    zero_m = jnp.zeros((1, 8, LANES), F32)
    _, s_meta, m_meta = _mlstm(mq_m, mk_m, mv_m, og_m, misc_m, zero_s, zero_m, pw["g_mh"],
                               nb=1, l=META_PAD, n_valid=N_META)

    tp = bp * seq
    xp2 = x_prompt.reshape(tp, D_MODEL)
    tm_p = _pick(seq, 1024)
    q_p, ckv_p, misc_p, mq_p, mk_p, mv_p, og_p = _project(xp2, _rope_table(np.arange(seq)), pw, tm_p)
    tq = _pick(seq, 1024)
    att_p = _attention(q_p, ckv_p, misc_p, ckv_m[None], misc_m[None], pw, nb=bp, lq=seq, lf=seq, tq=tq, tk=256,
                       causal=True, n_tail=N_META)
    hm_p, s_p, m_p = _mlstm(mq_p, mk_p, mv_p, og_p, misc_p, s_meta, m_meta, pw["g_mh"],
                            nb=bp, l=seq)
    y_p = _moe_layer(xp2, att_p, hm_p, pw, tm_route=tm_p, bm=1024, tm_disp=_pick(seq, 512),
                     tm_comb=_pick(seq, 512))

    ts = bs * dec
    xs2 = x_sample.reshape(ts, D_MODEL)
    q_s, ckv_s, misc_s, mq_s, mk_s, mv_s, og_s = _project(xs2, _rope_table(past + np.arange(dec)), pw, dec)
    kr_cache = jnp.concatenate([cache_krope[layer], jnp.zeros((bs, past, LANES - MLA_ROPE), F32)], axis=-1)
    n_tail = dec + N_META
    assert n_tail <= META_PAD

    def tail_rows(own, meta):
        w = own.shape[-1]
        return jnp.concatenate([own.reshape(bs, dec, w), jnp.broadcast_to(meta[None, :N_META], (bs, N_META, w)),
                                jnp.zeros((bs, META_PAD - n_tail, w), F32)], axis=1)

    att_s = _attention(q_s, cache_ckv[layer].reshape(bs * past, MLA_KV_LORA), kr_cache.reshape(bs * past, LANES),
                       tail_rows(ckv_s, ckv_m), tail_rows(misc_s, misc_m), pw, nb=bs, lq=dec, lf=past, tq=dec,
                       tk=_pick(past, 256), causal=False, n_tail=n_tail)
    s0 = _state_pack(state_mlstm_c[layer], state_mlstm_n[layer])
    m0 = jnp.concatenate([jnp.broadcast_to(state_mlstm_m[layer][:, :, None], (bs, M_HEADS, LANES)),
                          jnp.zeros((bs, 8 - M_HEADS, LANES), F32)], axis=1)
    hm_s, s_s, m_s = _mlstm(mq_s, mk_s, mv_s, og_s, misc_s, s0, m0, pw["g_mh"], nb=bs, l=dec)
    y_s = _moe_layer(xs2, att_s, hm_s, pw, tm_route=_pick(ts, 512), bm=128, tm_disp=_pick(ts, 512),
                     tm_comb=_pick(ts, 512))

    m_ckv = ckv_m[:N_META]
    m_kr = misc_m[:N_META, :MLA_ROPE]
    new_ckv_p = jnp.concatenate([jnp.broadcast_to(m_ckv[None], (bp, N_META, MLA_KV_LORA)),
                                 ckv_p.reshape(bp, seq, MLA_KV_LORA)], axis=1)[None]
    new_kr_p = jnp.concatenate([jnp.broadcast_to(m_kr[None], (bp, N_META, MLA_ROPE)),
                                misc_p[:, :MLA_ROPE].reshape(bp, seq, MLA_ROPE)], axis=1)[None]
    c_p, n_p, mm_p = _state_unpack(s_p, m_p)
    c_s, n_s, mm_s = _state_unpack(s_s, m_s)
    return (y_p.reshape(bp, seq, D_MODEL), y_s.reshape(bs, dec, D_MODEL),
            new_ckv_p, new_kr_p, c_p[None], n_p[None], mm_p[None],
            ckv_s.reshape(bs, dec, MLA_KV_LORA)[None], misc_s[:, :MLA_ROPE].reshape(bs, dec, MLA_ROPE)[None],
            c_s[None], n_s[None], mm_s[None])
```

```python
import functools

import numpy as np
import jax
import jax.numpy as jnp
from jax import lax
from jax.experimental import pallas as pl
from jax.experimental.pallas import tpu as pltpu

F32 = jnp.float32
BF16 = jnp.bfloat16
I32 = jnp.int32
U32 = jnp.uint32

D_MODEL = 1024
CHUNK = 64
N_META = 16
MLA_HEADS = 8
MLA_V = 64
MLA_NOPE = 64
MLA_ROPE = 32
MLA_QK = MLA_NOPE + MLA_ROPE
MLA_Q_LORA = 256
MLA_KV_LORA = 128
MLA_SCALE = MLA_QK ** -0.5
ROPE_BASE = 10000.0
M_HEADS = 4
M_DV = 128
M_DK = 64
N_GROUPS = 4
E_PER_GROUP = 8
N_EXPERTS = 32
EXPERT_FF = 512
EPS = 1e-6

LANES = 128
HEAD_PAD = 128
PROJ_PAD = 2048
META_PAD = 128
NEG_BIG = -1e30
VMEM_LIMIT = 56 * 1024 * 1024

_O_CQ, _O_CKV, _O_MQ, _O_MK, _O_MV, _O_MO, _O_MISC = 0, 256, 384, 640, 896, 1408, 1920
_L_IG, _L_FG = 64, 68


def _cparams(sem):
    return pltpu.CompilerParams(dimension_semantics=sem, vmem_limit_bytes=VMEM_LIMIT)


def _dot(a, b):
    return jnp.dot(a, b, preferred_element_type=F32)


def _dot_nt(a, b):
    return lax.dot_general(a, b, (((1,), (1,)), ((), ())), preferred_element_type=F32)


def _split3(x):
    x1 = x.astype(BF16)
    r1 = x - x1.astype(F32)
    x2 = r1.astype(BF16)
    x3 = (r1 - x2.astype(F32)).astype(BF16)
    return x1, x2, x3


def _proj_kernel(x_ref, tab_ref, w_ref, g_ref, gcq_ref, wq_ref, gckv_ref, gq_ref, bias_ref,
                 q_ref, ckv_ref, misc_ref, mq_ref, mk_ref, mv_ref, og_ref):
    x = x_ref[...]
    xn = x * lax.rsqrt(jnp.mean(x * x, axis=-1, keepdims=True) + EPS) * g_ref[...]
    xb = xn.astype(BF16)
    tab = tab_ref[...]
    lane = lax.broadcasted_iota(I32, tab.shape, 1)

    def proj(off, width):
        return _dot(xb, w_ref[:, off:off + width])

    cq = proj(_O_CQ, MLA_Q_LORA)
    cqn = cq * lax.rsqrt(jnp.mean(cq * cq, axis=-1, keepdims=True) + EPS) * gcq_ref[...]
    cqb = cqn.astype(BF16)
    qw = MLA_HEADS * HEAD_PAD
    qa = _dot(cqb, wq_ref[:, :qw])
    qb = _dot(cqb, wq_ref[:, qw:])
    ckv = proj(_O_CKV, MLA_KV_LORA)
    zm = proj(_O_MISC, LANES)
    z_mq = proj(_O_MQ, M_HEADS * M_DK)
    z_mk = proj(_O_MK, M_HEADS * M_DK)
    z_mv = proj(_O_MV, M_HEADS * M_DV)
    z_mo = proj(_O_MO, M_HEADS * M_DV)
    gq = gq_ref[...]
    tab_a = jnp.where(lane < MLA_NOPE, 1.0, jnp.where(lane < MLA_QK, tab, 0.0))
    tab_b = jnp.where((lane >= MLA_NOPE) & (lane < MLA_QK), pltpu.roll(tab, LANES - MLA_ROPE, 1), 0.0)
    slab = 2 * HEAD_PAD
    pair_ones = (lax.broadcasted_iota(I32, (slab, slab), 0) // HEAD_PAD
                 == lax.broadcasted_iota(I32, (slab, slab), 1) // HEAD_PAD).astype(BF16)
    for p in range(MLA_HEADS // 2):
        za = qa[:, p * slab:(p + 1) * slab]
        ms = _dot((za * za).astype(BF16), pair_ones) * (1.0 / MLA_QK)
        for e in range(2):
            h = 2 * p + e
            qh = za[:, e * HEAD_PAD:(e + 1) * HEAD_PAD] * tab_a + qb[:, h * HEAD_PAD:(h + 1) * HEAD_PAD] * tab_b
            q_ref[h] = (qh * lax.rsqrt(ms[:, e * HEAD_PAD:(e + 1) * HEAD_PAD] + EPS) * gq).astype(BF16)

    ckv_ref[...] = ckv * lax.rsqrt(jnp.mean(ckv * ckv, axis=-1, keepdims=True) + EPS) * gckv_ref[...]

    y = zm * tab
    rot = y + pltpu.roll(y, LANES - MLA_ROPE, 1)
    gate = zm + bias_ref[...]
    logf = jnp.minimum(gate, 0.0) - jnp.log1p(jnp.exp(-jnp.abs(gate)))
    misc = jnp.where(lane < MLA_ROPE, rot,
                     jnp.where((lane >= _L_IG) & (lane < _L_FG), gate,
                               jnp.where((lane >= _L_FG) & (lane < _L_FG + M_HEADS), logf, 0.0)))
    misc_ref[...] = misc

    mq_ref[...] = z_mq.astype(BF16)
    mk_ref[...] = (z_mk * (M_DK ** -0.5)).astype(BF16)
    mv_ref[...] = z_mv.astype(BF16)
    og_ref[...] = jax.nn.sigmoid(z_mo).astype(BF16)


def _project(x2d, tab, pw, tm):
    t = x2d.shape[0]
    nt = t // tm
    ntab = tab.shape[0] // tm
    row = lambda i: (i, 0)
    full = lambda i: (0, 0)
    return pl.pallas_call(
        _proj_kernel,
        grid=(nt,),
        in_specs=[
            pl.BlockSpec((tm, D_MODEL), row),
            pl.BlockSpec((tm, LANES), lambda i: (i % ntab, 0)),
            pl.BlockSpec((D_MODEL, PROJ_PAD), full),
            pl.BlockSpec((1, D_MODEL), full),
            pl.BlockSpec((1, MLA_Q_LORA), full),
            pl.BlockSpec((MLA_Q_LORA, 2 * MLA_HEADS * HEAD_PAD), full),
            pl.BlockSpec((1, MLA_KV_LORA), full),
            pl.BlockSpec((1, HEAD_PAD), full),
            pl.BlockSpec((1, LANES), full),
        ],
        out_specs=[
            pl.BlockSpec((MLA_HEADS, tm, HEAD_PAD), lambda i: (0, i, 0)),
            pl.BlockSpec((tm, MLA_KV_LORA), row),
            pl.BlockSpec((tm, LANES), row),
            pl.BlockSpec((tm, M_HEADS * M_DK), row),
            pl.BlockSpec((tm, M_HEADS * M_DK), row),
            pl.BlockSpec((tm, M_HEADS * M_DV), row),
            pl.BlockSpec((tm, M_HEADS * M_DV), row),
        ],
        out_shape=[
            jax.ShapeDtypeStruct((MLA_HEADS, t, HEAD_PAD), BF16),
            jax.ShapeDtypeStruct((t, MLA_KV_LORA), F32),
            jax.ShapeDtypeStruct((t, LANES), F32),
            jax.ShapeDtypeStruct((t, M_HEADS * M_DK), BF16),
            jax.ShapeDtypeStruct((t, M_HEADS * M_DK), BF16),
            jax.ShapeDtypeStruct((t, M_HEADS * M_DV), BF16),
            jax.ShapeDtypeStruct((t, M_HEADS * M_DV), BF16),
        ],
        compiler_params=_cparams(("arbitrary",)),
        name="projection",
    )(x2d, tab, pw["w_in"], pw["g_attn"], pw["g_cq"], pw["w_q"], pw["g_ckv"], pw["g_q"], pw["gate_bias"])


SAFE_BOUND = 40.0
LOG2E = 1.4426950408889634


def _attn_kernel(sb_ref, q_ref, ckv_ref, misc_ref, ckvt_ref, misct_ref, wk_ref, wv_ref, gk_ref, o_ref,
                 kt_scr, v_scr, acc_scr, p_scr, *, lf, tq, tk, bt, causal, n_tail):
    i = pl.program_id(1)
    gk = gk_ref[...]

    def build(ckv_rows, misc_rows, dst):
        n = ckv_rows.shape[0]
        lane = lax.broadcasted_iota(I32, (n, LANES), 1)
        cb = ckv_rows.astype(BF16)
        krp = jnp.where((lane >= MLA_NOPE) & (lane < MLA_QK), pltpu.roll(misc_rows, MLA_NOPE, 1), 0.0)
        onecol = (lane == MLA_V).astype(F32)
        for h in range(MLA_HEADS):
            kk = _dot(cb, wk_ref[h]) + krp
            ms = jnp.sum(kk * kk, axis=-1, keepdims=True) * (1.0 / MLA_QK)
            kt_scr[h, :, pl.ds(dst, n)] = (kk * lax.rsqrt(ms + EPS) * gk).T.astype(BF16)
            v_scr[h, pl.ds(dst, n), :] = (_dot(cb, wv_ref[h]) + onecol).astype(BF16)

    @pl.when(i == 0)
    def _():
        def body(r, carry):
            r0 = pl.multiple_of(r * bt, bt)
            build(ckv_ref[pl.ds(r0, bt), :], misc_ref[pl.ds(r0, bt), :], r0)
            return carry

        lax.fori_loop(0, lf // bt, body, 0)
        build(ckvt_ref[0], misct_ref[0], lf)

    ndiag = tq // tk if causal else 0
    nfull = i * ndiag if causal else lf // tk
    tail_mask = lax.broadcasted_iota(I32, (tq, META_PAD), 1) < n_tail

    def kt_tile(h, j):
        return kt_scr[h, :, pl.ds(pl.multiple_of(j * tk, tk), tk)]

    def v_tile(h, j):
        return v_scr[h, pl.ds(pl.multiple_of(j * tk, tk), tk), :]

    def chunk_ids(lo):
        rows = lax.broadcasted_iota(I32, (tq - lo, tk), 0) // CHUNK
        cols = lax.broadcasted_iota(I32, (tq - lo, tk), 1) // CHUNK
        return rows, cols

    def finish(h):
        acc = acc_scr[h]
        o_ref[:, h * MLA_V:(h + 1) * MLA_V] = (acc[:, :MLA_V] / acc[:, MLA_V:MLA_V + 1]).astype(BF16)

    safe = sb_ref[0] <= SAFE_BOUND * LOG2E

    @pl.when(safe)
    def _():
        bound = sb_ref[0]

        def scores(h, lo, kt):
            return _dot(q_ref[h, lo:, :], kt) * (MLA_SCALE * LOG2E) - bound

        def probs(s, mask):
            if mask is not None:
                s = jnp.where(mask, s, -jnp.inf)
            return jnp.exp2(s).astype(BF16)

        def pipe_step(j, lo, lo_prev, mask):
            ss = [scores(h, lo, kt_tile(h, j)) for h in range(MLA_HEADS)]
            for h in range(MLA_HEADS):
                acc_scr[h, lo_prev:, :] += _dot(p_scr[(j - 1) % 2, h, lo_prev:, :], v_tile(h, j - 1))
            for h in range(MLA_HEADS):
                p_scr[j % 2, h, lo:, :] = probs(ss[h], mask)

        for h in range(MLA_HEADS):
            p_tail = probs(scores(h, 0, kt_scr[h, :, lf:lf + META_PAD]), tail_mask)
            acc_scr[h] = _dot(p_tail, v_scr[h, lf:lf + META_PAD, :])
        rows0, cols0 = chunk_ids(0)
        if causal:
            first_mask = cols0 <= rows0 + jnp.where(nfull > 0, tk, 0)
        else:
            first_mask = None
        for h in range(MLA_HEADS):
            p_scr[0, h] = probs(scores(h, 0, kt_tile(h, 0)), first_mask)

        def body(j, carry):
            pipe_step(j, 0, 0, None)
            return carry

        lax.fori_loop(1, nfull, body, 0)
        if causal:
            @pl.when(nfull >= 1)
            def _():
                pipe_step(nfull, 0, 0, cols0 <= rows0)
            for d in range(1, ndiag):
                rows_d, cols_d = chunk_ids(d * tk)
                pipe_step(nfull + d, d * tk, (d - 1) * tk, cols_d <= rows_d)
            last, lo_last = nfull + ndiag - 1, (ndiag - 1) * tk
        else:
            last, lo_last = nfull - 1, 0
        for h in range(MLA_HEADS):
            acc_scr[h, lo_last:, :] += _dot(p_scr[last % 2, h, lo_last:, :], v_tile(h, last))
            finish(h)

    @pl.when(jnp.logical_not(safe))
    def _():
        rows0, cols0 = chunk_ids(0)
        for h in range(MLA_HEADS):
            qh = q_ref[h]

            def step(carry, kt, vrows, mask):
                m, acc = carry
                s = _dot(qh, kt) * MLA_SCALE
                if mask is not None:
                    s = jnp.where(mask, s, -jnp.inf)
                m_new = jnp.maximum(m, jnp.max(s, axis=-1, keepdims=True))
                acc = jnp.exp(m - m_new) * acc + _dot(jnp.exp(s - m_new).astype(BF16), vrows)
                return m_new, acc

            def body(j, c):
                mask = (cols0 + j * (tk // CHUNK) <= rows0 + i * (tq // CHUNK)) if causal else None
                return step(c, kt_tile(h, j), v_tile(h, j), mask)

            carry = (jnp.full((tq, 1), -jnp.inf, F32), jnp.zeros((tq, LANES), F32))
            carry = step(carry, kt_scr[h, :, lf:lf + META_PAD], v_scr[h, lf:lf + META_PAD, :], tail_mask)
            carry = lax.fori_loop(0, nfull + ndiag, body, carry)
            acc_scr[h] = carry[1]
            finish(h)


def _attention(q, ckv_f, misc_f, ckv_t, misc_t, pw, *, nb, lq, lf, tq, tk, causal, n_tail):
    nq = lq // tq
    bt = min(512, lf)
    assert lf % bt == 0 and lf % tk == 0 and lq % tq == 0 and tk % CHUNK == 0 and (not causal or tq % tk == 0)
    kern = functools.partial(_attn_kernel, lf=lf, tq=tq, tk=tk, bt=bt, causal=causal, n_tail=n_tail)
    full2 = lambda b, i, sb: (0, 0)
    full3 = lambda b, i, sb: (0, 0, 0)
    tail = full3 if ckv_t.shape[0] == 1 else (lambda b, i, sb: (b, 0, 0))
    lk = lf + META_PAD
    return pl.pallas_call(
        kern,
        grid_spec=pltpu.PrefetchScalarGridSpec(
            num_scalar_prefetch=1,
            grid=(nb, nq),
            in_specs=[
                pl.BlockSpec((MLA_HEADS, tq, HEAD_PAD), lambda b, i, sb: (0, b * nq + i, 0)),
                pl.BlockSpec((lf, MLA_KV_LORA), lambda b, i, sb: (b, 0)),
                pl.BlockSpec((lf, LANES), lambda b, i, sb: (b, 0)),
                pl.BlockSpec((1, META_PAD, MLA_KV_LORA), tail),
                pl.BlockSpec((1, META_PAD, LANES), tail),
                pl.BlockSpec((MLA_HEADS, MLA_KV_LORA, HEAD_PAD), full3),
                pl.BlockSpec((MLA_HEADS, MLA_KV_LORA, LANES), full3),
                pl.BlockSpec((1, HEAD_PAD), full2),
            ],
            out_specs=pl.BlockSpec((tq, MLA_HEADS * MLA_V), lambda b, i, sb: (b * nq + i, 0)),
            scratch_shapes=[
                pltpu.VMEM((MLA_HEADS, HEAD_PAD, lk), BF16),
                pltpu.VMEM((MLA_HEADS, lk, LANES), BF16),
                pltpu.VMEM((MLA_HEADS, tq, LANES), F32),
                pltpu.VMEM((2, MLA_HEADS, tq, tk), BF16),
            ],
        ),
        out_shape=jax.ShapeDtypeStruct((nb * lq, MLA_HEADS * MLA_V), BF16),
        compiler_params=_cparams(("arbitrary", "arbitrary")),
        name="attention",
    )(pw["score_bound"], q, ckv_f, misc_f, ckv_t, misc_t, pw["w_k"], pw["w_v"], pw["g_k"])


S_W = 2 * M_DV
S_SHAPE = (M_HEADS // 2, 2 * M_DK, S_W)


MLSTM_CHUNK = LANES


def _mlstm_kernel(q_ref, k_ref, v_ref, og_ref, misc_ref, s0_ref, m0_ref, gmh_ref,
                  h_ref, sout_ref, mout_ref, s_scr, m_scr, *, n_valid):
    c = pl.program_id(1)
    nc = pl.num_programs(1)
    lb = q_ref.shape[0]
    lc = MLSTM_CHUNK

    @pl.when(c == 0)
    def _():
        s_scr[...] = s0_ref[0]
        m_scr[...] = m0_ref[0]

    def rows(x):
        if lb == lc:
            return x
        return jnp.concatenate([x, jnp.zeros((lc - lb,) + x.shape[1:], x.dtype)], axis=0)

    g = rows(misc_ref[...])
    limit = lb if n_valid is None else n_valid - c * lb
    row = lax.broadcasted_iota(I32, g.shape, 0)
    lane = lax.broadcasted_iota(I32, g.shape, 1)
    g = jnp.where(row < limit, g, jnp.where((lane >= _L_IG) & (lane < _L_FG), NEG_BIG, 0.0))
    gt = g.T
    gt8 = gt[_L_IG:_L_IG + 2 * M_HEADS, :]
    rr = lax.broadcasted_iota(I32, (lc, lc), 0)
    cc = lax.broadcasted_iota(I32, (lc, lc), 1)
    causal = cc <= rr
    g1, g2, g3 = _split3(g)
    tri = causal.astype(BF16)
    b_cols = _dot(tri, g1) + _dot(tri, g2) + _dot(tri, g3)
    t1, t2, t3 = _split3(gt8)
    trit = (rr <= cc).astype(BF16)
    b_rows = (_dot(t1, trit) + _dot(t2, trit) + _dot(t3, trit))[M_HEADS:2 * M_HEADS, :]
    a = gt8[0:M_HEADS, :] - b_rows
    m_prev = m_scr[0:M_HEADS, :]
    u_b = [-jnp.maximum(m_prev[h:h + 1, :],
                        jnp.broadcast_to(jnp.max(jnp.where(causal, a[h:h + 1, :], -jnp.inf), axis=-1, keepdims=True),
                                         (lc, LANES))) for h in range(M_HEADS)]
    negm_b = [u_b[h] - jnp.broadcast_to(b_cols[:, _L_FG + h:_L_FG + h + 1], (lc, LANES)) for h in range(M_HEADS)]
    u_last = jnp.concatenate([u_b[h][lc - 1:lc, 0:1] for h in range(M_HEADS)], axis=0)
    m_new = jnp.broadcast_to(b_rows[:, lc - 1:lc] - u_last, (M_HEADS, LANES))
    w_state = jnp.exp(a + u_last)
    decay = jnp.exp(m_prev[:, 0:1] + u_last)

    q = rows(q_ref[...])
    k = rows(k_ref[...])
    v = rows(v_ref[...])
    kt = k.astype(F32).T
    ones_blk = jnp.ones((lc, M_DV), BF16)
    low_half = lax.broadcasted_iota(I32, (lc, LANES), 1) < M_DK

    heads = range(M_HEADS)
    mine = [jnp.logical_not(low_half) if h % 2 else low_half for h in heads]
    q_pair = [q[:, (h // 2) * LANES:(h // 2 + 1) * LANES] for h in heads]
    kh = [jnp.where(mine[h], k[:, (h // 2) * LANES:(h // 2 + 1) * LANES], jnp.zeros_like(q_pair[h])) for h in heads]
    qh = [jnp.where(mine[h], q_pair[h], jnp.zeros_like(q_pair[h])) for h in heads]
    vext = [jnp.concatenate([v[:, h * M_DV:(h + 1) * M_DV], ones_blk], axis=1) for h in heads]
    s_old = [s_scr[j] for j in range(M_HEADS // 2)]
    s_bf = [s.astype(BF16) for s in s_old]

    qk = [_dot_nt(q_pair[h], kh[h]) for h in heads]
    qs = [_dot(qh[h], s_bf[h // 2]) for h in heads]
    ktw = [(kt[h * M_DK:(h + 1) * M_DK, :] * w_state[h:h + 1, :]).astype(BF16) for h in heads]
    upd = [_dot(ktw[h], vext[h]) for h in heads]
    w =[(jnp.where(causal, jnp.exp(a[h:h + 1, :] + u_b[h]), 0.0) * qk[h]).astype(BF16) for h in heads]
    inter = [jnp.exp(m_prev[h:h + 1, :] + u_b[h]) for h in heads]
    r = [_dot(w[h], vext[h]) + jnp.tile(inter[h], (1, 2)) * qs[h] for h in heads]
    num = [r[h][:, :M_DV] for h in heads]
    sq = [num[h] * num[h] for h in heads]
    sq_hi = [s.astype(BF16) for s in sq]
    sq_lo = [(sq[h] - sq_hi[h].astype(F32)).astype(BF16) for h in heads]
    msn = [(_dot(sq_hi[h], ones_blk) + _dot(sq_lo[h], ones_blk)) * (1.0 / M_DV) for h in heads]
    for h in heads:
        rden = 1.0 / jnp.maximum(jnp.abs(r[h][:, M_DV:]), jnp.exp(negm_b[h]))
        scale = rden * lax.rsqrt(rden * rden * msn[h] + EPS)
        hn = num[h] * scale * gmh_ref[h:h + 1, :]
        out = (hn[:lb] * og_ref[:, h * M_DV:(h + 1) * M_DV].astype(F32)).astype(BF16)
        h_ref[:, h * M_DV:(h + 1) * M_DV] = out
    for j in range(M_HEADS // 2):
        new = [decay[h:h + 1, :] * s_old[j][(h % 2) * M_DK:(h % 2 + 1) * M_DK, :] + upd[h] for h in (2 * j, 2 * j + 1)]
        s_scr[j] = jnp.concatenate(new, axis=0)
    m_scr[0:M_HEADS, :] = m_new

    @pl.when(c == nc - 1)
    def _():
        sout_ref[0] = s_scr[...]
        mout_ref[0] = m_scr[...]


def _mlstm(mq, mk, mv, og, misc, s0, m0, gmh, *, nb, l, n_valid=None):
    lc = min(l, MLSTM_CHUNK)
    assert l % lc == 0 and lc % 16 == 0
    nc = l // lc
    shared = s0.shape[0] == 1
    st = (lambda b, c: (0, 0, 0, 0)) if shared else (lambda b, c: (b, 0, 0, 0))
    mst = (lambda b, c: (0, 0, 0)) if shared else (lambda b, c: (b, 0, 0))
    row = lambda b, c: (b * nc + c, 0)
    kern = functools.partial(_mlstm_kernel, n_valid=n_valid)
    return pl.pallas_call(
        kern,
        grid=(nb, nc),
        in_specs=[
            pl.BlockSpec((lc, M_HEADS * M_DK), row),
            pl.BlockSpec((lc, M_HEADS * M_DK), row),
            pl.BlockSpec((lc, M_HEADS * M_DV), row),
            pl.BlockSpec((lc, M_HEADS * M_DV), row),
            pl.BlockSpec((lc, LANES), row),
            pl.BlockSpec((1,) + S_SHAPE, st),
            pl.BlockSpec((1, 8, LANES), mst),
            pl.BlockSpec((M_HEADS, M_DV), lambda b, c: (0, 0)),
        ],
        out_specs=[
            pl.BlockSpec((lc, M_HEADS * M_DV), row),
            pl.BlockSpec((1,) + S_SHAPE, lambda b, c: (b, 0, 0, 0)),
            pl.BlockSpec((1, 8, LANES), lambda b, c: (b, 0, 0)),
        ],
        out_shape=[
            jax.ShapeDtypeStruct((nb * l, M_HEADS * M_DV), BF16),
            jax.ShapeDtypeStruct((nb,) + S_SHAPE, F32),
            jax.ShapeDtypeStruct((nb, 8, LANES), F32),
        ],
        scratch_shapes=[
            pltpu.VMEM(S_SHAPE, F32),
            pltpu.VMEM((8, LANES), F32),
        ],
        compiler_params=_cparams(("arbitrary", "arbitrary")),
        name="mlstm",
    )(mq, mk, mv, og, misc, s0, m0, gmh)


R_ROWS = 40


def _route_kernel(x_ref, att_ref, hm_ref, wo_ref, g_ref, wr_ref, br_ref,
                  x1_ref, hp_ref, ri_ref, rg_ref, cnt_ref, cnt_scr):
    i = pl.program_id(0)
    tm = x_ref.shape[0]

    @pl.when(i == 0)
    def _():
        cnt_scr[...] = jnp.zeros_like(cnt_scr)

    mix = jnp.concatenate([att_ref[...], hm_ref[...]], axis=1)
    x1 = x_ref[...] + _dot(mix, wo_ref[...])
    x1_ref[...] = x1
    hn = x1 * lax.rsqrt(jnp.mean(x1 * x1, axis=-1, keepdims=True) + EPS) * g_ref[...]
    hb = hn.astype(BF16)
    half = D_MODEL // 2
    hi = lax.bitcast_convert_type(hb[:, :half].astype(F32), U32)
    lo = lax.bitcast_convert_type(hb[:, half:].astype(F32), U32)
    hp_ref[...] = (hi & jnp.uint32(0xFFFF0000)) | (lo >> 16)

    logits = _dot_nt(wr_ref[...], hb) + br_ref[:, 0:1]
    e_log = logits[0:N_EXPERTS, :]
    g_log = logits[N_EXPERTS:N_EXPERTS + N_GROUPS, :]
    gmax = jnp.max(g_log, axis=0, keepdims=True)
    gsum = jnp.sum(jnp.exp(g_log - gmax), axis=0, keepdims=True)
    gi = lax.broadcasted_iota(I32, g_log.shape, 0)
    g_idx = jnp.min(jnp.where(g_log == gmax, gi, N_GROUPS), axis=0, keepdims=True)
    e_sel = jnp.zeros((E_PER_GROUP, tm), F32)
    for gg in range(N_GROUPS):
        e_sel = jnp.where(g_idx == gg, e_log[gg * E_PER_GROUP:(gg + 1) * E_PER_GROUP, :], e_sel)
    ei = lax.broadcasted_iota(I32, e_sel.shape, 0)
    m1 = jnp.max(e_sel, axis=0, keepdims=True)
    i1 = jnp.min(jnp.where(e_sel == m1, ei, E_PER_GROUP), axis=0, keepdims=True)
    e2 = jnp.where(ei == i1, -jnp.inf, e_sel)
    m2 = jnp.max(e2, axis=0, keepdims=True)
    i2 = jnp.min(jnp.where(e2 == m2, ei, E_PER_GROUP), axis=0, keepdims=True)
    ex = jnp.exp(m2 - m1)
    gp = 1.0 / gsum
    p1 = 1.0 / (1.0 + ex)
    gate1 = gp * p1
    gate2 = gp * (ex * p1)
    id1 = g_idx * E_PER_GROUP + i1
    id2 = g_idx * E_PER_GROUP + i2

    xi = lax.broadcasted_iota(I32, (N_EXPERTS, tm), 0)
    oh1 = xi == id1
    oh2 = xi == id2
    e_cnt = (oh1 | oh2).astype(F32)
    rr = lax.broadcasted_iota(I32, (tm, tm), 0)
    cc = lax.broadcasted_iota(I32, (tm, tm), 1)
    upper = (rr < cc).astype(BF16)
    pref = _dot(e_cnt.astype(BF16), upper) + cnt_scr[:, 0:1]
    rank1 = jnp.sum(jnp.where(oh1, pref, 0.0), axis=0, keepdims=True)
    rank2 = jnp.sum(jnp.where(oh2, pref, 0.0), axis=0, keepdims=True)
    cnt_new = cnt_scr[...] + jnp.sum(e_cnt, axis=1, keepdims=True)
    cnt_scr[...] = cnt_new
    cnt_ref[...] = cnt_new.astype(I32)

    zi = jnp.zeros((1, tm), I32)
    ri_ref[...] = jnp.concatenate([id1, id2, rank1.astype(I32), rank2.astype(I32), zi, zi, zi, zi], axis=0)
    zf = jnp.zeros((1, tm), F32)
    rg_ref[...] = jnp.concatenate([gate1, gate2, zf, zf, zf, zf, zf, zf], axis=0)


def _route(x2d, att, hm, pw, tm):
    t = x2d.shape[0]
    row = lambda i: (i, 0)
    col = lambda i: (0, i)
    full = lambda i: (0, 0)
    return pl.pallas_call(
        _route_kernel,
        grid=(t // tm,),
        in_specs=[
            pl.BlockSpec((tm, D_MODEL), row),
            pl.BlockSpec((tm, D_MODEL // 2), row),
            pl.BlockSpec((tm, D_MODEL // 2), row),
            pl.BlockSpec((D_MODEL, D_MODEL), full),
            pl.BlockSpec((1, D_MODEL), full),
            pl.BlockSpec((R_ROWS, D_MODEL), full),
            pl.BlockSpec((R_ROWS, LANES), full),
        ],
        out_specs=[
            pl.BlockSpec((tm, D_MODEL), row),
            pl.BlockSpec((tm, D_MODEL // 2), row),
            pl.BlockSpec((8, tm), col),
            pl.BlockSpec((8, tm), col),
            pl.BlockSpec((N_EXPERTS, LANES), full),
        ],
        out_shape=[
            jax.ShapeDtypeStruct((t, D_MODEL), F32),
            jax.ShapeDtypeStruct((t, D_MODEL // 2), U32),
            jax.ShapeDtypeStruct((8, t), I32),
            jax.ShapeDtypeStruct((8, t), F32),
            jax.ShapeDtypeStruct((N_EXPERTS, LANES), I32),
        ],
        scratch_shapes=[pltpu.VMEM((N_EXPERTS, LANES), F32)],
        compiler_params=_cparams(("arbitrary",)),
        name="route",
    )(x2d, att, hm, pw["w_out"], pw["g_ffn"], pw["w_r"], pw["b_r"])


SUBLANES = 8


N_ZERO_BLOCKS = 2 * N_EXPERTS


def _dispatch_kernel(zb_ref, dest_ref, hp_ref, xs_ref, zero_scr, sem, zsem, *, tm, bm):
    @pl.when(pl.program_id(0) == 0)
    def _():
        zero_scr[...] = jnp.zeros_like(zero_scr)

        def zcopy(k):
            return pltpu.make_async_copy(zero_scr, xs_ref.at[pl.ds(pl.multiple_of(zb_ref[k] * bm, bm), bm)], zsem)

        def zstart(k, carry):
            @pl.when(zb_ref[k] >= 0)
            def _():
                zcopy(k).start()
            return carry

        def zwait(k, carry):
            @pl.when(zb_ref[k] >= 0)
            def _():
                zcopy(k).wait()
            return carry

        lax.fori_loop(0, N_ZERO_BLOCKS, zstart, 0)
        lax.fori_loop(0, N_ZERO_BLOCKS, zwait, 0)

    def body(g, carry):
        for u in range(SUBLANES):
            t = g * SUBLANES + u
            src = hp_ref.at[g, pl.ds(u, 1), :]
            pltpu.make_async_copy(src, xs_ref.at[pl.ds(dest_ref[t], 1)], sem).start(priority=0)
            pltpu.make_async_copy(src, xs_ref.at[pl.ds(dest_ref[tm + t], 1)], sem).start(priority=1)
        return carry

    lax.fori_loop(0, tm // SUBLANES, body, 0)
    pltpu.make_async_copy(xs_ref.at[pl.ds(0, 2 * tm)], xs_ref.at[pl.ds(0, 2 * tm)], sem).wait()


def _tile_slots(dest, tm):
    t = dest.shape[1]
    return dest[0:2].reshape(2, t // tm, tm).transpose(1, 0, 2).reshape(2 * t)


def _dispatch(zero_blocks, dest, hp, n_slots, tm, bm):
    t = hp.shape[0]
    return pl.pallas_call(
        functools.partial(_dispatch_kernel, tm=tm, bm=bm),
        grid_spec=pltpu.PrefetchScalarGridSpec(
            num_scalar_prefetch=1,
            grid=(t // tm,),
            in_specs=[
                pl.BlockSpec((2 * tm,), lambda i, zb: (i,), memory_space=pltpu.SMEM),
                pl.BlockSpec((tm // SUBLANES, SUBLANES, D_MODEL // 2), lambda i, zb: (i, 0, 0)),
            ],
            out_specs=pl.BlockSpec(memory_space=pl.ANY),
            scratch_shapes=[
                pltpu.VMEM((bm, D_MODEL // 2), U32),
                pltpu.SemaphoreType.DMA,
                pltpu.SemaphoreType.DMA,
            ],
        ),
        out_shape=jax.ShapeDtypeStruct((n_slots, D_MODEL // 2), U32),
        compiler_params=_cparams(("arbitrary",)),
        name="dispatch",
    )(zero_blocks, _tile_slots(dest, tm), hp.reshape(t // SUBLANES, SUBLANES, D_MODEL // 2))


def _expert_kernel(be_ref, nu_ref, xs_ref, w1_ref, w3_ref, w2_ref, ys_ref, w1_scr, w3_scr, w2_scr):
    i = pl.program_id(0)
    new_expert = jnp.logical_or(i == 0, be_ref[i] != be_ref[jnp.maximum(i - 1, 0)])

    @pl.when(jnp.logical_and(new_expert, i < nu_ref[0]))
    def _():
        w1_scr[...] = w1_ref[0].astype(BF16)
        w3_scr[...] = w3_ref[0].astype(BF16)
        w2_scr[...] = w2_ref[0].astype(BF16)

    @pl.when(i < nu_ref[0])
    def _():
        xw = xs_ref[...]
        xa = lax.bitcast_convert_type(xw & jnp.uint32(0xFFFF0000), F32).astype(BF16)
        xb = lax.bitcast_convert_type(xw << 16, F32).astype(BF16)
        half = D_MODEL // 2
        h1 = _dot(xa, w1_scr[:half, :]) + _dot(xb, w1_scr[half:, :])
        h3 = _dot(xa, w3_scr[:half, :]) + _dot(xb, w3_scr[half:, :])
        a = (h1 * jax.nn.sigmoid(h1)) * h3
        ys_ref[...] = _dot(a.astype(BF16), w2_scr[...])

    @pl.when(pl.program_id(0) >= nu_ref[0])
    def _():
        ys_ref[...] = jnp.zeros_like(ys_ref)


def _experts(blk_e, n_used, xs, pw, bm):
    n_slots = xs.shape[0]
    nblk = n_slots // bm
    blk = lambda i, be, nu: (jnp.minimum(i, nu[0] - 1), 0)
    oblk = lambda i, be, nu: (i, 0)
    wsel = lambda i, be, nu: (be[i], 0, 0)
    return pl.pallas_call(
        _expert_kernel,
        grid_spec=pltpu.PrefetchScalarGridSpec(
            num_scalar_prefetch=2,
            grid=(nblk,),
            in_specs=[
                pl.BlockSpec((bm, D_MODEL // 2), blk),
                pl.BlockSpec((1, D_MODEL, EXPERT_FF), wsel),
                pl.BlockSpec((1, D_MODEL, EXPERT_FF), wsel),
                pl.BlockSpec((1, EXPERT_FF, D_MODEL), wsel),
            ],
            out_specs=pl.BlockSpec((bm, D_MODEL), oblk),
            scratch_shapes=[
                pltpu.VMEM((D_MODEL, EXPERT_FF), BF16),
                pltpu.VMEM((D_MODEL, EXPERT_FF), BF16),
                pltpu.VMEM((EXPERT_FF, D_MODEL), BF16),
            ],
        ),
        out_shape=jax.ShapeDtypeStruct((n_slots, D_MODEL), F32),
        compiler_params=_cparams(("arbitrary",)),
        name="experts",
    )(blk_e, n_used, xs, pw["w1"], pw["w3"], pw["w2"])


def _combine_kernel(dest_ref, x1_ref, rg_ref, ys_ref, y_ref, r0_scr, r1_scr, sem, *, tm):
    def body(g, carry):
        for u in range(SUBLANES):
            t = g * SUBLANES + u
            pltpu.make_async_copy(ys_ref.at[pl.ds(dest_ref[t], 1)], r0_scr.at[g, pl.ds(u, 1), :], sem).start(priority=0)
            pltpu.make_async_copy(ys_ref.at[pl.ds(dest_ref[tm + t], 1)], r1_scr.at[g, pl.ds(u, 1), :],
                                  sem).start(priority=1)
        return carry

    lax.fori_loop(0, tm // SUBLANES, body, 0)
    gt = jnp.concatenate([rg_ref[...], jnp.zeros((LANES - 8, tm), F32)], axis=0).T
    slab = ys_ref.at[pl.ds(0, tm)]
    pltpu.make_async_copy(slab, slab, sem).wait()
    pltpu.make_async_copy(slab, slab, sem).wait()
    r0 = r0_scr[...].reshape(tm, D_MODEL)
    r1 = r1_scr[...].reshape(tm, D_MODEL)
    y_ref[...] = x1_ref[...] + gt[:, 0:1] * r0 + gt[:, 1:2] * r1


def _combine(dest, x1, rg, ys, tm):
    t = x1.shape[0]
    row = lambda i: (i, 0)
    col = lambda i: (0, i)
    return pl.pallas_call(
        functools.partial(_combine_kernel, tm=tm),
        grid=(t // tm,),
        in_specs=[
            pl.BlockSpec((2 * tm,), lambda i: (i,), memory_space=pltpu.SMEM),
            pl.BlockSpec((tm, D_MODEL), row),
            pl.BlockSpec((8, tm), col),
            pl.BlockSpec(memory_space=pl.ANY),
        ],
        out_specs=pl.BlockSpec((tm, D_MODEL), row),
        out_shape=jax.ShapeDtypeStruct((t, D_MODEL), F32),
        scratch_shapes=[
            pltpu.VMEM((tm // SUBLANES, SUBLANES, D_MODEL), F32),
            pltpu.VMEM((tm // SUBLANES, SUBLANES, D_MODEL), F32),
            pltpu.SemaphoreType.DMA,
        ],
        compiler_params=_cparams(("arbitrary",)),
        name="combine",
    )(_tile_slots(dest, tm), x1, rg, ys)


def _prep_weights(g_attn, w_in, g_cq, w_uq, g_ckv, w_ukv, g_q, g_k, b_igate, b_fgate, g_mh,
                  w_out, g_ffn, w_group, b_group, w_erouter, b_erouter, w1, w3, w2):
    def cols(a, b):
        return w_in[:, a:b]

    o_cq, o_ckv, o_kr = 0, 256, 384
    o_mq, o_mk, o_mv, o_mi, o_mf, o_mo = 416, 672, 928, 1440, 1444, 1448
    hr = MLA_ROPE // 2
    misc = jnp.concatenate([
        cols(o_kr, o_kr + MLA_ROPE), -cols(o_kr + hr, o_kr + MLA_ROPE), cols(o_kr, o_kr + hr),
        cols(o_mi, o_mi + M_HEADS), cols(o_mf, o_mf + M_HEADS),
        jnp.zeros((D_MODEL, LANES - 2 * MLA_ROPE - 2 * M_HEADS), F32)], axis=1)
    w_p = jnp.concatenate([
        cols(o_cq, o_cq + 256), cols(o_ckv, o_ckv + 128), cols(o_mq, o_mq + 256), cols(o_mk, o_mk + 256),
        cols(o_mv, o_mv + 512), cols(o_mo, o_mo + 512), misc], axis=1).astype(BF16)
    nope, r1, r2 = w_uq[..., :MLA_NOPE], w_uq[..., MLA_NOPE:MLA_NOPE + hr], w_uq[..., MLA_NOPE + hr:]
    zq = lambda w: jnp.zeros((MLA_Q_LORA, MLA_HEADS, w), F32)
    w_qa = jnp.concatenate([nope, r1, r2, zq(HEAD_PAD - MLA_QK)], axis=-1).reshape(MLA_Q_LORA, MLA_HEADS * HEAD_PAD)
    w_qb = jnp.concatenate([zq(MLA_NOPE), -r2, r1, zq(HEAD_PAD - MLA_QK)], axis=-1).reshape(MLA_Q_LORA,
                                                                                           MLA_HEADS * HEAD_PAD)
    w_q = jnp.concatenate([w_qa, w_qb], axis=1).astype(BF16)
    pad_g = jnp.zeros((HEAD_PAD - MLA_QK,), F32)
    w_k = jnp.concatenate([w_ukv[..., :MLA_NOPE], jnp.zeros((MLA_KV_LORA, MLA_HEADS, HEAD_PAD - MLA_NOPE), F32)],
                          axis=-1).transpose(1, 0, 2).astype(BF16)
    w_v = jnp.concatenate([w_ukv[..., MLA_NOPE:], jnp.zeros((MLA_KV_LORA, MLA_HEADS, LANES - MLA_V), F32)],
                          axis=-1).transpose(1, 0, 2).astype(BF16)
    gate_bias = jnp.concatenate([jnp.zeros((_L_IG,), F32), b_igate, b_fgate,
                                 jnp.zeros((LANES - _L_FG - M_HEADS,), F32)])[None]
    w_r = jnp.concatenate([w_erouter.T, w_group.T, jnp.zeros((R_ROWS - N_EXPERTS - N_GROUPS, D_MODEL), F32)],
                          axis=0).astype(BF16)
    b_r = jnp.concatenate([b_erouter, b_group, jnp.zeros((R_ROWS - N_EXPERTS - N_GROUPS,), F32)])
    return {
        "w_in": w_p, "g_attn": g_attn[None], "g_cq": g_cq[None], "w_q": w_q, "g_ckv": g_ckv[None],
        "g_q": jnp.concatenate([g_q, pad_g])[None], "g_k": jnp.concatenate([g_k, pad_g])[None],
        "gate_bias": gate_bias, "w_k": w_k, "w_v": w_v, "g_mh": g_mh,
        "w_out": w_out.astype(BF16), "g_ffn": g_ffn[None], "w_r": w_r,
        "b_r": jnp.broadcast_to(b_r[:, None], (R_ROWS, LANES)),
        "w1": w1, "w3": w3, "w2": w2,
        "score_bound": (MLA_QK * MLA_SCALE * LOG2E * 1.01 * jnp.max(jnp.abs(g_q)) * jnp.max(jnp.abs(g_k))).reshape(1),
    }


def _rope_table(pos):
    half = MLA_ROPE // 2
    inv = ROPE_BASE ** (-np.arange(half, dtype=np.float64) / half)
    ang = np.asarray(pos, np.float64)[:, None] * inv[None, :]
    cos = np.cos(ang)
    sin = np.sin(ang)
    c2 = np.concatenate([cos, cos], axis=1)
    s2 = np.concatenate([sin, sin], axis=1)
    return jnp.asarray(np.concatenate([c2, s2, c2, s2], axis=1), F32)


def _pick(n, pref):
    return pref if n % pref == 0 else n


def _slots_kernel(ri_ref, ps_ref, d_ref):
    tm = ri_ref.shape[1]
    xi = lax.broadcasted_iota(I32, (N_EXPERTS, tm), 0)
    ps = ps_ref[:, 0:1]
    rows = []
    for k in range(2):
        start = jnp.sum(jnp.where(xi == ri_ref[k:k + 1, :], ps, 0), axis=0, keepdims=True)
        rows.append(start + ri_ref[2 + k:3 + k, :])
    d_ref[...] = jnp.concatenate(rows + [jnp.zeros((6, tm), I32)], axis=0)


def _slots(ri, pstart_b, tm):
    t = ri.shape[1]
    return pl.pallas_call(
        _slots_kernel,
        grid=(t // tm,),
        in_specs=[pl.BlockSpec((8, tm), lambda i: (0, i)), pl.BlockSpec((N_EXPERTS, LANES), lambda i: (0, 0))],
        out_specs=pl.BlockSpec((8, tm), lambda i: (0, i)),
        out_shape=jax.ShapeDtypeStruct((8, t), I32),
        compiler_params=_cparams(("arbitrary",)),
        name="slots",
    )(ri, pstart_b)


def _moe_layer(x2d, att, hm, pw, *, tm_route, bm, tm_disp, tm_comb):
    t = x2d.shape[0]
    x1, hp, ri, rg, cnt = _route(x2d, att, hm, pw, tm_route)
    counts = cnt[:, 0]
    padded = (counts + bm - 1) // bm * bm
    pend = jnp.cumsum(padded)
    pstart = pend - padded
    dest = _slots(ri, jnp.broadcast_to(pstart[:, None], (N_EXPERTS, LANES)), _pick(t, 2048))
    n_slots = (2 * t // bm + N_EXPERTS) * bm
    nblk = n_slots // bm
    n_used = (pend[-1] // bm).astype(I32)
    experts = jnp.arange(N_EXPERTS, dtype=I32)
    blk_first = jnp.arange(nblk, dtype=I32) * bm
    blk_e = jnp.sum((pend[None, :] <= blk_first[:, None]).astype(I32), axis=1)
    blk_e = jnp.minimum(blk_e, jnp.max(jnp.where(counts > 0, experts, 0)))
    zero_blocks = jnp.concatenate([jnp.where(counts > 0, pend // bm - 1, -1),
                                   jnp.where(n_used + experts < nblk, n_used + experts, -1)]).astype(I32)
    xs = _dispatch(zero_blocks, dest, hp, n_slots, tm_disp, bm)
    ys = _experts(blk_e, n_used[None], xs, pw, bm)
    return _combine(dest, x1, rg, ys, tm_comb)


def _state_pack(c, n):
    ct = jnp.swapaxes(c, -1, -2)
    s = jnp.concatenate([ct, jnp.broadcast_to(n[..., None], ct.shape[:-1] + (S_W - M_DV,))], axis=-1)
    return s.reshape((s.shape[0],) + S_SHAPE)


def _state_unpack(s, m):
    s = s.reshape(s.shape[0], M_HEADS, M_DK, S_W)
    return jnp.swapaxes(s[..., :M_DV], -1, -2), s[..., M_DV], m[:, :M_HEADS, 0]


def kernel(x_prompt, x_sample, cache_ckv, cache_krope, state_mlstm_c, state_mlstm_n, state_mlstm_m,
           meta_tokens, g_attn, w_in, g_cq, w_uq, g_ckv, w_ukv, g_q, g_k, b_igate, b_fgate, g_mh,
           w_out, g_ffn, w_group, b_group, w_erouter, b_erouter, w1, w3, w2):
    bp, seq = x_prompt.shape[:2]
    bs, dec = x_sample.shape[:2]
    past = cache_ckv.shape[2]
    layer = 0
    pw = _prep_weights(g_attn[layer], w_in[layer], g_cq[layer], w_uq[layer], g_ckv[layer], w_ukv[layer],
                       g_q[layer], g_k[layer], b_igate[layer], b_fgate[layer], g_mh[layer], w_out[layer],
                       g_ffn[layer], w_group[layer], b_group[layer], w_erouter[layer], b_erouter[layer],
                       w1[layer], w3[layer], w2[layer])

    xm = jnp.concatenate([meta_tokens, jnp.zeros((META_PAD - N_META, D_MODEL), F32)], axis=0)
    tab_m = _rope_table(np.arange(META_PAD) - N_META)
    _, ckv_m, misc_m, mq_m, mk_m, mv_m, og_m = _project(xm, tab_m, pw, META_PAD)
    zero_s = jnp.zeros((1,) + S_SHAPE, F32)
    zero_m = jnp.zeros((1, 8, LANES), F32)
    _, s_meta, m_meta = _mlstm(mq_m, mk_m, mv_m, og_m, misc_m, zero_s, zero_m, pw["g_mh"],
                               nb=1, l=META_PAD, n_valid=N_META)

    tp = bp * seq
    xp2 = x_prompt.reshape(tp, D_MODEL)
    tm_p = _pick(seq, 1024)
    q_p, ckv_p, misc_p, mq_p, mk_p, mv_p, og_p = _project(xp2, _rope_table(np.arange(seq)), pw, tm_p)
    tq = _pick(seq, 1024)
    att_p = _attention(q_p, ckv_p, misc_p, ckv_m[None], misc_m[None], pw, nb=bp, lq=seq, lf=seq, tq=tq, tk=256,
                       causal=True, n_tail=N_META)
    hm_p, s_p, m_p = _mlstm(mq_p, mk_p, mv_p, og_p, misc_p, s_meta, m_meta, pw["g_mh"],
                            nb=bp, l=seq)
    y_p = _moe_layer(xp2, att_p, hm_p, pw, tm_route=tm_p, bm=1024, tm_disp=_pick(seq, 512),
                     tm_comb=_pick(seq, 512))

    ts = bs * dec
    xs2 = x_sample.reshape(ts, D_MODEL)
    q_s, ckv_s, misc_s, mq_s, mk_s, mv_s, og_s = _project(xs2, _rope_table(past + np.arange(dec)), pw, dec)
    kr_cache = jnp.concatenate([cache_krope[layer], jnp.zeros((bs, past, LANES - MLA_ROPE), F32)], axis=-1)
    n_tail = dec + N_META
    assert n_tail <= META_PAD

    def tail_rows(own, meta):
        w = own.shape[-1]
        return jnp.concatenate([own.reshape(bs, dec, w), jnp.broadcast_to(meta[None, :N_META], (bs, N_META, w)),
                                jnp.zeros((bs, META_PAD - n_tail, w), F32)], axis=1)

    att_s = _attention(q_s, cache_ckv[layer].reshape(bs * past, MLA_KV_LORA), kr_cache.reshape(bs * past, LANES),
                       tail_rows(ckv_s, ckv_m), tail_rows(misc_s, misc_m), pw, nb=bs, lq=dec, lf=past, tq=dec,
                       tk=_pick(past, 256), causal=False, n_tail=n_tail)
    s0 = _state_pack(state_mlstm_c[layer], state_mlstm_n[layer])
    m0 = jnp.concatenate([jnp.broadcast_to(state_mlstm_m[layer][:, :, None], (bs, M_HEADS, LANES)),
                          jnp.zeros((bs, 8 - M_HEADS, LANES), F32)], axis=1)
    hm_s, s_s, m_s = _mlstm(mq_s, mk_s, mv_s, og_s, misc_s, s0, m0, pw["g_mh"], nb=bs, l=dec)
    y_s = _moe_layer(xs2, att_s, hm_s, pw, tm_route=_pick(ts, 512), bm=128, tm_disp=_pick(ts, 512),
                     tm_comb=_pick(ts, 512))

    m_ckv = ckv_m[:N_META]
    m_kr = misc_m[:N_META, :MLA_ROPE]
    new_ckv_p = jnp.concatenate([jnp.broadcast_to(m_ckv[None], (bp, N_META, MLA_KV_LORA)),
                                 ckv_p.reshape(bp, seq, MLA_KV_LORA)], axis=1)[None]
    new_kr_p = jnp.concatenate([jnp.broadcast_to(m_kr[None], (bp, N_META, MLA_ROPE)),
                                misc_p[:, :MLA_ROPE].reshape(bp, seq, MLA_ROPE)], axis=1)[None]
    c_p, n_p, mm_p = _state_unpack(s_p, m_p)
    c_s, n_s, mm_s = _state_unpack(s_s, m_s)
    return (y_p.reshape(bp, seq, D_MODEL), y_s.reshape(bs, dec, D_MODEL),
            new_ckv_p, new_kr_p, c_p[None], n_p[None], mm_p[None],
            ckv_s.reshape(bs, dec, MLA_KV_LORA)[None], misc_s[:, :MLA_ROPE].reshape(bs, dec, MLA_ROPE)[None],
            c_s[None], n_s[None], mm_s[None])
```

```python
import functools

import numpy as np
import jax
import jax.numpy as jnp
from jax import lax
from jax.experimental import pallas as pl
from jax.experimental.pallas import tpu as pltpu

F32 = jnp.float32
BF16 = jnp.bfloat16
I32 = jnp.int32
U32 = jnp.uint32

D_MODEL = 1024
CHUNK = 64
N_META = 16
MLA_HEADS = 8
MLA_V = 64
MLA_NOPE = 64
MLA_ROPE = 32
MLA_QK = MLA_NOPE + MLA_ROPE
MLA_Q_LORA = 256
MLA_KV_LORA = 128
MLA_SCALE = MLA_QK ** -0.5
ROPE_BASE = 10000.0
M_HEADS = 4
M_DV = 128
M_DK = 64
N_GROUPS = 4
E_PER_GROUP = 8
N_EXPERTS = 32
EXPERT_FF = 512
EPS = 1e-6

LANES = 128
HEAD_PAD = 128
PROJ_PAD = 2048
META_PAD = 128
NEG_BIG = -1e30
VMEM_LIMIT = 56 * 1024 * 1024

_O_CQ, _O_CKV, _O_MQ, _O_MK, _O_MV, _O_MO, _O_MISC = 0, 256, 384, 640, 896, 1408, 1920
_L_IG, _L_FG = 64, 68


def _cparams(sem):
    return pltpu.CompilerParams(dimension_semantics=sem, vmem_limit_bytes=VMEM_LIMIT)


def _dot(a, b):
    return jnp.dot(a, b, preferred_element_type=F32)


def _dot_nt(a, b):
    return lax.dot_general(a, b, (((1,), (1,)), ((), ())), preferred_element_type=F32)


def _split3(x):
    x1 = x.astype(BF16)
    r1 = x - x1.astype(F32)
    x2 = r1.astype(BF16)
    x3 = (r1 - x2.astype(F32)).astype(BF16)
    return x1, x2, x3


def _proj_kernel(x_ref, tab_ref, w_ref, g_ref, gcq_ref, wq_ref, gckv_ref, gq_ref, bias_ref,
                 q_ref, ckv_ref, misc_ref, mq_ref, mk_ref, mv_ref, og_ref):
    x = x_ref[...]
    xn = x * lax.rsqrt(jnp.mean(x * x, axis=-1, keepdims=True) + EPS) * g_ref[...]
    xb = xn.astype(BF16)
    tab = tab_ref[...]
    lane = lax.broadcasted_iota(I32, tab.shape, 1)

    def proj(off, width):
        return _dot(xb, w_ref[:, off:off + width])

    cq = proj(_O_CQ, MLA_Q_LORA)
    cqn = cq * lax.rsqrt(jnp.mean(cq * cq, axis=-1, keepdims=True) + EPS) * gcq_ref[...]
    cqb = cqn.astype(BF16)
    qw = MLA_HEADS * HEAD_PAD
    qa = _dot(cqb, wq_ref[:, :qw])
    qb = _dot(cqb, wq_ref[:, qw:])
    ckv = proj(_O_CKV, MLA_KV_LORA)
    zm = proj(_O_MISC, LANES)
    z_mq = proj(_O_MQ, M_HEADS * M_DK)
    z_mk = proj(_O_MK, M_HEADS * M_DK)
    z_mv = proj(_O_MV, M_HEADS * M_DV)
    z_mo = proj(_O_MO, M_HEADS * M_DV)
    gq = gq_ref[...]
    tab_a = jnp.where(lane < MLA_NOPE, 1.0, jnp.where(lane < MLA_QK, tab, 0.0))
    tab_b = jnp.where((lane >= MLA_NOPE) & (lane < MLA_QK), pltpu.roll(tab, LANES - MLA_ROPE, 1), 0.0)
    slab = 2 * HEAD_PAD
    pair_ones = (lax.broadcasted_iota(I32, (slab, slab), 0) // HEAD_PAD
                 == lax.broadcasted_iota(I32, (slab, slab), 1) // HEAD_PAD).astype(BF16)
    for p in range(MLA_HEADS // 2):
        za = qa[:, p * slab:(p + 1) * slab]
        ms = _dot((za * za).astype(BF16), pair_ones) * (1.0 / MLA_QK)
        for e in range(2):
            h = 2 * p + e
            qh = za[:, e * HEAD_PAD:(e + 1) * HEAD_PAD] * tab_a + qb[:, h * HEAD_PAD:(h + 1) * HEAD_PAD] * tab_b
            q_ref[h] = (qh * lax.rsqrt(ms[:, e * HEAD_PAD:(e + 1) * HEAD_PAD] + EPS) * gq).astype(BF16)

    ckv_ref[...] = ckv * lax.rsqrt(jnp.mean(ckv * ckv, axis=-1, keepdims=True) + EPS) * gckv_ref[...]

    y = zm * tab
    rot = y + pltpu.roll(y, LANES - MLA_ROPE, 1)
    gate = zm + bias_ref[...]
    logf = jnp.minimum(gate, 0.0) - jnp.log1p(jnp.exp(-jnp.abs(gate)))
    misc = jnp.where(lane < MLA_ROPE, rot,
                     jnp.where((lane >= _L_IG) & (lane < _L_FG), gate,
                               jnp.where((lane >= _L_FG) & (lane < _L_FG + M_HEADS), logf, 0.0)))
    misc_ref[...] = misc

    mq_ref[...] = z_mq.astype(BF16)
    mk_ref[...] = (z_mk * (M_DK ** -0.5)).astype(BF16)
    mv_ref[...] = z_mv.astype(BF16)
    og_ref[...] = jax.nn.sigmoid(z_mo).astype(BF16)


def _project(x2d, tab, pw, tm):
    t = x2d.shape[0]
    nt = t // tm
    ntab = tab.shape[0] // tm
    row = lambda i: (i, 0)
    full = lambda i: (0, 0)
    return pl.pallas_call(
        _proj_kernel,
        grid=(nt,),
        in_specs=[
            pl.BlockSpec((tm, D_MODEL), row),
            pl.BlockSpec((tm, LANES), lambda i: (i % ntab, 0)),
            pl.BlockSpec((D_MODEL, PROJ_PAD), full),
            pl.BlockSpec((1, D_MODEL), full),
            pl.BlockSpec((1, MLA_Q_LORA), full),
            pl.BlockSpec((MLA_Q_LORA, 2 * MLA_HEADS * HEAD_PAD), full),
            pl.BlockSpec((1, MLA_KV_LORA), full),
            pl.BlockSpec((1, HEAD_PAD), full),
            pl.BlockSpec((1, LANES), full),
        ],
        out_specs=[
            pl.BlockSpec((MLA_HEADS, tm, HEAD_PAD), lambda i: (0, i, 0)),
            pl.BlockSpec((tm, MLA_KV_LORA), row),
            pl.BlockSpec((tm, LANES), row),
            pl.BlockSpec((tm, M_HEADS * M_DK), row),
            pl.BlockSpec((tm, M_HEADS * M_DK), row),
            pl.BlockSpec((tm, M_HEADS * M_DV), row),
            pl.BlockSpec((tm, M_HEADS * M_DV), row),
        ],
        out_shape=[
            jax.ShapeDtypeStruct((MLA_HEADS, t, HEAD_PAD), BF16),
            jax.ShapeDtypeStruct((t, MLA_KV_LORA), F32),
            jax.ShapeDtypeStruct((t, LANES), F32),
            jax.ShapeDtypeStruct((t, M_HEADS * M_DK), BF16),
            jax.ShapeDtypeStruct((t, M_HEADS * M_DK), BF16),
            jax.ShapeDtypeStruct((t, M_HEADS * M_DV), BF16),
            jax.ShapeDtypeStruct((t, M_HEADS * M_DV), BF16),
        ],
        compiler_params=_cparams(("arbitrary",)),
        name="projection",
    )(x2d, tab, pw["w_in"], pw["g_attn"], pw["g_cq"], pw["w_q"], pw["g_ckv"], pw["g_q"], pw["gate_bias"])


SAFE_BOUND = 40.0
LOG2E = 1.4426950408889634


def _attn_kernel(sb_ref, q_ref, ckv_ref, misc_ref, ckvt_ref, misct_ref, wk_ref, wv_ref, gk_ref, o_ref,
                 kt_scr, v_scr, acc_scr, p_scr, *, lf, tq, tk, bt, causal, n_tail):
    i = pl.program_id(1)
    gk = gk_ref[...]

    def build(ckv_rows, misc_rows, dst):
        n = ckv_rows.shape[0]
        cb = ckv_rows.astype(BF16)
        kt_all = _dot_nt(wk_ref[...], cb)
        sel = (lax.broadcasted_iota(I32, (HEAD_PAD, LANES), 0)
               == lax.broadcasted_iota(I32, (HEAD_PAD, LANES), 1) + MLA_NOPE).astype(BF16)
        sel = jnp.where(lax.broadcasted_iota(I32, (HEAD_PAD, LANES), 1) < MLA_ROPE, sel, jnp.zeros_like(sel))
        m1, m2, m3 = _split3(misc_rows)
        kr_t = _dot_nt(sel, m1) + _dot_nt(sel, m2) + _dot_nt(sel, m3)
        v_all = _dot(cb, wv_ref[...])
        onecol = (lax.broadcasted_iota(I32, (n, LANES), 1) == MLA_V).astype(F32)
        gkc = jnp.tile(gk, (1, n // LANES))
        for h in range(MLA_HEADS):
            kk = kt_all[h * HEAD_PAD:(h + 1) * HEAD_PAD, :] + kr_t
            ms = jnp.sum(kk * kk, axis=0, keepdims=True) * (1.0 / MLA_QK)
            kt_scr[h, :, pl.ds(dst, n)] = (kk * lax.rsqrt(ms + EPS) * gkc).astype(BF16)
            v_scr[h, pl.ds(dst, n), :] = (v_all[:, h * LANES:(h + 1) * LANES] + onecol).astype(BF16)

    @pl.when(i == 0)
    def _():
        def body(r, carry):
            r0 = pl.multiple_of(r * bt, bt)
            build(ckv_ref[pl.ds(r0, bt), :], misc_ref[pl.ds(r0, bt), :], r0)
            return carry

        lax.fori_loop(0, lf // bt, body, 0)
        build(ckvt_ref[0], misct_ref[0], lf)

    ndiag = tq // tk if causal else 0
    nfull = i * ndiag if causal else lf // tk
    tail_mask = lax.broadcasted_iota(I32, (tq, META_PAD), 1) < n_tail

    def kt_tile(h, j):
        return kt_scr[h, :, pl.ds(pl.multiple_of(j * tk, tk), tk)]

    def v_tile(h, j):
        return v_scr[h, pl.ds(pl.multiple_of(j * tk, tk), tk), :]

    def chunk_ids(lo):
        rows = lax.broadcasted_iota(I32, (tq - lo, tk), 0) // CHUNK
        cols = lax.broadcasted_iota(I32, (tq - lo, tk), 1) // CHUNK
        return rows, cols

    def finish(h):
        acc = acc_scr[h]
        o_ref[:, h * MLA_V:(h + 1) * MLA_V] = (acc[:, :MLA_V] / acc[:, MLA_V:MLA_V + 1]).astype(BF16)

    safe = sb_ref[0] <= SAFE_BOUND * LOG2E

    @pl.when(safe)
    def _():
        bound = sb_ref[0]

        def scores(h, lo, kt):
            return _dot(q_ref[h, lo:, :], kt) * (MLA_SCALE * LOG2E) - bound

        def probs(s, mask):
            if mask is not None:
                s = jnp.where(mask, s, -jnp.inf)
            return jnp.exp2(s).astype(BF16)

        def pipe_step(j, lo, lo_prev, mask):
            for h in range(MLA_HEADS):
                acc_scr[h, lo_prev:, :] += _dot(p_scr[(j - 1) % 2, h, lo_prev:, :], v_tile(h, j - 1))
            for h in range(MLA_HEADS):
                p_scr[j % 2, h, lo:, :] = probs(scores(h, lo, kt_tile(h, j)), mask)

        for h in range(MLA_HEADS):
            p_tail = probs(scores(h, 0, kt_scr[h, :, lf:lf + META_PAD]), tail_mask)
            acc_scr[h] = _dot(p_tail, v_scr[h, lf:lf + META_PAD, :])
        rows0, cols0 = chunk_ids(0)
        if causal:
            first_mask = cols0 <= rows0 + jnp.where(nfull > 0, tk, 0)
        else:
            first_mask = None
        for h in range(MLA_HEADS):
            p_scr[0, h] = probs(scores(h, 0, kt_tile(h, 0)), first_mask)

        def body(j, carry):
            pipe_step(j, 0, 0, None)
            return carry

        lax.fori_loop(1, nfull, body, 0)
        if causal:
            @pl.when(nfull >= 1)
            def _():
                pipe_step(nfull, 0, 0, cols0 <= rows0)
            for d in range(1, ndiag):
                rows_d, cols_d = chunk_ids(d * tk)
                pipe_step(nfull + d, d * tk, (d - 1) * tk, cols_d <= rows_d)
            last, lo_last = nfull + ndiag - 1, (ndiag - 1) * tk
        else:
            last, lo_last = nfull - 1, 0
        for h in range(MLA_HEADS):
            acc_scr[h, lo_last:, :] += _dot(p_scr[last % 2, h, lo_last:, :], v_tile(h, last))
            finish(h)

    @pl.when(jnp.logical_not(safe))
    def _():
        rows0, cols0 = chunk_ids(0)
        for h in range(MLA_HEADS):
            qh = q_ref[h]

            def step(carry, kt, vrows, mask):
                m, acc = carry
                s = _dot(qh, kt) * MLA_SCALE
                if mask is not None:
                    s = jnp.where(mask, s, -jnp.inf)
                m_new = jnp.maximum(m, jnp.max(s, axis=-1, keepdims=True))
                acc = jnp.exp(m - m_new) * acc + _dot(jnp.exp(s - m_new).astype(BF16), vrows)
                return m_new, acc

            def body(j, c):
                mask = (cols0 + j * (tk // CHUNK) <= rows0 + i * (tq // CHUNK)) if causal else None
                return step(c, kt_tile(h, j), v_tile(h, j), mask)

            carry = (jnp.full((tq, 1), -jnp.inf, F32), jnp.zeros((tq, LANES), F32))
            carry = step(carry, kt_scr[h, :, lf:lf + META_PAD], v_scr[h, lf:lf + META_PAD, :], tail_mask)
            carry = lax.fori_loop(0, nfull + ndiag, body, carry)
            acc_scr[h] = carry[1]
            finish(h)


def _attention(q, ckv_f, misc_f, ckv_t, misc_t, pw, *, nb, lq, lf, tq, tk, causal, n_tail):
    nq = lq // tq
    bt = min(512, lf)
    assert lf % bt == 0 and lf % tk == 0 and lq % tq == 0 and tk % CHUNK == 0 and (not causal or tq % tk == 0)
    kern = functools.partial(_attn_kernel, lf=lf, tq=tq, tk=tk, bt=bt, causal=causal, n_tail=n_tail)
    full2 = lambda b, i, sb: (0, 0)
    full3 = lambda b, i, sb: (0, 0, 0)
    tail = full3 if ckv_t.shape[0] == 1 else (lambda b, i, sb: (b, 0, 0))
    lk = lf + META_PAD
    return pl.pallas_call(
        kern,
        grid_spec=pltpu.PrefetchScalarGridSpec(
            num_scalar_prefetch=1,
            grid=(nb, nq),
            in_specs=[
                pl.BlockSpec((MLA_HEADS, tq, HEAD_PAD), lambda b, i, sb: (0, b * nq + i, 0)),
                pl.BlockSpec((lf, MLA_KV_LORA), lambda b, i, sb: (b, 0)),
                pl.BlockSpec((lf, LANES), lambda b, i, sb: (b, 0)),
                pl.BlockSpec((1, META_PAD, MLA_KV_LORA), tail),
                pl.BlockSpec((1, META_PAD, LANES), tail),
                pl.BlockSpec((MLA_HEADS * HEAD_PAD, MLA_KV_LORA), full2),
                pl.BlockSpec((MLA_KV_LORA, MLA_HEADS * LANES), full2),
                pl.BlockSpec((HEAD_PAD, LANES), full2),
            ],
            out_specs=pl.BlockSpec((tq, MLA_HEADS * MLA_V), lambda b, i, sb: (b * nq + i, 0)),
            scratch_shapes=[
                pltpu.VMEM((MLA_HEADS, HEAD_PAD, lk), BF16),
                pltpu.VMEM((MLA_HEADS, lk, LANES), BF16),
                pltpu.VMEM((MLA_HEADS, tq, LANES), F32),
                pltpu.VMEM((2, MLA_HEADS, tq, tk), BF16),
            ],
        ),
        out_shape=jax.ShapeDtypeStruct((nb * lq, MLA_HEADS * MLA_V), BF16),
        compiler_params=_cparams(("arbitrary", "arbitrary")),
        name="attention",
    )(pw["score_bound"], q, ckv_f, misc_f, ckv_t, misc_t, pw["w_k"], pw["w_v"], pw["g_k"])


S_W = 2 * M_DV
S_SHAPE = (M_HEADS // 2, 2 * M_DK, S_W)


MLSTM_CHUNK = LANES


def _mlstm_kernel(q_ref, k_ref, v_ref, og_ref, misc_ref, s0_ref, m0_ref, gmh_ref,
                  h_ref, sout_ref, mout_ref, s_scr, m_scr, *, n_valid):
    c = pl.program_id(1)
    nc = pl.num_programs(1)
    lb = q_ref.shape[0]
    lc = MLSTM_CHUNK

    @pl.when(c == 0)
    def _():
        s_scr[...] = s0_ref[0]
        m_scr[...] = m0_ref[0]

    def rows(x):
        if lb == lc:
            return x
        return jnp.concatenate([x, jnp.zeros((lc - lb,) + x.shape[1:], x.dtype)], axis=0)

    g = rows(misc_ref[...])
    limit = lb if n_valid is None else n_valid - c * lb
    row = lax.broadcasted_iota(I32, g.shape, 0)
    lane = lax.broadcasted_iota(I32, g.shape, 1)
    g = jnp.where(row < limit, g, jnp.where((lane >= _L_IG) & (lane < _L_FG), NEG_BIG, 0.0))
    gt = g.T
    gt8 = gt[_L_IG:_L_IG + 2 * M_HEADS, :]
    rr = lax.broadcasted_iota(I32, (lc, lc), 0)
    cc = lax.broadcasted_iota(I32, (lc, lc), 1)
    causal = cc <= rr
    g1, g2, g3 = _split3(g)
    tri = causal.astype(BF16)
    b_cols = _dot(tri, g1) + _dot(tri, g2) + _dot(tri, g3)
    t1, t2, t3 = _split3(gt8)
    trit = (rr <= cc).astype(BF16)
    b_rows = (_dot(t1, trit) + _dot(t2, trit) + _dot(t3, trit))[M_HEADS:2 * M_HEADS, :]
    a = gt8[0:M_HEADS, :] - b_rows
    m_prev = m_scr[0:M_HEADS, :]
    u_b = [-jnp.maximum(m_prev[h:h + 1, :],
                        jnp.broadcast_to(jnp.max(jnp.where(causal, a[h:h + 1, :], -jnp.inf), axis=-1, keepdims=True),
                                         (lc, LANES))) for h in range(M_HEADS)]
    negm_b = [u_b[h] - jnp.broadcast_to(b_cols[:, _L_FG + h:_L_FG + h + 1], (lc, LANES)) for h in range(M_HEADS)]
    u_last = jnp.concatenate([u_b[h][lc - 1:lc, 0:1] for h in range(M_HEADS)], axis=0)
    m_new = jnp.broadcast_to(b_rows[:, lc - 1:lc] - u_last, (M_HEADS, LANES))
    w_state = jnp.exp(a + u_last)
    decay = jnp.exp(m_prev[:, 0:1] + u_last)

    q = rows(q_ref[...])
    k = rows(k_ref[...])
    v = rows(v_ref[...])
    kt = k.astype(F32).T
    ones_blk = jnp.ones((lc, M_DV), BF16)
    low_half = lax.broadcasted_iota(I32, (lc, LANES), 1) < M_DK

    heads = range(M_HEADS)
    mine = [jnp.logical_not(low_half) if h % 2 else low_half for h in heads]
    q_pair = [q[:, (h // 2) * LANES:(h // 2 + 1) * LANES] for h in heads]
    kh = [jnp.where(mine[h], k[:, (h // 2) * LANES:(h // 2 + 1) * LANES], jnp.zeros_like(q_pair[h])) for h in heads]
    qh = [jnp.where(mine[h], q_pair[h], jnp.zeros_like(q_pair[h])) for h in heads]
    vext = [jnp.concatenate([v[:, h * M_DV:(h + 1) * M_DV], ones_blk], axis=1) for h in heads]
    s_old = [s_scr[j] for j in range(M_HEADS // 2)]
    s_bf = [s.astype(BF16) for s in s_old]

    qk = [_dot_nt(q_pair[h], kh[h]) for h in heads]
    qs = [_dot(qh[h], s_bf[h // 2]) for h in heads]
    ktw = [(kt[h * M_DK:(h + 1) * M_DK, :] * w_state[h:h + 1, :]).astype(BF16) for h in heads]
    upd = [_dot(ktw[h], vext[h]) for h in heads]
    w =[(jnp.where(causal, jnp.exp(a[h:h + 1, :] + u_b[h]), 0.0) * qk[h]).astype(BF16) for h in heads]
    inter = [jnp.exp(m_prev[h:h + 1, :] + u_b[h]) for h in heads]
    r = [_dot(w[h], vext[h]) + jnp.tile(inter[h], (1, 2)) * qs[h] for h in heads]
    num = [r[h][:, :M_DV] for h in heads]
    sq = [num[h] * num[h] for h in heads]
    sq_hi = [s.astype(BF16) for s in sq]
    sq_lo = [(sq[h] - sq_hi[h].astype(F32)).astype(BF16) for h in heads]
    msn = [(_dot(sq_hi[h], ones_blk) + _dot(sq_lo[h], ones_blk)) * (1.0 / M_DV) for h in heads]
    for h in heads:
        rden = 1.0 / jnp.maximum(jnp.abs(r[h][:, M_DV:]), jnp.exp(negm_b[h]))
        scale = rden * lax.rsqrt(rden * rden * msn[h] + EPS)
        hn = num[h] * scale * gmh_ref[h:h + 1, :]
        out = (hn[:lb] * og_ref[:, h * M_DV:(h + 1) * M_DV].astype(F32)).astype(BF16)
        h_ref[:, h * M_DV:(h + 1) * M_DV] = out
    for j in range(M_HEADS // 2):
        new = [decay[h:h + 1, :] * s_old[j][(h % 2) * M_DK:(h % 2 + 1) * M_DK, :] + upd[h] for h in (2 * j, 2 * j + 1)]
        s_scr[j] = jnp.concatenate(new, axis=0)
    m_scr[0:M_HEADS, :] = m_new

    @pl.when(c == nc - 1)
    def _():
        sout_ref[0] = s_scr[...]
        mout_ref[0] = m_scr[...]


def _mlstm(mq, mk, mv, og, misc, s0, m0, gmh, *, nb, l, n_valid=None):
    lc = min(l, MLSTM_CHUNK)
    assert l % lc == 0 and lc % 16 == 0
    nc = l // lc
    shared = s0.shape[0] == 1
    st = (lambda b, c: (0, 0, 0, 0)) if shared else (lambda b, c: (b, 0, 0, 0))
    mst = (lambda b, c: (0, 0, 0)) if shared else (lambda b, c: (b, 0, 0))
    row = lambda b, c: (b * nc + c, 0)
    kern = functools.partial(_mlstm_kernel, n_valid=n_valid)
    return pl.pallas_call(
        kern,
        grid=(nb, nc),
        in_specs=[
            pl.BlockSpec((lc, M_HEADS * M_DK), row),
            pl.BlockSpec((lc, M_HEADS * M_DK), row),
            pl.BlockSpec((lc, M_HEADS * M_DV), row),
            pl.BlockSpec((lc, M_HEADS * M_DV), row),
            pl.BlockSpec((lc, LANES), row),
            pl.BlockSpec((1,) + S_SHAPE, st),
            pl.BlockSpec((1, 8, LANES), mst),
            pl.BlockSpec((M_HEADS, M_DV), lambda b, c: (0, 0)),
        ],
        out_specs=[
            pl.BlockSpec((lc, M_HEADS * M_DV), row),
            pl.BlockSpec((1,) + S_SHAPE, lambda b, c: (b, 0, 0, 0)),
            pl.BlockSpec((1, 8, LANES), lambda b, c: (b, 0, 0)),
        ],
        out_shape=[
            jax.ShapeDtypeStruct((nb * l, M_HEADS * M_DV), BF16),
            jax.ShapeDtypeStruct((nb,) + S_SHAPE, F32),
            jax.ShapeDtypeStruct((nb, 8, LANES), F32),
        ],
        scratch_shapes=[
            pltpu.VMEM(S_SHAPE, F32),
            pltpu.VMEM((8, LANES), F32),
        ],
        compiler_params=_cparams(("arbitrary", "arbitrary")),
        name="mlstm",
    )(mq, mk, mv, og, misc, s0, m0, gmh)


R_ROWS = 40


def _route_kernel(x_ref, att_ref, hm_ref, wo_ref, g_ref, wr_ref, br_ref,
                  x1_ref, hp_ref, ri_ref, rg_ref, cnt_ref, cnt_scr):
    i = pl.program_id(0)
    tm = x_ref.shape[0]

    @pl.when(i == 0)
    def _():
        cnt_scr[...] = jnp.zeros_like(cnt_scr)

    mix = jnp.concatenate([att_ref[...], hm_ref[...]], axis=1)
    x1 = x_ref[...] + _dot(mix, wo_ref[...])
    x1_ref[...] = x1
    hn = x1 * lax.rsqrt(jnp.mean(x1 * x1, axis=-1, keepdims=True) + EPS) * g_ref[...]
    hb = hn.astype(BF16)
    half = D_MODEL // 2
    hi = lax.bitcast_convert_type(hb[:, :half].astype(F32), U32)
    lo = lax.bitcast_convert_type(hb[:, half:].astype(F32), U32)
    hp_ref[...] = (hi & jnp.uint32(0xFFFF0000)) | (lo >> 16)

    logits = _dot_nt(wr_ref[...], hb) + br_ref[:, 0:1]
    e_log = logits[0:N_EXPERTS, :]
    g_log = logits[N_EXPERTS:N_EXPERTS + N_GROUPS, :]
    gmax = jnp.max(g_log, axis=0, keepdims=True)
    gsum = jnp.sum(jnp.exp(g_log - gmax), axis=0, keepdims=True)
    gi = lax.broadcasted_iota(I32, g_log.shape, 0)
    g_idx = jnp.min(jnp.where(g_log == gmax, gi, N_GROUPS), axis=0, keepdims=True)
    e_sel = jnp.zeros((E_PER_GROUP, tm), F32)
    for gg in range(N_GROUPS):
        e_sel = jnp.where(g_idx == gg, e_log[gg * E_PER_GROUP:(gg + 1) * E_PER_GROUP, :], e_sel)
    ei = lax.broadcasted_iota(I32, e_sel.shape, 0)
    m1 = jnp.max(e_sel, axis=0, keepdims=True)
    i1 = jnp.min(jnp.where(e_sel == m1, ei, E_PER_GROUP), axis=0, keepdims=True)
    e2 = jnp.where(ei == i1, -jnp.inf, e_sel)
    m2 = jnp.max(e2, axis=0, keepdims=True)
    i2 = jnp.min(jnp.where(e2 == m2, ei, E_PER_GROUP), axis=0, keepdims=True)
    ex = jnp.exp(m2 - m1)
    gp = 1.0 / gsum
    p1 = 1.0 / (1.0 + ex)
    gate1 = gp * p1
    gate2 = gp * (ex * p1)
    id1 = g_idx * E_PER_GROUP + i1
    id2 = g_idx * E_PER_GROUP + i2

    xi = lax.broadcasted_iota(I32, (N_EXPERTS, tm), 0)
    oh1 = xi == id1
    oh2 = xi == id2
    e_cnt = (oh1 | oh2).astype(F32)
    rr = lax.broadcasted_iota(I32, (tm, tm), 0)
    cc = lax.broadcasted_iota(I32, (tm, tm), 1)
    upper = (rr < cc).astype(BF16)
    pref = _dot(e_cnt.astype(BF16), upper) + cnt_scr[:, 0:1]
    rank1 = jnp.sum(jnp.where(oh1, pref, 0.0), axis=0, keepdims=True)
    rank2 = jnp.sum(jnp.where(oh2, pref, 0.0), axis=0, keepdims=True)
    cnt_new = cnt_scr[...] + jnp.sum(e_cnt, axis=1, keepdims=True)
    cnt_scr[...] = cnt_new
    cnt_ref[...] = cnt_new.astype(I32)

    zi = jnp.zeros((1, tm), I32)
    ri_ref[...] = jnp.concatenate([id1, id2, rank1.astype(I32), rank2.astype(I32), zi, zi, zi, zi], axis=0)
    zf = jnp.zeros((1, tm), F32)
    rg_ref[...] = jnp.concatenate([gate1, gate2, zf, zf, zf, zf, zf, zf], axis=0)


def _route(x2d, att, hm, pw, tm):
    t = x2d.shape[0]
    row = lambda i: (i, 0)
    col = lambda i: (0, i)
    full = lambda i: (0, 0)
    return pl.pallas_call(
        _route_kernel,
        grid=(t // tm,),
        in_specs=[
            pl.BlockSpec((tm, D_MODEL), row),
            pl.BlockSpec((tm, D_MODEL // 2), row),
            pl.BlockSpec((tm, D_MODEL // 2), row),
            pl.BlockSpec((D_MODEL, D_MODEL), full),
            pl.BlockSpec((1, D_MODEL), full),
            pl.BlockSpec((R_ROWS, D_MODEL), full),
            pl.BlockSpec((R_ROWS, LANES), full),
        ],
        out_specs=[
            pl.BlockSpec((tm, D_MODEL), row),
            pl.BlockSpec((tm, D_MODEL // 2), row),
            pl.BlockSpec((8, tm), col),
            pl.BlockSpec((8, tm), col),
            pl.BlockSpec((N_EXPERTS, LANES), full),
        ],
        out_shape=[
            jax.ShapeDtypeStruct((t, D_MODEL), F32),
            jax.ShapeDtypeStruct((t, D_MODEL // 2), U32),
            jax.ShapeDtypeStruct((8, t), I32),
            jax.ShapeDtypeStruct((8, t), F32),
            jax.ShapeDtypeStruct((N_EXPERTS, LANES), I32),
        ],
        scratch_shapes=[pltpu.VMEM((N_EXPERTS, LANES), F32)],
        compiler_params=_cparams(("arbitrary",)),
        name="route",
    )(x2d, att, hm, pw["w_out"], pw["g_ffn"], pw["w_r"], pw["b_r"])


SUBLANES = 8


N_ZERO_BLOCKS = 2 * N_EXPERTS


def _dispatch_kernel(zb_ref, dest_ref, hp_ref, xs_ref, zero_scr, sem, zsem, *, tm, bm):
    @pl.when(pl.program_id(0) == 0)
    def _():
        zero_scr[...] = jnp.zeros_like(zero_scr)

        def zcopy(k):
            return pltpu.make_async_copy(zero_scr, xs_ref.at[pl.ds(pl.multiple_of(zb_ref[k] * bm, bm), bm)], zsem)

        def zstart(k, carry):
            @pl.when(zb_ref[k] >= 0)
            def _():
                zcopy(k).start()
            return carry

        def zwait(k, carry):
            @pl.when(zb_ref[k] >= 0)
            def _():
                zcopy(k).wait()
            return carry

        lax.fori_loop(0, N_ZERO_BLOCKS, zstart, 0)
        lax.fori_loop(0, N_ZERO_BLOCKS, zwait, 0)

    def body(g, carry):
        for u in range(SUBLANES):
            t = g * SUBLANES + u
            src = hp_ref.at[g, pl.ds(u, 1), :]
            pltpu.make_async_copy(src, xs_ref.at[pl.ds(dest_ref[t], 1)], sem).start(priority=0)
            pltpu.make_async_copy(src, xs_ref.at[pl.ds(dest_ref[tm + t], 1)], sem).start(priority=1)
        return carry

    lax.fori_loop(0, tm // SUBLANES, body, 0)
    pltpu.make_async_copy(xs_ref.at[pl.ds(0, 2 * tm)], xs_ref.at[pl.ds(0, 2 * tm)], sem).wait()


def _tile_slots(dest, tm):
    t = dest.shape[1]
    return dest[0:2].reshape(2, t // tm, tm).transpose(1, 0, 2).reshape(2 * t)


def _dispatch(zero_blocks, dest, hp, n_slots, tm, bm):
    t = hp.shape[0]
    return pl.pallas_call(
        functools.partial(_dispatch_kernel, tm=tm, bm=bm),
        grid_spec=pltpu.PrefetchScalarGridSpec(
            num_scalar_prefetch=1,
            grid=(t // tm,),
            in_specs=[
                pl.BlockSpec((2 * tm,), lambda i, zb: (i,), memory_space=pltpu.SMEM),
                pl.BlockSpec((tm // SUBLANES, SUBLANES, D_MODEL // 2), lambda i, zb: (i, 0, 0)),
            ],
            out_specs=pl.BlockSpec(memory_space=pl.ANY),
            scratch_shapes=[
                pltpu.VMEM((bm, D_MODEL // 2), U32),
                pltpu.SemaphoreType.DMA,
                pltpu.SemaphoreType.DMA,
            ],
        ),
        out_shape=jax.ShapeDtypeStruct((n_slots, D_MODEL // 2), U32),
        compiler_params=_cparams(("arbitrary",)),
        name="dispatch",
    )(zero_blocks, _tile_slots(dest, tm), hp.reshape(t // SUBLANES, SUBLANES, D_MODEL // 2))


def _expert_kernel(be_ref, nu_ref, xs_ref, w1_ref, w3_ref, w2_ref, ys_ref, w1_scr, w3_scr, w2_scr):
    i = pl.program_id(0)
    new_expert = jnp.logical_or(i == 0, be_ref[i] != be_ref[jnp.maximum(i - 1, 0)])

    @pl.when(jnp.logical_and(new_expert, i < nu_ref[0]))
    def _():
        w1_scr[...] = w1_ref[0].astype(BF16)
        w3_scr[...] = w3_ref[0].astype(BF16)
        w2_scr[...] = w2_ref[0].astype(BF16)

    @pl.when(i < nu_ref[0])
    def _():
        xw = xs_ref[...]
        xa = lax.bitcast_convert_type(xw & jnp.uint32(0xFFFF0000), F32).astype(BF16)
        xb = lax.bitcast_convert_type(xw << 16, F32).astype(BF16)
        half = D_MODEL // 2
        h1 = _dot(xa, w1_scr[:half, :]) + _dot(xb, w1_scr[half:, :])
        h3 = _dot(xa, w3_scr[:half, :]) + _dot(xb, w3_scr[half:, :])
        a = (h1 * jax.nn.sigmoid(h1)) * h3
        ys_ref[...] = _dot(a.astype(BF16), w2_scr[...])

    @pl.when(pl.program_id(0) >= nu_ref[0])
    def _():
        ys_ref[...] = jnp.zeros_like(ys_ref)


def _experts(blk_e, n_used, xs, pw, bm):
    n_slots = xs.shape[0]
    nblk = n_slots // bm
    blk = lambda i, be, nu: (jnp.minimum(i, nu[0] - 1), 0)
    oblk = lambda i, be, nu: (i, 0)
    wsel = lambda i, be, nu: (be[i], 0, 0)
    return pl.pallas_call(
        _expert_kernel,
        grid_spec=pltpu.PrefetchScalarGridSpec(
            num_scalar_prefetch=2,
            grid=(nblk,),
            in_specs=[
                pl.BlockSpec((bm, D_MODEL // 2), blk),
                pl.BlockSpec((1, D_MODEL, EXPERT_FF), wsel),
                pl.BlockSpec((1, D_MODEL, EXPERT_FF), wsel),
                pl.BlockSpec((1, EXPERT_FF, D_MODEL), wsel),
            ],
            out_specs=pl.BlockSpec((bm, D_MODEL), oblk),
            scratch_shapes=[
                pltpu.VMEM((D_MODEL, EXPERT_FF), BF16),
                pltpu.VMEM((D_MODEL, EXPERT_FF), BF16),
                pltpu.VMEM((EXPERT_FF, D_MODEL), BF16),
            ],
        ),
        out_shape=jax.ShapeDtypeStruct((n_slots, D_MODEL), F32),
        compiler_params=_cparams(("arbitrary",)),
        name="experts",
    )(blk_e, n_used, xs, pw["w1"], pw["w3"], pw["w2"])


def _combine_kernel(dest_ref, x1_ref, rg_ref, ys_ref, y_ref, r0_scr, r1_scr, sem, *, tm):
    def body(g, carry):
        for u in range(SUBLANES):
            t = g * SUBLANES + u
            pltpu.make_async_copy(ys_ref.at[pl.ds(dest_ref[t], 1)], r0_scr.at[g, pl.ds(u, 1), :], sem).start(priority=0)
            pltpu.make_async_copy(ys_ref.at[pl.ds(dest_ref[tm + t], 1)], r1_scr.at[g, pl.ds(u, 1), :],
                                  sem).start(priority=1)
        return carry

    lax.fori_loop(0, tm // SUBLANES, body, 0)
    gt = jnp.concatenate([rg_ref[...], jnp.zeros((LANES - 8, tm), F32)], axis=0).T
    slab = ys_ref.at[pl.ds(0, tm)]
    pltpu.make_async_copy(slab, slab, sem).wait()
    pltpu.make_async_copy(slab, slab, sem).wait()
    r0 = r0_scr[...].reshape(tm, D_MODEL)
    r1 = r1_scr[...].reshape(tm, D_MODEL)
    y_ref[...] = x1_ref[...] + gt[:, 0:1] * r0 + gt[:, 1:2] * r1


def _combine(dest, x1, rg, ys, tm):
    t = x1.shape[0]
    row = lambda i: (i, 0)
    col = lambda i: (0, i)
    return pl.pallas_call(
        functools.partial(_combine_kernel, tm=tm),
        grid=(t // tm,),
        in_specs=[
            pl.BlockSpec((2 * tm,), lambda i: (i,), memory_space=pltpu.SMEM),
            pl.BlockSpec((tm, D_MODEL), row),
            pl.BlockSpec((8, tm), col),
            pl.BlockSpec(memory_space=pl.ANY),
        ],
        out_specs=pl.BlockSpec((tm, D_MODEL), row),
        out_shape=jax.ShapeDtypeStruct((t, D_MODEL), F32),
        scratch_shapes=[
            pltpu.VMEM((tm // SUBLANES, SUBLANES, D_MODEL), F32),
            pltpu.VMEM((tm // SUBLANES, SUBLANES, D_MODEL), F32),
            pltpu.SemaphoreType.DMA,
        ],
        compiler_params=_cparams(("arbitrary",)),
        name="combine",
    )(_tile_slots(dest, tm), x1, rg, ys)


def _prep_weights(g_attn, w_in, g_cq, w_uq, g_ckv, w_ukv, g_q, g_k, b_igate, b_fgate, g_mh,
                  w_out, g_ffn, w_group, b_group, w_erouter, b_erouter, w1, w3, w2):
    def cols(a, b):
        return w_in[:, a:b]

    o_cq, o_ckv, o_kr = 0, 256, 384
    o_mq, o_mk, o_mv, o_mi, o_mf, o_mo = 416, 672, 928, 1440, 1444, 1448
    hr = MLA_ROPE // 2
    misc = jnp.concatenate([
        cols(o_kr, o_kr + MLA_ROPE), -cols(o_kr + hr, o_kr + MLA_ROPE), cols(o_kr, o_kr + hr),
        cols(o_mi, o_mi + M_HEADS), cols(o_mf, o_mf + M_HEADS),
        jnp.zeros((D_MODEL, LANES - 2 * MLA_ROPE - 2 * M_HEADS), F32)], axis=1)
    w_p = jnp.concatenate([
        cols(o_cq, o_cq + 256), cols(o_ckv, o_ckv + 128), cols(o_mq, o_mq + 256), cols(o_mk, o_mk + 256),
        cols(o_mv, o_mv + 512), cols(o_mo, o_mo + 512), misc], axis=1).astype(BF16)
    nope, r1, r2 = w_uq[..., :MLA_NOPE], w_uq[..., MLA_NOPE:MLA_NOPE + hr], w_uq[..., MLA_NOPE + hr:]
    zq = lambda w: jnp.zeros((MLA_Q_LORA, MLA_HEADS, w), F32)
    w_qa = jnp.concatenate([nope, r1, r2, zq(HEAD_PAD - MLA_QK)], axis=-1).reshape(MLA_Q_LORA, MLA_HEADS * HEAD_PAD)
    w_qb = jnp.concatenate([zq(MLA_NOPE), -r2, r1, zq(HEAD_PAD - MLA_QK)], axis=-1).reshape(MLA_Q_LORA,
                                                                                           MLA_HEADS * HEAD_PAD)
    w_q = jnp.concatenate([w_qa, w_qb], axis=1).astype(BF16)
    pad_g = jnp.zeros((HEAD_PAD - MLA_QK,), F32)
    w_k = jnp.concatenate([w_ukv[..., :MLA_NOPE], jnp.zeros((MLA_KV_LORA, MLA_HEADS, HEAD_PAD - MLA_NOPE), F32)],
                          axis=-1).reshape(MLA_KV_LORA, MLA_HEADS * HEAD_PAD).T.astype(BF16)
    w_v = jnp.concatenate([w_ukv[..., MLA_NOPE:], jnp.zeros((MLA_KV_LORA, MLA_HEADS, LANES - MLA_V), F32)],
                          axis=-1).reshape(MLA_KV_LORA, MLA_HEADS * LANES).astype(BF16)
    gate_bias = jnp.concatenate([jnp.zeros((_L_IG,), F32), b_igate, b_fgate,
                                 jnp.zeros((LANES - _L_FG - M_HEADS,), F32)])[None]
    w_r = jnp.concatenate([w_erouter.T, w_group.T, jnp.zeros((R_ROWS - N_EXPERTS - N_GROUPS, D_MODEL), F32)],
                          axis=0).astype(BF16)
    b_r = jnp.concatenate([b_erouter, b_group, jnp.zeros((R_ROWS - N_EXPERTS - N_GROUPS,), F32)])
    return {
        "w_in": w_p, "g_attn": g_attn[None], "g_cq": g_cq[None], "w_q": w_q, "g_ckv": g_ckv[None],
        "g_q": jnp.concatenate([g_q, pad_g])[None], "g_k": jnp.broadcast_to(jnp.concatenate([g_k, pad_g])[:, None], (HEAD_PAD, LANES)),
        "gate_bias": gate_bias, "w_k": w_k, "w_v": w_v, "g_mh": g_mh,
        "w_out": w_out.astype(BF16), "g_ffn": g_ffn[None], "w_r": w_r,
        "b_r": jnp.broadcast_to(b_r[:, None], (R_ROWS, LANES)),
        "w1": w1, "w3": w3, "w2": w2,
        "score_bound": (MLA_QK * MLA_SCALE * LOG2E * 1.01 * jnp.max(jnp.abs(g_q)) * jnp.max(jnp.abs(g_k))).reshape(1),
    }


def _rope_table(pos):
    half = MLA_ROPE // 2
    inv = ROPE_BASE ** (-np.arange(half, dtype=np.float64) / half)
    ang = np.asarray(pos, np.float64)[:, None] * inv[None, :]
    cos = np.cos(ang)
    sin = np.sin(ang)
    c2 = np.concatenate([cos, cos], axis=1)
    s2 = np.concatenate([sin, sin], axis=1)
    return jnp.asarray(np.concatenate([c2, s2, c2, s2], axis=1), F32)


def _pick(n, pref):
    return pref if n % pref == 0 else n


def _slots_kernel(ri_ref, ps_ref, d_ref):
    tm = ri_ref.shape[1]
    xi = lax.broadcasted_iota(I32, (N_EXPERTS, tm), 0)
    ps = ps_ref[:, 0:1]
    rows = []
    for k in range(2):
        start = jnp.sum(jnp.where(xi == ri_ref[k:k + 1, :], ps, 0), axis=0, keepdims=True)
        rows.append(start + ri_ref[2 + k:3 + k, :])
    d_ref[...] = jnp.concatenate(rows + [jnp.zeros((6, tm), I32)], axis=0)


def _slots(ri, pstart_b, tm):
    t = ri.shape[1]
    return pl.pallas_call(
        _slots_kernel,
        grid=(t // tm,),
        in_specs=[pl.BlockSpec((8, tm), lambda i: (0, i)), pl.BlockSpec((N_EXPERTS, LANES), lambda i: (0, 0))],
        out_specs=pl.BlockSpec((8, tm), lambda i: (0, i)),
        out_shape=jax.ShapeDtypeStruct((8, t), I32),
        compiler_params=_cparams(("arbitrary",)),
        name="slots",
    )(ri, pstart_b)


def _moe_layer(x2d, att, hm, pw, *, tm_route, bm, tm_disp, tm_comb):
    t = x2d.shape[0]
    x1, hp, ri, rg, cnt = _route(x2d, att, hm, pw, tm_route)
    counts = cnt[:, 0]
    padded = (counts + bm - 1) // bm * bm
    pend = jnp.cumsum(padded)
    pstart = pend - padded
    dest = _slots(ri, jnp.broadcast_to(pstart[:, None], (N_EXPERTS, LANES)), _pick(t, 2048))
    n_slots = (2 * t // bm + N_EXPERTS) * bm
    nblk = n_slots // bm
    n_used = (pend[-1] // bm).astype(I32)
    experts = jnp.arange(N_EXPERTS, dtype=I32)
    blk_first = jnp.arange(nblk, dtype=I32) * bm
    blk_e = jnp.sum((pend[None, :] <= blk_first[:, None]).astype(I32), axis=1)
    blk_e = jnp.minimum(blk_e, jnp.max(jnp.where(counts > 0, experts, 0)))
    zero_blocks = jnp.concatenate([jnp.where(counts > 0, pend // bm - 1, -1),
                                   jnp.where(n_used + experts < nblk, n_used + experts, -1)]).astype(I32)
    xs = _dispatch(zero_blocks, dest, hp, n_slots, tm_disp, bm)
    ys = _experts(blk_e, n_used[None], xs, pw, bm)
    return _combine(dest, x1, rg, ys, tm_comb)


def _state_pack(c, n):
    ct = jnp.swapaxes(c, -1, -2)
    s = jnp.concatenate([ct, jnp.broadcast_to(n[..., None], ct.shape[:-1] + (S_W - M_DV,))], axis=-1)
    return s.reshape((s.shape[0],) + S_SHAPE)


def _state_unpack(s, m):
    s = s.reshape(s.shape[0], M_HEADS, M_DK, S_W)
    return jnp.swapaxes(s[..., :M_DV], -1, -2), s[..., M_DV], m[:, :M_HEADS, 0]


def kernel(x_prompt, x_sample, cache_ckv, cache_krope, state_mlstm_c, state_mlstm_n, state_mlstm_m,
           meta_tokens, g_attn, w_in, g_cq, w_uq, g_ckv, w_ukv, g_q, g_k, b_igate, b_fgate, g_mh,
           w_out, g_ffn, w_group, b_group, w_erouter, b_erouter, w1, w3, w2):
    bp, seq = x_prompt.shape[:2]
    bs, dec = x_sample.shape[:2]
    past = cache_ckv.shape[2]
    layer = 0
    pw = _prep_weights(g_attn[layer], w_in[layer], g_cq[layer], w_uq[layer], g_ckv[layer], w_ukv[layer],
                       g_q[layer], g_k[layer], b_igate[layer], b_fgate[layer], g_mh[layer], w_out[layer],
                       g_ffn[layer], w_group[layer], b_group[layer], w_erouter[layer], b_erouter[layer],
                       w1[layer], w3[layer], w2[layer])

    xm = jnp.concatenate([meta_tokens, jnp.zeros((META_PAD - N_META, D_MODEL), F32)], axis=0)
    tab_m = _rope_table(np.arange(META_PAD) - N_META)
    _, ckv_m, misc_m, mq_m, mk_m, mv_m, og_m = _project(xm, tab_m, pw, META_PAD)
    zero_s = jnp.zeros((1,) + S_SHAPE, F32)
    zero_m = jnp.zeros((1, 8, LANES), F32)
    _, s_meta, m_meta = _mlstm(mq_m, mk_m, mv_m, og_m, misc_m, zero_s, zero_m, pw["g_mh"],
                               nb=1, l=META_PAD, n_valid=N_META)

    tp = bp * seq
    xp2 = x_prompt.reshape(tp, D_MODEL)
    tm_p = _pick(seq, 1024)
    q_p, ckv_p, misc_p, mq_p, mk_p, mv_p, og_p = _project(xp2, _rope_table(np.arange(seq)), pw, tm_p)
    tq = _pick(seq, 1024)
    att_p = _attention(q_p, ckv_p, misc_p, ckv_m[None], misc_m[None], pw, nb=bp, lq=seq, lf=seq, tq=tq, tk=256,
                       causal=True, n_tail=N_META)
    hm_p, s_p, m_p = _mlstm(mq_p, mk_p, mv_p, og_p, misc_p, s_meta, m_meta, pw["g_mh"],
                            nb=bp, l=seq)
    y_p = _moe_layer(xp2, att_p, hm_p, pw, tm_route=tm_p, bm=1024, tm_disp=_pick(seq, 512),
                     tm_comb=_pick(seq, 512))

    ts = bs * dec
    xs2 = x_sample.reshape(ts, D_MODEL)
    q_s, ckv_s, misc_s, mq_s, mk_s, mv_s, og_s = _project(xs2, _rope_table(past + np.arange(dec)), pw, dec)
    kr_cache = jnp.concatenate([cache_krope[layer], jnp.zeros((bs, past, LANES - MLA_ROPE), F32)], axis=-1)
    n_tail = dec + N_META
    assert n_tail <= META_PAD

    def tail_rows(own, meta):
        w = own.shape[-1]
        return jnp.concatenate([own.reshape(bs, dec, w), jnp.broadcast_to(meta[None, :N_META], (bs, N_META, w)),
                                jnp.zeros((bs, META_PAD - n_tail, w), F32)], axis=1)

    att_s = _attention(q_s, cache_ckv[layer].reshape(bs * past, MLA_KV_LORA), kr_cache.reshape(bs * past, LANES),
                       tail_rows(ckv_s, ckv_m), tail_rows(misc_s, misc_m), pw, nb=bs, lq=dec, lf=past, tq=dec,
                       tk=_pick(past, 256), causal=False, n_tail=n_tail)
    s0 = _state_pack(state_mlstm_c[layer], state_mlstm_n[layer])
    m0 = jnp.concatenate([jnp.broadcast_to(state_mlstm_m[layer][:, :, None], (bs, M_HEADS, LANES)),
                          jnp.zeros((bs, 8 - M_HEADS, LANES), F32)], axis=1)
    hm_s, s_s, m_s = _mlstm(mq_s, mk_s, mv_s, og_s, misc_s, s0, m0, pw["g_mh"], nb=bs, l=dec)
    y_s = _moe_layer(xs2, att_s, hm_s, pw, tm_route=_pick(ts, 512), bm=128, tm_disp=_pick(ts, 512),
                     tm_comb=_pick(ts, 512))

    m_ckv = ckv_m[:N_META]
    m_kr = misc_m[:N_META, :MLA_ROPE]
    new_ckv_p = jnp.concatenate([jnp.broadcast_to(m_ckv[None], (bp, N_META, MLA_KV_LORA)),
                                 ckv_p.reshape(bp, seq, MLA_KV_LORA)], axis=1)[None]
    new_kr_p = jnp.concatenate([jnp.broadcast_to(m_kr[None], (bp, N_META, MLA_ROPE)),
                                misc_p[:, :MLA_ROPE].reshape(bp, seq, MLA_ROPE)], axis=1)[None]
    c_p, n_p, mm_p = _state_unpack(s_p, m_p)
    c_s, n_s, mm_s = _state_unpack(s_s, m_s)
    return (y_p.reshape(bp, seq, D_MODEL), y_s.reshape(bs, dec, D_MODEL),
            new_ckv_p, new_kr_p, c_p[None], n_p[None], mm_p[None],
            ckv_s.reshape(bs, dec, MLA_KV_LORA)[None], misc_s[:, :MLA_ROPE].reshape(bs, dec, MLA_ROPE)[None],
            c_s[None], n_s[None], mm_s[None])
```

```python
import functools

import numpy as np
import jax
import jax.numpy as jnp
from jax import lax
from jax.experimental import pallas as pl
from jax.experimental.pallas import tpu as pltpu

F32 = jnp.float32
BF16 = jnp.bfloat16
I32 = jnp.int32
U32 = jnp.uint32

D_MODEL = 1024
CHUNK = 64
N_META = 16
MLA_HEADS = 8
MLA_V = 64
MLA_NOPE = 64
MLA_ROPE = 32
MLA_QK = MLA_NOPE + MLA_ROPE
MLA_Q_LORA = 256
MLA_KV_LORA = 128
MLA_SCALE = MLA_QK ** -0.5
ROPE_BASE = 10000.0
M_HEADS = 4
M_DV = 128
M_DK = 64
N_GROUPS = 4
E_PER_GROUP = 8
N_EXPERTS = 32
EXPERT_FF = 512
EPS = 1e-6

LANES = 128
HEAD_PAD = 128
PROJ_PAD = 2048
META_PAD = 128
NEG_BIG = -1e30
VMEM_LIMIT = 56 * 1024 * 1024

_O_CQ, _O_CKV, _O_MQ, _O_MK, _O_MV, _O_MO, _O_MISC = 0, 256, 384, 640, 896, 1408, 1920
_L_IG, _L_FG = 64, 68


def _cparams(sem):
    return pltpu.CompilerParams(dimension_semantics=sem, vmem_limit_bytes=VMEM_LIMIT)


def _dot(a, b):
    return jnp.dot(a, b, preferred_element_type=F32)


def _dot_nt(a, b):
    return lax.dot_general(a, b, (((1,), (1,)), ((), ())), preferred_element_type=F32)


def _split3(x):
    x1 = x.astype(BF16)
    r1 = x - x1.astype(F32)
    x2 = r1.astype(BF16)
    x3 = (r1 - x2.astype(F32)).astype(BF16)
    return x1, x2, x3


def _proj_kernel(x_ref, tab_ref, w_ref, g_ref, gcq_ref, wq_ref, gckv_ref, gq_ref, bias_ref,
                 q_ref, ckv_ref, misc_ref, mq_ref, mk_ref, mv_ref, og_ref):
    x = x_ref[...]
    xn = x * lax.rsqrt(jnp.mean(x * x, axis=-1, keepdims=True) + EPS) * g_ref[...]
    xb = xn.astype(BF16)
    tab = tab_ref[...]
    lane = lax.broadcasted_iota(I32, tab.shape, 1)

    def proj(off, width):
        return _dot(xb, w_ref[:, off:off + width])

    cq = proj(_O_CQ, MLA_Q_LORA)
    cqn = cq * lax.rsqrt(jnp.mean(cq * cq, axis=-1, keepdims=True) + EPS) * gcq_ref[...]
    cqb = cqn.astype(BF16)
    qw = MLA_HEADS * HEAD_PAD
    gq = gq_ref[...]
    tab_a = jnp.where(lane < MLA_NOPE, 1.0, jnp.where(lane < MLA_QK, tab, 0.0))
    tab_b = jnp.where((lane >= MLA_NOPE) & (lane < MLA_QK), pltpu.roll(tab, LANES - MLA_ROPE, 1), 0.0)
    slab = 2 * HEAD_PAD
    pair_ones = (lax.broadcasted_iota(I32, (slab, slab), 0) // HEAD_PAD
                 == lax.broadcasted_iota(I32, (slab, slab), 1) // HEAD_PAD).astype(BF16)
    for p in range(MLA_HEADS // 2):
        za = _dot(cqb, wq_ref[:, p * slab:(p + 1) * slab])
        zb = _dot(cqb, wq_ref[:, qw + p * slab:qw + (p + 1) * slab])
        ms = _dot((za * za).astype(BF16), pair_ones) * (1.0 / MLA_QK)
        for e in range(2):
            cols = slice(e * HEAD_PAD, (e + 1) * HEAD_PAD)
            qh = za[:, cols] * tab_a + zb[:, cols] * tab_b
            q_ref[2 * p + e] = (qh * lax.rsqrt(ms[:, cols] + EPS) * gq).astype(BF16)

    ckv = proj(_O_CKV, MLA_KV_LORA)
    ckv_ref[...] = ckv * lax.rsqrt(jnp.mean(ckv * ckv, axis=-1, keepdims=True) + EPS) * gckv_ref[...]

    zm = proj(_O_MISC, LANES)
    y = zm * tab
    rot = y + pltpu.roll(y, LANES - MLA_ROPE, 1)
    gate = zm + bias_ref[...]
    logf = jnp.minimum(gate, 0.0) - jnp.log1p(jnp.exp(-jnp.abs(gate)))
    misc = jnp.where(lane < MLA_ROPE, rot,
                     jnp.where((lane >= _L_IG) & (lane < _L_FG), gate,
                               jnp.where((lane >= _L_FG) & (lane < _L_FG + M_HEADS), logf, 0.0)))
    misc_ref[...] = misc

    mq_ref[...] = proj(_O_MQ, M_HEADS * M_DK).astype(BF16)
    mk_ref[...] = (proj(_O_MK, M_HEADS * M_DK) * (M_DK ** -0.5)).astype(BF16)
    mv_ref[...] = proj(_O_MV, M_HEADS * M_DV).astype(BF16)
    og_ref[...] = jax.nn.sigmoid(proj(_O_MO, M_HEADS * M_DV)).astype(BF16)


def _project(x2d, tab, pw, tm):
    t = x2d.shape[0]
    nt = t // tm
    ntab = tab.shape[0] // tm
    row = lambda i: (i, 0)
    full = lambda i: (0, 0)
    return pl.pallas_call(
        _proj_kernel,
        grid=(nt,),
        in_specs=[
            pl.BlockSpec((tm, D_MODEL), row),
            pl.BlockSpec((tm, LANES), lambda i: (i % ntab, 0)),
            pl.BlockSpec((D_MODEL, PROJ_PAD), full),
            pl.BlockSpec((1, D_MODEL), full),
            pl.BlockSpec((1, MLA_Q_LORA), full),
            pl.BlockSpec((MLA_Q_LORA, 2 * MLA_HEADS * HEAD_PAD), full),
            pl.BlockSpec((1, MLA_KV_LORA), full),
            pl.BlockSpec((1, HEAD_PAD), full),
            pl.BlockSpec((1, LANES), full),
        ],
        out_specs=[
            pl.BlockSpec((MLA_HEADS, tm, HEAD_PAD), lambda i: (0, i, 0)),
            pl.BlockSpec((tm, MLA_KV_LORA), row),
            pl.BlockSpec((tm, LANES), row),
            pl.BlockSpec((tm, M_HEADS * M_DK), row),
            pl.BlockSpec((tm, M_HEADS * M_DK), row),
            pl.BlockSpec((tm, M_HEADS * M_DV), row),
            pl.BlockSpec((tm, M_HEADS * M_DV), row),
        ],
        out_shape=[
            jax.ShapeDtypeStruct((MLA_HEADS, t, HEAD_PAD), BF16),
            jax.ShapeDtypeStruct((t, MLA_KV_LORA), F32),
            jax.ShapeDtypeStruct((t, LANES), F32),
            jax.ShapeDtypeStruct((t, M_HEADS * M_DK), BF16),
            jax.ShapeDtypeStruct((t, M_HEADS * M_DK), BF16),
            jax.ShapeDtypeStruct((t, M_HEADS * M_DV), BF16),
            jax.ShapeDtypeStruct((t, M_HEADS * M_DV), BF16),
        ],
        compiler_params=_cparams(("arbitrary",)),
        name="projection",
    )(x2d, tab, pw["w_in"], pw["g_attn"], pw["g_cq"], pw["w_q"], pw["g_ckv"], pw["g_q"], pw["gate_bias"])


SAFE_BOUND = 40.0
LOG2E = 1.4426950408889634


def _attn_kernel(sb_ref, q_ref, ckv_ref, misc_ref, ckvt_ref, misct_ref, wk_ref, wv_ref, gk_ref, o_ref,
                 kt_scr, v_scr, acc_scr, p_scr, *, lf, tq, tk, bt, causal, n_tail):
    i = pl.program_id(1)
    gk = gk_ref[...]

    def build(ckv_rows, misc_rows, dst):
        n = ckv_rows.shape[0]
        cb = ckv_rows.astype(BF16)
        kt_all = _dot_nt(wk_ref[...], cb)
        sel = (lax.broadcasted_iota(I32, (HEAD_PAD, LANES), 0)
               == lax.broadcasted_iota(I32, (HEAD_PAD, LANES), 1) + MLA_NOPE).astype(BF16)
        sel = jnp.where(lax.broadcasted_iota(I32, (HEAD_PAD, LANES), 1) < MLA_ROPE, sel, jnp.zeros_like(sel))
        m1, m2, m3 = _split3(misc_rows)
        kr_t = _dot_nt(sel, m1) + _dot_nt(sel, m2) + _dot_nt(sel, m3)
        v_all = _dot(cb, wv_ref[...])
        onecol = (lax.broadcasted_iota(I32, (n, LANES), 1) == MLA_V).astype(F32)
        gkc = jnp.tile(gk, (1, n // LANES))
        for h in range(MLA_HEADS):
            kk = kt_all[h * HEAD_PAD:(h + 1) * HEAD_PAD, :] + kr_t
            ms = jnp.sum(kk * kk, axis=0, keepdims=True) * (1.0 / MLA_QK)
            kt_scr[h, :, pl.ds(dst, n)] = (kk * lax.rsqrt(ms + EPS) * gkc).astype(BF16)
            v_scr[h, pl.ds(dst, n), :] = (v_all[:, h * LANES:(h + 1) * LANES] + onecol).astype(BF16)

    @pl.when(i == 0)
    def _():
        def body(r, carry):
            r0 = pl.multiple_of(r * bt, bt)
            build(ckv_ref[pl.ds(r0, bt), :], misc_ref[pl.ds(r0, bt), :], r0)
            return carry

        lax.fori_loop(0, lf // bt, body, 0)
        build(ckvt_ref[0], misct_ref[0], lf)

    ndiag = tq // tk if causal else 0
    nfull = i * ndiag if causal else lf // tk
    tail_mask = lax.broadcasted_iota(I32, (tq, META_PAD), 1) < n_tail

    def kt_tile(h, j):
        return kt_scr[h, :, pl.ds(pl.multiple_of(j * tk, tk), tk)]

    def v_tile(h, j):
        return v_scr[h, pl.ds(pl.multiple_of(j * tk, tk), tk), :]

    def chunk_ids(lo):
        rows = lax.broadcasted_iota(I32, (tq - lo, tk), 0) // CHUNK
        cols = lax.broadcasted_iota(I32, (tq - lo, tk), 1) // CHUNK
        return rows, cols

    def finish(h):
        acc = acc_scr[h]
        o_ref[:, h * MLA_V:(h + 1) * MLA_V] = (acc[:, :MLA_V] / acc[:, MLA_V:MLA_V + 1]).astype(BF16)

    safe = sb_ref[0] <= SAFE_BOUND * LOG2E

    @pl.when(safe)
    def _():
        bound = sb_ref[0]

        def scores(h, lo, kt):
            return _dot(q_ref[h, lo:, :], kt) * (MLA_SCALE * LOG2E) - bound

        def probs(s, mask):
            if mask is not None:
                s = jnp.where(mask, s, -jnp.inf)
            return jnp.exp2(s).astype(BF16)

        def pipe_step(j, lo, lo_prev, mask):
            for h in range(MLA_HEADS):
                acc_scr[h, lo_prev:, :] += _dot(p_scr[(j - 1) % 2, h, lo_prev:, :], v_tile(h, j - 1))
            for h in range(MLA_HEADS):
                p_scr[j % 2, h, lo:, :] = probs(scores(h, lo, kt_tile(h, j)), mask)

        for h in range(MLA_HEADS):
            p_tail = probs(scores(h, 0, kt_scr[h, :, lf:lf + META_PAD]), tail_mask)
            acc_scr[h] = _dot(p_tail, v_scr[h, lf:lf + META_PAD, :])
        rows0, cols0 = chunk_ids(0)
        if causal:
            first_mask = cols0 <= rows0 + jnp.where(nfull > 0, tk, 0)
        else:
            first_mask = None
        for h in range(MLA_HEADS):
            p_scr[0, h] = probs(scores(h, 0, kt_tile(h, 0)), first_mask)

        def body(j, carry):
            pipe_step(j, 0, 0, None)
            return carry

        lax.fori_loop(1, nfull, body, 0)
        if causal:
            @pl.when(nfull >= 1)
            def _():
                pipe_step(nfull, 0, 0, cols0 <= rows0)
            for d in range(1, ndiag):
                rows_d, cols_d = chunk_ids(d * tk)
                pipe_step(nfull + d, d * tk, (d - 1) * tk, cols_d <= rows_d)
            last, lo_last = nfull + ndiag - 1, (ndiag - 1) * tk
        else:
            last, lo_last = nfull - 1, 0
        for h in range(MLA_HEADS):
            acc_scr[h, lo_last:, :] += _dot(p_scr[last % 2, h, lo_last:, :], v_tile(h, last))
            finish(h)

    @pl.when(jnp.logical_not(safe))
    def _():
        rows0, cols0 = chunk_ids(0)
        for h in range(MLA_HEADS):
            qh = q_ref[h]

            def step(carry, kt, vrows, mask):
                m, acc = carry
                s = _dot(qh, kt) * MLA_SCALE
                if mask is not None:
                    s = jnp.where(mask, s, -jnp.inf)
                m_new = jnp.maximum(m, jnp.max(s, axis=-1, keepdims=True))
                acc = jnp.exp(m - m_new) * acc + _dot(jnp.exp(s - m_new).astype(BF16), vrows)
                return m_new, acc

            def body(j, c):
                mask = (cols0 + j * (tk // CHUNK) <= rows0 + i * (tq // CHUNK)) if causal else None
                return step(c, kt_tile(h, j), v_tile(h, j), mask)

            carry = (jnp.full((tq, 1), -jnp.inf, F32), jnp.zeros((tq, LANES), F32))
            carry = step(carry, kt_scr[h, :, lf:lf + META_PAD], v_scr[h, lf:lf + META_PAD, :], tail_mask)
            carry = lax.fori_loop(0, nfull + ndiag, body, carry)
            acc_scr[h] = carry[1]
            finish(h)


def _attention(q, ckv_f, misc_f, ckv_t, misc_t, pw, *, nb, lq, lf, tq, tk, causal, n_tail):
    nq = lq // tq
    bt = min(512, lf)
    assert lf % bt == 0 and lf % tk == 0 and lq % tq == 0 and tk % CHUNK == 0 and (not causal or tq % tk == 0)
    kern = functools.partial(_attn_kernel, lf=lf, tq=tq, tk=tk, bt=bt, causal=causal, n_tail=n_tail)
    full2 = lambda b, i, sb: (0, 0)
    full3 = lambda b, i, sb: (0, 0, 0)
    tail = full3 if ckv_t.shape[0] == 1 else (lambda b, i, sb: (b, 0, 0))
    lk = lf + META_PAD
    return pl.pallas_call(
        kern,
        grid_spec=pltpu.PrefetchScalarGridSpec(
            num_scalar_prefetch=1,
            grid=(nb, nq),
            in_specs=[
                pl.BlockSpec((MLA_HEADS, tq, HEAD_PAD), lambda b, i, sb: (0, b * nq + i, 0)),
                pl.BlockSpec((lf, MLA_KV_LORA), lambda b, i, sb: (b, 0)),
                pl.BlockSpec((lf, LANES), lambda b, i, sb: (b, 0)),
                pl.BlockSpec((1, META_PAD, MLA_KV_LORA), tail),
                pl.BlockSpec((1, META_PAD, LANES), tail),
                pl.BlockSpec((MLA_HEADS * HEAD_PAD, MLA_KV_LORA), full2),
                pl.BlockSpec((MLA_KV_LORA, MLA_HEADS * LANES), full2),
                pl.BlockSpec((HEAD_PAD, LANES), full2),
            ],
            out_specs=pl.BlockSpec((tq, MLA_HEADS * MLA_V), lambda b, i, sb: (b * nq + i, 0)),
            scratch_shapes=[
                pltpu.VMEM((MLA_HEADS, HEAD_PAD, lk), BF16),
                pltpu.VMEM((MLA_HEADS, lk, LANES), BF16),
                pltpu.VMEM((MLA_HEADS, tq, LANES), F32),
                pltpu.VMEM((2, MLA_HEADS, tq, tk), BF16),
            ],
        ),
        out_shape=jax.ShapeDtypeStruct((nb * lq, MLA_HEADS * MLA_V), BF16),
        compiler_params=_cparams(("arbitrary", "arbitrary")),
        name="attention",
    )(pw["score_bound"], q, ckv_f, misc_f, ckv_t, misc_t, pw["w_k"], pw["w_v"], pw["g_k"])


S_W = 2 * M_DV
S_SHAPE = (M_HEADS // 2, 2 * M_DK, S_W)


MLSTM_CHUNK = LANES


def _mlstm_kernel(q_ref, k_ref, v_ref, og_ref, misc_ref, s0_ref, m0_ref, gmh_ref,
                  h_ref, sout_ref, mout_ref, s_scr, m_scr, *, n_valid):
    c = pl.program_id(1)
    nc = pl.num_programs(1)
    ns = q_ref.shape[1]
    lb = q_ref.shape[2]
    lc = MLSTM_CHUNK

    @pl.when(c == 0)
    def _():
        for sl in range(ns):
            s_scr[sl] = s0_ref[0, min(sl, s0_ref.shape[1] - 1)]
            m_scr[sl] = m0_ref[0, min(sl, m0_ref.shape[1] - 1)]

    def rows(x):
        if lb == lc:
            return x
        return jnp.concatenate([x, jnp.zeros((lc - lb,) + x.shape[1:], x.dtype)], axis=0)

    rr = lax.broadcasted_iota(I32, (lc, lc), 0)
    cc = lax.broadcasted_iota(I32, (lc, lc), 1)
    causal = cc <= rr
    tri = causal.astype(BF16)
    trit = (rr <= cc).astype(BF16)
    ones_blk = jnp.ones((lc, M_DV), BF16)
    low_half = lax.broadcasted_iota(I32, (lc, LANES), 1) < M_DK
    limit = lb if n_valid is None else n_valid - c * lb

    def gates(sl):
        g = rows(misc_ref[0, sl])
        row = lax.broadcasted_iota(I32, g.shape, 0)
        lane = lax.broadcasted_iota(I32, g.shape, 1)
        g = jnp.where(row < limit, g, jnp.where((lane >= _L_IG) & (lane < _L_FG), NEG_BIG, 0.0))
        gt8 = g.T[_L_IG:_L_IG + 2 * M_HEADS, :]
        g1, g2, g3 = _split3(g)
        b_cols = _dot(tri, g1) + _dot(tri, g2) + _dot(tri, g3)
        t1, t2, t3 = _split3(gt8)
        b_rows = (_dot(t1, trit) + _dot(t2, trit) + _dot(t3, trit))[M_HEADS:2 * M_HEADS, :]
        a = gt8[0:M_HEADS, :] - b_rows
        m_prev = m_scr[sl, 0:M_HEADS, :]
        u_b = [-jnp.maximum(m_prev[h:h + 1, :], jnp.broadcast_to(
            jnp.max(jnp.where(causal, a[h:h + 1, :], -jnp.inf), axis=-1, keepdims=True), (lc, LANES)))
            for h in range(M_HEADS)]
        negm_b = [u_b[h] - jnp.broadcast_to(b_cols[:, _L_FG + h:_L_FG + h + 1], (lc, LANES)) for h in range(M_HEADS)]
        u_last = jnp.concatenate([u_b[h][lc - 1:lc, 0:1] for h in range(M_HEADS)], axis=0)
        return dict(a=a, m_prev=m_prev, u_b=u_b, negm_b=negm_b,
                    m_new=jnp.broadcast_to(b_rows[:, lc - 1:lc] - u_last, (M_HEADS, LANES)),
                    w_state=jnp.exp(a + u_last), decay=jnp.exp(m_prev[:, 0:1] + u_last))

    gt_ = [gates(sl) for sl in range(ns)]
    q = [rows(q_ref[0, sl]) for sl in range(ns)]
    k = [rows(k_ref[0, sl]) for sl in range(ns)]
    v = [rows(v_ref[0, sl]) for sl in range(ns)]
    kt = [x.astype(F32).T for x in k]

    units = [(sl, h) for sl in range(ns) for h in range(M_HEADS)]
    mine = {u: (jnp.logical_not(low_half) if u[1] % 2 else low_half) for u in units}
    pair = lambda x, u: x[u[0]][:, (u[1] // 2) * LANES:(u[1] // 2 + 1) * LANES]
    kh = {u: jnp.where(mine[u], pair(k, u), jnp.zeros_like(pair(k, u))) for u in units}
    qh = {u: jnp.where(mine[u], pair(q, u), jnp.zeros_like(pair(q, u))) for u in units}
    vext = {u: jnp.concatenate([v[u[0]][:, u[1] * M_DV:(u[1] + 1) * M_DV], ones_blk], axis=1) for u in units}
    s_old = {(sl, j): s_scr[sl, j] for sl in range(ns) for j in range(M_HEADS // 2)}
    s_bf = {key: val.astype(BF16) for key, val in s_old.items()}

    qk = {u: _dot_nt(pair(q, u), kh[u]) for u in units}
    qs = {u: _dot(qh[u], s_bf[(u[0], u[1] // 2)]) for u in units}
    ktw = {u: (kt[u[0]][u[1] * M_DK:(u[1] + 1) * M_DK, :] * gt_[u[0]]["w_state"][u[1]:u[1] + 1, :]).astype(BF16)
           for u in units}
    upd = {u: _dot(ktw[u], vext[u]) for u in units}
    w = {u: (jnp.where(causal, jnp.exp(gt_[u[0]]["a"][u[1]:u[1] + 1, :] + gt_[u[0]]["u_b"][u[1]]), 0.0)
             * qk[u]).astype(BF16) for u in units}
    inter = {u: jnp.exp(gt_[u[0]]["m_prev"][u[1]:u[1] + 1, :] + gt_[u[0]]["u_b"][u[1]]) for u in units}
    r = {u: _dot(w[u], vext[u]) + jnp.tile(inter[u], (1, 2)) * qs[u] for u in units}
    num = {u: r[u][:, :M_DV] for u in units}
    sq = {u: num[u] * num[u] for u in units}
    sq_hi = {u: sq[u].astype(BF16) for u in units}
    sq_lo = {u: (sq[u] - sq_hi[u].astype(F32)).astype(BF16) for u in units}
    msn = {u: (_dot(sq_hi[u], ones_blk) + _dot(sq_lo[u], ones_blk)) * (1.0 / M_DV) for u in units}
    for u in units:
        sl, h = u
        rden = 1.0 / jnp.maximum(jnp.abs(r[u][:, M_DV:]), jnp.exp(gt_[sl]["negm_b"][h]))
        scale = rden * lax.rsqrt(rden * rden * msn[u] + EPS)
        hn = num[u] * scale * gmh_ref[h:h + 1, :]
        out = (hn[:lb] * og_ref[0, sl, :, h * M_DV:(h + 1) * M_DV].astype(F32)).astype(BF16)
        h_ref[0, sl, :, h * M_DV:(h + 1) * M_DV] = out
    for sl in range(ns):
        for j in range(M_HEADS // 2):
            new = [gt_[sl]["decay"][h:h + 1, :] * s_old[(sl, j)][(h % 2) * M_DK:(h % 2 + 1) * M_DK, :] + upd[(sl, h)]
                   for h in (2 * j, 2 * j + 1)]
            s_scr[sl, j] = jnp.concatenate(new, axis=0)
        m_scr[sl, 0:M_HEADS, :] = gt_[sl]["m_new"]

    @pl.when(c == nc - 1)
    def _():
        sout_ref[0] = s_scr[...]
        mout_ref[0] = m_scr[...]


def _mlstm(mq, mk, mv, og, misc, s0, m0, gmh, *, nb, l, n_valid=None):
    lc = min(l, MLSTM_CHUNK)
    assert l % lc == 0 and lc % 16 == 0
    nc = l // lc
    ns = 2 if nb % 2 == 0 else 1
    ng = nb // ns
    if s0.shape[0] == 1:
        s0, m0 = s0[:, None], m0[:, None]
        st = lambda g, c: (0, 0, 0, 0, 0)
        mst = lambda g, c: (0, 0, 0, 0)
    else:
        s0, m0 = s0.reshape((ng, ns) + S_SHAPE), m0.reshape(ng, ns, 8, LANES)
        st = lambda g, c: (g, 0, 0, 0, 0)
        mst = lambda g, c: (g, 0, 0, 0)
    blk = lambda g, c: (g, 0, c, 0)
    split = lambda x: x.reshape(ng, ns, l, x.shape[-1])
    kern = functools.partial(_mlstm_kernel, n_valid=n_valid)
    h, s, m = pl.pallas_call(
        kern,
        grid=(ng, nc),
        in_specs=[
            pl.BlockSpec((1, ns, lc, M_HEADS * M_DK), blk),
            pl.BlockSpec((1, ns, lc, M_HEADS * M_DK), blk),
            pl.BlockSpec((1, ns, lc, M_HEADS * M_DV), blk),
            pl.BlockSpec((1, ns, lc, M_HEADS * M_DV), blk),
            pl.BlockSpec((1, ns, lc, LANES), blk),
            pl.BlockSpec((1, s0.shape[1]) + S_SHAPE, st),
            pl.BlockSpec((1, m0.shape[1], 8, LANES), mst),
            pl.BlockSpec((M_HEADS, M_DV), lambda g, c: (0, 0)),
        ],
        out_specs=[
            pl.BlockSpec((1, ns, lc, M_HEADS * M_DV), blk),
            pl.BlockSpec((1, ns) + S_SHAPE, lambda g, c: (g, 0, 0, 0, 0)),
            pl.BlockSpec((1, ns, 8, LANES), lambda g, c: (g, 0, 0, 0)),
        ],
        out_shape=[
            jax.ShapeDtypeStruct((ng, ns, l, M_HEADS * M_DV), BF16),
            jax.ShapeDtypeStruct((ng, ns) + S_SHAPE, F32),
            jax.ShapeDtypeStruct((ng, ns, 8, LANES), F32),
        ],
        scratch_shapes=[
            pltpu.VMEM((ns,) + S_SHAPE, F32),
            pltpu.VMEM((ns, 8, LANES), F32),
        ],
        compiler_params=_cparams(("arbitrary", "arbitrary")),
        name="mlstm",
    )(split(mq), split(mk), split(mv), split(og), split(misc), s0, m0, gmh)
    return h.reshape(nb * l, M_HEADS * M_DV), s.reshape((nb,) + S_SHAPE), m.reshape(nb, 8, LANES)


R_ROWS = 40


def _route_kernel(x_ref, att_ref, hm_ref, wo_ref, g_ref, wr_ref, br_ref,
                  x1_ref, hp_ref, ri_ref, rg_ref, cnt_ref, cnt_scr):
    i = pl.program_id(0)
    tm = x_ref.shape[0]

    @pl.when(i == 0)
    def _():
        cnt_scr[...] = jnp.zeros_like(cnt_scr)

    mix = jnp.concatenate([att_ref[...], hm_ref[...]], axis=1)
    x1 = x_ref[...] + _dot(mix, wo_ref[...])
    x1_ref[...] = x1
    hn = x1 * lax.rsqrt(jnp.mean(x1 * x1, axis=-1, keepdims=True) + EPS) * g_ref[...]
    hb = hn.astype(BF16)
    half = D_MODEL // 2
    hi = lax.bitcast_convert_type(hb[:, :half].astype(F32), U32)
    lo = lax.bitcast_convert_type(hb[:, half:].astype(F32), U32)
    hp_ref[...] = (hi & jnp.uint32(0xFFFF0000)) | (lo >> 16)

    logits = _dot_nt(wr_ref[...], hb) + br_ref[:, 0:1]
    e_log = logits[0:N_EXPERTS, :]
    g_log = logits[N_EXPERTS:N_EXPERTS + N_GROUPS, :]
    gmax = jnp.max(g_log, axis=0, keepdims=True)
    gsum = jnp.sum(jnp.exp(g_log - gmax), axis=0, keepdims=True)
    gi = lax.broadcasted_iota(I32, g_log.shape, 0)
    g_idx = jnp.min(jnp.where(g_log == gmax, gi, N_GROUPS), axis=0, keepdims=True)
    e_sel = jnp.zeros((E_PER_GROUP, tm), F32)
    for gg in range(N_GROUPS):
        e_sel = jnp.where(g_idx == gg, e_log[gg * E_PER_GROUP:(gg + 1) * E_PER_GROUP, :], e_sel)
    ei = lax.broadcasted_iota(I32, e_sel.shape, 0)
    m1 = jnp.max(e_sel, axis=0, keepdims=True)
    i1 = jnp.min(jnp.where(e_sel == m1, ei, E_PER_GROUP), axis=0, keepdims=True)
    e2 = jnp.where(ei == i1, -jnp.inf, e_sel)
    m2 = jnp.max(e2, axis=0, keepdims=True)
    i2 = jnp.min(jnp.where(e2 == m2, ei, E_PER_GROUP), axis=0, keepdims=True)
    ex = jnp.exp(m2 - m1)
    gp = 1.0 / gsum
    p1 = 1.0 / (1.0 + ex)
    gate1 = gp * p1
    gate2 = gp * (ex * p1)
    id1 = g_idx * E_PER_GROUP + i1
    id2 = g_idx * E_PER_GROUP + i2

    xi = lax.broadcasted_iota(I32, (N_EXPERTS, tm), 0)
    oh1 = xi == id1
    oh2 = xi == id2
    e_cnt = (oh1 | oh2).astype(F32)
    rr = lax.broadcasted_iota(I32, (tm, tm), 0)
    cc = lax.broadcasted_iota(I32, (tm, tm), 1)
    upper = (rr < cc).astype(BF16)
    pref = _dot(e_cnt.astype(BF16), upper) + cnt_scr[:, 0:1]
    rank1 = jnp.sum(jnp.where(oh1, pref, 0.0), axis=0, keepdims=True)
    rank2 = jnp.sum(jnp.where(oh2, pref, 0.0), axis=0, keepdims=True)
    cnt_new = cnt_scr[...] + jnp.sum(e_cnt, axis=1, keepdims=True)
    cnt_scr[...] = cnt_new
    cnt_ref[...] = cnt_new.astype(I32)

    zi = jnp.zeros((1, tm), I32)
    ri_ref[...] = jnp.concatenate([id1, id2, rank1.astype(I32), rank2.astype(I32), zi, zi, zi, zi], axis=0)
    zf = jnp.zeros((1, tm), F32)
    rg_ref[...] = jnp.concatenate([gate1, gate2, zf, zf, zf, zf, zf, zf], axis=0)


def _route(x2d, att, hm, pw, tm):
    t = x2d.shape[0]
    row = lambda i: (i, 0)
    col = lambda i: (0, i)
    full = lambda i: (0, 0)
    return pl.pallas_call(
        _route_kernel,
        grid=(t // tm,),
        in_specs=[
            pl.BlockSpec((tm, D_MODEL), row),
            pl.BlockSpec((tm, D_MODEL // 2), row),
            pl.BlockSpec((tm, D_MODEL // 2), row),
            pl.BlockSpec((D_MODEL, D_MODEL), full),
            pl.BlockSpec((1, D_MODEL), full),
            pl.BlockSpec((R_ROWS, D_MODEL), full),
            pl.BlockSpec((R_ROWS, LANES), full),
        ],
        out_specs=[
            pl.BlockSpec((tm, D_MODEL), row),
            pl.BlockSpec((tm, D_MODEL // 2), row),
            pl.BlockSpec((8, tm), col),
            pl.BlockSpec((8, tm), col),
            pl.BlockSpec((N_EXPERTS, LANES), full),
        ],
        out_shape=[
            jax.ShapeDtypeStruct((t, D_MODEL), F32),
            jax.ShapeDtypeStruct((t, D_MODEL // 2), U32),
            jax.ShapeDtypeStruct((8, t), I32),
            jax.ShapeDtypeStruct((8, t), F32),
            jax.ShapeDtypeStruct((N_EXPERTS, LANES), I32),
        ],
        scratch_shapes=[pltpu.VMEM((N_EXPERTS, LANES), F32)],
        compiler_params=_cparams(("arbitrary",)),
        name="route",
    )(x2d, att, hm, pw["w_out"], pw["g_ffn"], pw["w_r"], pw["b_r"])


SUBLANES = 8


N_ZERO_BLOCKS = 2 * N_EXPERTS


def _dispatch_kernel(zb_ref, dest_ref, hp_ref, xs_ref, zero_scr, sem, zsem, *, tm, bm):
    @pl.when(pl.program_id(0) == 0)
    def _():
        zero_scr[...] = jnp.zeros_like(zero_scr)

        def zcopy(k):
            return pltpu.make_async_copy(zero_scr, xs_ref.at[pl.ds(pl.multiple_of(zb_ref[k] * bm, bm), bm)], zsem)

        def zstart(k, carry):
            @pl.when(zb_ref[k] >= 0)
            def _():
                zcopy(k).start()
            return carry

        def zwait(k, carry):
            @pl.when(zb_ref[k] >= 0)
            def _():
                zcopy(k).wait()
            return carry

        lax.fori_loop(0, N_ZERO_BLOCKS, zstart, 0)
        lax.fori_loop(0, N_ZERO_BLOCKS, zwait, 0)

    def body(g, carry):
        for u in range(SUBLANES):
            t = g * SUBLANES + u
            src = hp_ref.at[g, pl.ds(u, 1), :]
            pltpu.make_async_copy(src, xs_ref.at[pl.ds(dest_ref[t], 1)], sem).start(priority=0)
            pltpu.make_async_copy(src, xs_ref.at[pl.ds(dest_ref[tm + t], 1)], sem).start(priority=1)
        return carry

    lax.fori_loop(0, tm // SUBLANES, body, 0)
    pltpu.make_async_copy(xs_ref.at[pl.ds(0, 2 * tm)], xs_ref.at[pl.ds(0, 2 * tm)], sem).wait()


def _tile_slots(dest, tm):
    t = dest.shape[1]
    return dest[0:2].reshape(2, t // tm, tm).transpose(1, 0, 2).reshape(2 * t)


def _dispatch(zero_blocks, dest, hp, n_slots, tm, bm):
    t = hp.shape[0]
    return pl.pallas_call(
        functools.partial(_dispatch_kernel, tm=tm, bm=bm),
        grid_spec=pltpu.PrefetchScalarGridSpec(
            num_scalar_prefetch=1,
            grid=(t // tm,),
            in_specs=[
                pl.BlockSpec((2 * tm,), lambda i, zb: (i,), memory_space=pltpu.SMEM),
                pl.BlockSpec((tm // SUBLANES, SUBLANES, D_MODEL // 2), lambda i, zb: (i, 0, 0)),
            ],
            out_specs=pl.BlockSpec(memory_space=pl.ANY),
            scratch_shapes=[
                pltpu.VMEM((bm, D_MODEL // 2), U32),
                pltpu.SemaphoreType.DMA,
                pltpu.SemaphoreType.DMA,
            ],
        ),
        out_shape=jax.ShapeDtypeStruct((n_slots, D_MODEL // 2), U32),
        compiler_params=_cparams(("arbitrary",)),
        name="dispatch",
    )(zero_blocks, _tile_slots(dest, tm), hp.reshape(t // SUBLANES, SUBLANES, D_MODEL // 2))


def _expert_kernel(be_ref, nu_ref, xs_ref, w1_ref, w3_ref, w2_ref, ys_ref, w1_scr, w3_scr, w2_scr):
    i = pl.program_id(0)
    new_expert = jnp.logical_or(i == 0, be_ref[i] != be_ref[jnp.maximum(i - 1, 0)])

    @pl.when(jnp.logical_and(new_expert, i < nu_ref[0]))
    def _():
        w1_scr[...] = w1_ref[0].astype(BF16)
        w3_scr[...] = w3_ref[0].astype(BF16)
        w2_scr[...] = w2_ref[0].astype(BF16)

    @pl.when(i < nu_ref[0])
    def _():
        xw = xs_ref[...]
        xa = lax.bitcast_convert_type(xw & jnp.uint32(0xFFFF0000), F32).astype(BF16)
        xb = lax.bitcast_convert_type(xw << 16, F32).astype(BF16)
        half = D_MODEL // 2
        h1 = _dot(xa, w1_scr[:half, :]) + _dot(xb, w1_scr[half:, :])
        h3 = _dot(xa, w3_scr[:half, :]) + _dot(xb, w3_scr[half:, :])
        a = (h1 * jax.nn.sigmoid(h1)) * h3
        ys_ref[...] = _dot(a.astype(BF16), w2_scr[...])

    @pl.when(pl.program_id(0) >= nu_ref[0])
    def _():
        ys_ref[...] = jnp.zeros_like(ys_ref)


def _experts(blk_e, n_used, xs, pw, bm):
    n_slots = xs.shape[0]
    nblk = n_slots // bm
    blk = lambda i, be, nu: (jnp.minimum(i, nu[0] - 1), 0)
    oblk = lambda i, be, nu: (i, 0)
    wsel = lambda i, be, nu: (be[i], 0, 0)
    return pl.pallas_call(
        _expert_kernel,
        grid_spec=pltpu.PrefetchScalarGridSpec(
            num_scalar_prefetch=2,
            grid=(nblk,),
            in_specs=[
                pl.BlockSpec((bm, D_MODEL // 2), blk),
                pl.BlockSpec((1, D_MODEL, EXPERT_FF), wsel),
                pl.BlockSpec((1, D_MODEL, EXPERT_FF), wsel),
                pl.BlockSpec((1, EXPERT_FF, D_MODEL), wsel),
            ],
            out_specs=pl.BlockSpec((bm, D_MODEL), oblk),
            scratch_shapes=[
                pltpu.VMEM((D_MODEL, EXPERT_FF), BF16),
                pltpu.VMEM((D_MODEL, EXPERT_FF), BF16),
                pltpu.VMEM((EXPERT_FF, D_MODEL), BF16),
            ],
        ),
        out_shape=jax.ShapeDtypeStruct((n_slots, D_MODEL), F32),
        compiler_params=_cparams(("arbitrary",)),
        name="experts",
    )(blk_e, n_used, xs, pw["w1"], pw["w3"], pw["w2"])


def _combine_kernel(dest_ref, x1_ref, rg_ref, ys_ref, y_ref, r0_scr, r1_scr, sem, *, tm):
    def body(g, carry):
        for u in range(SUBLANES):
            t = g * SUBLANES + u
            pltpu.make_async_copy(ys_ref.at[pl.ds(dest_ref[t], 1)], r0_scr.at[g, pl.ds(u, 1), :], sem).start(priority=0)
            pltpu.make_async_copy(ys_ref.at[pl.ds(dest_ref[tm + t], 1)], r1_scr.at[g, pl.ds(u, 1), :],
                                  sem).start(priority=1)
        return carry

    lax.fori_loop(0, tm // SUBLANES, body, 0)
    gt = jnp.concatenate([rg_ref[...], jnp.zeros((LANES - 8, tm), F32)], axis=0).T
    slab = ys_ref.at[pl.ds(0, tm)]
    pltpu.make_async_copy(slab, slab, sem).wait()
    pltpu.make_async_copy(slab, slab, sem).wait()
    r0 = r0_scr[...].reshape(tm, D_MODEL)
    r1 = r1_scr[...].reshape(tm, D_MODEL)
    y_ref[...] = x1_ref[...] + gt[:, 0:1] * r0 + gt[:, 1:2] * r1


def _combine(dest, x1, rg, ys, tm):
    t = x1.shape[0]
    row = lambda i: (i, 0)
    col = lambda i: (0, i)
    return pl.pallas_call(
        functools.partial(_combine_kernel, tm=tm),
        grid=(t // tm,),
        in_specs=[
            pl.BlockSpec((2 * tm,), lambda i: (i,), memory_space=pltpu.SMEM),
            pl.BlockSpec((tm, D_MODEL), row),
            pl.BlockSpec((8, tm), col),
            pl.BlockSpec(memory_space=pl.ANY),
        ],
        out_specs=pl.BlockSpec((tm, D_MODEL), row),
        out_shape=jax.ShapeDtypeStruct((t, D_MODEL), F32),
        scratch_shapes=[
            pltpu.VMEM((tm // SUBLANES, SUBLANES, D_MODEL), F32),
            pltpu.VMEM((tm // SUBLANES, SUBLANES, D_MODEL), F32),
            pltpu.SemaphoreType.DMA,
        ],
        compiler_params=_cparams(("arbitrary",)),
        name="combine",
    )(_tile_slots(dest, tm), x1, rg, ys)


def _prep_weights(g_attn, w_in, g_cq, w_uq, g_ckv, w_ukv, g_q, g_k, b_igate, b_fgate, g_mh,
                  w_out, g_ffn, w_group, b_group, w_erouter, b_erouter, w1, w3, w2):
    def cols(a, b):
        return w_in[:, a:b]

    o_cq, o_ckv, o_kr = 0, 256, 384
    o_mq, o_mk, o_mv, o_mi, o_mf, o_mo = 416, 672, 928, 1440, 1444, 1448
    hr = MLA_ROPE // 2
    misc = jnp.concatenate([
        cols(o_kr, o_kr + MLA_ROPE), -cols(o_kr + hr, o_kr + MLA_ROPE), cols(o_kr, o_kr + hr),
        cols(o_mi, o_mi + M_HEADS), cols(o_mf, o_mf + M_HEADS),
        jnp.zeros((D_MODEL, LANES - 2 * MLA_ROPE - 2 * M_HEADS), F32)], axis=1)
    w_p = jnp.concatenate([
        cols(o_cq, o_cq + 256), cols(o_ckv, o_ckv + 128), cols(o_mq, o_mq + 256), cols(o_mk, o_mk + 256),
        cols(o_mv, o_mv + 512), cols(o_mo, o_mo + 512), misc], axis=1).astype(BF16)
    nope, r1, r2 = w_uq[..., :MLA_NOPE], w_uq[..., MLA_NOPE:MLA_NOPE + hr], w_uq[..., MLA_NOPE + hr:]
    zq = lambda w: jnp.zeros((MLA_Q_LORA, MLA_HEADS, w), F32)
    w_qa = jnp.concatenate([nope, r1, r2, zq(HEAD_PAD - MLA_QK)], axis=-1).reshape(MLA_Q_LORA, MLA_HEADS * HEAD_PAD)
    w_qb = jnp.concatenate([zq(MLA_NOPE), -r2, r1, zq(HEAD_PAD - MLA_QK)], axis=-1).reshape(MLA_Q_LORA,
                                                                                           MLA_HEADS * HEAD_PAD)
    w_q = jnp.concatenate([w_qa, w_qb], axis=1).astype(BF16)
    pad_g = jnp.zeros((HEAD_PAD - MLA_QK,), F32)
    w_k = jnp.concatenate([w_ukv[..., :MLA_NOPE], jnp.zeros((MLA_KV_LORA, MLA_HEADS, HEAD_PAD - MLA_NOPE), F32)],
                          axis=-1).reshape(MLA_KV_LORA, MLA_HEADS * HEAD_PAD).T.astype(BF16)
    w_v = jnp.concatenate([w_ukv[..., MLA_NOPE:], jnp.zeros((MLA_KV_LORA, MLA_HEADS, LANES - MLA_V), F32)],
                          axis=-1).reshape(MLA_KV_LORA, MLA_HEADS * LANES).astype(BF16)
    gate_bias = jnp.concatenate([jnp.zeros((_L_IG,), F32), b_igate, b_fgate,
                                 jnp.zeros((LANES - _L_FG - M_HEADS,), F32)])[None]
    w_r = jnp.concatenate([w_erouter.T, w_group.T, jnp.zeros((R_ROWS - N_EXPERTS - N_GROUPS, D_MODEL), F32)],
                          axis=0).astype(BF16)
    b_r = jnp.concatenate([b_erouter, b_group, jnp.zeros((R_ROWS - N_EXPERTS - N_GROUPS,), F32)])
    return {
        "w_in": w_p, "g_attn": g_attn[None], "g_cq": g_cq[None], "w_q": w_q, "g_ckv": g_ckv[None],
        "g_q": jnp.concatenate([g_q, pad_g])[None], "g_k": jnp.broadcast_to(jnp.concatenate([g_k, pad_g])[:, None], (HEAD_PAD, LANES)),
        "gate_bias": gate_bias, "w_k": w_k, "w_v": w_v, "g_mh": g_mh,
        "w_out": w_out.astype(BF16), "g_ffn": g_ffn[None], "w_r": w_r,
        "b_r": jnp.broadcast_to(b_r[:, None], (R_ROWS, LANES)),
        "w1": w1, "w3": w3, "w2": w2,
        "score_bound": (MLA_QK * MLA_SCALE * LOG2E * 1.01 * jnp.max(jnp.abs(g_q)) * jnp.max(jnp.abs(g_k))).reshape(1),
    }


def _rope_table(pos):
    half = MLA_ROPE // 2
    inv = ROPE_BASE ** (-np.arange(half, dtype=np.float64) / half)
    ang = np.asarray(pos, np.float64)[:, None] * inv[None, :]
    cos = np.cos(ang)
    sin = np.sin(ang)
    c2 = np.concatenate([cos, cos], axis=1)
    s2 = np.concatenate([sin, sin], axis=1)
    return jnp.asarray(np.concatenate([c2, s2, c2, s2], axis=1), F32)


def _pick(n, pref):
    return pref if n % pref == 0 else n


def _slots_kernel(ri_ref, ps_ref, d_ref):
    tm = ri_ref.shape[1]
    xi = lax.broadcasted_iota(I32, (N_EXPERTS, tm), 0)
    ps = ps_ref[:, 0:1]
    rows = []
    for k in range(2):
        start = jnp.sum(jnp.where(xi == ri_ref[k:k + 1, :], ps, 0), axis=0, keepdims=True)
        rows.append(start + ri_ref[2 + k:3 + k, :])
    d_ref[...] = jnp.concatenate(rows + [jnp.zeros((6, tm), I32)], axis=0)


def _slots(ri, pstart_b, tm):
    t = ri.shape[1]
    return pl.pallas_call(
        _slots_kernel,
        grid=(t // tm,),
        in_specs=[pl.BlockSpec((8, tm), lambda i: (0, i)), pl.BlockSpec((N_EXPERTS, LANES), lambda i: (0, 0))],
        out_specs=pl.BlockSpec((8, tm), lambda i: (0, i)),
        out_shape=jax.ShapeDtypeStruct((8, t), I32),
        compiler_params=_cparams(("arbitrary",)),
        name="slots",
    )(ri, pstart_b)


def _moe_layer(x2d, att, hm, pw, *, tm_route, bm, tm_disp, tm_comb):
    t = x2d.shape[0]
    x1, hp, ri, rg, cnt = _route(x2d, att, hm, pw, tm_route)
    counts = cnt[:, 0]
    padded = (counts + bm - 1) // bm * bm
    pend = jnp.cumsum(padded)
    pstart = pend - padded
    dest = _slots(ri, jnp.broadcast_to(pstart[:, None], (N_EXPERTS, LANES)), _pick(t, 2048))
    n_slots = (2 * t // bm + N_EXPERTS) * bm
    nblk = n_slots // bm
    n_used = (pend[-1] // bm).astype(I32)
    experts = jnp.arange(N_EXPERTS, dtype=I32)
    blk_first = jnp.arange(nblk, dtype=I32) * bm
    blk_e = jnp.sum((pend[None, :] <= blk_first[:, None]).astype(I32), axis=1)
    blk_e = jnp.minimum(blk_e, jnp.max(jnp.where(counts > 0, experts, 0)))
    zero_blocks = jnp.concatenate([jnp.where(counts > 0, pend // bm - 1, -1),
                                   jnp.where(n_used + experts < nblk, n_used + experts, -1)]).astype(I32)
    xs = _dispatch(zero_blocks, dest, hp, n_slots, tm_disp, bm)
    ys = _experts(blk_e, n_used[None], xs, pw, bm)
    return _combine(dest, x1, rg, ys, tm_comb)


def _state_pack(c, n):
    ct = jnp.swapaxes(c, -1, -2)
    s = jnp.concatenate([ct, jnp.broadcast_to(n[..., None], ct.shape[:-1] + (S_W - M_DV,))], axis=-1)
    return s.reshape((s.shape[0],) + S_SHAPE)


def _state_unpack(s, m):
    s = s.reshape(s.shape[0], M_HEADS, M_DK, S_W)
    return jnp.swapaxes(s[..., :M_DV], -1, -2), s[..., M_DV], m[:, :M_HEADS, 0]


def kernel(x_prompt, x_sample, cache_ckv, cache_krope, state_mlstm_c, state_mlstm_n, state_mlstm_m,
           meta_tokens, g_attn, w_in, g_cq, w_uq, g_ckv, w_ukv, g_q, g_k, b_igate, b_fgate, g_mh,
           w_out, g_ffn, w_group, b_group, w_erouter, b_erouter, w1, w3, w2):
    bp, seq = x_prompt.shape[:2]
    bs, dec = x_sample.shape[:2]
    past = cache_ckv.shape[2]
    layer = 0
    pw = _prep_weights(g_attn[layer], w_in[layer], g_cq[layer], w_uq[layer], g_ckv[layer], w_ukv[layer],
                       g_q[layer], g_k[layer], b_igate[layer], b_fgate[layer], g_mh[layer], w_out[layer],
                       g_ffn[layer], w_group[layer], b_group[layer], w_erouter[layer], b_erouter[layer],
                       w1[layer], w3[layer], w2[layer])

    xm = jnp.concatenate([meta_tokens, jnp.zeros((META_PAD - N_META, D_MODEL), F32)], axis=0)
    tab_m = _rope_table(np.arange(META_PAD) - N_META)
    _, ckv_m, misc_m, mq_m, mk_m, mv_m, og_m = _project(xm, tab_m, pw, META_PAD)
    zero_s = jnp.zeros((1,) + S_SHAPE, F32)
    zero_m = jnp.zeros((1, 8, LANES), F32)
    _, s_meta, m_meta = _mlstm(mq_m, mk_m, mv_m, og_m, misc_m, zero_s, zero_m, pw["g_mh"],
                               nb=1, l=META_PAD, n_valid=N_META)

    tp = bp * seq
    xp2 = x_prompt.reshape(tp, D_MODEL)
    tm_p = _pick(seq, 1024)
    q_p, ckv_p, misc_p, mq_p, mk_p, mv_p, og_p = _project(xp2, _rope_table(np.arange(seq)), pw, tm_p)
    tq = _pick(seq, 1024)
    att_p = _attention(q_p, ckv_p, misc_p, ckv_m[None], misc_m[None], pw, nb=bp, lq=seq, lf=seq, tq=tq, tk=256,
                       causal=True, n_tail=N_META)
    hm_p, s_p, m_p = _mlstm(mq_p, mk_p, mv_p, og_p, misc_p, s_meta, m_meta, pw["g_mh"],
                            nb=bp, l=seq)
    y_p = _moe_layer(xp2, att_p, hm_p, pw, tm_route=tm_p, bm=1024, tm_disp=_pick(seq, 512),
                     tm_comb=_pick(seq, 512))

    ts = bs * dec
    xs2 = x_sample.reshape(ts, D_MODEL)
    q_s, ckv_s, misc_s, mq_s, mk_s, mv_s, og_s = _project(xs2, _rope_table(past + np.arange(dec)), pw, dec)
    kr_cache = jnp.concatenate([cache_krope[layer], jnp.zeros((bs, past, LANES - MLA_ROPE), F32)], axis=-1)
    n_tail = dec + N_META
    assert n_tail <= META_PAD

    def tail_rows(own, meta):
        w = own.shape[-1]
        return jnp.concatenate([own.reshape(bs, dec, w), jnp.broadcast_to(meta[None, :N_META], (bs, N_META, w)),
                                jnp.zeros((bs, META_PAD - n_tail, w), F32)], axis=1)

    att_s = _attention(q_s, cache_ckv[layer].reshape(bs * past, MLA_KV_LORA), kr_cache.reshape(bs * past, LANES),
                       tail_rows(ckv_s, ckv_m), tail_rows(misc_s, misc_m), pw, nb=bs, lq=dec, lf=past, tq=dec,
                       tk=_pick(past, 256), causal=False, n_tail=n_tail)
    s0 = _state_pack(state_mlstm_c[layer], state_mlstm_n[layer])
    m0 = jnp.concatenate([jnp.broadcast_to(state_mlstm_m[layer][:, :, None], (bs, M_HEADS, LANES)),
                          jnp.zeros((bs, 8 - M_HEADS, LANES), F32)], axis=1)
    hm_s, s_s, m_s = _mlstm(mq_s, mk_s, mv_s, og_s, misc_s, s0, m0, pw["g_mh"], nb=bs, l=dec)
    y_s = _moe_layer(xs2, att_s, hm_s, pw, tm_route=_pick(ts, 512), bm=128, tm_disp=_pick(ts, 512),
                     tm_comb=_pick(ts, 512))

    m_ckv = ckv_m[:N_META]
    m_kr = misc_m[:N_META, :MLA_ROPE]
    new_ckv_p = jnp.concatenate([jnp.broadcast_to(m_ckv[None], (bp, N_META, MLA_KV_LORA)),
                                 ckv_p.reshape(bp, seq, MLA_KV_LORA)], axis=1)[None]
    new_kr_p = jnp.concatenate([jnp.broadcast_to(m_kr[None], (bp, N_META, MLA_ROPE)),
                                misc_p[:, :MLA_ROPE].reshape(bp, seq, MLA_ROPE)], axis=1)[None]
    c_p, n_p, mm_p = _state_unpack(s_p, m_p)
    c_s, n_s, mm_s = _state_unpack(s_s, m_s)
    return (y_p.reshape(bp, seq, D_MODEL), y_s.reshape(bs, dec, D_MODEL),
            new_ckv_p, new_kr_p, c_p[None], n_p[None], mm_p[None],
            ckv_s.reshape(bs, dec, MLA_KV_LORA)[None], misc_s[:, :MLA_ROPE].reshape(bs, dec, MLA_ROPE)[None],
            c_s[None], n_s[None], mm_s[None])
```

```python
import functools

import numpy as np
import jax
import jax.numpy as jnp
from jax import lax
from jax.experimental import pallas as pl
from jax.experimental.pallas import tpu as pltpu

F32 = jnp.float32
BF16 = jnp.bfloat16
I32 = jnp.int32
U32 = jnp.uint32

D_MODEL = 1024
CHUNK = 64
N_META = 16
MLA_HEADS = 8
MLA_V = 64
MLA_NOPE = 64
MLA_ROPE = 32
MLA_QK = MLA_NOPE + MLA_ROPE
MLA_Q_LORA = 256
MLA_KV_LORA = 128
MLA_SCALE = MLA_QK ** -0.5
ROPE_BASE = 10000.0
M_HEADS = 4
M_DV = 128
M_DK = 64
N_GROUPS = 4
E_PER_GROUP = 8
N_EXPERTS = 32
EXPERT_FF = 512
EPS = 1e-6

LANES = 128
HEAD_PAD = 128
PROJ_PAD = 2048
META_PAD = 128
NEG_BIG = -1e30
VMEM_LIMIT = 56 * 1024 * 1024

_O_CQ, _O_CKV, _O_MQ, _O_MK, _O_MV, _O_MO, _O_MISC = 0, 256, 384, 640, 896, 1408, 1920
_L_IG, _L_FG = 64, 68


def _cparams(sem):
    return pltpu.CompilerParams(dimension_semantics=sem, vmem_limit_bytes=VMEM_LIMIT)


def _dot(a, b):
    return jnp.dot(a, b, preferred_element_type=F32)


def _dot_nt(a, b):
    return lax.dot_general(a, b, (((1,), (1,)), ((), ())), preferred_element_type=F32)


def _split3(x):
    x1 = x.astype(BF16)
    r1 = x - x1.astype(F32)
    x2 = r1.astype(BF16)
    x3 = (r1 - x2.astype(F32)).astype(BF16)
    return x1, x2, x3


def _proj_kernel(x_ref, tab_ref, w_ref, g_ref, gcq_ref, wq_ref, gckv_ref, gq_ref, bias_ref,
                 q_ref, ckv_ref, misc_ref, mq_ref, mk_ref, mv_ref, og_ref):
    x = x_ref[...]
    xn = x * lax.rsqrt(jnp.mean(x * x, axis=-1, keepdims=True) + EPS) * g_ref[...]
    xb = xn.astype(BF16)
    tab = tab_ref[...]
    lane = lax.broadcasted_iota(I32, tab.shape, 1)

    def proj(off, width):
        return _dot(xb, w_ref[:, off:off + width])

    cq = proj(_O_CQ, MLA_Q_LORA)
    cqn = cq * lax.rsqrt(jnp.mean(cq * cq, axis=-1, keepdims=True) + EPS) * gcq_ref[...]
    cqb = cqn.astype(BF16)
    qw = MLA_HEADS * HEAD_PAD
    gq = gq_ref[...]
    tab_a = jnp.where(lane < MLA_NOPE, 1.0, jnp.where(lane < MLA_QK, tab, 0.0))
    tab_b = jnp.where((lane >= MLA_NOPE) & (lane < MLA_QK), pltpu.roll(tab, LANES - MLA_ROPE, 1), 0.0)
    slab = 2 * HEAD_PAD
    pair_ones = (lax.broadcasted_iota(I32, (slab, slab), 0) // HEAD_PAD
                 == lax.broadcasted_iota(I32, (slab, slab), 1) // HEAD_PAD).astype(BF16)
    for p in range(MLA_HEADS // 2):
        za = _dot(cqb, wq_ref[:, p * slab:(p + 1) * slab])
        zb = _dot(cqb, wq_ref[:, qw + p * slab:qw + (p + 1) * slab])
        ms = _dot((za * za).astype(BF16), pair_ones) * (1.0 / MLA_QK)
        for e in range(2):
            cols = slice(e * HEAD_PAD, (e + 1) * HEAD_PAD)
            qh = za[:, cols] * tab_a + zb[:, cols] * tab_b
            q_ref[2 * p + e] = (qh * lax.rsqrt(ms[:, cols] + EPS) * gq).astype(BF16)

    ckv = proj(_O_CKV, MLA_KV_LORA)
    ckv_ref[...] = ckv * lax.rsqrt(jnp.mean(ckv * ckv, axis=-1, keepdims=True) + EPS) * gckv_ref[...]

    zm = proj(_O_MISC, LANES)
    y = zm * tab
    rot = y + pltpu.roll(y, LANES - MLA_ROPE, 1)
    gate = zm + bias_ref[...]
    logf = jnp.minimum(gate, 0.0) - jnp.log1p(jnp.exp(-jnp.abs(gate)))
    misc = jnp.where(lane < MLA_ROPE, rot,
                     jnp.where((lane >= _L_IG) & (lane < _L_FG), gate,
                               jnp.where((lane >= _L_FG) & (lane < _L_FG + M_HEADS), logf, 0.0)))
    misc_ref[...] = misc

    mq_ref[...] = proj(_O_MQ, M_HEADS * M_DK).astype(BF16)
    mk_ref[...] = (proj(_O_MK, M_HEADS * M_DK) * (M_DK ** -0.5)).astype(BF16)
    mv_ref[...] = proj(_O_MV, M_HEADS * M_DV).astype(BF16)
    og_ref[...] = jax.nn.sigmoid(proj(_O_MO, M_HEADS * M_DV)).astype(BF16)


def _project(x2d, tab, pw, tm):
    t = x2d.shape[0]
    nt = t // tm
    ntab = tab.shape[0] // tm
    row = lambda i: (i, 0)
    full = lambda i: (0, 0)
    return pl.pallas_call(
        _proj_kernel,
        grid=(nt,),
        in_specs=[
            pl.BlockSpec((tm, D_MODEL), row),
            pl.BlockSpec((tm, LANES), lambda i: (i % ntab, 0)),
            pl.BlockSpec((D_MODEL, PROJ_PAD), full),
            pl.BlockSpec((1, D_MODEL), full),
            pl.BlockSpec((1, MLA_Q_LORA), full),
            pl.BlockSpec((MLA_Q_LORA, 2 * MLA_HEADS * HEAD_PAD), full),
            pl.BlockSpec((1, MLA_KV_LORA), full),
            pl.BlockSpec((1, HEAD_PAD), full),
            pl.BlockSpec((1, LANES), full),
        ],
        out_specs=[
            pl.BlockSpec((MLA_HEADS, tm, HEAD_PAD), lambda i: (0, i, 0)),
            pl.BlockSpec((tm, MLA_KV_LORA), row),
            pl.BlockSpec((tm, LANES), row),
            pl.BlockSpec((tm, M_HEADS * M_DK), row),
            pl.BlockSpec((tm, M_HEADS * M_DK), row),
            pl.BlockSpec((tm, M_HEADS * M_DV), row),
            pl.BlockSpec((tm, M_HEADS * M_DV), row),
        ],
        out_shape=[
            jax.ShapeDtypeStruct((MLA_HEADS, t, HEAD_PAD), BF16),
            jax.ShapeDtypeStruct((t, MLA_KV_LORA), F32),
            jax.ShapeDtypeStruct((t, LANES), F32),
            jax.ShapeDtypeStruct((t, M_HEADS * M_DK), BF16),
            jax.ShapeDtypeStruct((t, M_HEADS * M_DK), BF16),
            jax.ShapeDtypeStruct((t, M_HEADS * M_DV), BF16),
            jax.ShapeDtypeStruct((t, M_HEADS * M_DV), BF16),
        ],
        compiler_params=_cparams(("arbitrary",)),
        name="projection",
    )(x2d, tab, pw["w_in"], pw["g_attn"], pw["g_cq"], pw["w_q"], pw["g_ckv"], pw["g_q"], pw["gate_bias"])


SAFE_BOUND = 40.0
LOG2E = 1.4426950408889634


def _attn_kernel(sb_ref, q_ref, ckv_ref, misc_ref, ckvt_ref, misct_ref, wk_ref, wv_ref, gk_ref, o_ref,
                 kt_scr, v_scr, acc_scr, p_scr, *, lf, tq, tk, bt, causal, n_tail):
    i = pl.program_id(1)
    gk = gk_ref[...]

    def build(ckv_rows, misc_rows, dst):
        n = ckv_rows.shape[0]
        cb = ckv_rows.astype(BF16)
        kt_all = _dot_nt(wk_ref[...], cb)
        sel = (lax.broadcasted_iota(I32, (HEAD_PAD, LANES), 0)
               == lax.broadcasted_iota(I32, (HEAD_PAD, LANES), 1) + MLA_NOPE).astype(BF16)
        sel = jnp.where(lax.broadcasted_iota(I32, (HEAD_PAD, LANES), 1) < MLA_ROPE, sel, jnp.zeros_like(sel))
        m1, m2, m3 = _split3(misc_rows)
        kr_t = _dot_nt(sel, m1) + _dot_nt(sel, m2) + _dot_nt(sel, m3)
        v_all = _dot(cb, wv_ref[...])
        onecol = (lax.broadcasted_iota(I32, (n, LANES), 1) == MLA_V).astype(F32)
        gkc = jnp.tile(gk, (1, n // LANES))
        for h in range(MLA_HEADS):
            kk = kt_all[h * HEAD_PAD:(h + 1) * HEAD_PAD, :] + kr_t
            ms = jnp.sum(kk * kk, axis=0, keepdims=True) * (1.0 / MLA_QK)
            kt_scr[h, :, pl.ds(dst, n)] = (kk * lax.rsqrt(ms + EPS) * gkc).astype(BF16)
            v_scr[h, pl.ds(dst, n), :] = (v_all[:, h * LANES:(h + 1) * LANES] + onecol).astype(BF16)

    @pl.when(i == 0)
    def _():
        def body(r, carry):
            r0 = pl.multiple_of(r * bt, bt)
            build(ckv_ref[pl.ds(r0, bt), :], misc_ref[pl.ds(r0, bt), :], r0)
            return carry

        lax.fori_loop(0, lf // bt, body, 0)
        build(ckvt_ref[0], misct_ref[0], lf)

    ndiag = tq // tk if causal else 0
    nfull = i * ndiag if causal else lf // tk
    tail_mask = lax.broadcasted_iota(I32, (tq, META_PAD), 1) < n_tail

    def kt_tile(h, j):
        return kt_scr[h, :, pl.ds(pl.multiple_of(j * tk, tk), tk)]

    def v_tile(h, j):
        return v_scr[h, pl.ds(pl.multiple_of(j * tk, tk), tk), :]

    def chunk_ids(lo):
        rows = lax.broadcasted_iota(I32, (tq - lo, tk), 0) // CHUNK
        cols = lax.broadcasted_iota(I32, (tq - lo, tk), 1) // CHUNK
        return rows, cols

    def finish(h):
        acc = acc_scr[h]
        o_ref[:, h * MLA_V:(h + 1) * MLA_V] = (acc[:, :MLA_V] / acc[:, MLA_V:MLA_V + 1]).astype(BF16)

    safe = sb_ref[0] <= SAFE_BOUND * LOG2E

    @pl.when(safe)
    def _():
        bound = sb_ref[0]

        def scores(h, lo, kt):
            return _dot(q_ref[h, lo:, :], kt) * (MLA_SCALE * LOG2E) - bound

        def probs(s, mask):
            if mask is not None:
                s = jnp.where(mask, s, -jnp.inf)
            return jnp.exp2(s).astype(BF16)

        def pipe_step(j, lo, lo_prev, mask):
            for h in range(MLA_HEADS):
                acc_scr[h, lo_prev:, :] += _dot(p_scr[(j - 1) % 2, h, lo_prev:, :], v_tile(h, j - 1))
            for h in range(MLA_HEADS):
                p_scr[j % 2, h, lo:, :] = probs(scores(h, lo, kt_tile(h, j)), mask)

        for h in range(MLA_HEADS):
            p_tail = probs(scores(h, 0, kt_scr[h, :, lf:lf + META_PAD]), tail_mask)
            acc_scr[h] = _dot(p_tail, v_scr[h, lf:lf + META_PAD, :])
        rows0, cols0 = chunk_ids(0)
        if causal:
            first_mask = cols0 <= rows0 + jnp.where(nfull > 0, tk, 0)
        else:
            first_mask = None
        for h in range(MLA_HEADS):
            p_scr[0, h] = probs(scores(h, 0, kt_tile(h, 0)), first_mask)

        def body(j, carry):
            pipe_step(j, 0, 0, None)
            return carry

        lax.fori_loop(1, nfull, body, 0)
        if causal:
            @pl.when(nfull >= 1)
            def _():
                pipe_step(nfull, 0, 0, cols0 <= rows0)
            for d in range(1, ndiag):
                rows_d, cols_d = chunk_ids(d * tk)
                pipe_step(nfull + d, d * tk, (d - 1) * tk, cols_d <= rows_d)
            last, lo_last = nfull + ndiag - 1, (ndiag - 1) * tk
        else:
            last, lo_last = nfull - 1, 0
        for h in range(MLA_HEADS):
            acc_scr[h, lo_last:, :] += _dot(p_scr[last % 2, h, lo_last:, :], v_tile(h, last))
            finish(h)

    @pl.when(jnp.logical_not(safe))
    def _():
        rows0, cols0 = chunk_ids(0)
        for h in range(MLA_HEADS):
            qh = q_ref[h]

            def step(carry, kt, vrows, mask):
                m, acc = carry
                s = _dot(qh, kt) * MLA_SCALE
                if mask is not None:
                    s = jnp.where(mask, s, -jnp.inf)
                m_new = jnp.maximum(m, jnp.max(s, axis=-1, keepdims=True))
                acc = jnp.exp(m - m_new) * acc + _dot(jnp.exp(s - m_new).astype(BF16), vrows)
                return m_new, acc

            def body(j, c):
                mask = (cols0 + j * (tk // CHUNK) <= rows0 + i * (tq // CHUNK)) if causal else None
                return step(c, kt_tile(h, j), v_tile(h, j), mask)

            carry = (jnp.full((tq, 1), -jnp.inf, F32), jnp.zeros((tq, LANES), F32))
            carry = step(carry, kt_scr[h, :, lf:lf + META_PAD], v_scr[h, lf:lf + META_PAD, :], tail_mask)
            carry = lax.fori_loop(0, nfull + ndiag, body, carry)
            acc_scr[h] = carry[1]
            finish(h)


def _attention(q, ckv_f, misc_f, ckv_t, misc_t, pw, *, nb, lq, lf, tq, tk, causal, n_tail):
    nq = lq // tq
    bt = min(512, lf)
    assert lf % bt == 0 and lf % tk == 0 and lq % tq == 0 and tk % CHUNK == 0 and (not causal or tq % tk == 0)
    kern = functools.partial(_attn_kernel, lf=lf, tq=tq, tk=tk, bt=bt, causal=causal, n_tail=n_tail)
    full2 = lambda b, i, sb: (0, 0)
    full3 = lambda b, i, sb: (0, 0, 0)
    tail = full3 if ckv_t.shape[0] == 1 else (lambda b, i, sb: (b, 0, 0))
    lk = lf + META_PAD
    return pl.pallas_call(
        kern,
        grid_spec=pltpu.PrefetchScalarGridSpec(
            num_scalar_prefetch=1,
            grid=(nb, nq),
            in_specs=[
                pl.BlockSpec((MLA_HEADS, tq, HEAD_PAD), lambda b, i, sb: (0, b * nq + i, 0)),
                pl.BlockSpec((lf, MLA_KV_LORA), lambda b, i, sb: (b, 0)),
                pl.BlockSpec((lf, LANES), lambda b, i, sb: (b, 0)),
                pl.BlockSpec((1, META_PAD, MLA_KV_LORA), tail),
                pl.BlockSpec((1, META_PAD, LANES), tail),
                pl.BlockSpec((MLA_HEADS * HEAD_PAD, MLA_KV_LORA), full2),
                pl.BlockSpec((MLA_KV_LORA, MLA_HEADS * LANES), full2),
                pl.BlockSpec((HEAD_PAD, LANES), full2),
            ],
            out_specs=pl.BlockSpec((tq, MLA_HEADS * MLA_V), lambda b, i, sb: (b * nq + i, 0)),
            scratch_shapes=[
                pltpu.VMEM((MLA_HEADS, HEAD_PAD, lk), BF16),
                pltpu.VMEM((MLA_HEADS, lk, LANES), BF16),
                pltpu.VMEM((MLA_HEADS, tq, LANES), F32),
                pltpu.VMEM((2, MLA_HEADS, tq, tk), BF16),
            ],
        ),
        out_shape=jax.ShapeDtypeStruct((nb * lq, MLA_HEADS * MLA_V), BF16),
        compiler_params=_cparams(("arbitrary", "arbitrary")),
        name="attention",
    )(pw["score_bound"], q, ckv_f, misc_f, ckv_t, misc_t, pw["w_k"], pw["w_v"], pw["g_k"])


S_W = 2 * M_DV
S_SHAPE = (M_HEADS // 2, 2 * M_DK, S_W)


MLSTM_CHUNK = LANES


def _mlstm_kernel(q_ref, k_ref, v_ref, og_ref, misc_ref, s0_ref, m0_ref, gmh_ref,
                  h_ref, sout_ref, mout_ref, s_scr, m_scr, *, n_valid):
    c = pl.program_id(1)
    nc = pl.num_programs(1)
    ns = q_ref.shape[1]
    lb = q_ref.shape[2]
    lc = MLSTM_CHUNK

    @pl.when(c == 0)
    def _():
        for sl in range(ns):
            s_scr[sl] = s0_ref[0, min(sl, s0_ref.shape[1] - 1)]
            m_scr[sl] = m0_ref[0, min(sl, m0_ref.shape[1] - 1)]

    def rows(x):
        if lb == lc:
            return x
        return jnp.concatenate([x, jnp.zeros((lc - lb,) + x.shape[1:], x.dtype)], axis=0)

    rr = lax.broadcasted_iota(I32, (lc, lc), 0)
    cc = lax.broadcasted_iota(I32, (lc, lc), 1)
    causal = cc <= rr
    tri = causal.astype(BF16)
    trit = (rr <= cc).astype(BF16)
    ones_blk = jnp.ones((lc, M_DV), BF16)
    low_half = lax.broadcasted_iota(I32, (lc, LANES), 1) < M_DK
    limit = lb if n_valid is None else n_valid - c * lb

    def gates(sl):
        g = rows(misc_ref[0, sl])
        row = lax.broadcasted_iota(I32, g.shape, 0)
        lane = lax.broadcasted_iota(I32, g.shape, 1)
        g = jnp.where(row < limit, g, jnp.where((lane >= _L_IG) & (lane < _L_FG), NEG_BIG, 0.0))
        gt8 = g.T[_L_IG:_L_IG + 2 * M_HEADS, :]
        g1, g2, g3 = _split3(g)
        b_cols = _dot(tri, g1) + _dot(tri, g2) + _dot(tri, g3)
        t1, t2, t3 = _split3(gt8)
        b_rows = (_dot(t1, trit) + _dot(t2, trit) + _dot(t3, trit))[M_HEADS:2 * M_HEADS, :]
        a = gt8[0:M_HEADS, :] - b_rows
        m_prev = m_scr[sl, 0:M_HEADS, :]
        u_b = [-jnp.maximum(m_prev[h:h + 1, :], jnp.broadcast_to(
            jnp.max(jnp.where(causal, a[h:h + 1, :], -jnp.inf), axis=-1, keepdims=True), (lc, LANES)))
            for h in range(M_HEADS)]
        negm_b = [u_b[h] - jnp.broadcast_to(b_cols[:, _L_FG + h:_L_FG + h + 1], (lc, LANES)) for h in range(M_HEADS)]
        u_last = jnp.concatenate([u_b[h][lc - 1:lc, 0:1] for h in range(M_HEADS)], axis=0)
        return dict(a=a, m_prev=m_prev, u_b=u_b, negm_b=negm_b,
                    m_new=jnp.broadcast_to(b_rows[:, lc - 1:lc] - u_last, (M_HEADS, LANES)),
                    w_state=jnp.exp(a + u_last), decay=jnp.exp(m_prev[:, 0:1] + u_last))

    gt_ = [gates(sl) for sl in range(ns)]
    q = [rows(q_ref[0, sl]) for sl in range(ns)]
    k = [rows(k_ref[0, sl]) for sl in range(ns)]
    v = [rows(v_ref[0, sl]) for sl in range(ns)]
    kt = [x.astype(F32).T for x in k]

    units = [(sl, h) for sl in range(ns) for h in range(M_HEADS)]
    mine = {u: (jnp.logical_not(low_half) if u[1] % 2 else low_half) for u in units}
    pair = lambda x, u: x[u[0]][:, (u[1] // 2) * LANES:(u[1] // 2 + 1) * LANES]
    kh = {u: jnp.where(mine[u], pair(k, u), jnp.zeros_like(pair(k, u))) for u in units}
    qh = {u: jnp.where(mine[u], pair(q, u), jnp.zeros_like(pair(q, u))) for u in units}
    vext = {u: jnp.concatenate([v[u[0]][:, u[1] * M_DV:(u[1] + 1) * M_DV], ones_blk], axis=1) for u in units}
    s_old = {(sl, j): s_scr[sl, j] for sl in range(ns) for j in range(M_HEADS // 2)}
    s_bf = {key: val.astype(BF16) for key, val in s_old.items()}

    qk = {u: _dot_nt(pair(q, u), kh[u]) for u in units}
    qs = {u: _dot(qh[u], s_bf[(u[0], u[1] // 2)]) for u in units}
    ktw = {u: (kt[u[0]][u[1] * M_DK:(u[1] + 1) * M_DK, :] * gt_[u[0]]["w_state"][u[1]:u[1] + 1, :]).astype(BF16)
           for u in units}
    upd = {u: _dot(ktw[u], vext[u]) for u in units}
    w = {u: (jnp.where(causal, jnp.exp(gt_[u[0]]["a"][u[1]:u[1] + 1, :] + gt_[u[0]]["u_b"][u[1]]), 0.0)
             * qk[u]).astype(BF16) for u in units}
    inter = {u: jnp.exp(gt_[u[0]]["m_prev"][u[1]:u[1] + 1, :] + gt_[u[0]]["u_b"][u[1]]) for u in units}
    r = {u: _dot(w[u], vext[u]) + jnp.tile(inter[u], (1, 2)) * qs[u] for u in units}
    num = {u: r[u][:, :M_DV] for u in units}
    sq = {u: num[u] * num[u] for u in units}
    sq_hi = {u: sq[u].astype(BF16) for u in units}
    sq_lo = {u: (sq[u] - sq_hi[u].astype(F32)).astype(BF16) for u in units}
    msn = {u: (_dot(sq_hi[u], ones_blk) + _dot(sq_lo[u], ones_blk)) * (1.0 / M_DV) for u in units}
    for u in units:
        sl, h = u
        rden = 1.0 / jnp.maximum(jnp.abs(r[u][:, M_DV:]), jnp.exp(gt_[sl]["negm_b"][h]))
        scale = rden * lax.rsqrt(rden * rden * msn[u] + EPS)
        hn = num[u] * scale * gmh_ref[h:h + 1, :]
        out = (hn[:lb] * og_ref[0, sl, :, h * M_DV:(h + 1) * M_DV].astype(F32)).astype(BF16)
        h_ref[0, sl, :, h * M_DV:(h + 1) * M_DV] = out
    for sl in range(ns):
        for j in range(M_HEADS // 2):
            new = [gt_[sl]["decay"][h:h + 1, :] * s_old[(sl, j)][(h % 2) * M_DK:(h % 2 + 1) * M_DK, :] + upd[(sl, h)]
                   for h in (2 * j, 2 * j + 1)]
            s_scr[sl, j] = jnp.concatenate(new, axis=0)
        m_scr[sl, 0:M_HEADS, :] = gt_[sl]["m_new"]

    @pl.when(c == nc - 1)
    def _():
        sout_ref[0] = s_scr[...]
        mout_ref[0] = m_scr[...]


def _mlstm(mq, mk, mv, og, misc, s0, m0, gmh, *, nb, l, n_valid=None):
    lc = min(l, MLSTM_CHUNK)
    assert l % lc == 0 and lc % 16 == 0
    nc = l // lc
    ns = next(n for n in (4, 2, 1) if nb % n == 0)
    ng = nb // ns
    if s0.shape[0] == 1:
        s0, m0 = s0[:, None], m0[:, None]
        st = lambda g, c: (0, 0, 0, 0, 0)
        mst = lambda g, c: (0, 0, 0, 0)
    else:
        s0, m0 = s0.reshape((ng, ns) + S_SHAPE), m0.reshape(ng, ns, 8, LANES)
        st = lambda g, c: (g, 0, 0, 0, 0)
        mst = lambda g, c: (g, 0, 0, 0)
    blk = lambda g, c: (g, 0, c, 0)
    split = lambda x: x.reshape(ng, ns, l, x.shape[-1])
    kern = functools.partial(_mlstm_kernel, n_valid=n_valid)
    h, s, m = pl.pallas_call(
        kern,
        grid=(ng, nc),
        in_specs=[
            pl.BlockSpec((1, ns, lc, M_HEADS * M_DK), blk),
            pl.BlockSpec((1, ns, lc, M_HEADS * M_DK), blk),
            pl.BlockSpec((1, ns, lc, M_HEADS * M_DV), blk),
            pl.BlockSpec((1, ns, lc, M_HEADS * M_DV), blk),
            pl.BlockSpec((1, ns, lc, LANES), blk),
            pl.BlockSpec((1, s0.shape[1]) + S_SHAPE, st),
            pl.BlockSpec((1, m0.shape[1], 8, LANES), mst),
            pl.BlockSpec((M_HEADS, M_DV), lambda g, c: (0, 0)),
        ],
        out_specs=[
            pl.BlockSpec((1, ns, lc, M_HEADS * M_DV), blk),
            pl.BlockSpec((1, ns) + S_SHAPE, lambda g, c: (g, 0, 0, 0, 0)),
            pl.BlockSpec((1, ns, 8, LANES), lambda g, c: (g, 0, 0, 0)),
        ],
        out_shape=[
            jax.ShapeDtypeStruct((ng, ns, l, M_HEADS * M_DV), BF16),
            jax.ShapeDtypeStruct((ng, ns) + S_SHAPE, F32),
            jax.ShapeDtypeStruct((ng, ns, 8, LANES), F32),
        ],
        scratch_shapes=[
            pltpu.VMEM((ns,) + S_SHAPE, F32),
            pltpu.VMEM((ns, 8, LANES), F32),
        ],
        compiler_params=_cparams(("arbitrary", "arbitrary")),
        name="mlstm",
    )(split(mq), split(mk), split(mv), split(og), split(misc), s0, m0, gmh)
    return h.reshape(nb * l, M_HEADS * M_DV), s.reshape((nb,) + S_SHAPE), m.reshape(nb, 8, LANES)


R_ROWS = 40


def _route_kernel(x_ref, att_ref, hm_ref, wo_ref, g_ref, wr_ref, br_ref,
                  x1_ref, hp_ref, ri_ref, rg_ref, cnt_ref, cnt_scr):
    i = pl.program_id(0)
    tm = x_ref.shape[0]

    @pl.when(i == 0)
    def _():
        cnt_scr[...] = jnp.zeros_like(cnt_scr)

    mix = jnp.concatenate([att_ref[...], hm_ref[...]], axis=1)
    x1 = x_ref[...] + _dot(mix, wo_ref[...])
    x1_ref[...] = x1
    hn = x1 * lax.rsqrt(jnp.mean(x1 * x1, axis=-1, keepdims=True) + EPS) * g_ref[...]
    hb = hn.astype(BF16)
    half = D_MODEL // 2
    hi = lax.bitcast_convert_type(hb[:, :half].astype(F32), U32)
    lo = lax.bitcast_convert_type(hb[:, half:].astype(F32), U32)
    hp_ref[...] = (hi & jnp.uint32(0xFFFF0000)) | (lo >> 16)

    logits = _dot_nt(wr_ref[...], hb) + br_ref[:, 0:1]
    e_log = logits[0:N_EXPERTS, :]
    g_log = logits[N_EXPERTS:N_EXPERTS + N_GROUPS, :]
    gmax = jnp.max(g_log, axis=0, keepdims=True)
    gsum = jnp.sum(jnp.exp(g_log - gmax), axis=0, keepdims=True)
    gi = lax.broadcasted_iota(I32, g_log.shape, 0)
    g_idx = jnp.min(jnp.where(g_log == gmax, gi, N_GROUPS), axis=0, keepdims=True)
    e_sel = jnp.zeros((E_PER_GROUP, tm), F32)
    for gg in range(N_GROUPS):
        e_sel = jnp.where(g_idx == gg, e_log[gg * E_PER_GROUP:(gg + 1) * E_PER_GROUP, :], e_sel)
    ei = lax.broadcasted_iota(I32, e_sel.shape, 0)
    m1 = jnp.max(e_sel, axis=0, keepdims=True)
    i1 = jnp.min(jnp.where(e_sel == m1, ei, E_PER_GROUP), axis=0, keepdims=True)
    e2 = jnp.where(ei == i1, -jnp.inf, e_sel)
    m2 = jnp.max(e2, axis=0, keepdims=True)
    i2 = jnp.min(jnp.where(e2 == m2, ei, E_PER_GROUP), axis=0, keepdims=True)
    ex = jnp.exp(m2 - m1)
    gp = 1.0 / gsum
    p1 = 1.0 / (1.0 + ex)
    gate1 = gp * p1
    gate2 = gp * (ex * p1)
    id1 = g_idx * E_PER_GROUP + i1
    id2 = g_idx * E_PER_GROUP + i2

    xi = lax.broadcasted_iota(I32, (N_EXPERTS, tm), 0)
    oh1 = xi == id1
    oh2 = xi == id2
    e_cnt = (oh1 | oh2).astype(F32)
    rr = lax.broadcasted_iota(I32, (tm, tm), 0)
    cc = lax.broadcasted_iota(I32, (tm, tm), 1)
    upper = (rr < cc).astype(BF16)
    pref = _dot(e_cnt.astype(BF16), upper) + cnt_scr[:, 0:1]
    rank1 = jnp.sum(jnp.where(oh1, pref, 0.0), axis=0, keepdims=True)
    rank2 = jnp.sum(jnp.where(oh2, pref, 0.0), axis=0, keepdims=True)
    cnt_new = cnt_scr[...] + jnp.sum(e_cnt, axis=1, keepdims=True)
    cnt_scr[...] = cnt_new
    cnt_ref[...] = cnt_new.astype(I32)

    zi = jnp.zeros((1, tm), I32)
    ri_ref[...] = jnp.concatenate([id1, id2, rank1.astype(I32), rank2.astype(I32), zi, zi, zi, zi], axis=0)
    zf = jnp.zeros((1, tm), F32)
    rg_ref[...] = jnp.concatenate([gate1, gate2, zf, zf, zf, zf, zf, zf], axis=0)


def _route(x2d, att, hm, pw, tm):
    t = x2d.shape[0]
    row = lambda i: (i, 0)
    col = lambda i: (0, i)
    full = lambda i: (0, 0)
    return pl.pallas_call(
        _route_kernel,
        grid=(t // tm,),
        in_specs=[
            pl.BlockSpec((tm, D_MODEL), row),
            pl.BlockSpec((tm, D_MODEL // 2), row),
            pl.BlockSpec((tm, D_MODEL // 2), row),
            pl.BlockSpec((D_MODEL, D_MODEL), full),
            pl.BlockSpec((1, D_MODEL), full),
            pl.BlockSpec((R_ROWS, D_MODEL), full),
            pl.BlockSpec((R_ROWS, LANES), full),
        ],
        out_specs=[
            pl.BlockSpec((tm, D_MODEL), row),
            pl.BlockSpec((tm, D_MODEL // 2), row),
            pl.BlockSpec((8, tm), col),
            pl.BlockSpec((8, tm), col),
            pl.BlockSpec((N_EXPERTS, LANES), full),
        ],
        out_shape=[
            jax.ShapeDtypeStruct((t, D_MODEL), F32),
            jax.ShapeDtypeStruct((t, D_MODEL // 2), U32),
            jax.ShapeDtypeStruct((8, t), I32),
            jax.ShapeDtypeStruct((8, t), F32),
            jax.ShapeDtypeStruct((N_EXPERTS, LANES), I32),
        ],
        scratch_shapes=[pltpu.VMEM((N_EXPERTS, LANES), F32)],
        compiler_params=_cparams(("arbitrary",)),
        name="route",
    )(x2d, att, hm, pw["w_out"], pw["g_ffn"], pw["w_r"], pw["b_r"])


SUBLANES = 8


N_ZERO_BLOCKS = 2 * N_EXPERTS


def _dispatch_kernel(zb_ref, dest_ref, hp_ref, xs_ref, zero_scr, sem, zsem, *, tm, bm):
    @pl.when(pl.program_id(0) == 0)
    def _():
        zero_scr[...] = jnp.zeros_like(zero_scr)

        def zcopy(k):
            return pltpu.make_async_copy(zero_scr, xs_ref.at[pl.ds(pl.multiple_of(zb_ref[k] * bm, bm), bm)], zsem)

        def zstart(k, carry):
            @pl.when(zb_ref[k] >= 0)
            def _():
                zcopy(k).start()
            return carry

        def zwait(k, carry):
            @pl.when(zb_ref[k] >= 0)
            def _():
                zcopy(k).wait()
            return carry

        lax.fori_loop(0, N_ZERO_BLOCKS, zstart, 0)
        lax.fori_loop(0, N_ZERO_BLOCKS, zwait, 0)

    def body(g, carry):
        for u in range(SUBLANES):
            t = g * SUBLANES + u
            src = hp_ref.at[g, pl.ds(u, 1), :]
            pltpu.make_async_copy(src, xs_ref.at[pl.ds(dest_ref[t], 1)], sem).start(priority=0)
            pltpu.make_async_copy(src, xs_ref.at[pl.ds(dest_ref[tm + t], 1)], sem).start(priority=1)
        return carry

    lax.fori_loop(0, tm // SUBLANES, body, 0)
    pltpu.make_async_copy(xs_ref.at[pl.ds(0, 2 * tm)], xs_ref.at[pl.ds(0, 2 * tm)], sem).wait()


def _tile_slots(dest, tm):
    t = dest.shape[1]
    return dest[0:2].reshape(2, t // tm, tm).transpose(1, 0, 2).reshape(2 * t)


def _dispatch(zero_blocks, dest, hp, n_slots, tm, bm):
    t = hp.shape[0]
    return pl.pallas_call(
        functools.partial(_dispatch_kernel, tm=tm, bm=bm),
        grid_spec=pltpu.PrefetchScalarGridSpec(
            num_scalar_prefetch=1,
            grid=(t // tm,),
            in_specs=[
                pl.BlockSpec((2 * tm,), lambda i, zb: (i,), memory_space=pltpu.SMEM),
                pl.BlockSpec((tm // SUBLANES, SUBLANES, D_MODEL // 2), lambda i, zb: (i, 0, 0)),
            ],
            out_specs=pl.BlockSpec(memory_space=pl.ANY),
            scratch_shapes=[
                pltpu.VMEM((bm, D_MODEL // 2), U32),
                pltpu.SemaphoreType.DMA,
                pltpu.SemaphoreType.DMA,
            ],
        ),
        out_shape=jax.ShapeDtypeStruct((n_slots, D_MODEL // 2), U32),
        compiler_params=_cparams(("arbitrary",)),
        name="dispatch",
    )(zero_blocks, _tile_slots(dest, tm), hp.reshape(t // SUBLANES, SUBLANES, D_MODEL // 2))


def _expert_kernel(be_ref, nu_ref, xs_ref, w1_ref, w3_ref, w2_ref, ys_ref, w1_scr, w3_scr, w2_scr):
    i = pl.program_id(0)
    new_expert = jnp.logical_or(i == 0, be_ref[i] != be_ref[jnp.maximum(i - 1, 0)])

    @pl.when(jnp.logical_and(new_expert, i < nu_ref[0]))
    def _():
        w1_scr[...] = w1_ref[0].astype(BF16)
        w3_scr[...] = w3_ref[0].astype(BF16)
        w2_scr[...] = w2_ref[0].astype(BF16)

    @pl.when(i < nu_ref[0])
    def _():
        xw = xs_ref[...]
        xa = lax.bitcast_convert_type(xw & jnp.uint32(0xFFFF0000), F32).astype(BF16)
        xb = lax.bitcast_convert_type(xw << 16, F32).astype(BF16)
        half = D_MODEL // 2
        h1 = _dot(xa, w1_scr[:half, :]) + _dot(xb, w1_scr[half:, :])
        h3 = _dot(xa, w3_scr[:half, :]) + _dot(xb, w3_scr[half:, :])
        a = (h1 * jax.nn.sigmoid(h1)) * h3
        ys_ref[...] = _dot(a.astype(BF16), w2_scr[...])

    @pl.when(pl.program_id(0) >= nu_ref[0])
    def _():
        ys_ref[...] = jnp.zeros_like(ys_ref)


def _experts(blk_e, n_used, xs, pw, bm):
    n_slots = xs.shape[0]
    nblk = n_slots // bm
    blk = lambda i, be, nu: (jnp.minimum(i, nu[0] - 1), 0)
    oblk = lambda i, be, nu: (i, 0)
    wsel = lambda i, be, nu: (be[i], 0, 0)
    return pl.pallas_call(
        _expert_kernel,
        grid_spec=pltpu.PrefetchScalarGridSpec(
            num_scalar_prefetch=2,
            grid=(nblk,),
            in_specs=[
                pl.BlockSpec((bm, D_MODEL // 2), blk),
                pl.BlockSpec((1, D_MODEL, EXPERT_FF), wsel),
                pl.BlockSpec((1, D_MODEL, EXPERT_FF), wsel),
                pl.BlockSpec((1, EXPERT_FF, D_MODEL), wsel),
            ],
            out_specs=pl.BlockSpec((bm, D_MODEL), oblk),
            scratch_shapes=[
                pltpu.VMEM((D_MODEL, EXPERT_FF), BF16),
                pltpu.VMEM((D_MODEL, EXPERT_FF), BF16),
                pltpu.VMEM((EXPERT_FF, D_MODEL), BF16),
            ],
        ),
        out_shape=jax.ShapeDtypeStruct((n_slots, D_MODEL), F32),
        compiler_params=_cparams(("arbitrary",)),
        name="experts",
    )(blk_e, n_used, xs, pw["w1"], pw["w3"], pw["w2"])


def _combine_kernel(dest_ref, dnext_ref, x1_ref, rg_ref, ys_ref, y_ref, r0_scr, r1_scr, sem, *, tm):
    i = pl.program_id(0)
    slot = i % 2

    def issue(slots_ref, s):
        def body(g, carry):
            for u in range(SUBLANES):
                t = g * SUBLANES + u
                pltpu.make_async_copy(ys_ref.at[pl.ds(slots_ref[t], 1)], r0_scr.at[s, g, pl.ds(u, 1), :],
                                      sem.at[s]).start(priority=0)
                pltpu.make_async_copy(ys_ref.at[pl.ds(slots_ref[tm + t], 1)], r1_scr.at[s, g, pl.ds(u, 1), :],
                                      sem.at[s]).start(priority=1)
            return carry

        lax.fori_loop(0, tm // SUBLANES, body, 0)

    @pl.when(i == 0)
    def _():
        issue(dest_ref, 0)

    @pl.when(i + 1 < pl.num_programs(0))
    def _():
        issue(dnext_ref, 1 - slot)

    gt = jnp.concatenate([rg_ref[...], jnp.zeros((LANES - 8, tm), F32)], axis=0).T
    slab = ys_ref.at[pl.ds(0, tm)]
    pltpu.make_async_copy(slab, slab, sem.at[slot]).wait()
    pltpu.make_async_copy(slab, slab, sem.at[slot]).wait()
    r0 = r0_scr[slot].reshape(tm, D_MODEL)
    r1 = r1_scr[slot].reshape(tm, D_MODEL)
    y_ref[...] = x1_ref[...] + gt[:, 0:1] * r0 + gt[:, 1:2] * r1


def _combine(dest, x1, rg, ys, tm):
    t = x1.shape[0]
    nt = t // tm
    row = lambda i: (i, 0)
    col = lambda i: (0, i)
    slots = _tile_slots(dest, tm)
    return pl.pallas_call(
        functools.partial(_combine_kernel, tm=tm),
        grid=(nt,),
        in_specs=[
            pl.BlockSpec((2 * tm,), lambda i: (i,), memory_space=pltpu.SMEM),
            pl.BlockSpec((2 * tm,), lambda i: (jnp.minimum(i + 1, nt - 1),), memory_space=pltpu.SMEM),
            pl.BlockSpec((tm, D_MODEL), row),
            pl.BlockSpec((8, tm), col),
            pl.BlockSpec(memory_space=pl.ANY),
        ],
        out_specs=pl.BlockSpec((tm, D_MODEL), row),
        out_shape=jax.ShapeDtypeStruct((t, D_MODEL), F32),
        scratch_shapes=[
            pltpu.VMEM((2, tm // SUBLANES, SUBLANES, D_MODEL), F32),
            pltpu.VMEM((2, tm // SUBLANES, SUBLANES, D_MODEL), F32),
            pltpu.SemaphoreType.DMA((2,)),
        ],
        compiler_params=_cparams(("arbitrary",)),
        name="combine",
    )(slots, slots, x1, rg, ys)


def _prep_weights(g_attn, w_in, g_cq, w_uq, g_ckv, w_ukv, g_q, g_k, b_igate, b_fgate, g_mh,
                  w_out, g_ffn, w_group, b_group, w_erouter, b_erouter, w1, w3, w2):
    def cols(a, b):
        return w_in[:, a:b]

    o_cq, o_ckv, o_kr = 0, 256, 384
    o_mq, o_mk, o_mv, o_mi, o_mf, o_mo = 416, 672, 928, 1440, 1444, 1448
    hr = MLA_ROPE // 2
    misc = jnp.concatenate([
        cols(o_kr, o_kr + MLA_ROPE), -cols(o_kr + hr, o_kr + MLA_ROPE), cols(o_kr, o_kr + hr),
        cols(o_mi, o_mi + M_HEADS), cols(o_mf, o_mf + M_HEADS),
        jnp.zeros((D_MODEL, LANES - 2 * MLA_ROPE - 2 * M_HEADS), F32)], axis=1)
    w_p = jnp.concatenate([
        cols(o_cq, o_cq + 256), cols(o_ckv, o_ckv + 128), cols(o_mq, o_mq + 256), cols(o_mk, o_mk + 256),
        cols(o_mv, o_mv + 512), cols(o_mo, o_mo + 512), misc], axis=1).astype(BF16)
    nope, r1, r2 = w_uq[..., :MLA_NOPE], w_uq[..., MLA_NOPE:MLA_NOPE + hr], w_uq[..., MLA_NOPE + hr:]
    zq = lambda w: jnp.zeros((MLA_Q_LORA, MLA_HEADS, w), F32)
    w_qa = jnp.concatenate([nope, r1, r2, zq(HEAD_PAD - MLA_QK)], axis=-1).reshape(MLA_Q_LORA, MLA_HEADS * HEAD_PAD)
    w_qb = jnp.concatenate([zq(MLA_NOPE), -r2, r1, zq(HEAD_PAD - MLA_QK)], axis=-1).reshape(MLA_Q_LORA,
                                                                                           MLA_HEADS * HEAD_PAD)
    w_q = jnp.concatenate([w_qa, w_qb], axis=1).astype(BF16)
    pad_g = jnp.zeros((HEAD_PAD - MLA_QK,), F32)
    w_k = jnp.concatenate([w_ukv[..., :MLA_NOPE], jnp.zeros((MLA_KV_LORA, MLA_HEADS, HEAD_PAD - MLA_NOPE), F32)],
                          axis=-1).reshape(MLA_KV_LORA, MLA_HEADS * HEAD_PAD).T.astype(BF16)
    w_v = jnp.concatenate([w_ukv[..., MLA_NOPE:], jnp.zeros((MLA_KV_LORA, MLA_HEADS, LANES - MLA_V), F32)],
                          axis=-1).reshape(MLA_KV_LORA, MLA_HEADS * LANES).astype(BF16)
    gate_bias = jnp.concatenate([jnp.zeros((_L_IG,), F32), b_igate, b_fgate,
                                 jnp.zeros((LANES - _L_FG - M_HEADS,), F32)])[None]
    w_r = jnp.concatenate([w_erouter.T, w_group.T, jnp.zeros((R_ROWS - N_EXPERTS - N_GROUPS, D_MODEL), F32)],
                          axis=0).astype(BF16)
    b_r = jnp.concatenate([b_erouter, b_group, jnp.zeros((R_ROWS - N_EXPERTS - N_GROUPS,), F32)])
    return {
        "w_in": w_p, "g_attn": g_attn[None], "g_cq": g_cq[None], "w_q": w_q, "g_ckv": g_ckv[None],
        "g_q": jnp.concatenate([g_q, pad_g])[None], "g_k": jnp.broadcast_to(jnp.concatenate([g_k, pad_g])[:, None], (HEAD_PAD, LANES)),
        "gate_bias": gate_bias, "w_k": w_k, "w_v": w_v, "g_mh": g_mh,
        "w_out": w_out.astype(BF16), "g_ffn": g_ffn[None], "w_r": w_r,
        "b_r": jnp.broadcast_to(b_r[:, None], (R_ROWS, LANES)),
        "w1": w1, "w3": w3, "w2": w2,
        "score_bound": (MLA_QK * MLA_SCALE * LOG2E * 1.01 * jnp.max(jnp.abs(g_q)) * jnp.max(jnp.abs(g_k))).reshape(1),
    }


def _rope_table(pos):
    half = MLA_ROPE // 2
    inv = ROPE_BASE ** (-np.arange(half, dtype=np.float64) / half)
    ang = np.asarray(pos, np.float64)[:, None] * inv[None, :]
    cos = np.cos(ang)
    sin = np.sin(ang)
    c2 = np.concatenate([cos, cos], axis=1)
    s2 = np.concatenate([sin, sin], axis=1)
    return jnp.asarray(np.concatenate([c2, s2, c2, s2], axis=1), F32)


def _pick(n, pref):
    return pref if n % pref == 0 else n


def _slots_kernel(ri_ref, ps_ref, d_ref):
    tm = ri_ref.shape[1]
    xi = lax.broadcasted_iota(I32, (N_EXPERTS, tm), 0)
    ps = ps_ref[:, 0:1]
    rows = []
    for k in range(2):
        start = jnp.sum(jnp.where(xi == ri_ref[k:k + 1, :], ps, 0), axis=0, keepdims=True)
        rows.append(start + ri_ref[2 + k:3 + k, :])
    d_ref[...] = jnp.concatenate(rows + [jnp.zeros((6, tm), I32)], axis=0)


def _slots(ri, pstart_b, tm):
    t = ri.shape[1]
    return pl.pallas_call(
        _slots_kernel,
        grid=(t // tm,),
        in_specs=[pl.BlockSpec((8, tm), lambda i: (0, i)), pl.BlockSpec((N_EXPERTS, LANES), lambda i: (0, 0))],
        out_specs=pl.BlockSpec((8, tm), lambda i: (0, i)),
        out_shape=jax.ShapeDtypeStruct((8, t), I32),
        compiler_params=_cparams(("arbitrary",)),
        name="slots",
    )(ri, pstart_b)


def _moe_layer(x2d, att, hm, pw, *, tm_route, bm, tm_disp, tm_comb):
    t = x2d.shape[0]
    x1, hp, ri, rg, cnt = _route(x2d, att, hm, pw, tm_route)
    counts = cnt[:, 0]
    padded = (counts + bm - 1) // bm * bm
    pend = jnp.cumsum(padded)
    pstart = pend - padded
    dest = _slots(ri, jnp.broadcast_to(pstart[:, None], (N_EXPERTS, LANES)), _pick(t, 2048))
    n_slots = (2 * t // bm + N_EXPERTS) * bm
    nblk = n_slots // bm
    n_used = (pend[-1] // bm).astype(I32)
    experts = jnp.arange(N_EXPERTS, dtype=I32)
    blk_first = jnp.arange(nblk, dtype=I32) * bm
    blk_e = jnp.sum((pend[None, :] <= blk_first[:, None]).astype(I32), axis=1)
    blk_e = jnp.minimum(blk_e, jnp.max(jnp.where(counts > 0, experts, 0)))
    zero_blocks = jnp.concatenate([jnp.where(counts > 0, pend // bm - 1, -1),
                                   jnp.where(n_used + experts < nblk, n_used + experts, -1)]).astype(I32)
    xs = _dispatch(zero_blocks, dest, hp, n_slots, tm_disp, bm)
    ys = _experts(blk_e, n_used[None], xs, pw, bm)
    return _combine(dest, x1, rg, ys, tm_comb)


def _state_pack(c, n):
    ct = jnp.swapaxes(c, -1, -2)
    s = jnp.concatenate([ct, jnp.broadcast_to(n[..., None], ct.shape[:-1] + (S_W - M_DV,))], axis=-1)
    return s.reshape((s.shape[0],) + S_SHAPE)


def _state_unpack(s, m):
    s = s.reshape(s.shape[0], M_HEADS, M_DK, S_W)
    return jnp.swapaxes(s[..., :M_DV], -1, -2), s[..., M_DV], m[:, :M_HEADS, 0]


def kernel(x_prompt, x_sample, cache_ckv, cache_krope, state_mlstm_c, state_mlstm_n, state_mlstm_m,
           meta_tokens, g_attn, w_in, g_cq, w_uq, g_ckv, w_ukv, g_q, g_k, b_igate, b_fgate, g_mh,
           w_out, g_ffn, w_group, b_group, w_erouter, b_erouter, w1, w3, w2):
    bp, seq = x_prompt.shape[:2]
    bs, dec = x_sample.shape[:2]
    past = cache_ckv.shape[2]
    layer = 0
    pw = _prep_weights(g_attn[layer], w_in[layer], g_cq[layer], w_uq[layer], g_ckv[layer], w_ukv[layer],
                       g_q[layer], g_k[layer], b_igate[layer], b_fgate[layer], g_mh[layer], w_out[layer],
                       g_ffn[layer], w_group[layer], b_group[layer], w_erouter[layer], b_erouter[layer],
                       w1[layer], w3[layer], w2[layer])

    xm = jnp.concatenate([meta_tokens, jnp.zeros((META_PAD - N_META, D_MODEL), F32)], axis=0)
    tab_m = _rope_table(np.arange(META_PAD) - N_META)
    _, ckv_m, misc_m, mq_m, mk_m, mv_m, og_m = _project(xm, tab_m, pw, META_PAD)
    zero_s = jnp.zeros((1,) + S_SHAPE, F32)
    zero_m = jnp.zeros((1, 8, LANES), F32)
    _, s_meta, m_meta = _mlstm(mq_m, mk_m, mv_m, og_m, misc_m, zero_s, zero_m, pw["g_mh"],
                               nb=1, l=META_PAD, n_valid=N_META)

    tp = bp * seq
    xp2 = x_prompt.reshape(tp, D_MODEL)
    tm_p = _pick(seq, 1024)
    q_p, ckv_p, misc_p, mq_p, mk_p, mv_p, og_p = _project(xp2, _rope_table(np.arange(seq)), pw, tm_p)
    tq = _pick(seq, 1024)
    att_p = _attention(q_p, ckv_p, misc_p, ckv_m[None], misc_m[None], pw, nb=bp, lq=seq, lf=seq, tq=tq, tk=256,
                       causal=True, n_tail=N_META)
    hm_p, s_p, m_p = _mlstm(mq_p, mk_p, mv_p, og_p, misc_p, s_meta, m_meta, pw["g_mh"],
                            nb=bp, l=seq)
    y_p = _moe_layer(xp2, att_p, hm_p, pw, tm_route=tm_p, bm=1024, tm_disp=_pick(seq, 512),
                     tm_comb=_pick(seq, 512))

    ts = bs * dec
    xs2 = x_sample.reshape(ts, D_MODEL)
    q_s, ckv_s, misc_s, mq_s, mk_s, mv_s, og_s = _project(xs2, _rope_table(past + np.arange(dec)), pw, dec)
    kr_cache = jnp.concatenate([cache_krope[layer], jnp.zeros((bs, past, LANES - MLA_ROPE), F32)], axis=-1)
    n_tail = dec + N_META
    assert n_tail <= META_PAD

    def tail_rows(own, meta):
        w = own.shape[-1]
        return jnp.concatenate([own.reshape(bs, dec, w), jnp.broadcast_to(meta[None, :N_META], (bs, N_META, w)),
                                jnp.zeros((bs, META_PAD - n_tail, w), F32)], axis=1)

    att_s = _attention(q_s, cache_ckv[layer].reshape(bs * past, MLA_KV_LORA), kr_cache.reshape(bs * past, LANES),
                       tail_rows(ckv_s, ckv_m), tail_rows(misc_s, misc_m), pw, nb=bs, lq=dec, lf=past, tq=dec,
                       tk=_pick(past, 256), causal=False, n_tail=n_tail)
    s0 = _state_pack(state_mlstm_c[layer], state_mlstm_n[layer])
    m0 = jnp.concatenate([jnp.broadcast_to(state_mlstm_m[layer][:, :, None], (bs, M_HEADS, LANES)),
                          jnp.zeros((bs, 8 - M_HEADS, LANES), F32)], axis=1)
    hm_s, s_s, m_s = _mlstm(mq_s, mk_s, mv_s, og_s, misc_s, s0, m0, pw["g_mh"], nb=bs, l=dec)
    y_s = _moe_layer(xs2, att_s, hm_s, pw, tm_route=_pick(ts, 512), bm=128, tm_disp=_pick(ts, 512),
                     tm_comb=_pick(ts, 512))

    m_ckv = ckv_m[:N_META]
    m_kr = misc_m[:N_META, :MLA_ROPE]
    new_ckv_p = jnp.concatenate([jnp.broadcast_to(m_ckv[None], (bp, N_META, MLA_KV_LORA)),
                                 ckv_p.reshape(bp, seq, MLA_KV_LORA)], axis=1)[None]
    new_kr_p = jnp.concatenate([jnp.broadcast_to(m_kr[None], (bp, N_META, MLA_ROPE)),
                                misc_p[:, :MLA_ROPE].reshape(bp, seq, MLA_ROPE)], axis=1)[None]
    c_p, n_p, mm_p = _state_unpack(s_p, m_p)
    c_s, n_s, mm_s = _state_unpack(s_s, m_s)
    return (y_p.reshape(bp, seq, D_MODEL), y_s.reshape(bs, dec, D_MODEL),
            new_ckv_p, new_kr_p, c_p[None], n_p[None], mm_p[None],
            ckv_s.reshape(bs, dec, MLA_KV_LORA)[None], misc_s[:, :MLA_ROPE].reshape(bs, dec, MLA_ROPE)[None],
            c_s[None], n_s[None], mm_s[None])
```

```python
import functools

import numpy as np
import jax
import jax.numpy as jnp
from jax import lax
from jax.experimental import pallas as pl
from jax.experimental.pallas import tpu as pltpu

F32 = jnp.float32
BF16 = jnp.bfloat16
I32 = jnp.int32
U32 = jnp.uint32

D_MODEL = 1024
CHUNK = 64
N_META = 16
MLA_HEADS = 8
MLA_V = 64
MLA_NOPE = 64
MLA_ROPE = 32
MLA_QK = MLA_NOPE + MLA_ROPE
MLA_Q_LORA = 256
MLA_KV_LORA = 128
MLA_SCALE = MLA_QK ** -0.5
ROPE_BASE = 10000.0
M_HEADS = 4
M_DV = 128
M_DK = 64
N_GROUPS = 4
E_PER_GROUP = 8
N_EXPERTS = 32
EXPERT_FF = 512
EPS = 1e-6

LANES = 128
HEAD_PAD = 128
PROJ_PAD = 2048
META_PAD = 128
NEG_BIG = -1e30
VMEM_LIMIT = 56 * 1024 * 1024

_O_CQ, _O_CKV, _O_MQ, _O_MK, _O_MV, _O_MO, _O_MISC = 0, 256, 384, 640, 896, 1408, 1920
_L_IG, _L_FG = 64, 68


def _cparams(sem):
    return pltpu.CompilerParams(dimension_semantics=sem, vmem_limit_bytes=VMEM_LIMIT)


def _dot(a, b):
    return jnp.dot(a, b, preferred_element_type=F32)


def _dot_nt(a, b):
    return lax.dot_general(a, b, (((1,), (1,)), ((), ())), preferred_element_type=F32)


def _split3(x):
    x1 = x.astype(BF16)
    r1 = x - x1.astype(F32)
    x2 = r1.astype(BF16)
    x3 = (r1 - x2.astype(F32)).astype(BF16)
    return x1, x2, x3


def _proj_kernel(x_ref, tab_ref, w_ref, g_ref, gcq_ref, wq_ref, gckv_ref, gq_ref, bias_ref,
                 q_ref, ckv_ref, misc_ref, mq_ref, mk_ref, mv_ref, og_ref):
    x = x_ref[...]
    xn = x * lax.rsqrt(jnp.mean(x * x, axis=-1, keepdims=True) + EPS) * g_ref[...]
    xb = xn.astype(BF16)
    tab = tab_ref[...]
    lane = lax.broadcasted_iota(I32, tab.shape, 1)

    def proj(off, width):
        return _dot(xb, w_ref[:, off:off + width])

    cq = proj(_O_CQ, MLA_Q_LORA)
    cqn = cq * lax.rsqrt(jnp.mean(cq * cq, axis=-1, keepdims=True) + EPS) * gcq_ref[...]
    cqb = cqn.astype(BF16)
    qw = MLA_HEADS * HEAD_PAD
    gq = gq_ref[...]
    tab_a = jnp.where(lane < MLA_NOPE, 1.0, jnp.where(lane < MLA_QK, tab, 0.0))
    tab_b = jnp.where((lane >= MLA_NOPE) & (lane < MLA_QK), pltpu.roll(tab, LANES - MLA_ROPE, 1), 0.0)
    slab = 2 * HEAD_PAD
    pair_ones = (lax.broadcasted_iota(I32, (slab, slab), 0) // HEAD_PAD
                 == lax.broadcasted_iota(I32, (slab, slab), 1) // HEAD_PAD).astype(BF16)
    for p in range(MLA_HEADS // 2):
        za = _dot(cqb, wq_ref[:, p * slab:(p + 1) * slab])
        zb = _dot(cqb, wq_ref[:, qw + p * slab:qw + (p + 1) * slab])
        ms = _dot((za * za).astype(BF16), pair_ones) * (1.0 / MLA_QK)
        for e in range(2):
            cols = slice(e * HEAD_PAD, (e + 1) * HEAD_PAD)
            qh = za[:, cols] * tab_a + zb[:, cols] * tab_b
            q_ref[2 * p + e] = (qh * lax.rsqrt(ms[:, cols] + EPS) * gq).astype(BF16)

    ckv = proj(_O_CKV, MLA_KV_LORA)
    ckv_ref[...] = ckv * lax.rsqrt(jnp.mean(ckv * ckv, axis=-1, keepdims=True) + EPS) * gckv_ref[...]

    zm = proj(_O_MISC, LANES)
    y = zm * tab
    rot = y + pltpu.roll(y, LANES - MLA_ROPE, 1)
    gate = zm + bias_ref[...]
    logf = jnp.minimum(gate, 0.0) - jnp.log1p(jnp.exp(-jnp.abs(gate)))
    misc = jnp.where(lane < MLA_ROPE, rot,
                     jnp.where((lane >= _L_IG) & (lane < _L_FG), gate,
                               jnp.where((lane >= _L_FG) & (lane < _L_FG + M_HEADS), logf, 0.0)))
    misc_ref[...] = misc

    mq_ref[...] = proj(_O_MQ, M_HEADS * M_DK).astype(BF16)
    mk_ref[...] = (proj(_O_MK, M_HEADS * M_DK) * (M_DK ** -0.5)).astype(BF16)
    mv_ref[...] = proj(_O_MV, M_HEADS * M_DV).astype(BF16)
    og_ref[...] = jax.nn.sigmoid(proj(_O_MO, M_HEADS * M_DV)).astype(BF16)


def _project(x2d, tab, pw, tm):
    t = x2d.shape[0]
    nt = t // tm
    ntab = tab.shape[0] // tm
    row = lambda i: (i, 0)
    full = lambda i: (0, 0)
    return pl.pallas_call(
        _proj_kernel,
        grid=(nt,),
        in_specs=[
            pl.BlockSpec((tm, D_MODEL), row),
            pl.BlockSpec((tm, LANES), lambda i: (i % ntab, 0)),
            pl.BlockSpec((D_MODEL, PROJ_PAD), full),
            pl.BlockSpec((1, D_MODEL), full),
            pl.BlockSpec((1, MLA_Q_LORA), full),
            pl.BlockSpec((MLA_Q_LORA, 2 * MLA_HEADS * HEAD_PAD), full),
            pl.BlockSpec((1, MLA_KV_LORA), full),
            pl.BlockSpec((1, HEAD_PAD), full),
            pl.BlockSpec((1, LANES), full),
        ],
        out_specs=[
            pl.BlockSpec((MLA_HEADS, tm, HEAD_PAD), lambda i: (0, i, 0)),
            pl.BlockSpec((tm, MLA_KV_LORA), row),
            pl.BlockSpec((tm, LANES), row),
            pl.BlockSpec((tm, M_HEADS * M_DK), row),
            pl.BlockSpec((tm, M_HEADS * M_DK), row),
            pl.BlockSpec((tm, M_HEADS * M_DV), row),
            pl.BlockSpec((tm, M_HEADS * M_DV), row),
        ],
        out_shape=[
            jax.ShapeDtypeStruct((MLA_HEADS, t, HEAD_PAD), BF16),
            jax.ShapeDtypeStruct((t, MLA_KV_LORA), F32),
            jax.ShapeDtypeStruct((t, LANES), F32),
            jax.ShapeDtypeStruct((t, M_HEADS * M_DK), BF16),
            jax.ShapeDtypeStruct((t, M_HEADS * M_DK), BF16),
            jax.ShapeDtypeStruct((t, M_HEADS * M_DV), BF16),
            jax.ShapeDtypeStruct((t, M_HEADS * M_DV), BF16),
        ],
        compiler_params=_cparams(("arbitrary",)),
        name="projection",
    )(x2d, tab, pw["w_in"], pw["g_attn"], pw["g_cq"], pw["w_q"], pw["g_ckv"], pw["g_q"], pw["gate_bias"])


SAFE_BOUND = 40.0
LOG2E = 1.4426950408889634


def _attn_kernel(sb_ref, q_ref, ckv_ref, misc_ref, ckvt_ref, misct_ref, wk_ref, wv_ref, gk_ref, o_ref,
                 kt_scr, v_scr, acc_scr, p_scr, *, lf, tq, tk, bt, causal, n_tail):
    i = pl.program_id(1)
    gk = gk_ref[...]

    def build(ckv_rows, misc_rows, dst):
        n = ckv_rows.shape[0]
        cb = ckv_rows.astype(BF16)
        kt_all = _dot_nt(wk_ref[...], cb)
        kr_t = jnp.concatenate([jnp.zeros((MLA_NOPE, n), F32), misc_rows.T[0:MLA_ROPE, :],
                                jnp.zeros((HEAD_PAD - MLA_QK, n), F32)], axis=0)
        v_all = _dot(cb, wv_ref[...])
        onecol = (lax.broadcasted_iota(I32, (n, LANES), 1) == MLA_V).astype(F32)
        gkc = jnp.tile(gk, (1, n // LANES))
        for h in range(MLA_HEADS):
            kk = kt_all[h * HEAD_PAD:(h + 1) * HEAD_PAD, :] + kr_t
            ms = jnp.sum(kk * kk, axis=0, keepdims=True) * (1.0 / MLA_QK)
            kt_scr[h, :, pl.ds(dst, n)] = (kk * lax.rsqrt(ms + EPS) * gkc).astype(BF16)
            v_scr[h, pl.ds(dst, n), :] = (v_all[:, h * LANES:(h + 1) * LANES] + onecol).astype(BF16)

    @pl.when(i == 0)
    def _():
        def body(r, carry):
            r0 = pl.multiple_of(r * bt, bt)
            build(ckv_ref[pl.ds(r0, bt), :], misc_ref[pl.ds(r0, bt), :], r0)
            return carry

        lax.fori_loop(0, lf // bt, body, 0)
        build(ckvt_ref[0], misct_ref[0], lf)

    ndiag = tq // tk if causal else 0
    nfull = i * ndiag if causal else lf // tk
    tail_mask = lax.broadcasted_iota(I32, (tq, META_PAD), 1) < n_tail

    def kt_tile(h, j):
        return kt_scr[h, :, pl.ds(pl.multiple_of(j * tk, tk), tk)]

    def v_tile(h, j):
        return v_scr[h, pl.ds(pl.multiple_of(j * tk, tk), tk), :]

    def chunk_ids(lo):
        rows = lax.broadcasted_iota(I32, (tq - lo, tk), 0) // CHUNK
        cols = lax.broadcasted_iota(I32, (tq - lo, tk), 1) // CHUNK
        return rows, cols

    def finish(h):
        acc = acc_scr[h]
        o_ref[:, h * MLA_V:(h + 1) * MLA_V] = (acc[:, :MLA_V] / acc[:, MLA_V:MLA_V + 1]).astype(BF16)

    safe = sb_ref[0] <= SAFE_BOUND * LOG2E

    @pl.when(safe)
    def _():
        bound = sb_ref[0]

        def scores(h, lo, kt):
            return _dot(q_ref[h, lo:, :], kt) * (MLA_SCALE * LOG2E) - bound

        def probs(s, mask):
            if mask is not None:
                s = jnp.where(mask, s, -jnp.inf)
            return jnp.exp2(s).astype(BF16)

        def pipe_step(j, lo, lo_prev, mask):
            for h in range(MLA_HEADS):
                acc_scr[h, lo_prev:, :] += _dot(p_scr[(j - 1) % 2, h, lo_prev:, :], v_tile(h, j - 1))
            for h in range(MLA_HEADS):
                p_scr[j % 2, h, lo:, :] = probs(scores(h, lo, kt_tile(h, j)), mask)

        for h in range(MLA_HEADS):
            p_tail = probs(scores(h, 0, kt_scr[h, :, lf:lf + META_PAD]), tail_mask)
            acc_scr[h] = _dot(p_tail, v_scr[h, lf:lf + META_PAD, :])
        rows0, cols0 = chunk_ids(0)
        if causal:
            first_mask = cols0 <= rows0 + jnp.where(nfull > 0, tk, 0)
        else:
            first_mask = None
        for h in range(MLA_HEADS):
            p_scr[0, h] = probs(scores(h, 0, kt_tile(h, 0)), first_mask)

        def body(j, carry):
            pipe_step(j, 0, 0, None)
            return carry

        lax.fori_loop(1, nfull, body, 0)
        if causal:
            @pl.when(nfull >= 1)
            def _():
                pipe_step(nfull, 0, 0, cols0 <= rows0)
            for d in range(1, ndiag):
                rows_d, cols_d = chunk_ids(d * tk)
                pipe_step(nfull + d, d * tk, (d - 1) * tk, cols_d <= rows_d)
            last, lo_last = nfull + ndiag - 1, (ndiag - 1) * tk
        else:
            last, lo_last = nfull - 1, 0
        for h in range(MLA_HEADS):
            acc_scr[h, lo_last:, :] += _dot(p_scr[last % 2, h, lo_last:, :], v_tile(h, last))
            finish(h)

    @pl.when(jnp.logical_not(safe))
    def _():
        rows0, cols0 = chunk_ids(0)
        for h in range(MLA_HEADS):
            qh = q_ref[h]

            def step(carry, kt, vrows, mask):
                m, acc = carry
                s = _dot(qh, kt) * MLA_SCALE
                if mask is not None:
                    s = jnp.where(mask, s, -jnp.inf)
                m_new = jnp.maximum(m, jnp.max(s, axis=-1, keepdims=True))
                acc = jnp.exp(m - m_new) * acc + _dot(jnp.exp(s - m_new).astype(BF16), vrows)
                return m_new, acc

            def body(j, c):
                mask = (cols0 + j * (tk // CHUNK) <= rows0 + i * (tq // CHUNK)) if causal else None
                return step(c, kt_tile(h, j), v_tile(h, j), mask)

            carry = (jnp.full((tq, 1), -jnp.inf, F32), jnp.zeros((tq, LANES), F32))
            carry = step(carry, kt_scr[h, :, lf:lf + META_PAD], v_scr[h, lf:lf + META_PAD, :], tail_mask)
            carry = lax.fori_loop(0, nfull + ndiag, body, carry)
            acc_scr[h] = carry[1]
            finish(h)


def _attention(q, ckv_f, misc_f, ckv_t, misc_t, pw, *, nb, lq, lf, tq, tk, causal, n_tail):
    nq = lq // tq
    bt = min(512, lf)
    assert lf % bt == 0 and lf % tk == 0 and lq % tq == 0 and tk % CHUNK == 0 and (not causal or tq % tk == 0)
    kern = functools.partial(_attn_kernel, lf=lf, tq=tq, tk=tk, bt=bt, causal=causal, n_tail=n_tail)
    full2 = lambda b, i, sb: (0, 0)
    full3 = lambda b, i, sb: (0, 0, 0)
    tail = full3 if ckv_t.shape[0] == 1 else (lambda b, i, sb: (b, 0, 0))
    lk = lf + META_PAD
    return pl.pallas_call(
        kern,
        grid_spec=pltpu.PrefetchScalarGridSpec(
            num_scalar_prefetch=1,
            grid=(nb, nq),
            in_specs=[
                pl.BlockSpec((MLA_HEADS, tq, HEAD_PAD), lambda b, i, sb: (0, b * nq + i, 0)),
                pl.BlockSpec((lf, MLA_KV_LORA), lambda b, i, sb: (b, 0)),
                pl.BlockSpec((lf, LANES), lambda b, i, sb: (b, 0)),
                pl.BlockSpec((1, META_PAD, MLA_KV_LORA), tail),
                pl.BlockSpec((1, META_PAD, LANES), tail),
                pl.BlockSpec((MLA_HEADS * HEAD_PAD, MLA_KV_LORA), full2),
                pl.BlockSpec((MLA_KV_LORA, MLA_HEADS * LANES), full2),
                pl.BlockSpec((HEAD_PAD, LANES), full2),
            ],
            out_specs=pl.BlockSpec((tq, MLA_HEADS * MLA_V), lambda b, i, sb: (b * nq + i, 0)),
            scratch_shapes=[
                pltpu.VMEM((MLA_HEADS, HEAD_PAD, lk), BF16),
                pltpu.VMEM((MLA_HEADS, lk, LANES), BF16),
                pltpu.VMEM((MLA_HEADS, tq, LANES), F32),
                pltpu.VMEM((2, MLA_HEADS, tq, tk), BF16),
            ],
        ),
        out_shape=jax.ShapeDtypeStruct((nb * lq, MLA_HEADS * MLA_V), BF16),
        compiler_params=_cparams(("arbitrary", "arbitrary")),
        name="attention",
    )(pw["score_bound"], q, ckv_f, misc_f, ckv_t, misc_t, pw["w_k"], pw["w_v"], pw["g_k"])


S_W = 2 * M_DV
S_SHAPE = (M_HEADS // 2, 2 * M_DK, S_W)


MLSTM_CHUNK = LANES


def _mlstm_kernel(q_ref, k_ref, v_ref, og_ref, misc_ref, s0_ref, m0_ref, gmh_ref,
                  h_ref, sout_ref, mout_ref, s_scr, m_scr, *, n_valid):
    c = pl.program_id(1)
    nc = pl.num_programs(1)
    ns = q_ref.shape[1]
    lb = q_ref.shape[2]
    lc = MLSTM_CHUNK

    @pl.when(c == 0)
    def _():
        for sl in range(ns):
            s_scr[sl] = s0_ref[0, min(sl, s0_ref.shape[1] - 1)]
            m_scr[sl] = m0_ref[0, min(sl, m0_ref.shape[1] - 1)]

    def rows(x):
        if lb == lc:
            return x
        return jnp.concatenate([x, jnp.zeros((lc - lb,) + x.shape[1:], x.dtype)], axis=0)

    rr = lax.broadcasted_iota(I32, (lc, lc), 0)
    cc = lax.broadcasted_iota(I32, (lc, lc), 1)
    causal = cc <= rr
    tri = causal.astype(BF16)
    trit = (rr <= cc).astype(BF16)
    ones_blk = jnp.ones((lc, M_DV), BF16)
    low_half = lax.broadcasted_iota(I32, (lc, LANES), 1) < M_DK
    limit = lb if n_valid is None else n_valid - c * lb

    def gates(sl):
        g = rows(misc_ref[0, sl])
        row = lax.broadcasted_iota(I32, g.shape, 0)
        lane = lax.broadcasted_iota(I32, g.shape, 1)
        g = jnp.where(row < limit, g, jnp.where((lane >= _L_IG) & (lane < _L_FG), NEG_BIG, 0.0))
        gt8 = g.T[_L_IG:_L_IG + 2 * M_HEADS, :]
        g1, g2, g3 = _split3(g)
        b_cols = _dot(tri, g1) + _dot(tri, g2) + _dot(tri, g3)
        t1, t2, t3 = _split3(gt8)
        b_rows = (_dot(t1, trit) + _dot(t2, trit) + _dot(t3, trit))[M_HEADS:2 * M_HEADS, :]
        a = gt8[0:M_HEADS, :] - b_rows
        m_prev = m_scr[sl, 0:M_HEADS, :]
        u_b = [-jnp.maximum(m_prev[h:h + 1, :], jnp.broadcast_to(
            jnp.max(jnp.where(causal, a[h:h + 1, :], -jnp.inf), axis=-1, keepdims=True), (lc, LANES)))
            for h in range(M_HEADS)]
        negm_b = [u_b[h] - jnp.broadcast_to(b_cols[:, _L_FG + h:_L_FG + h + 1], (lc, LANES)) for h in range(M_HEADS)]
        u_last = jnp.concatenate([u_b[h][lc - 1:lc, 0:1] for h in range(M_HEADS)], axis=0)
        return dict(a=a, m_prev=m_prev, u_b=u_b, negm_b=negm_b,
                    m_new=jnp.broadcast_to(b_rows[:, lc - 1:lc] - u_last, (M_HEADS, LANES)),
                    w_state=jnp.exp(a + u_last), decay=jnp.exp(m_prev[:, 0:1] + u_last))

    gt_ = [gates(sl) for sl in range(ns)]
    q = [rows(q_ref[0, sl]) for sl in range(ns)]
    k = [rows(k_ref[0, sl]) for sl in range(ns)]
    v = [rows(v_ref[0, sl]) for sl in range(ns)]
    kt = [x.astype(F32).T for x in k]

    units = [(sl, h) for sl in range(ns) for h in range(M_HEADS)]
    mine = {u: (jnp.logical_not(low_half) if u[1] % 2 else low_half) for u in units}
    pair = lambda x, u: x[u[0]][:, (u[1] // 2) * LANES:(u[1] // 2 + 1) * LANES]
    kh = {u: jnp.where(mine[u], pair(k, u), jnp.zeros_like(pair(k, u))) for u in units}
    qh = {u: jnp.where(mine[u], pair(q, u), jnp.zeros_like(pair(q, u))) for u in units}
    vext = {u: jnp.concatenate([v[u[0]][:, u[1] * M_DV:(u[1] + 1) * M_DV], ones_blk], axis=1) for u in units}
    s_old = {(sl, j): s_scr[sl, j] for sl in range(ns) for j in range(M_HEADS // 2)}
    s_bf = {key: val.astype(BF16) for key, val in s_old.items()}

    qk = {u: _dot_nt(pair(q, u), kh[u]) for u in units}
    qs = {u: _dot(qh[u], s_bf[(u[0], u[1] // 2)]) for u in units}
    ktw = {u: (kt[u[0]][u[1] * M_DK:(u[1] + 1) * M_DK, :] * gt_[u[0]]["w_state"][u[1]:u[1] + 1, :]).astype(BF16)
           for u in units}
    upd = {u: _dot(ktw[u], vext[u]) for u in units}
    w = {u: (jnp.where(causal, jnp.exp(gt_[u[0]]["a"][u[1]:u[1] + 1, :] + gt_[u[0]]["u_b"][u[1]]), 0.0)
             * qk[u]).astype(BF16) for u in units}
    inter = {u: jnp.exp(gt_[u[0]]["m_prev"][u[1]:u[1] + 1, :] + gt_[u[0]]["u_b"][u[1]]) for u in units}
    r = {u: _dot(w[u], vext[u]) + jnp.tile(inter[u], (1, 2)) * qs[u] for u in units}
    num = {u: r[u][:, :M_DV] for u in units}
    sq = {u: num[u] * num[u] for u in units}
    sq_hi = {u: sq[u].astype(BF16) for u in units}
    sq_lo = {u: (sq[u] - sq_hi[u].astype(F32)).astype(BF16) for u in units}
    msn = {u: (_dot(sq_hi[u], ones_blk) + _dot(sq_lo[u], ones_blk)) * (1.0 / M_DV) for u in units}
    for u in units:
        sl, h = u
        rden = 1.0 / jnp.maximum(jnp.abs(r[u][:, M_DV:]), jnp.exp(gt_[sl]["negm_b"][h]))
        scale = rden * lax.rsqrt(rden * rden * msn[u] + EPS)
        hn = num[u] * scale * gmh_ref[h:h + 1, :]
        out = (hn[:lb] * og_ref[0, sl, :, h * M_DV:(h + 1) * M_DV].astype(F32)).astype(BF16)
        h_ref[0, sl, :, h * M_DV:(h + 1) * M_DV] = out
    for sl in range(ns):
        for j in range(M_HEADS // 2):
            new = [gt_[sl]["decay"][h:h + 1, :] * s_old[(sl, j)][(h % 2) * M_DK:(h % 2 + 1) * M_DK, :] + upd[(sl, h)]
                   for h in (2 * j, 2 * j + 1)]
            s_scr[sl, j] = jnp.concatenate(new, axis=0)
        m_scr[sl, 0:M_HEADS, :] = gt_[sl]["m_new"]

    @pl.when(c == nc - 1)
    def _():
        sout_ref[0] = s_scr[...]
        mout_ref[0] = m_scr[...]


def _mlstm(mq, mk, mv, og, misc, s0, m0, gmh, *, nb, l, n_valid=None):
    lc = min(l, MLSTM_CHUNK)
    assert l % lc == 0 and lc % 16 == 0
    nc = l // lc
    ns = next(n for n in (4, 2, 1) if nb % n == 0)
    ng = nb // ns
    if s0.shape[0] == 1:
        s0, m0 = s0[:, None], m0[:, None]
        st = lambda g, c: (0, 0, 0, 0, 0)
        mst = lambda g, c: (0, 0, 0, 0)
    else:
        s0, m0 = s0.reshape((ng, ns) + S_SHAPE), m0.reshape(ng, ns, 8, LANES)
        st = lambda g, c: (g, 0, 0, 0, 0)
        mst = lambda g, c: (g, 0, 0, 0)
    blk = lambda g, c: (g, 0, c, 0)
    split = lambda x: x.reshape(ng, ns, l, x.shape[-1])
    kern = functools.partial(_mlstm_kernel, n_valid=n_valid)
    h, s, m = pl.pallas_call(
        kern,
        grid=(ng, nc),
        in_specs=[
            pl.BlockSpec((1, ns, lc, M_HEADS * M_DK), blk),
            pl.BlockSpec((1, ns, lc, M_HEADS * M_DK), blk),
            pl.BlockSpec((1, ns, lc, M_HEADS * M_DV), blk),
            pl.BlockSpec((1, ns, lc, M_HEADS * M_DV), blk),
            pl.BlockSpec((1, ns, lc, LANES), blk),
            pl.BlockSpec((1, s0.shape[1]) + S_SHAPE, st),
            pl.BlockSpec((1, m0.shape[1], 8, LANES), mst),
            pl.BlockSpec((M_HEADS, M_DV), lambda g, c: (0, 0)),
        ],
        out_specs=[
            pl.BlockSpec((1, ns, lc, M_HEADS * M_DV), blk),
            pl.BlockSpec((1, ns) + S_SHAPE, lambda g, c: (g, 0, 0, 0, 0)),
            pl.BlockSpec((1, ns, 8, LANES), lambda g, c: (g, 0, 0, 0)),
        ],
        out_shape=[
            jax.ShapeDtypeStruct((ng, ns, l, M_HEADS * M_DV), BF16),
            jax.ShapeDtypeStruct((ng, ns) + S_SHAPE, F32),
            jax.ShapeDtypeStruct((ng, ns, 8, LANES), F32),
        ],
        scratch_shapes=[
            pltpu.VMEM((ns,) + S_SHAPE, F32),
            pltpu.VMEM((ns, 8, LANES), F32),
        ],
        compiler_params=_cparams(("arbitrary", "arbitrary")),
        name="mlstm",
    )(split(mq), split(mk), split(mv), split(og), split(misc), s0, m0, gmh)
    return h.reshape(nb * l, M_HEADS * M_DV), s.reshape((nb,) + S_SHAPE), m.reshape(nb, 8, LANES)


R_ROWS = 40


def _route_kernel(x_ref, att_ref, hm_ref, wo_ref, g_ref, wr_ref, br_ref,
                  x1_ref, hp_ref, ri_ref, rg_ref, cnt_ref, cnt_scr):
    i = pl.program_id(0)
    tm = x_ref.shape[0]

    @pl.when(i == 0)
    def _():
        cnt_scr[...] = jnp.zeros_like(cnt_scr)

    mix = jnp.concatenate([att_ref[...], hm_ref[...]], axis=1)
    x1 = x_ref[...] + _dot(mix, wo_ref[...])
    x1_ref[...] = x1
    hn = x1 * lax.rsqrt(jnp.mean(x1 * x1, axis=-1, keepdims=True) + EPS) * g_ref[...]
    hb = hn.astype(BF16)
    half = D_MODEL // 2
    hi = lax.bitcast_convert_type(hb[:, :half].astype(F32), U32)
    lo = lax.bitcast_convert_type(hb[:, half:].astype(F32), U32)
    hp_ref[...] = (hi & jnp.uint32(0xFFFF0000)) | (lo >> 16)

    logits = _dot_nt(wr_ref[...], hb) + br_ref[:, 0:1]
    e_log = logits[0:N_EXPERTS, :]
    g_log = logits[N_EXPERTS:N_EXPERTS + N_GROUPS, :]
    gmax = jnp.max(g_log, axis=0, keepdims=True)
    gsum = jnp.sum(jnp.exp(g_log - gmax), axis=0, keepdims=True)
    gi = lax.broadcasted_iota(I32, g_log.shape, 0)
    g_idx = jnp.min(jnp.where(g_log == gmax, gi, N_GROUPS), axis=0, keepdims=True)
    e_sel = jnp.zeros((E_PER_GROUP, tm), F32)
    for gg in range(N_GROUPS):
        e_sel = jnp.where(g_idx == gg, e_log[gg * E_PER_GROUP:(gg + 1) * E_PER_GROUP, :], e_sel)
    ei = lax.broadcasted_iota(I32, e_sel.shape, 0)
    m1 = jnp.max(e_sel, axis=0, keepdims=True)
    i1 = jnp.min(jnp.where(e_sel == m1, ei, E_PER_GROUP), axis=0, keepdims=True)
    e2 = jnp.where(ei == i1, -jnp.inf, e_sel)
    m2 = jnp.max(e2, axis=0, keepdims=True)
    i2 = jnp.min(jnp.where(e2 == m2, ei, E_PER_GROUP), axis=0, keepdims=True)
    ex = jnp.exp(m2 - m1)
    gp = 1.0 / gsum
    p1 = 1.0 / (1.0 + ex)
    gate1 = gp * p1
    gate2 = gp * (ex * p1)
    id1 = g_idx * E_PER_GROUP + i1
    id2 = g_idx * E_PER_GROUP + i2

    xi = lax.broadcasted_iota(I32, (N_EXPERTS, tm), 0)
    oh1 = xi == id1
    oh2 = xi == id2
    e_cnt = (oh1 | oh2).astype(F32)
    rr = lax.broadcasted_iota(I32, (tm, tm), 0)
    cc = lax.broadcasted_iota(I32, (tm, tm), 1)
    upper = (rr < cc).astype(BF16)
    pref = _dot(e_cnt.astype(BF16), upper) + cnt_scr[:, 0:1]
    rank1 = jnp.sum(jnp.where(oh1, pref, 0.0), axis=0, keepdims=True)
    rank2 = jnp.sum(jnp.where(oh2, pref, 0.0), axis=0, keepdims=True)
    cnt_new = cnt_scr[...] + jnp.sum(e_cnt, axis=1, keepdims=True)
    cnt_scr[...] = cnt_new
    cnt_ref[...] = cnt_new.astype(I32)

    zi = jnp.zeros((1, tm), I32)
    ri_ref[...] = jnp.concatenate([id1, id2, rank1.astype(I32), rank2.astype(I32), zi, zi, zi, zi], axis=0)
    zf = jnp.zeros((1, tm), F32)
    rg_ref[...] = jnp.concatenate([gate1, gate2, zf, zf, zf, zf, zf, zf], axis=0)


def _route(x2d, att, hm, pw, tm):
    t = x2d.shape[0]
    row = lambda i: (i, 0)
    col = lambda i: (0, i)
    full = lambda i: (0, 0)
    return pl.pallas_call(
        _route_kernel,
        grid=(t // tm,),
        in_specs=[
            pl.BlockSpec((tm, D_MODEL), row),
            pl.BlockSpec((tm, D_MODEL // 2), row),
            pl.BlockSpec((tm, D_MODEL // 2), row),
            pl.BlockSpec((D_MODEL, D_MODEL), full),
            pl.BlockSpec((1, D_MODEL), full),
            pl.BlockSpec((R_ROWS, D_MODEL), full),
            pl.BlockSpec((R_ROWS, LANES), full),
        ],
        out_specs=[
            pl.BlockSpec((tm, D_MODEL), row),
            pl.BlockSpec((tm, D_MODEL // 2), row),
            pl.BlockSpec((8, tm), col),
            pl.BlockSpec((8, tm), col),
            pl.BlockSpec((N_EXPERTS, LANES), full),
        ],
        out_shape=[
            jax.ShapeDtypeStruct((t, D_MODEL), F32),
            jax.ShapeDtypeStruct((t, D_MODEL // 2), U32),
            jax.ShapeDtypeStruct((8, t), I32),
            jax.ShapeDtypeStruct((8, t), F32),
            jax.ShapeDtypeStruct((N_EXPERTS, LANES), I32),
        ],
        scratch_shapes=[pltpu.VMEM((N_EXPERTS, LANES), F32)],
        compiler_params=_cparams(("arbitrary",)),
        name="route",
    )(x2d, att, hm, pw["w_out"], pw["g_ffn"], pw["w_r"], pw["b_r"])


SUBLANES = 8


N_ZERO_BLOCKS = 2 * N_EXPERTS


def _dispatch_kernel(zb_ref, dest_ref, hp_ref, xs_ref, zero_scr, sem, zsem, *, tm, bm):
    @pl.when(pl.program_id(0) == 0)
    def _():
        zero_scr[...] = jnp.zeros_like(zero_scr)

        def zcopy(k):
            return pltpu.make_async_copy(zero_scr, xs_ref.at[pl.ds(pl.multiple_of(zb_ref[k] * bm, bm), bm)], zsem)

        def zstart(k, carry):
            @pl.when(zb_ref[k] >= 0)
            def _():
                zcopy(k).start()
            return carry

        def zwait(k, carry):
            @pl.when(zb_ref[k] >= 0)
            def _():
                zcopy(k).wait()
            return carry

        lax.fori_loop(0, N_ZERO_BLOCKS, zstart, 0)
        lax.fori_loop(0, N_ZERO_BLOCKS, zwait, 0)

    def body(g, carry):
        for u in range(SUBLANES):
            t = g * SUBLANES + u
            src = hp_ref.at[g, pl.ds(u, 1), :]
            pltpu.make_async_copy(src, xs_ref.at[pl.ds(dest_ref[t], 1)], sem).start(priority=0)
            pltpu.make_async_copy(src, xs_ref.at[pl.ds(dest_ref[tm + t], 1)], sem).start(priority=1)
        return carry

    lax.fori_loop(0, tm // SUBLANES, body, 0)
    pltpu.make_async_copy(xs_ref.at[pl.ds(0, 2 * tm)], xs_ref.at[pl.ds(0, 2 * tm)], sem).wait()


def _tile_slots(dest, tm):
    t = dest.shape[1]
    return dest[0:2].reshape(2, t // tm, tm).transpose(1, 0, 2).reshape(2 * t)


def _dispatch(zero_blocks, dest, hp, n_slots, tm, bm):
    t = hp.shape[0]
    return pl.pallas_call(
        functools.partial(_dispatch_kernel, tm=tm, bm=bm),
        grid_spec=pltpu.PrefetchScalarGridSpec(
            num_scalar_prefetch=1,
            grid=(t // tm,),
            in_specs=[
                pl.BlockSpec((2 * tm,), lambda i, zb: (i,), memory_space=pltpu.SMEM),
                pl.BlockSpec((tm // SUBLANES, SUBLANES, D_MODEL // 2), lambda i, zb: (i, 0, 0)),
            ],
            out_specs=pl.BlockSpec(memory_space=pl.ANY),
            scratch_shapes=[
                pltpu.VMEM((bm, D_MODEL // 2), U32),
                pltpu.SemaphoreType.DMA,
                pltpu.SemaphoreType.DMA,
            ],
        ),
        out_shape=jax.ShapeDtypeStruct((n_slots, D_MODEL // 2), U32),
        compiler_params=_cparams(("arbitrary",)),
        name="dispatch",
    )(zero_blocks, _tile_slots(dest, tm), hp.reshape(t // SUBLANES, SUBLANES, D_MODEL // 2))


def _expert_kernel(be_ref, nu_ref, xs_ref, w1_ref, w3_ref, w2_ref, ys_ref, w1_scr, w3_scr, w2_scr):
    i = pl.program_id(0)
    new_expert = jnp.logical_or(i == 0, be_ref[i] != be_ref[jnp.maximum(i - 1, 0)])

    @pl.when(jnp.logical_and(new_expert, i < nu_ref[0]))
    def _():
        w1_scr[...] = w1_ref[0].astype(BF16)
        w3_scr[...] = w3_ref[0].astype(BF16)
        w2_scr[...] = w2_ref[0].astype(BF16)

    @pl.when(i < nu_ref[0])
    def _():
        xw = xs_ref[...]
        xa = lax.bitcast_convert_type(xw & jnp.uint32(0xFFFF0000), F32).astype(BF16)
        xb = lax.bitcast_convert_type(xw << 16, F32).astype(BF16)
        half = D_MODEL // 2
        h1 = _dot(xa, w1_scr[:half, :]) + _dot(xb, w1_scr[half:, :])
        h3 = _dot(xa, w3_scr[:half, :]) + _dot(xb, w3_scr[half:, :])
        a = (h1 * jax.nn.sigmoid(h1)) * h3
        ys_ref[...] = _dot(a.astype(BF16), w2_scr[...])

    @pl.when(pl.program_id(0) >= nu_ref[0])
    def _():
        ys_ref[...] = jnp.zeros_like(ys_ref)


def _experts(blk_e, n_used, xs, pw, bm):
    n_slots = xs.shape[0]
    nblk = n_slots // bm
    blk = lambda i, be, nu: (jnp.minimum(i, nu[0] - 1), 0)
    oblk = lambda i, be, nu: (i, 0)
    wsel = lambda i, be, nu: (be[i], 0, 0)
    return pl.pallas_call(
        _expert_kernel,
        grid_spec=pltpu.PrefetchScalarGridSpec(
            num_scalar_prefetch=2,
            grid=(nblk,),
            in_specs=[
                pl.BlockSpec((bm, D_MODEL // 2), blk),
                pl.BlockSpec((1, D_MODEL, EXPERT_FF), wsel),
                pl.BlockSpec((1, D_MODEL, EXPERT_FF), wsel),
                pl.BlockSpec((1, EXPERT_FF, D_MODEL), wsel),
            ],
            out_specs=pl.BlockSpec((bm, D_MODEL), oblk),
            scratch_shapes=[
                pltpu.VMEM((D_MODEL, EXPERT_FF), BF16),
                pltpu.VMEM((D_MODEL, EXPERT_FF), BF16),
                pltpu.VMEM((EXPERT_FF, D_MODEL), BF16),
            ],
        ),
        out_shape=jax.ShapeDtypeStruct((n_slots, D_MODEL), F32),
        compiler_params=_cparams(("arbitrary",)),
        name="experts",
    )(blk_e, n_used, xs, pw["w1"], pw["w3"], pw["w2"])


def _combine_kernel(dest_ref, dnext_ref, x1_ref, rg_ref, ys_ref, y_ref, r0_scr, r1_scr, sem, *, tm):
    i = pl.program_id(0)
    slot = i % 2

    def issue(slots_ref, s):
        def body(g, carry):
            for u in range(SUBLANES):
                t = g * SUBLANES + u
                pltpu.make_async_copy(ys_ref.at[pl.ds(slots_ref[t], 1)], r0_scr.at[s, g, pl.ds(u, 1), :],
                                      sem.at[s]).start(priority=0)
                pltpu.make_async_copy(ys_ref.at[pl.ds(slots_ref[tm + t], 1)], r1_scr.at[s, g, pl.ds(u, 1), :],
                                      sem.at[s]).start(priority=1)
            return carry

        lax.fori_loop(0, tm // SUBLANES, body, 0)

    @pl.when(i == 0)
    def _():
        issue(dest_ref, 0)

    @pl.when(i + 1 < pl.num_programs(0))
    def _():
        issue(dnext_ref, 1 - slot)

    gt = jnp.concatenate([rg_ref[...], jnp.zeros((LANES - 8, tm), F32)], axis=0).T
    slab = ys_ref.at[pl.ds(0, tm)]
    pltpu.make_async_copy(slab, slab, sem.at[slot]).wait()
    pltpu.make_async_copy(slab, slab, sem.at[slot]).wait()
    r0 = r0_scr[slot].reshape(tm, D_MODEL)
    r1 = r1_scr[slot].reshape(tm, D_MODEL)
    y_ref[...] = x1_ref[...] + gt[:, 0:1] * r0 + gt[:, 1:2] * r1


def _combine(dest, x1, rg, ys, tm):
    t = x1.shape[0]
    nt = t // tm
    row = lambda i: (i, 0)
    col = lambda i: (0, i)
    slots = _tile_slots(dest, tm)
    return pl.pallas_call(
        functools.partial(_combine_kernel, tm=tm),
        grid=(nt,),
        in_specs=[
            pl.BlockSpec((2 * tm,), lambda i: (i,), memory_space=pltpu.SMEM),
            pl.BlockSpec((2 * tm,), lambda i: (jnp.minimum(i + 1, nt - 1),), memory_space=pltpu.SMEM),
            pl.BlockSpec((tm, D_MODEL), row),
            pl.BlockSpec((8, tm), col),
            pl.BlockSpec(memory_space=pl.ANY),
        ],
        out_specs=pl.BlockSpec((tm, D_MODEL), row),
        out_shape=jax.ShapeDtypeStruct((t, D_MODEL), F32),
        scratch_shapes=[
            pltpu.VMEM((2, tm // SUBLANES, SUBLANES, D_MODEL), F32),
            pltpu.VMEM((2, tm // SUBLANES, SUBLANES, D_MODEL), F32),
            pltpu.SemaphoreType.DMA((2,)),
        ],
        compiler_params=_cparams(("arbitrary",)),
        name="combine",
    )(slots, slots, x1, rg, ys)


def _prep_weights(g_attn, w_in, g_cq, w_uq, g_ckv, w_ukv, g_q, g_k, b_igate, b_fgate, g_mh,
                  w_out, g_ffn, w_group, b_group, w_erouter, b_erouter, w1, w3, w2):
    def cols(a, b):
        return w_in[:, a:b]

    o_cq, o_ckv, o_kr = 0, 256, 384
    o_mq, o_mk, o_mv, o_mi, o_mf, o_mo = 416, 672, 928, 1440, 1444, 1448
    hr = MLA_ROPE // 2
    misc = jnp.concatenate([
        cols(o_kr, o_kr + MLA_ROPE), -cols(o_kr + hr, o_kr + MLA_ROPE), cols(o_kr, o_kr + hr),
        cols(o_mi, o_mi + M_HEADS), cols(o_mf, o_mf + M_HEADS),
        jnp.zeros((D_MODEL, LANES - 2 * MLA_ROPE - 2 * M_HEADS), F32)], axis=1)
    w_p = jnp.concatenate([
        cols(o_cq, o_cq + 256), cols(o_ckv, o_ckv + 128), cols(o_mq, o_mq + 256), cols(o_mk, o_mk + 256),
        cols(o_mv, o_mv + 512), cols(o_mo, o_mo + 512), misc], axis=1).astype(BF16)
    nope, r1, r2 = w_uq[..., :MLA_NOPE], w_uq[..., MLA_NOPE:MLA_NOPE + hr], w_uq[..., MLA_NOPE + hr:]
    zq = lambda w: jnp.zeros((MLA_Q_LORA, MLA_HEADS, w), F32)
    w_qa = jnp.concatenate([nope, r1, r2, zq(HEAD_PAD - MLA_QK)], axis=-1).reshape(MLA_Q_LORA, MLA_HEADS * HEAD_PAD)
    w_qb = jnp.concatenate([zq(MLA_NOPE), -r2, r1, zq(HEAD_PAD - MLA_QK)], axis=-1).reshape(MLA_Q_LORA,
                                                                                           MLA_HEADS * HEAD_PAD)
    w_q = jnp.concatenate([w_qa, w_qb], axis=1).astype(BF16)
    pad_g = jnp.zeros((HEAD_PAD - MLA_QK,), F32)
    w_k = jnp.concatenate([w_ukv[..., :MLA_NOPE], jnp.zeros((MLA_KV_LORA, MLA_HEADS, HEAD_PAD - MLA_NOPE), F32)],
                          axis=-1).reshape(MLA_KV_LORA, MLA_HEADS * HEAD_PAD).T.astype(BF16)
    w_v = jnp.concatenate([w_ukv[..., MLA_NOPE:], jnp.zeros((MLA_KV_LORA, MLA_HEADS, LANES - MLA_V), F32)],
                          axis=-1).reshape(MLA_KV_LORA, MLA_HEADS * LANES).astype(BF16)
    gate_bias = jnp.concatenate([jnp.zeros((_L_IG,), F32), b_igate, b_fgate,
                                 jnp.zeros((LANES - _L_FG - M_HEADS,), F32)])[None]
    w_r = jnp.concatenate([w_erouter.T, w_group.T, jnp.zeros((R_ROWS - N_EXPERTS - N_GROUPS, D_MODEL), F32)],
                          axis=0).astype(BF16)
    b_r = jnp.concatenate([b_erouter, b_group, jnp.zeros((R_ROWS - N_EXPERTS - N_GROUPS,), F32)])
    return {
        "w_in": w_p, "g_attn": g_attn[None], "g_cq": g_cq[None], "w_q": w_q, "g_ckv": g_ckv[None],
        "g_q": jnp.concatenate([g_q, pad_g])[None], "g_k": jnp.broadcast_to(jnp.concatenate([g_k, pad_g])[:, None], (HEAD_PAD, LANES)),
        "gate_bias": gate_bias, "w_k": w_k, "w_v": w_v, "g_mh": g_mh,
        "w_out": w_out.astype(BF16), "g_ffn": g_ffn[None], "w_r": w_r,
        "b_r": jnp.broadcast_to(b_r[:, None], (R_ROWS, LANES)),
        "w1": w1, "w3": w3, "w2": w2,
        "score_bound": (MLA_QK * MLA_SCALE * LOG2E * 1.01 * jnp.max(jnp.abs(g_q)) * jnp.max(jnp.abs(g_k))).reshape(1),
    }


def _rope_table(pos):
    half = MLA_ROPE // 2
    inv = ROPE_BASE ** (-np.arange(half, dtype=np.float64) / half)
    ang = np.asarray(pos, np.float64)[:, None] * inv[None, :]
    cos = np.cos(ang)
    sin = np.sin(ang)
    c2 = np.concatenate([cos, cos], axis=1)
    s2 = np.concatenate([sin, sin], axis=1)
    return jnp.asarray(np.concatenate([c2, s2, c2, s2], axis=1), F32)


def _pick(n, pref):
    return pref if n % pref == 0 else n


def _slots_kernel(ri_ref, ps_ref, d_ref):
    tm = ri_ref.shape[1]
    xi = lax.broadcasted_iota(I32, (N_EXPERTS, tm), 0)
    ps = ps_ref[:, 0:1]
    rows = []
    for k in range(2):
        start = jnp.sum(jnp.where(xi == ri_ref[k:k + 1, :], ps, 0), axis=0, keepdims=True)
        rows.append(start + ri_ref[2 + k:3 + k, :])
    d_ref[...] = jnp.concatenate(rows + [jnp.zeros((6, tm), I32)], axis=0)


def _slots(ri, pstart_b, tm):
    t = ri.shape[1]
    return pl.pallas_call(
        _slots_kernel,
        grid=(t // tm,),
        in_specs=[pl.BlockSpec((8, tm), lambda i: (0, i)), pl.BlockSpec((N_EXPERTS, LANES), lambda i: (0, 0))],
        out_specs=pl.BlockSpec((8, tm), lambda i: (0, i)),
        out_shape=jax.ShapeDtypeStruct((8, t), I32),
        compiler_params=_cparams(("arbitrary",)),
        name="slots",
    )(ri, pstart_b)


def _moe_layer(x2d, att, hm, pw, *, tm_route, bm, tm_disp, tm_comb):
    t = x2d.shape[0]
    x1, hp, ri, rg, cnt = _route(x2d, att, hm, pw, tm_route)
    counts = cnt[:, 0]
    padded = (counts + bm - 1) // bm * bm
    pend = jnp.cumsum(padded)
    pstart = pend - padded
    dest = _slots(ri, jnp.broadcast_to(pstart[:, None], (N_EXPERTS, LANES)), _pick(t, 2048))
    n_slots = (2 * t // bm + N_EXPERTS) * bm
    nblk = n_slots // bm
    n_used = (pend[-1] // bm).astype(I32)
    experts = jnp.arange(N_EXPERTS, dtype=I32)
    blk_first = jnp.arange(nblk, dtype=I32) * bm
    blk_e = jnp.sum((pend[None, :] <= blk_first[:, None]).astype(I32), axis=1)
    blk_e = jnp.minimum(blk_e, jnp.max(jnp.where(counts > 0, experts, 0)))
    zero_blocks = jnp.concatenate([jnp.where(counts > 0, pend // bm - 1, -1),
                                   jnp.where(n_used + experts < nblk, n_used + experts, -1)]).astype(I32)
    xs = _dispatch(zero_blocks, dest, hp, n_slots, tm_disp, bm)
    ys = _experts(blk_e, n_used[None], xs, pw, bm)
    return _combine(dest, x1, rg, ys, tm_comb)


def _state_pack(c, n):
    ct = jnp.swapaxes(c, -1, -2)
    s = jnp.concatenate([ct, jnp.broadcast_to(n[..., None], ct.shape[:-1] + (S_W - M_DV,))], axis=-1)
    return s.reshape((s.shape[0],) + S_SHAPE)


def _state_unpack(s, m):
    s = s.reshape(s.shape[0], M_HEADS, M_DK, S_W)
    return jnp.swapaxes(s[..., :M_DV], -1, -2), s[..., M_DV], m[:, :M_HEADS, 0]


def kernel(x_prompt, x_sample, cache_ckv, cache_krope, state_mlstm_c, state_mlstm_n, state_mlstm_m,
           meta_tokens, g_attn, w_in, g_cq, w_uq, g_ckv, w_ukv, g_q, g_k, b_igate, b_fgate, g_mh,
           w_out, g_ffn, w_group, b_group, w_erouter, b_erouter, w1, w3, w2):
    bp, seq = x_prompt.shape[:2]
    bs, dec = x_sample.shape[:2]
    past = cache_ckv.shape[2]
    layer = 0
    pw = _prep_weights(g_attn[layer], w_in[layer], g_cq[layer], w_uq[layer], g_ckv[layer], w_ukv[layer],
                       g_q[layer], g_k[layer], b_igate[layer], b_fgate[layer], g_mh[layer], w_out[layer],
                       g_ffn[layer], w_group[layer], b_group[layer], w_erouter[layer], b_erouter[layer],
                       w1[layer], w3[layer], w2[layer])

    xm = jnp.concatenate([meta_tokens, jnp.zeros((META_PAD - N_META, D_MODEL), F32)], axis=0)
    tab_m = _rope_table(np.arange(META_PAD) - N_META)
    _, ckv_m, misc_m, mq_m, mk_m, mv_m, og_m = _project(xm, tab_m, pw, META_PAD)
    zero_s = jnp.zeros((1,) + S_SHAPE, F32)
    zero_m = jnp.zeros((1, 8, LANES), F32)
    _, s_meta, m_meta = _mlstm(mq_m, mk_m, mv_m, og_m, misc_m, zero_s, zero_m, pw["g_mh"],
                               nb=1, l=META_PAD, n_valid=N_META)

    tp = bp * seq
    xp2 = x_prompt.reshape(tp, D_MODEL)
    tm_p = _pick(seq, 1024)
    q_p, ckv_p, misc_p, mq_p, mk_p, mv_p, og_p = _project(xp2, _rope_table(np.arange(seq)), pw, tm_p)
    tq = _pick(seq, 1024)
    att_p = _attention(q_p, ckv_p, misc_p, ckv_m[None], misc_m[None], pw, nb=bp, lq=seq, lf=seq, tq=tq, tk=256,
                       causal=True, n_tail=N_META)
    hm_p, s_p, m_p = _mlstm(mq_p, mk_p, mv_p, og_p, misc_p, s_meta, m_meta, pw["g_mh"],
                            nb=bp, l=seq)
    y_p = _moe_layer(xp2, att_p, hm_p, pw, tm_route=tm_p, bm=1024, tm_disp=_pick(seq, 512),
                     tm_comb=_pick(seq, 512))

    ts = bs * dec
    xs2 = x_sample.reshape(ts, D_MODEL)
    q_s, ckv_s, misc_s, mq_s, mk_s, mv_s, og_s = _project(xs2, _rope_table(past + np.arange(dec)), pw, dec)
    kr_cache = jnp.concatenate([cache_krope[layer], jnp.zeros((bs, past, LANES - MLA_ROPE), F32)], axis=-1)
    n_tail = dec + N_META
    assert n_tail <= META_PAD

    def tail_rows(own, meta):
        w = own.shape[-1]
        return jnp.concatenate([own.reshape(bs, dec, w), jnp.broadcast_to(meta[None, :N_META], (bs, N_META, w)),
                                jnp.zeros((bs, META_PAD - n_tail, w), F32)], axis=1)

    att_s = _attention(q_s, cache_ckv[layer].reshape(bs * past, MLA_KV_LORA), kr_cache.reshape(bs * past, LANES),
                       tail_rows(ckv_s, ckv_m), tail_rows(misc_s, misc_m), pw, nb=bs, lq=dec, lf=past, tq=dec,
                       tk=_pick(past, 256), causal=False, n_tail=n_tail)
    s0 = _state_pack(state_mlstm_c[layer], state_mlstm_n[layer])
    m0 = jnp.concatenate([jnp.broadcast_to(state_mlstm_m[layer][:, :, None], (bs, M_HEADS, LANES)),
                          jnp.zeros((bs, 8 - M_HEADS, LANES), F32)], axis=1)
    hm_s, s_s, m_s = _mlstm(mq_s, mk_s, mv_s, og_s, misc_s, s0, m0, pw["g_mh"], nb=bs, l=dec)
    y_s = _moe_layer(xs2, att_s, hm_s, pw, tm_route=_pick(ts, 512), bm=128, tm_disp=_pick(ts, 512),
                     tm_comb=_pick(ts, 512))

    m_ckv = ckv_m[:N_META]
    m_kr = misc_m[:N_META, :MLA_ROPE]
    new_ckv_p = jnp.concatenate([jnp.broadcast_to(m_ckv[None], (bp, N_META, MLA_KV_LORA)),
                                 ckv_p.reshape(bp, seq, MLA_KV_LORA)], axis=1)[None]
    new_kr_p = jnp.concatenate([jnp.broadcast_to(m_kr[None], (bp, N_META, MLA_ROPE)),
                                misc_p[:, :MLA_ROPE].reshape(bp, seq, MLA_ROPE)], axis=1)[None]
    c_p, n_p, mm_p = _state_unpack(s_p, m_p)
    c_s, n_s, mm_s = _state_unpack(s_s, m_s)
    return (y_p.reshape(bp, seq, D_MODEL), y_s.reshape(bs, dec, D_MODEL),
            new_ckv_p, new_kr_p, c_p[None], n_p[None], mm_p[None],
            ckv_s.reshape(bs, dec, MLA_KV_LORA)[None], misc_s[:, :MLA_ROPE].reshape(bs, dec, MLA_ROPE)[None],
            c_s[None], n_s[None], mm_s[None])
```

```python
import functools

import numpy as np
import jax
import jax.numpy as jnp
from jax import lax
from jax.experimental import pallas as pl
from jax.experimental.pallas import tpu as pltpu

F32 = jnp.float32
BF16 = jnp.bfloat16
I32 = jnp.int32
U32 = jnp.uint32

D_MODEL = 1024
CHUNK = 64
N_META = 16
MLA_HEADS = 8
MLA_V = 64
MLA_NOPE = 64
MLA_ROPE = 32
MLA_QK = MLA_NOPE + MLA_ROPE
MLA_Q_LORA = 256
MLA_KV_LORA = 128
MLA_SCALE = MLA_QK ** -0.5
ROPE_BASE = 10000.0
M_HEADS = 4
M_DV = 128
M_DK = 64
N_GROUPS = 4
E_PER_GROUP = 8
N_EXPERTS = 32
EXPERT_FF = 512
EPS = 1e-6

LANES = 128
HEAD_PAD = 128
PROJ_PAD = 2048
META_PAD = 128
NEG_BIG = -1e30
VMEM_LIMIT = 56 * 1024 * 1024

_O_CQ, _O_CKV, _O_MQ, _O_MK, _O_MV, _O_MO, _O_MISC = 0, 256, 384, 640, 896, 1408, 1920
_L_IG, _L_FG = 64, 68


def _cparams(sem):
    return pltpu.CompilerParams(dimension_semantics=sem, vmem_limit_bytes=VMEM_LIMIT)


def _dot(a, b):
    return jnp.dot(a, b, preferred_element_type=F32)


def _dot_nt(a, b):
    return lax.dot_general(a, b, (((1,), (1,)), ((), ())), preferred_element_type=F32)


def _split3(x):
    x1 = x.astype(BF16)
    r1 = x - x1.astype(F32)
    x2 = r1.astype(BF16)
    x3 = (r1 - x2.astype(F32)).astype(BF16)
    return x1, x2, x3


def _proj_kernel(x_ref, tab_ref, w_ref, g_ref, gcq_ref, wq_ref, gckv_ref, gq_ref, bias_ref,
                 q_ref, ckv_ref, misc_ref, mq_ref, mk_ref, mv_ref, og_ref):
    x = x_ref[...]
    xn = x * lax.rsqrt(jnp.mean(x * x, axis=-1, keepdims=True) + EPS) * g_ref[...]
    xb = xn.astype(BF16)
    tab = tab_ref[...]
    lane = lax.broadcasted_iota(I32, tab.shape, 1)

    def proj(off, width):
        return _dot(xb, w_ref[:, off:off + width])

    cq = proj(_O_CQ, MLA_Q_LORA)
    cqn = cq * lax.rsqrt(jnp.mean(cq * cq, axis=-1, keepdims=True) + EPS) * gcq_ref[...]
    cqb = cqn.astype(BF16)
    qw = MLA_HEADS * HEAD_PAD
    gq = gq_ref[...]
    tab_a = jnp.where(lane < MLA_NOPE, 1.0, jnp.where(lane < MLA_QK, tab, 0.0))
    tab_b = jnp.where((lane >= MLA_NOPE) & (lane < MLA_QK), pltpu.roll(tab, LANES - MLA_ROPE, 1), 0.0)
    slab = 2 * HEAD_PAD
    pair_ones = (lax.broadcasted_iota(I32, (slab, slab), 0) // HEAD_PAD
                 == lax.broadcasted_iota(I32, (slab, slab), 1) // HEAD_PAD).astype(BF16)
    for p in range(MLA_HEADS // 2):
        za = _dot(cqb, wq_ref[:, p * slab:(p + 1) * slab])
        zb = _dot(cqb, wq_ref[:, qw + p * slab:qw + (p + 1) * slab])
        ms = _dot((za * za).astype(BF16), pair_ones) * (1.0 / MLA_QK)
        for e in range(2):
            cols = slice(e * HEAD_PAD, (e + 1) * HEAD_PAD)
            qh = za[:, cols] * tab_a + zb[:, cols] * tab_b
            q_ref[2 * p + e] = (qh * lax.rsqrt(ms[:, cols] + EPS) * gq).astype(BF16)

    ckv = proj(_O_CKV, MLA_KV_LORA)
    ckv_ref[...] = ckv * lax.rsqrt(jnp.mean(ckv * ckv, axis=-1, keepdims=True) + EPS) * gckv_ref[...]

    zm = proj(_O_MISC, LANES)
    y = zm * tab
    rot = y + pltpu.roll(y, LANES - MLA_ROPE, 1)
    gate = zm + bias_ref[...]
    logf = jnp.minimum(gate, 0.0) - jnp.log1p(jnp.exp(-jnp.abs(gate)))
    misc = jnp.where(lane < MLA_ROPE, rot,
                     jnp.where((lane >= _L_IG) & (lane < _L_FG), gate,
                               jnp.where((lane >= _L_FG) & (lane < _L_FG + M_HEADS), logf, 0.0)))
    misc_ref[...] = misc

    mq_ref[...] = proj(_O_MQ, M_HEADS * M_DK).astype(BF16)
    mk_ref[...] = (proj(_O_MK, M_HEADS * M_DK) * (M_DK ** -0.5)).astype(BF16)
    mv_ref[...] = proj(_O_MV, M_HEADS * M_DV).astype(BF16)
    og_ref[...] = jax.nn.sigmoid(proj(_O_MO, M_HEADS * M_DV)).astype(BF16)


def _project(x2d, tab, pw, tm):
    t = x2d.shape[0]
    nt = t // tm
    ntab = tab.shape[0] // tm
    row = lambda i: (i, 0)
    full = lambda i: (0, 0)
    return pl.pallas_call(
        _proj_kernel,
        grid=(nt,),
        in_specs=[
            pl.BlockSpec((tm, D_MODEL), row),
            pl.BlockSpec((tm, LANES), lambda i: (i % ntab, 0)),
            pl.BlockSpec((D_MODEL, PROJ_PAD), full),
            pl.BlockSpec((1, D_MODEL), full),
            pl.BlockSpec((1, MLA_Q_LORA), full),
            pl.BlockSpec((MLA_Q_LORA, 2 * MLA_HEADS * HEAD_PAD), full),
            pl.BlockSpec((1, MLA_KV_LORA), full),
            pl.BlockSpec((1, HEAD_PAD), full),
            pl.BlockSpec((1, LANES), full),
        ],
        out_specs=[
            pl.BlockSpec((MLA_HEADS, tm, HEAD_PAD), lambda i: (0, i, 0)),
            pl.BlockSpec((tm, MLA_KV_LORA), row),
            pl.BlockSpec((tm, LANES), row),
            pl.BlockSpec((tm, M_HEADS * M_DK), row),
            pl.BlockSpec((tm, M_HEADS * M_DK), row),
            pl.BlockSpec((tm, M_HEADS * M_DV), row),
            pl.BlockSpec((tm, M_HEADS * M_DV), row),
        ],
        out_shape=[
            jax.ShapeDtypeStruct((MLA_HEADS, t, HEAD_PAD), BF16),
            jax.ShapeDtypeStruct((t, MLA_KV_LORA), F32),
            jax.ShapeDtypeStruct((t, LANES), F32),
            jax.ShapeDtypeStruct((t, M_HEADS * M_DK), BF16),
            jax.ShapeDtypeStruct((t, M_HEADS * M_DK), BF16),
            jax.ShapeDtypeStruct((t, M_HEADS * M_DV), BF16),
            jax.ShapeDtypeStruct((t, M_HEADS * M_DV), BF16),
        ],
        compiler_params=_cparams(("arbitrary",)),
        name="projection",
    )(x2d, tab, pw["w_in"], pw["g_attn"], pw["g_cq"], pw["w_q"], pw["g_ckv"], pw["g_q"], pw["gate_bias"])


SAFE_BOUND = 40.0
LOG2E = 1.4426950408889634


def _attn_kernel(sb_ref, q_ref, ckv_ref, misc_ref, ckvt_ref, misct_ref, wk_ref, wv_ref, gk_ref, o_ref,
                 kt_scr, v_scr, acc_scr, p_scr, *, lf, tq, tk, bt, causal, n_tail):
    i = pl.program_id(1)
    gk = gk_ref[...]

    def build(ckv_rows, misc_rows, dst):
        n = ckv_rows.shape[0]
        cb = ckv_rows.astype(BF16)
        kt_all = _dot_nt(wk_ref[...], cb)
        kr_t = jnp.concatenate([jnp.zeros((MLA_NOPE, n), F32), misc_rows.T[0:MLA_ROPE, :],
                                jnp.zeros((HEAD_PAD - MLA_QK, n), F32)], axis=0)
        v_all = _dot(cb, wv_ref[...])
        onecol = (lax.broadcasted_iota(I32, (n, LANES), 1) == MLA_V).astype(F32)
        gkc = jnp.tile(gk, (1, n // LANES))
        for h in range(MLA_HEADS):
            kk = kt_all[h * HEAD_PAD:(h + 1) * HEAD_PAD, :] + kr_t
            ms = jnp.sum(kk * kk, axis=0, keepdims=True) * (1.0 / MLA_QK)
            kt_scr[h, :, pl.ds(dst, n)] = (kk * lax.rsqrt(ms + EPS) * gkc).astype(BF16)
            v_scr[h, pl.ds(dst, n), :] = (v_all[:, h * LANES:(h + 1) * LANES] + onecol).astype(BF16)

    @pl.when(i == 0)
    def _():
        def body(r, carry):
            r0 = pl.multiple_of(r * bt, bt)
            build(ckv_ref[pl.ds(r0, bt), :], misc_ref[pl.ds(r0, bt), :], r0)
            return carry

        lax.fori_loop(0, lf // bt, body, 0)
        build(ckvt_ref[0], misct_ref[0], lf)

    ndiag = tq // tk if causal else 0
    nfull = i * ndiag if causal else lf // tk
    tail_mask = lax.broadcasted_iota(I32, (tq, META_PAD), 1) < n_tail

    def kt_tile(h, j):
        return kt_scr[h, :, pl.ds(pl.multiple_of(j * tk, tk), tk)]

    def v_tile(h, j):
        return v_scr[h, pl.ds(pl.multiple_of(j * tk, tk), tk), :]

    def chunk_ids(lo):
        rows = lax.broadcasted_iota(I32, (tq - lo, tk), 0) // CHUNK
        cols = lax.broadcasted_iota(I32, (tq - lo, tk), 1) // CHUNK
        return rows, cols

    def finish(h):
        acc = acc_scr[h]
        o_ref[:, h * MLA_V:(h + 1) * MLA_V] = (acc[:, :MLA_V] / acc[:, MLA_V:MLA_V + 1]).astype(BF16)

    safe = sb_ref[0] <= SAFE_BOUND * LOG2E

    @pl.when(safe)
    def _():
        bound = sb_ref[0]

        def scores(h, lo, kt):
            return _dot(q_ref[h, lo:, :], kt) * (MLA_SCALE * LOG2E) - bound

        def probs(s, mask):
            if mask is not None:
                s = jnp.where(mask, s, -jnp.inf)
            return jnp.exp2(s).astype(BF16)

        def pipe_step(j, lo, lo_prev, mask):
            for h in range(MLA_HEADS):
                acc_scr[h, lo_prev:, :] += _dot(p_scr[(j - 1) % 2, h, lo_prev:, :], v_tile(h, j - 1))
            for h in range(MLA_HEADS):
                p_scr[j % 2, h, lo:, :] = probs(scores(h, lo, kt_tile(h, j)), mask)

        for h in range(MLA_HEADS):
            p_tail = probs(scores(h, 0, kt_scr[h, :, lf:lf + META_PAD]), tail_mask)
            acc_scr[h] = _dot(p_tail, v_scr[h, lf:lf + META_PAD, :])
        rows0, cols0 = chunk_ids(0)
        if causal:
            first_mask = cols0 <= rows0 + jnp.where(nfull > 0, tk, 0)
        else:
            first_mask = None
        for h in range(MLA_HEADS):
            p_scr[0, h] = probs(scores(h, 0, kt_tile(h, 0)), first_mask)

        def body(j, carry):
            pipe_step(j, 0, 0, None)
            return carry

        lax.fori_loop(1, nfull, body, 0)
        if causal:
            @pl.when(nfull >= 1)
            def _():
                pipe_step(nfull, 0, 0, cols0 <= rows0)
            for d in range(1, ndiag):
                rows_d, cols_d = chunk_ids(d * tk)
                pipe_step(nfull + d, d * tk, (d - 1) * tk, cols_d <= rows_d)
            last, lo_last = nfull + ndiag - 1, (ndiag - 1) * tk
        else:
            last, lo_last = nfull - 1, 0
        for h in range(MLA_HEADS):
            acc_scr[h, lo_last:, :] += _dot(p_scr[last % 2, h, lo_last:, :], v_tile(h, last))
            finish(h)

    @pl.when(jnp.logical_not(safe))
    def _():
        rows0, cols0 = chunk_ids(0)
        for h in range(MLA_HEADS):
            qh = q_ref[h]

            def step(carry, kt, vrows, mask):
                m, acc = carry
                s = _dot(qh, kt) * MLA_SCALE
                if mask is not None:
                    s = jnp.where(mask, s, -jnp.inf)
                m_new = jnp.maximum(m, jnp.max(s, axis=-1, keepdims=True))
                acc = jnp.exp(m - m_new) * acc + _dot(jnp.exp(s - m_new).astype(BF16), vrows)
                return m_new, acc

            def body(j, c):
                mask = (cols0 + j * (tk // CHUNK) <= rows0 + i * (tq // CHUNK)) if causal else None
                return step(c, kt_tile(h, j), v_tile(h, j), mask)

            carry = (jnp.full((tq, 1), -jnp.inf, F32), jnp.zeros((tq, LANES), F32))
            carry = step(carry, kt_scr[h, :, lf:lf + META_PAD], v_scr[h, lf:lf + META_PAD, :], tail_mask)
            carry = lax.fori_loop(0, nfull + ndiag, body, carry)
            acc_scr[h] = carry[1]
            finish(h)


def _attention(q, ckv_f, misc_f, ckv_t, misc_t, pw, *, nb, lq, lf, tq, tk, causal, n_tail):
    nq = lq // tq
    bt = min(512, lf)
    assert lf % bt == 0 and lf % tk == 0 and lq % tq == 0 and tk % CHUNK == 0 and (not causal or tq % tk == 0)
    kern = functools.partial(_attn_kernel, lf=lf, tq=tq, tk=tk, bt=bt, causal=causal, n_tail=n_tail)
    full2 = lambda b, i, sb: (0, 0)
    full3 = lambda b, i, sb: (0, 0, 0)
    tail = full3 if ckv_t.shape[0] == 1 else (lambda b, i, sb: (b, 0, 0))
    lk = lf + META_PAD
    return pl.pallas_call(
        kern,
        grid_spec=pltpu.PrefetchScalarGridSpec(
            num_scalar_prefetch=1,
            grid=(nb, nq),
            in_specs=[
                pl.BlockSpec((MLA_HEADS, tq, HEAD_PAD), lambda b, i, sb: (0, b * nq + i, 0)),
                pl.BlockSpec((lf, MLA_KV_LORA), lambda b, i, sb: (b, 0)),
                pl.BlockSpec((lf, LANES), lambda b, i, sb: (b, 0)),
                pl.BlockSpec((1, META_PAD, MLA_KV_LORA), tail),
                pl.BlockSpec((1, META_PAD, LANES), tail),
                pl.BlockSpec((MLA_HEADS * HEAD_PAD, MLA_KV_LORA), full2),
                pl.BlockSpec((MLA_KV_LORA, MLA_HEADS * LANES), full2),
                pl.BlockSpec((HEAD_PAD, LANES), full2),
            ],
            out_specs=pl.BlockSpec((tq, MLA_HEADS * MLA_V), lambda b, i, sb: (b * nq + i, 0)),
            scratch_shapes=[
                pltpu.VMEM((MLA_HEADS, HEAD_PAD, lk), BF16),
                pltpu.VMEM((MLA_HEADS, lk, LANES), BF16),
                pltpu.VMEM((MLA_HEADS, tq, LANES), F32),
                pltpu.VMEM((2, MLA_HEADS, tq, tk), BF16),
            ],
        ),
        out_shape=jax.ShapeDtypeStruct((nb * lq, MLA_HEADS * MLA_V), BF16),
        compiler_params=_cparams(("arbitrary", "arbitrary")),
        name="attention",
    )(pw["score_bound"], q, ckv_f, misc_f, ckv_t, misc_t, pw["w_k"], pw["w_v"], pw["g_k"])


S_W = 2 * M_DV
S_SHAPE = (M_HEADS // 2, 2 * M_DK, S_W)


MLSTM_CHUNK = LANES


def _mlstm_kernel(q_ref, k_ref, v_ref, og_ref, misc_ref, s0_ref, m0_ref, gmh_ref,
                  h_ref, sout_ref, mout_ref, s_scr, m_scr, *, n_valid):
    c = pl.program_id(1)
    nc = pl.num_programs(1)
    ns = q_ref.shape[1]
    lb = q_ref.shape[2]
    lc = MLSTM_CHUNK

    @pl.when(c == 0)
    def _():
        for sl in range(ns):
            s_scr[sl] = s0_ref[0, min(sl, s0_ref.shape[1] - 1)]
            m_scr[sl] = m0_ref[0, min(sl, m0_ref.shape[1] - 1)]

    def rows(x):
        if lb == lc:
            return x
        return jnp.concatenate([x, jnp.zeros((lc - lb,) + x.shape[1:], x.dtype)], axis=0)

    rr = lax.broadcasted_iota(I32, (lc, lc), 0)
    cc = lax.broadcasted_iota(I32, (lc, lc), 1)
    causal = cc <= rr
    tri = causal.astype(BF16)
    trit = (rr <= cc).astype(BF16)
    ones_blk = jnp.ones((lc, M_DV), BF16)
    low_half = lax.broadcasted_iota(I32, (lc, LANES), 1) < M_DK
    limit = lb if n_valid is None else n_valid - c * lb

    def gates(sl):
        g = rows(misc_ref[0, sl])
        row = lax.broadcasted_iota(I32, g.shape, 0)
        lane = lax.broadcasted_iota(I32, g.shape, 1)
        g = jnp.where(row < limit, g, jnp.where((lane >= _L_IG) & (lane < _L_FG), NEG_BIG, 0.0))
        gt8 = g.T[_L_IG:_L_IG + 2 * M_HEADS, :]
        g1, g2, g3 = _split3(g)
        b_cols = _dot(tri, g1) + _dot(tri, g2) + _dot(tri, g3)
        t1, t2, t3 = _split3(gt8)
        b_rows = (_dot(t1, trit) + _dot(t2, trit) + _dot(t3, trit))[M_HEADS:2 * M_HEADS, :]
        a = gt8[0:M_HEADS, :] - b_rows
        m_prev = m_scr[sl, 0:M_HEADS, :]
        u_b = [-jnp.maximum(m_prev[h:h + 1, :], jnp.broadcast_to(
            jnp.max(jnp.where(causal, a[h:h + 1, :], -jnp.inf), axis=-1, keepdims=True), (lc, LANES)))
            for h in range(M_HEADS)]
        negm_b = [u_b[h] - jnp.broadcast_to(b_cols[:, _L_FG + h:_L_FG + h + 1], (lc, LANES)) for h in range(M_HEADS)]
        u_last = jnp.concatenate([u_b[h][lc - 1:lc, 0:1] for h in range(M_HEADS)], axis=0)
        return dict(a=a, m_prev=m_prev, u_b=u_b, negm_b=negm_b,
                    m_new=jnp.broadcast_to(b_rows[:, lc - 1:lc] - u_last, (M_HEADS, LANES)),
                    w_state=jnp.exp(a + u_last), decay=jnp.exp(m_prev[:, 0:1] + u_last))

    gt_ = [gates(sl) for sl in range(ns)]
    q = [rows(q_ref[0, sl]) for sl in range(ns)]
    k = [rows(k_ref[0, sl]) for sl in range(ns)]
    v = [rows(v_ref[0, sl]) for sl in range(ns)]
    kt = [x.astype(F32).T for x in k]

    units = [(sl, h) for sl in range(ns) for h in range(M_HEADS)]
    mine = {u: (jnp.logical_not(low_half) if u[1] % 2 else low_half) for u in units}
    pair = lambda x, u: x[u[0]][:, (u[1] // 2) * LANES:(u[1] // 2 + 1) * LANES]
    kh = {u: jnp.where(mine[u], pair(k, u), jnp.zeros_like(pair(k, u))) for u in units}
    qh = {u: jnp.where(mine[u], pair(q, u), jnp.zeros_like(pair(q, u))) for u in units}
    vext = {u: jnp.concatenate([v[u[0]][:, u[1] * M_DV:(u[1] + 1) * M_DV], ones_blk], axis=1) for u in units}
    s_old = {(sl, j): s_scr[sl, j] for sl in range(ns) for j in range(M_HEADS // 2)}
    s_bf = {key: val.astype(BF16) for key, val in s_old.items()}

    qk = {u: _dot_nt(pair(q, u), kh[u]) for u in units}
    qs = {u: _dot(qh[u], s_bf[(u[0], u[1] // 2)]) for u in units}
    ktw = {u: (kt[u[0]][u[1] * M_DK:(u[1] + 1) * M_DK, :] * gt_[u[0]]["w_state"][u[1]:u[1] + 1, :]).astype(BF16)
           for u in units}
    upd = {u: _dot(ktw[u], vext[u]) for u in units}
    w = {u: (jnp.where(causal, jnp.exp(gt_[u[0]]["a"][u[1]:u[1] + 1, :] + gt_[u[0]]["u_b"][u[1]]), 0.0)
             * qk[u]).astype(BF16) for u in units}
    inter = {u: jnp.exp(gt_[u[0]]["m_prev"][u[1]:u[1] + 1, :] + gt_[u[0]]["u_b"][u[1]]) for u in units}
    r = {u: _dot(w[u], vext[u]) + jnp.tile(inter[u], (1, 2)) * qs[u] for u in units}
    num = {u: r[u][:, :M_DV] for u in units}
    sq = {u: num[u] * num[u] for u in units}
    sq_hi = {u: sq[u].astype(BF16) for u in units}
    sq_lo = {u: (sq[u] - sq_hi[u].astype(F32)).astype(BF16) for u in units}
    msn = {u: (_dot(sq_hi[u], ones_blk) + _dot(sq_lo[u], ones_blk)) * (1.0 / M_DV) for u in units}
    for u in units:
        sl, h = u
        rden = 1.0 / jnp.maximum(jnp.abs(r[u][:, M_DV:]), jnp.exp(gt_[sl]["negm_b"][h]))
        scale = rden * lax.rsqrt(rden * rden * msn[u] + EPS)
        hn = num[u] * scale * gmh_ref[h:h + 1, :]
        out = (hn[:lb] * og_ref[0, sl, :, h * M_DV:(h + 1) * M_DV].astype(F32)).astype(BF16)
        h_ref[0, sl, :, h * M_DV:(h + 1) * M_DV] = out
    for sl in range(ns):
        for j in range(M_HEADS // 2):
            new = [gt_[sl]["decay"][h:h + 1, :] * s_old[(sl, j)][(h % 2) * M_DK:(h % 2 + 1) * M_DK, :] + upd[(sl, h)]
                   for h in (2 * j, 2 * j + 1)]
            s_scr[sl, j] = jnp.concatenate(new, axis=0)
        m_scr[sl, 0:M_HEADS, :] = gt_[sl]["m_new"]

    @pl.when(c == nc - 1)
    def _():
        sout_ref[0] = s_scr[...]
        mout_ref[0] = m_scr[...]


def _mlstm(mq, mk, mv, og, misc, s0, m0, gmh, *, nb, l, n_valid=None):
    lc = min(l, MLSTM_CHUNK)
    assert l % lc == 0 and lc % 16 == 0
    nc = l // lc
    ns = next(n for n in (4, 2, 1) if nb % n == 0)
    ng = nb // ns
    if s0.shape[0] == 1:
        s0, m0 = s0[:, None], m0[:, None]
        st = lambda g, c: (0, 0, 0, 0, 0)
        mst = lambda g, c: (0, 0, 0, 0)
    else:
        s0, m0 = s0.reshape((ng, ns) + S_SHAPE), m0.reshape(ng, ns, 8, LANES)
        st = lambda g, c: (g, 0, 0, 0, 0)
        mst = lambda g, c: (g, 0, 0, 0)
    blk = lambda g, c: (g, 0, c, 0)
    split = lambda x: x.reshape(ng, ns, l, x.shape[-1])
    kern = functools.partial(_mlstm_kernel, n_valid=n_valid)
    h, s, m = pl.pallas_call(
        kern,
        grid=(ng, nc),
        in_specs=[
            pl.BlockSpec((1, ns, lc, M_HEADS * M_DK), blk),
            pl.BlockSpec((1, ns, lc, M_HEADS * M_DK), blk),
            pl.BlockSpec((1, ns, lc, M_HEADS * M_DV), blk),
            pl.BlockSpec((1, ns, lc, M_HEADS * M_DV), blk),
            pl.BlockSpec((1, ns, lc, LANES), blk),
            pl.BlockSpec((1, s0.shape[1]) + S_SHAPE, st),
            pl.BlockSpec((1, m0.shape[1], 8, LANES), mst),
            pl.BlockSpec((M_HEADS, M_DV), lambda g, c: (0, 0)),
        ],
        out_specs=[
            pl.BlockSpec((1, ns, lc, M_HEADS * M_DV), blk),
            pl.BlockSpec((1, ns) + S_SHAPE, lambda g, c: (g, 0, 0, 0, 0)),
            pl.BlockSpec((1, ns, 8, LANES), lambda g, c: (g, 0, 0, 0)),
        ],
        out_shape=[
            jax.ShapeDtypeStruct((ng, ns, l, M_HEADS * M_DV), BF16),
            jax.ShapeDtypeStruct((ng, ns) + S_SHAPE, F32),
            jax.ShapeDtypeStruct((ng, ns, 8, LANES), F32),
        ],
        scratch_shapes=[
            pltpu.VMEM((ns,) + S_SHAPE, F32),
            pltpu.VMEM((ns, 8, LANES), F32),
        ],
        compiler_params=_cparams(("arbitrary", "arbitrary")),
        name="mlstm",
    )(split(mq), split(mk), split(mv), split(og), split(misc), s0, m0, gmh)
    return h.reshape(nb * l, M_HEADS * M_DV), s.reshape((nb,) + S_SHAPE), m.reshape(nb, 8, LANES)


R_ROWS = 40


def _route_kernel(x_ref, att_ref, hm_ref, wo_ref, g_ref, wr_ref, br_ref,
                  x1_ref, hp_ref, ri_ref, rg_ref, cnt_ref, cnt_scr):
    i = pl.program_id(0)
    tm = x_ref.shape[0]

    @pl.when(i == 0)
    def _():
        cnt_scr[...] = jnp.zeros_like(cnt_scr)

    mix = jnp.concatenate([att_ref[...], hm_ref[...]], axis=1)
    x1 = x_ref[...] + _dot(mix, wo_ref[...])
    x1_ref[...] = x1
    hn = x1 * lax.rsqrt(jnp.mean(x1 * x1, axis=-1, keepdims=True) + EPS) * g_ref[...]
    hb = hn.astype(BF16)
    half = D_MODEL // 2
    hi = lax.bitcast_convert_type(hb[:, :half].astype(F32), U32)
    lo = lax.bitcast_convert_type(hb[:, half:].astype(F32), U32)
    hp_ref[...] = (hi & jnp.uint32(0xFFFF0000)) | (lo >> 16)

    logits = _dot_nt(wr_ref[...], hb) + br_ref[:, 0:1]
    e_log = logits[0:N_EXPERTS, :]
    g_log = logits[N_EXPERTS:N_EXPERTS + N_GROUPS, :]
    gmax = jnp.max(g_log, axis=0, keepdims=True)
    gsum = jnp.sum(jnp.exp(g_log - gmax), axis=0, keepdims=True)
    gi = lax.broadcasted_iota(I32, g_log.shape, 0)
    g_idx = jnp.min(jnp.where(g_log == gmax, gi, N_GROUPS), axis=0, keepdims=True)
    e_sel = jnp.zeros((E_PER_GROUP, tm), F32)
    for gg in range(N_GROUPS):
        e_sel = jnp.where(g_idx == gg, e_log[gg * E_PER_GROUP:(gg + 1) * E_PER_GROUP, :], e_sel)
    ei = lax.broadcasted_iota(I32, e_sel.shape, 0)
    m1 = jnp.max(e_sel, axis=0, keepdims=True)
    i1 = jnp.min(jnp.where(e_sel == m1, ei, E_PER_GROUP), axis=0, keepdims=True)
    e2 = jnp.where(ei == i1, -jnp.inf, e_sel)
    m2 = jnp.max(e2, axis=0, keepdims=True)
    i2 = jnp.min(jnp.where(e2 == m2, ei, E_PER_GROUP), axis=0, keepdims=True)
    ex = jnp.exp(m2 - m1)
    gp = 1.0 / gsum
    p1 = 1.0 / (1.0 + ex)
    gate1 = gp * p1
    gate2 = gp * (ex * p1)
    id1 = g_idx * E_PER_GROUP + i1
    id2 = g_idx * E_PER_GROUP + i2

    xi = lax.broadcasted_iota(I32, (N_EXPERTS, tm), 0)
    oh1 = xi == id1
    oh2 = xi == id2
    e_cnt = (oh1 | oh2).astype(F32)
    rr = lax.broadcasted_iota(I32, (tm, tm), 0)
    cc = lax.broadcasted_iota(I32, (tm, tm), 1)
    upper = (rr < cc).astype(BF16)
    pref = _dot(e_cnt.astype(BF16), upper) + cnt_scr[:, 0:1]
    rank1 = jnp.sum(jnp.where(oh1, pref, 0.0), axis=0, keepdims=True)
    rank2 = jnp.sum(jnp.where(oh2, pref, 0.0), axis=0, keepdims=True)
    cnt_new = cnt_scr[...] + jnp.sum(e_cnt, axis=1, keepdims=True)
    cnt_scr[...] = cnt_new
    cnt_ref[...] = cnt_new.astype(I32)

    zi = jnp.zeros((1, tm), I32)
    ri_ref[...] = jnp.concatenate([id1, id2, rank1.astype(I32), rank2.astype(I32), zi, zi, zi, zi], axis=0)
    zf = jnp.zeros((1, tm), F32)
    rg_ref[...] = jnp.concatenate([gate1, gate2, zf, zf, zf, zf, zf, zf], axis=0)


def _route(x2d, att, hm, pw, tm):
    t = x2d.shape[0]
    row = lambda i: (i, 0)
    col = lambda i: (0, i)
    full = lambda i: (0, 0)
    return pl.pallas_call(
        _route_kernel,
        grid=(t // tm,),
        in_specs=[
            pl.BlockSpec((tm, D_MODEL), row),
            pl.BlockSpec((tm, D_MODEL // 2), row),
            pl.BlockSpec((tm, D_MODEL // 2), row),
            pl.BlockSpec((D_MODEL, D_MODEL), full),
            pl.BlockSpec((1, D_MODEL), full),
            pl.BlockSpec((R_ROWS, D_MODEL), full),
            pl.BlockSpec((R_ROWS, LANES), full),
        ],
        out_specs=[
            pl.BlockSpec((tm, D_MODEL), row),
            pl.BlockSpec((tm, D_MODEL // 2), row),
            pl.BlockSpec((8, tm), col),
            pl.BlockSpec((8, tm), col),
            pl.BlockSpec((N_EXPERTS, LANES), full),
        ],
        out_shape=[
            jax.ShapeDtypeStruct((t, D_MODEL), F32),
            jax.ShapeDtypeStruct((t, D_MODEL // 2), U32),
            jax.ShapeDtypeStruct((8, t), I32),
            jax.ShapeDtypeStruct((8, t), F32),
            jax.ShapeDtypeStruct((N_EXPERTS, LANES), I32),
        ],
        scratch_shapes=[pltpu.VMEM((N_EXPERTS, LANES), F32)],
        compiler_params=_cparams(("arbitrary",)),
        name="route",
    )(x2d, att, hm, pw["w_out"], pw["g_ffn"], pw["w_r"], pw["b_r"])


SUBLANES = 8


N_ZERO_BLOCKS = 2 * N_EXPERTS


def _dispatch_kernel(zb_ref, dest_ref, hp_ref, xs_ref, zero_scr, sem, zsem, *, tm, bm):
    @pl.when(pl.program_id(0) == 0)
    def _():
        zero_scr[...] = jnp.zeros_like(zero_scr)

        def zcopy(k):
            return pltpu.make_async_copy(zero_scr, xs_ref.at[pl.ds(pl.multiple_of(zb_ref[k] * bm, bm), bm)], zsem)

        def zstart(k, carry):
            @pl.when(zb_ref[k] >= 0)
            def _():
                zcopy(k).start()
            return carry

        def zwait(k, carry):
            @pl.when(zb_ref[k] >= 0)
            def _():
                zcopy(k).wait()
            return carry

        lax.fori_loop(0, N_ZERO_BLOCKS, zstart, 0)
        lax.fori_loop(0, N_ZERO_BLOCKS, zwait, 0)

    def body(g, carry):
        for u in range(SUBLANES):
            t = g * SUBLANES + u
            src = hp_ref.at[g, pl.ds(u, 1), :]
            pltpu.make_async_copy(src, xs_ref.at[pl.ds(dest_ref[t], 1)], sem).start(priority=0)
            pltpu.make_async_copy(src, xs_ref.at[pl.ds(dest_ref[tm + t], 1)], sem).start(priority=1)
        return carry

    lax.fori_loop(0, tm // SUBLANES, body, 0)
    pltpu.make_async_copy(xs_ref.at[pl.ds(0, 2 * tm)], xs_ref.at[pl.ds(0, 2 * tm)], sem).wait()


def _tile_slots(dest, tm):
    t = dest.shape[1]
    return dest[0:2].reshape(2, t // tm, tm).transpose(1, 0, 2).reshape(2 * t)


def _dispatch(zero_blocks, dest, hp, n_slots, tm, bm):
    t = hp.shape[0]
    return pl.pallas_call(
        functools.partial(_dispatch_kernel, tm=tm, bm=bm),
        grid_spec=pltpu.PrefetchScalarGridSpec(
            num_scalar_prefetch=1,
            grid=(t // tm,),
            in_specs=[
                pl.BlockSpec((2 * tm,), lambda i, zb: (i,), memory_space=pltpu.SMEM),
                pl.BlockSpec((tm // SUBLANES, SUBLANES, D_MODEL // 2), lambda i, zb: (i, 0, 0)),
            ],
            out_specs=pl.BlockSpec(memory_space=pl.ANY),
            scratch_shapes=[
                pltpu.VMEM((bm, D_MODEL // 2), U32),
                pltpu.SemaphoreType.DMA,
                pltpu.SemaphoreType.DMA,
            ],
        ),
        out_shape=jax.ShapeDtypeStruct((n_slots, D_MODEL // 2), U32),
        compiler_params=_cparams(("arbitrary",)),
        name="dispatch",
    )(zero_blocks, _tile_slots(dest, tm), hp.reshape(t // SUBLANES, SUBLANES, D_MODEL // 2))


def _expert_kernel(be_ref, nu_ref, xs_ref, w1_ref, w3_ref, w2_ref, ys_ref, w1_scr, w3_scr, w2_scr):
    i = pl.program_id(0)
    new_expert = jnp.logical_or(i == 0, be_ref[i] != be_ref[jnp.maximum(i - 1, 0)])

    @pl.when(jnp.logical_and(new_expert, i < nu_ref[0]))
    def _():
        w1_scr[...] = w1_ref[0].astype(BF16)
        w3_scr[...] = w3_ref[0].astype(BF16)
        w2_scr[...] = w2_ref[0].astype(BF16)

    @pl.when(i < nu_ref[0])
    def _():
        xw = xs_ref[...]
        xa = lax.bitcast_convert_type(xw & jnp.uint32(0xFFFF0000), F32).astype(BF16)
        xb = lax.bitcast_convert_type(xw << 16, F32).astype(BF16)
        half = D_MODEL // 2
        h1 = _dot(xa, w1_scr[:half, :]) + _dot(xb, w1_scr[half:, :])
        h3 = _dot(xa, w3_scr[:half, :]) + _dot(xb, w3_scr[half:, :])
        a = (h1 * jax.nn.sigmoid(h1)) * h3
        ys_ref[...] = _dot(a.astype(BF16), w2_scr[...])

    @pl.when(pl.program_id(0) >= nu_ref[0])
    def _():
        ys_ref[...] = jnp.zeros_like(ys_ref)


def _experts(blk_e, n_used, xs, pw, bm):
    n_slots = xs.shape[0]
    nblk = n_slots // bm
    blk = lambda i, be, nu: (jnp.minimum(i, nu[0] - 1), 0)
    oblk = lambda i, be, nu: (i, 0)
    wsel = lambda i, be, nu: (be[i], 0, 0)
    return pl.pallas_call(
        _expert_kernel,
        grid_spec=pltpu.PrefetchScalarGridSpec(
            num_scalar_prefetch=2,
            grid=(nblk,),
            in_specs=[
                pl.BlockSpec((bm, D_MODEL // 2), blk),
                pl.BlockSpec((1, D_MODEL, EXPERT_FF), wsel),
                pl.BlockSpec((1, D_MODEL, EXPERT_FF), wsel),
                pl.BlockSpec((1, EXPERT_FF, D_MODEL), wsel),
            ],
            out_specs=pl.BlockSpec((bm, D_MODEL), oblk),
            scratch_shapes=[
                pltpu.VMEM((D_MODEL, EXPERT_FF), BF16),
                pltpu.VMEM((D_MODEL, EXPERT_FF), BF16),
                pltpu.VMEM((EXPERT_FF, D_MODEL), BF16),
            ],
        ),
        out_shape=jax.ShapeDtypeStruct((n_slots, D_MODEL), F32),
        compiler_params=_cparams(("arbitrary",)),
        name="experts",
    )(blk_e, n_used, xs, pw["w1"], pw["w3"], pw["w2"])


def _combine_kernel(dest_ref, dnext_ref, x1_ref, rg_ref, ys_ref, y_ref, r0_scr, r1_scr, sem, *, tm):
    i = pl.program_id(0)
    slot = i % 2

    def issue(slots_ref, s):
        def body(g, carry):
            for u in range(SUBLANES):
                t = g * SUBLANES + u
                pltpu.make_async_copy(ys_ref.at[pl.ds(slots_ref[t], 1)], r0_scr.at[s, g, pl.ds(u, 1), :],
                                      sem.at[s]).start(priority=0)
                pltpu.make_async_copy(ys_ref.at[pl.ds(slots_ref[tm + t], 1)], r1_scr.at[s, g, pl.ds(u, 1), :],
                                      sem.at[s]).start(priority=1)
            return carry

        lax.fori_loop(0, tm // SUBLANES, body, 0)

    @pl.when(i == 0)
    def _():
        issue(dest_ref, 0)

    @pl.when(i + 1 < pl.num_programs(0))
    def _():
        issue(dnext_ref, 1 - slot)

    gt = jnp.concatenate([rg_ref[...], jnp.zeros((LANES - 8, tm), F32)], axis=0).T
    slab = ys_ref.at[pl.ds(0, tm)]
    pltpu.make_async_copy(slab, slab, sem.at[slot]).wait()
    pltpu.make_async_copy(slab, slab, sem.at[slot]).wait()
    r0 = r0_scr[slot].reshape(tm, D_MODEL)
    r1 = r1_scr[slot].reshape(tm, D_MODEL)
    y_ref[...] = x1_ref[...] + gt[:, 0:1] * r0 + gt[:, 1:2] * r1


def _combine(dest, x1, rg, ys, tm):
    t = x1.shape[0]
    nt = t // tm
    row = lambda i: (i, 0)
    col = lambda i: (0, i)
    slots = _tile_slots(dest, tm)
    return pl.pallas_call(
        functools.partial(_combine_kernel, tm=tm),
        grid=(nt,),
        in_specs=[
            pl.BlockSpec((2 * tm,), lambda i: (i,), memory_space=pltpu.SMEM),
            pl.BlockSpec((2 * tm,), lambda i: (jnp.minimum(i + 1, nt - 1),), memory_space=pltpu.SMEM),
            pl.BlockSpec((tm, D_MODEL), row),
            pl.BlockSpec((8, tm), col),
            pl.BlockSpec(memory_space=pl.ANY),
        ],
        out_specs=pl.BlockSpec((tm, D_MODEL), row),
        out_shape=jax.ShapeDtypeStruct((t, D_MODEL), F32),
        scratch_shapes=[
            pltpu.VMEM((2, tm // SUBLANES, SUBLANES, D_MODEL), F32),
            pltpu.VMEM((2, tm // SUBLANES, SUBLANES, D_MODEL), F32),
            pltpu.SemaphoreType.DMA((2,)),
        ],
        compiler_params=_cparams(("arbitrary",)),
        name="combine",
    )(slots, slots, x1, rg, ys)


def _prep_weights(g_attn, w_in, g_cq, w_uq, g_ckv, w_ukv, g_q, g_k, b_igate, b_fgate, g_mh,
                  w_out, g_ffn, w_group, b_group, w_erouter, b_erouter, w1, w3, w2):
    def cols(a, b):
        return w_in[:, a:b]

    o_cq, o_ckv, o_kr = 0, 256, 384
    o_mq, o_mk, o_mv, o_mi, o_mf, o_mo = 416, 672, 928, 1440, 1444, 1448
    hr = MLA_ROPE // 2
    misc = jnp.concatenate([
        cols(o_kr, o_kr + MLA_ROPE), -cols(o_kr + hr, o_kr + MLA_ROPE), cols(o_kr, o_kr + hr),
        cols(o_mi, o_mi + M_HEADS), cols(o_mf, o_mf + M_HEADS),
        jnp.zeros((D_MODEL, LANES - 2 * MLA_ROPE - 2 * M_HEADS), F32)], axis=1)
    w_p = jnp.concatenate([
        cols(o_cq, o_cq + 256), cols(o_ckv, o_ckv + 128), cols(o_mq, o_mq + 256), cols(o_mk, o_mk + 256),
        cols(o_mv, o_mv + 512), cols(o_mo, o_mo + 512), misc], axis=1).astype(BF16)
    nope, r1, r2 = w_uq[..., :MLA_NOPE], w_uq[..., MLA_NOPE:MLA_NOPE + hr], w_uq[..., MLA_NOPE + hr:]
    zq = lambda w: jnp.zeros((MLA_Q_LORA, MLA_HEADS, w), F32)
    w_qa = jnp.concatenate([nope, r1, r2, zq(HEAD_PAD - MLA_QK)], axis=-1).reshape(MLA_Q_LORA, MLA_HEADS * HEAD_PAD)
    w_qb = jnp.concatenate([zq(MLA_NOPE), -r2, r1, zq(HEAD_PAD - MLA_QK)], axis=-1).reshape(MLA_Q_LORA,
                                                                                           MLA_HEADS * HEAD_PAD)
    w_q = jnp.concatenate([w_qa, w_qb], axis=1).astype(BF16)
    pad_g = jnp.zeros((HEAD_PAD - MLA_QK,), F32)
    w_k = jnp.concatenate([w_ukv[..., :MLA_NOPE], jnp.zeros((MLA_KV_LORA, MLA_HEADS, HEAD_PAD - MLA_NOPE), F32)],
                          axis=-1).reshape(MLA_KV_LORA, MLA_HEADS * HEAD_PAD).T.astype(BF16)
    w_v = jnp.concatenate([w_ukv[..., MLA_NOPE:], jnp.zeros((MLA_KV_LORA, MLA_HEADS, LANES - MLA_V), F32)],
                          axis=-1).reshape(MLA_KV_LORA, MLA_HEADS * LANES).astype(BF16)
    gate_bias = jnp.concatenate([jnp.zeros((_L_IG,), F32), b_igate, b_fgate,
                                 jnp.zeros((LANES - _L_FG - M_HEADS,), F32)])[None]
    w_r = jnp.concatenate([w_erouter.T, w_group.T, jnp.zeros((R_ROWS - N_EXPERTS - N_GROUPS, D_MODEL), F32)],
                          axis=0).astype(BF16)
    b_r = jnp.concatenate([b_erouter, b_group, jnp.zeros((R_ROWS - N_EXPERTS - N_GROUPS,), F32)])
    return {
        "w_in": w_p, "g_attn": g_attn[None], "g_cq": g_cq[None], "w_q": w_q, "g_ckv": g_ckv[None],
        "g_q": jnp.concatenate([g_q, pad_g])[None], "g_k": jnp.broadcast_to(jnp.concatenate([g_k, pad_g])[:, None], (HEAD_PAD, LANES)),
        "gate_bias": gate_bias, "w_k": w_k, "w_v": w_v, "g_mh": g_mh,
        "w_out": w_out.astype(BF16), "g_ffn": g_ffn[None], "w_r": w_r,
        "b_r": jnp.broadcast_to(b_r[:, None], (R_ROWS, LANES)),
        "w1": w1, "w3": w3, "w2": w2,
        "score_bound": (MLA_QK * MLA_SCALE * LOG2E * 1.01 * jnp.max(jnp.abs(g_q)) * jnp.max(jnp.abs(g_k))).reshape(1),
    }


def _rope_table(pos):
    half = MLA_ROPE // 2
    inv = ROPE_BASE ** (-np.arange(half, dtype=np.float64) / half)
    ang = np.asarray(pos, np.float64)[:, None] * inv[None, :]
    cos = np.cos(ang)
    sin = np.sin(ang)
    c2 = np.concatenate([cos, cos], axis=1)
    s2 = np.concatenate([sin, sin], axis=1)
    return jnp.asarray(np.concatenate([c2, s2, c2, s2], axis=1), F32)


def _pick(n, pref):
    return pref if n % pref == 0 else n


def _slots_kernel(ri_ref, ps_ref, d_ref):
    tm = ri_ref.shape[1]
    xi = lax.broadcasted_iota(I32, (N_EXPERTS, tm), 0)
    ps = ps_ref[:, 0:1]
    rows = []
    for k in range(2):
        start = jnp.sum(jnp.where(xi == ri_ref[k:k + 1, :], ps, 0), axis=0, keepdims=True)
        rows.append(start + ri_ref[2 + k:3 + k, :])
    d_ref[...] = jnp.concatenate(rows + [jnp.zeros((6, tm), I32)], axis=0)


def _slots(ri, pstart_b, tm):
    t = ri.shape[1]
    return pl.pallas_call(
        _slots_kernel,
        grid=(t // tm,),
        in_specs=[pl.BlockSpec((8, tm), lambda i: (0, i)), pl.BlockSpec((N_EXPERTS, LANES), lambda i: (0, 0))],
        out_specs=pl.BlockSpec((8, tm), lambda i: (0, i)),
        out_shape=jax.ShapeDtypeStruct((8, t), I32),
        compiler_params=_cparams(("arbitrary",)),
        name="slots",
    )(ri, pstart_b)


def _moe_layer(x2d, att, hm, pw, *, tm_route, bm, tm_disp, tm_comb):
    t = x2d.shape[0]
    x1, hp, ri, rg, cnt = _route(x2d, att, hm, pw, tm_route)
    counts = cnt[:, 0]
    padded = (counts + bm - 1) // bm * bm
    pend = jnp.cumsum(padded)
    pstart = pend - padded
    dest = _slots(ri, jnp.broadcast_to(pstart[:, None], (N_EXPERTS, LANES)), _pick(t, 2048))
    n_slots = (2 * t // bm + N_EXPERTS) * bm
    nblk = n_slots // bm
    n_used = (pend[-1] // bm).astype(I32)
    experts = jnp.arange(N_EXPERTS, dtype=I32)
    blk_first = jnp.arange(nblk, dtype=I32) * bm
    blk_e = jnp.sum((pend[None, :] <= blk_first[:, None]).astype(I32), axis=1)
    blk_e = jnp.minimum(blk_e, jnp.max(jnp.where(counts > 0, experts, 0)))
    zero_blocks = jnp.concatenate([jnp.where(counts > 0, pend // bm - 1, -1),
                                   jnp.where(n_used + experts < nblk, n_used + experts, -1)]).astype(I32)
    xs = _dispatch(zero_blocks, dest, hp, n_slots, tm_disp, bm)
    ys = _experts(blk_e, n_used[None], xs, pw, bm)
    return _combine(dest, x1, rg, ys, tm_comb)


def _state_pack(c, n):
    ct = jnp.swapaxes(c, -1, -2)
    s = jnp.concatenate([ct, jnp.broadcast_to(n[..., None], ct.shape[:-1] + (S_W - M_DV,))], axis=-1)
    return s.reshape((s.shape[0],) + S_SHAPE)


def _state_unpack(s, m):
    s = s.reshape(s.shape[0], M_HEADS, M_DK, S_W)
    return jnp.swapaxes(s[..., :M_DV], -1, -2), s[..., M_DV], m[:, :M_HEADS, 0]


def kernel(x_prompt, x_sample, cache_ckv, cache_krope, state_mlstm_c, state_mlstm_n, state_mlstm_m,
           meta_tokens, g_attn, w_in, g_cq, w_uq, g_ckv, w_ukv, g_q, g_k, b_igate, b_fgate, g_mh,
           w_out, g_ffn, w_group, b_group, w_erouter, b_erouter, w1, w3, w2):
    bp, seq = x_prompt.shape[:2]
    bs, dec = x_sample.shape[:2]
    past = cache_ckv.shape[2]
    layer = 0
    pw = _prep_weights(g_attn[layer], w_in[layer], g_cq[layer], w_uq[layer], g_ckv[layer], w_ukv[layer],
                       g_q[layer], g_k[layer], b_igate[layer], b_fgate[layer], g_mh[layer], w_out[layer],
                       g_ffn[layer], w_group[layer], b_group[layer], w_erouter[layer], b_erouter[layer],
                       w1[layer], w3[layer], w2[layer])

    xm = jnp.concatenate([meta_tokens, jnp.zeros((META_PAD - N_META, D_MODEL), F32)], axis=0)
    tab_m = _rope_table(np.arange(META_PAD) - N_META)
    _, ckv_m, misc_m, mq_m, mk_m, mv_m, og_m = _project(xm, tab_m, pw, META_PAD)
    zero_s = jnp.zeros((1,) + S_SHAPE, F32)
    zero_m = jnp.zeros((1, 8, LANES), F32)
    _, s_meta, m_meta = _mlstm(mq_m, mk_m, mv_m, og_m, misc_m, zero_s, zero_m, pw["g_mh"],
                               nb=1, l=META_PAD, n_valid=N_META)

    tp = bp * seq
    xp2 = x_prompt.reshape(tp, D_MODEL)
    tm_p = _pick(seq, 1024)
    q_p, ckv_p, misc_p, mq_p, mk_p, mv_p, og_p = _project(xp2, _rope_table(np.arange(seq)), pw, tm_p)
    tq = _pick(seq, 1024)
    att_p = _attention(q_p, ckv_p, misc_p, ckv_m[None], misc_m[None], pw, nb=bp, lq=seq, lf=seq, tq=tq, tk=256,
                       causal=True, n_tail=N_META)
    hm_p, s_p, m_p = _mlstm(mq_p, mk_p, mv_p, og_p, misc_p, s_meta, m_meta, pw["g_mh"],
                            nb=bp, l=seq)
    y_p = _moe_layer(xp2, att_p, hm_p, pw, tm_route=tm_p, bm=1024, tm_disp=_pick(seq, 1024),
                     tm_comb=_pick(seq, 512))

    ts = bs * dec
    xs2 = x_sample.reshape(ts, D_MODEL)
    q_s, ckv_s, misc_s, mq_s, mk_s, mv_s, og_s = _project(xs2, _rope_table(past + np.arange(dec)), pw, dec)
    kr_cache = jnp.concatenate([cache_krope[layer], jnp.zeros((bs, past, LANES - MLA_ROPE), F32)], axis=-1)
    n_tail = dec + N_META
    assert n_tail <= META_PAD

    def tail_rows(own, meta):
        w = own.shape[-1]
        return jnp.concatenate([own.reshape(bs, dec, w), jnp.broadcast_to(meta[None, :N_META], (bs, N_META, w)),
                                jnp.zeros((bs, META_PAD - n_tail, w), F32)], axis=1)

    att_s = _attention(q_s, cache_ckv[layer].reshape(bs * past, MLA_KV_LORA), kr_cache.reshape(bs * past, LANES),
                       tail_rows(ckv_s, ckv_m), tail_rows(misc_s, misc_m), pw, nb=bs, lq=dec, lf=past, tq=dec,
                       tk=_pick(past, 256), causal=False, n_tail=n_tail)
    s0 = _state_pack(state_mlstm_c[layer], state_mlstm_n[layer])
    m0 = jnp.concatenate([jnp.broadcast_to(state_mlstm_m[layer][:, :, None], (bs, M_HEADS, LANES)),
                          jnp.zeros((bs, 8 - M_HEADS, LANES), F32)], axis=1)
    hm_s, s_s, m_s = _mlstm(mq_s, mk_s, mv_s, og_s, misc_s, s0, m0, pw["g_mh"], nb=bs, l=dec)
    y_s = _moe_layer(xs2, att_s, hm_s, pw, tm_route=_pick(ts, 512), bm=128, tm_disp=_pick(ts, 512),
                     tm_comb=_pick(ts, 512))

    m_ckv = ckv_m[:N_META]
    m_kr = misc_m[:N_META, :MLA_ROPE]
    new_ckv_p = jnp.concatenate([jnp.broadcast_to(m_ckv[None], (bp, N_META, MLA_KV_LORA)),
                                 ckv_p.reshape(bp, seq, MLA_KV_LORA)], axis=1)[None]
    new_kr_p = jnp.concatenate([jnp.broadcast_to(m_kr[None], (bp, N_META, MLA_ROPE)),
                                misc_p[:, :MLA_ROPE].reshape(bp, seq, MLA_ROPE)], axis=1)[None]
    c_p, n_p, mm_p = _state_unpack(s_p, m_p)
    c_s, n_s, mm_s = _state_unpack(s_s, m_s)
    return (y_p.reshape(bp, seq, D_MODEL), y_s.reshape(bs, dec, D_MODEL),
            new_ckv_p, new_kr_p, c_p[None], n_p[None], mm_p[None],
            ckv_s.reshape(bs, dec, MLA_KV_LORA)[None], misc_s[:, :MLA_ROPE].reshape(bs, dec, MLA_ROPE)[None],
            c_s[None], n_s[None], mm_s[None])
```

```python
import functools

import numpy as np
import jax
import jax.numpy as jnp
from jax import lax
from jax.experimental import pallas as pl
from jax.experimental.pallas import tpu as pltpu

F32 = jnp.float32
BF16 = jnp.bfloat16
I32 = jnp.int32
U32 = jnp.uint32

D_MODEL = 1024
CHUNK = 64
N_META = 16
MLA_HEADS = 8
MLA_V = 64
MLA_NOPE = 64
MLA_ROPE = 32
MLA_QK = MLA_NOPE + MLA_ROPE
MLA_Q_LORA = 256
MLA_KV_LORA = 128
MLA_SCALE = MLA_QK ** -0.5
ROPE_BASE = 10000.0
M_HEADS = 4
M_DV = 128
M_DK = 64
N_GROUPS = 4
E_PER_GROUP = 8
N_EXPERTS = 32
EXPERT_FF = 512
EPS = 1e-6

LANES = 128
HEAD_PAD = 128
PROJ_PAD = 2048
META_PAD = 128
NEG_BIG = -1e30
VMEM_LIMIT = 56 * 1024 * 1024

_O_CQ, _O_CKV, _O_MQ, _O_MK, _O_MV, _O_MO, _O_MISC = 0, 256, 384, 640, 896, 1408, 1920
_L_IG, _L_FG = 64, 68


def _cparams(sem):
    return pltpu.CompilerParams(dimension_semantics=sem, vmem_limit_bytes=VMEM_LIMIT)


def _dot(a, b):
    return jnp.dot(a, b, preferred_element_type=F32)


def _dot_nt(a, b):
    return lax.dot_general(a, b, (((1,), (1,)), ((), ())), preferred_element_type=F32)


def _split3(x):
    x1 = x.astype(BF16)
    r1 = x - x1.astype(F32)
    x2 = r1.astype(BF16)
    x3 = (r1 - x2.astype(F32)).astype(BF16)
    return x1, x2, x3


def _proj_kernel(x_ref, tab_ref, w_ref, g_ref, gcq_ref, wq_ref, gckv_ref, gq_ref, bias_ref,
                 q_ref, ckv_ref, misc_ref, mq_ref, mk_ref, mv_ref, og_ref):
    x = x_ref[...]
    xn = x * lax.rsqrt(jnp.mean(x * x, axis=-1, keepdims=True) + EPS) * g_ref[...]
    xb = xn.astype(BF16)
    tab = tab_ref[...]
    lane = lax.broadcasted_iota(I32, tab.shape, 1)

    def proj(off, width):
        return _dot(xb, w_ref[:, off:off + width])

    cq = proj(_O_CQ, MLA_Q_LORA)
    cqn = cq * lax.rsqrt(jnp.mean(cq * cq, axis=-1, keepdims=True) + EPS) * gcq_ref[...]
    cqb = cqn.astype(BF16)
    qw = MLA_HEADS * HEAD_PAD
    gq = gq_ref[...]
    tab_a = jnp.where(lane < MLA_NOPE, 1.0, jnp.where(lane < MLA_QK, tab, 0.0))
    tab_b = jnp.where((lane >= MLA_NOPE) & (lane < MLA_QK), pltpu.roll(tab, LANES - MLA_ROPE, 1), 0.0)
    slab = 2 * HEAD_PAD
    pair_ones = (lax.broadcasted_iota(I32, (slab, slab), 0) // HEAD_PAD
                 == lax.broadcasted_iota(I32, (slab, slab), 1) // HEAD_PAD).astype(BF16)
    for p in range(MLA_HEADS // 2):
        za = _dot(cqb, wq_ref[:, p * slab:(p + 1) * slab])
        zb = _dot(cqb, wq_ref[:, qw + p * slab:qw + (p + 1) * slab])
        ms = _dot((za * za).astype(BF16), pair_ones) * (1.0 / MLA_QK)
        for e in range(2):
            cols = slice(e * HEAD_PAD, (e + 1) * HEAD_PAD)
            qh = za[:, cols] * tab_a + zb[:, cols] * tab_b
            q_ref[2 * p + e] = (qh * lax.rsqrt(ms[:, cols] + EPS) * gq).astype(BF16)

    ckv = proj(_O_CKV, MLA_KV_LORA)
    ckv_ref[...] = ckv * lax.rsqrt(jnp.mean(ckv * ckv, axis=-1, keepdims=True) + EPS) * gckv_ref[...]

    zm = proj(_O_MISC, LANES)
    y = zm * tab
    rot = y + pltpu.roll(y, LANES - MLA_ROPE, 1)
    gate = zm + bias_ref[...]
    logf = jnp.minimum(gate, 0.0) - jnp.log1p(jnp.exp(-jnp.abs(gate)))
    misc = jnp.where(lane < MLA_ROPE, rot,
                     jnp.where((lane >= _L_IG) & (lane < _L_FG), gate,
                               jnp.where((lane >= _L_FG) & (lane < _L_FG + M_HEADS), logf, 0.0)))
    misc_ref[...] = misc

    mq_ref[...] = proj(_O_MQ, M_HEADS * M_DK).astype(BF16)
    mk_ref[...] = (proj(_O_MK, M_HEADS * M_DK) * (M_DK ** -0.5)).astype(BF16)
    mv_ref[...] = proj(_O_MV, M_HEADS * M_DV).astype(BF16)
    og_ref[...] = jax.nn.sigmoid(proj(_O_MO, M_HEADS * M_DV)).astype(BF16)


def _project(x2d, tab, pw, tm):
    t = x2d.shape[0]
    nt = t // tm
    ntab = tab.shape[0] // tm
    row = lambda i: (i, 0)
    full = lambda i: (0, 0)
    return pl.pallas_call(
        _proj_kernel,
        grid=(nt,),
        in_specs=[
            pl.BlockSpec((tm, D_MODEL), row),
            pl.BlockSpec((tm, LANES), lambda i: (i % ntab, 0)),
            pl.BlockSpec((D_MODEL, PROJ_PAD), full),
            pl.BlockSpec((1, D_MODEL), full),
            pl.BlockSpec((1, MLA_Q_LORA), full),
            pl.BlockSpec((MLA_Q_LORA, 2 * MLA_HEADS * HEAD_PAD), full),
            pl.BlockSpec((1, MLA_KV_LORA), full),
            pl.BlockSpec((1, HEAD_PAD), full),
            pl.BlockSpec((1, LANES), full),
        ],
        out_specs=[
            pl.BlockSpec((MLA_HEADS, tm, HEAD_PAD), lambda i: (0, i, 0)),
            pl.BlockSpec((tm, MLA_KV_LORA), row),
            pl.BlockSpec((tm, LANES), row),
            pl.BlockSpec((tm, M_HEADS * M_DK), row),
            pl.BlockSpec((tm, M_HEADS * M_DK), row),
            pl.BlockSpec((tm, M_HEADS * M_DV), row),
            pl.BlockSpec((tm, M_HEADS * M_DV), row),
        ],
        out_shape=[
            jax.ShapeDtypeStruct((MLA_HEADS, t, HEAD_PAD), BF16),
            jax.ShapeDtypeStruct((t, MLA_KV_LORA), F32),
            jax.ShapeDtypeStruct((t, LANES), F32),
            jax.ShapeDtypeStruct((t, M_HEADS * M_DK), BF16),
            jax.ShapeDtypeStruct((t, M_HEADS * M_DK), BF16),
            jax.ShapeDtypeStruct((t, M_HEADS * M_DV), BF16),
            jax.ShapeDtypeStruct((t, M_HEADS * M_DV), BF16),
        ],
        compiler_params=_cparams(("arbitrary",)),
        name="projection",
    )(x2d, tab, pw["w_in"], pw["g_attn"], pw["g_cq"], pw["w_q"], pw["g_ckv"], pw["g_q"], pw["gate_bias"])


SAFE_BOUND = 40.0
LOG2E = 1.4426950408889634


def _attn_kernel(sb_ref, q_ref, ckv_ref, misc_ref, ckvt_ref, misct_ref, wk_ref, wv_ref, gk_ref, o_ref,
                 kt_scr, v_scr, acc_scr, p_scr, *, lf, tq, tk, bt, causal, n_tail):
    i = pl.program_id(1)
    gk = gk_ref[...]

    def build(ckv_rows, misc_rows, dst):
        n = ckv_rows.shape[0]
        cb = ckv_rows.astype(BF16)
        kt_all = _dot_nt(wk_ref[...], cb)
        kr_t = jnp.concatenate([jnp.zeros((MLA_NOPE, n), F32), misc_rows.T[0:MLA_ROPE, :],
                                jnp.zeros((HEAD_PAD - MLA_QK, n), F32)], axis=0)
        v_all = _dot(cb, wv_ref[...])
        onecol = (lax.broadcasted_iota(I32, (n, LANES), 1) == MLA_V).astype(F32)
        gkc = jnp.tile(gk, (1, n // LANES))
        for h in range(MLA_HEADS):
            kk = kt_all[h * HEAD_PAD:(h + 1) * HEAD_PAD, :] + kr_t
            ms = jnp.sum(kk * kk, axis=0, keepdims=True) * (1.0 / MLA_QK)
            kt_scr[h, :, pl.ds(dst, n)] = (kk * lax.rsqrt(ms + EPS) * gkc).astype(BF16)
            v_scr[h, pl.ds(dst, n), :] = (v_all[:, h * LANES:(h + 1) * LANES] + onecol).astype(BF16)

    @pl.when(i == 0)
    def _():
        def body(r, carry):
            r0 = pl.multiple_of(r * bt, bt)
            build(ckv_ref[pl.ds(r0, bt), :], misc_ref[pl.ds(r0, bt), :], r0)
            return carry

        lax.fori_loop(0, lf // bt, body, 0)
        build(ckvt_ref[0], misct_ref[0], lf)

    ndiag = tq // tk if causal else 0
    nfull = i * ndiag if causal else lf // tk
    tail_mask = lax.broadcasted_iota(I32, (tq, META_PAD), 1) < n_tail

    def kt_tile(h, j):
        return kt_scr[h, :, pl.ds(pl.multiple_of(j * tk, tk), tk)]

    def v_tile(h, j):
        return v_scr[h, pl.ds(pl.multiple_of(j * tk, tk), tk), :]

    def chunk_ids(lo):
        rows = lax.broadcasted_iota(I32, (tq - lo, tk), 0) // CHUNK
        cols = lax.broadcasted_iota(I32, (tq - lo, tk), 1) // CHUNK
        return rows, cols

    def finish(h):
        acc = acc_scr[h]
        o_ref[:, h * MLA_V:(h + 1) * MLA_V] = (acc[:, :MLA_V] / acc[:, MLA_V:MLA_V + 1]).astype(BF16)

    safe = sb_ref[0] <= SAFE_BOUND * LOG2E

    @pl.when(safe)
    def _():
        bound = sb_ref[0]

        def scores(h, lo, kt):
            return _dot(q_ref[h, lo:, :], kt) * (MLA_SCALE * LOG2E) - bound

        def probs(s, mask):
            if mask is not None:
                s = jnp.where(mask, s, -jnp.inf)
            return jnp.exp2(s).astype(BF16)

        def pipe_step(j, lo, lo_prev, mask):
            for h in range(MLA_HEADS):
                acc_scr[h, lo_prev:, :] += _dot(p_scr[(j - 1) % 2, h, lo_prev:, :], v_tile(h, j - 1))
            for h in range(MLA_HEADS):
                p_scr[j % 2, h, lo:, :] = probs(scores(h, lo, kt_tile(h, j)), mask)

        for h in range(MLA_HEADS):
            p_tail = probs(scores(h, 0, kt_scr[h, :, lf:lf + META_PAD]), tail_mask)
            acc_scr[h] = _dot(p_tail, v_scr[h, lf:lf + META_PAD, :])
        rows0, cols0 = chunk_ids(0)
        if causal:
            first_mask = cols0 <= rows0 + jnp.where(nfull > 0, tk, 0)
        else:
            first_mask = None
        for h in range(MLA_HEADS):
            p_scr[0, h] = probs(scores(h, 0, kt_tile(h, 0)), first_mask)

        def body(j, carry):
            pipe_step(j, 0, 0, None)
            return carry

        lax.fori_loop(1, nfull, body, 0)
        if causal:
            @pl.when(nfull >= 1)
            def _():
                pipe_step(nfull, 0, 0, cols0 <= rows0)
            for d in range(1, ndiag):
                rows_d, cols_d = chunk_ids(d * tk)
                pipe_step(nfull + d, d * tk, (d - 1) * tk, cols_d <= rows_d)
            last, lo_last = nfull + ndiag - 1, (ndiag - 1) * tk
        else:
            last, lo_last = nfull - 1, 0
        for h in range(MLA_HEADS):
            acc_scr[h, lo_last:, :] += _dot(p_scr[last % 2, h, lo_last:, :], v_tile(h, last))
            finish(h)

    @pl.when(jnp.logical_not(safe))
    def _():
        rows0, cols0 = chunk_ids(0)
        for h in range(MLA_HEADS):
            qh = q_ref[h]

            def step(carry, kt, vrows, mask):
                m, acc = carry
                s = _dot(qh, kt) * MLA_SCALE
                if mask is not None:
                    s = jnp.where(mask, s, -jnp.inf)
                m_new = jnp.maximum(m, jnp.max(s, axis=-1, keepdims=True))
                acc = jnp.exp(m - m_new) * acc + _dot(jnp.exp(s - m_new).astype(BF16), vrows)
                return m_new, acc

            def body(j, c):
                mask = (cols0 + j * (tk // CHUNK) <= rows0 + i * (tq // CHUNK)) if causal else None
                return step(c, kt_tile(h, j), v_tile(h, j), mask)

            carry = (jnp.full((tq, 1), -jnp.inf, F32), jnp.zeros((tq, LANES), F32))
            carry = step(carry, kt_scr[h, :, lf:lf + META_PAD], v_scr[h, lf:lf + META_PAD, :], tail_mask)
            carry = lax.fori_loop(0, nfull + ndiag, body, carry)
            acc_scr[h] = carry[1]
            finish(h)


def _attention(q, ckv_f, misc_f, ckv_t, misc_t, pw, *, nb, lq, lf, tq, tk, causal, n_tail):
    nq = lq // tq
    bt = min(512, lf)
    assert lf % bt == 0 and lf % tk == 0 and lq % tq == 0 and tk % CHUNK == 0 and (not causal or tq % tk == 0)
    kern = functools.partial(_attn_kernel, lf=lf, tq=tq, tk=tk, bt=bt, causal=causal, n_tail=n_tail)
    full2 = lambda b, i, sb: (0, 0)
    full3 = lambda b, i, sb: (0, 0, 0)
    tail = full3 if ckv_t.shape[0] == 1 else (lambda b, i, sb: (b, 0, 0))
    lk = lf + META_PAD
    return pl.pallas_call(
        kern,
        grid_spec=pltpu.PrefetchScalarGridSpec(
            num_scalar_prefetch=1,
            grid=(nb, nq),
            in_specs=[
                pl.BlockSpec((MLA_HEADS, tq, HEAD_PAD), lambda b, i, sb: (0, b * nq + i, 0)),
                pl.BlockSpec((lf, MLA_KV_LORA), lambda b, i, sb: (b, 0)),
                pl.BlockSpec((lf, LANES), lambda b, i, sb: (b, 0)),
                pl.BlockSpec((1, META_PAD, MLA_KV_LORA), tail),
                pl.BlockSpec((1, META_PAD, LANES), tail),
                pl.BlockSpec((MLA_HEADS * HEAD_PAD, MLA_KV_LORA), full2),
                pl.BlockSpec((MLA_KV_LORA, MLA_HEADS * LANES), full2),
                pl.BlockSpec((HEAD_PAD, LANES), full2),
            ],
            out_specs=pl.BlockSpec((tq, MLA_HEADS * MLA_V), lambda b, i, sb: (b * nq + i, 0)),
            scratch_shapes=[
                pltpu.VMEM((MLA_HEADS, HEAD_PAD, lk), BF16),
                pltpu.VMEM((MLA_HEADS, lk, LANES), BF16),
                pltpu.VMEM((MLA_HEADS, tq, LANES), F32),
                pltpu.VMEM((2, MLA_HEADS, tq, tk), BF16),
            ],
        ),
        out_shape=jax.ShapeDtypeStruct((nb * lq, MLA_HEADS * MLA_V), BF16),
        compiler_params=_cparams(("arbitrary", "arbitrary")),
        name="attention",
    )(pw["score_bound"], q, ckv_f, misc_f, ckv_t, misc_t, pw["w_k"], pw["w_v"], pw["g_k"])


S_W = 2 * M_DV
S_SHAPE = (M_HEADS // 2, 2 * M_DK, S_W)


MLSTM_CHUNK = LANES


def _mlstm_kernel(q_ref, k_ref, v_ref, og_ref, misc_ref, s0_ref, m0_ref, gmh_ref,
                  h_ref, sout_ref, mout_ref, s_scr, m_scr, *, n_valid):
    c = pl.program_id(1)
    nc = pl.num_programs(1)
    ns = q_ref.shape[1]
    lb = q_ref.shape[2]
    lc = MLSTM_CHUNK

    @pl.when(c == 0)
    def _():
        for sl in range(ns):
            s_scr[sl] = s0_ref[0, min(sl, s0_ref.shape[1] - 1)]
            m_scr[sl] = m0_ref[0, min(sl, m0_ref.shape[1] - 1)]

    def rows(x):
        if lb == lc:
            return x
        return jnp.concatenate([x, jnp.zeros((lc - lb,) + x.shape[1:], x.dtype)], axis=0)

    rr = lax.broadcasted_iota(I32, (lc, lc), 0)
    cc = lax.broadcasted_iota(I32, (lc, lc), 1)
    causal = cc <= rr
    tri = causal.astype(BF16)
    trit = (rr <= cc).astype(BF16)
    ones_blk = jnp.ones((lc, M_DV), BF16)
    low_half = lax.broadcasted_iota(I32, (lc, LANES), 1) < M_DK
    limit = lb if n_valid is None else n_valid - c * lb

    def gates(sl):
        g = rows(misc_ref[0, sl])
        row = lax.broadcasted_iota(I32, g.shape, 0)
        lane = lax.broadcasted_iota(I32, g.shape, 1)
        g = jnp.where(row < limit, g, jnp.where((lane >= _L_IG) & (lane < _L_FG), NEG_BIG, 0.0))
        gt8 = g.T[_L_IG:_L_IG + 2 * M_HEADS, :]
        g1, g2, g3 = _split3(g)
        b_cols = _dot(tri, g1) + _dot(tri, g2) + _dot(tri, g3)
        t1, t2, t3 = _split3(gt8)
        b_rows = (_dot(t1, trit) + _dot(t2, trit) + _dot(t3, trit))[M_HEADS:2 * M_HEADS, :]
        a = gt8[0:M_HEADS, :] - b_rows
        m_prev = m_scr[sl, 0:M_HEADS, :]
        u_b = [-jnp.maximum(m_prev[h:h + 1, :], jnp.broadcast_to(
            jnp.max(jnp.where(causal, a[h:h + 1, :], -jnp.inf), axis=-1, keepdims=True), (lc, LANES)))
            for h in range(M_HEADS)]
        negm_b = [u_b[h] - jnp.broadcast_to(b_cols[:, _L_FG + h:_L_FG + h + 1], (lc, LANES)) for h in range(M_HEADS)]
        u_last = jnp.concatenate([u_b[h][lc - 1:lc, 0:1] for h in range(M_HEADS)], axis=0)
        return dict(a=a, m_prev=m_prev, u_b=u_b, negm_b=negm_b,
                    m_new=jnp.broadcast_to(b_rows[:, lc - 1:lc] - u_last, (M_HEADS, LANES)),
                    w_state=jnp.exp(a + u_last), decay=jnp.exp(m_prev[:, 0:1] + u_last))

    gt_ = [gates(sl) for sl in range(ns)]
    q = [rows(q_ref[0, sl]) for sl in range(ns)]
    k = [rows(k_ref[0, sl]) for sl in range(ns)]
    v = [rows(v_ref[0, sl]) for sl in range(ns)]
    kt = [x.astype(F32).T for x in k]

    units = [(sl, h) for sl in range(ns) for h in range(M_HEADS)]
    mine = {u: (jnp.logical_not(low_half) if u[1] % 2 else low_half) for u in units}
    pair = lambda x, u: x[u[0]][:, (u[1] // 2) * LANES:(u[1] // 2 + 1) * LANES]
    kh = {u: jnp.where(mine[u], pair(k, u), jnp.zeros_like(pair(k, u))) for u in units}
    qh = {u: jnp.where(mine[u], pair(q, u), jnp.zeros_like(pair(q, u))) for u in units}
    vext = {u: jnp.concatenate([v[u[0]][:, u[1] * M_DV:(u[1] + 1) * M_DV], ones_blk], axis=1) for u in units}
    s_old = {(sl, j): s_scr[sl, j] for sl in range(ns) for j in range(M_HEADS // 2)}
    s_bf = {key: val.astype(BF16) for key, val in s_old.items()}

    qk = {u: _dot_nt(pair(q, u), kh[u]) for u in units}
    qs = {u: _dot(qh[u], s_bf[(u[0], u[1] // 2)]) for u in units}
    ktw = {u: (kt[u[0]][u[1] * M_DK:(u[1] + 1) * M_DK, :] * gt_[u[0]]["w_state"][u[1]:u[1] + 1, :]).astype(BF16)
           for u in units}
    upd = {u: _dot(ktw[u], vext[u]) for u in units}
    w = {u: (jnp.where(causal, jnp.exp(gt_[u[0]]["a"][u[1]:u[1] + 1, :] + gt_[u[0]]["u_b"][u[1]]), 0.0)
             * qk[u]).astype(BF16) for u in units}
    inter = {u: jnp.exp(gt_[u[0]]["m_prev"][u[1]:u[1] + 1, :] + gt_[u[0]]["u_b"][u[1]]) for u in units}
    r = {u: _dot(w[u], vext[u]) + jnp.tile(inter[u], (1, 2)) * qs[u] for u in units}
    num = {u: r[u][:, :M_DV] for u in units}
    sq = {u: num[u] * num[u] for u in units}
    sq_hi = {u: sq[u].astype(BF16) for u in units}
    sq_lo = {u: (sq[u] - sq_hi[u].astype(F32)).astype(BF16) for u in units}
    msn = {u: (_dot(sq_hi[u], ones_blk) + _dot(sq_lo[u], ones_blk)) * (1.0 / M_DV) for u in units}
    for u in units:
        sl, h = u
        rden = 1.0 / jnp.maximum(jnp.abs(r[u][:, M_DV:]), jnp.exp(gt_[sl]["negm_b"][h]))
        scale = rden * lax.rsqrt(rden * rden * msn[u] + EPS)
        hn = num[u] * scale * gmh_ref[h:h + 1, :]
        out = (hn[:lb] * og_ref[0, sl, :, h * M_DV:(h + 1) * M_DV].astype(F32)).astype(BF16)
        h_ref[0, sl, :, h * M_DV:(h + 1) * M_DV] = out
    for sl in range(ns):
        for j in range(M_HEADS // 2):
            new = [gt_[sl]["decay"][h:h + 1, :] * s_old[(sl, j)][(h % 2) * M_DK:(h % 2 + 1) * M_DK, :] + upd[(sl, h)]
                   for h in (2 * j, 2 * j + 1)]
            s_scr[sl, j] = jnp.concatenate(new, axis=0)
        m_scr[sl, 0:M_HEADS, :] = gt_[sl]["m_new"]

    @pl.when(c == nc - 1)
    def _():
        sout_ref[0] = s_scr[...]
        mout_ref[0] = m_scr[...]


def _mlstm(mq, mk, mv, og, misc, s0, m0, gmh, *, nb, l, n_valid=None):
    lc = min(l, MLSTM_CHUNK)
    assert l % lc == 0 and lc % 16 == 0
    nc = l // lc
    ns = next(n for n in (4, 2, 1) if nb % n == 0)
    ng = nb // ns
    if s0.shape[0] == 1:
        s0, m0 = s0[:, None], m0[:, None]
        st = lambda g, c: (0, 0, 0, 0, 0)
        mst = lambda g, c: (0, 0, 0, 0)
    else:
        s0, m0 = s0.reshape((ng, ns) + S_SHAPE), m0.reshape(ng, ns, 8, LANES)
        st = lambda g, c: (g, 0, 0, 0, 0)
        mst = lambda g, c: (g, 0, 0, 0)
    blk = lambda g, c: (g, 0, c, 0)
    split = lambda x: x.reshape(ng, ns, l, x.shape[-1])
    kern = functools.partial(_mlstm_kernel, n_valid=n_valid)
    h, s, m = pl.pallas_call(
        kern,
        grid=(ng, nc),
        in_specs=[
            pl.BlockSpec((1, ns, lc, M_HEADS * M_DK), blk),
            pl.BlockSpec((1, ns, lc, M_HEADS * M_DK), blk),
            pl.BlockSpec((1, ns, lc, M_HEADS * M_DV), blk),
            pl.BlockSpec((1, ns, lc, M_HEADS * M_DV), blk),
            pl.BlockSpec((1, ns, lc, LANES), blk),
            pl.BlockSpec((1, s0.shape[1]) + S_SHAPE, st),
            pl.BlockSpec((1, m0.shape[1], 8, LANES), mst),
            pl.BlockSpec((M_HEADS, M_DV), lambda g, c: (0, 0)),
        ],
        out_specs=[
            pl.BlockSpec((1, ns, lc, M_HEADS * M_DV), blk),
            pl.BlockSpec((1, ns) + S_SHAPE, lambda g, c: (g, 0, 0, 0, 0)),
            pl.BlockSpec((1, ns, 8, LANES), lambda g, c: (g, 0, 0, 0)),
        ],
        out_shape=[
            jax.ShapeDtypeStruct((ng, ns, l, M_HEADS * M_DV), BF16),
            jax.ShapeDtypeStruct((ng, ns) + S_SHAPE, F32),
            jax.ShapeDtypeStruct((ng, ns, 8, LANES), F32),
        ],
        scratch_shapes=[
            pltpu.VMEM((ns,) + S_SHAPE, F32),
            pltpu.VMEM((ns, 8, LANES), F32),
        ],
        compiler_params=_cparams(("arbitrary", "arbitrary")),
        name="mlstm",
    )(split(mq), split(mk), split(mv), split(og), split(misc), s0, m0, gmh)
    return h.reshape(nb * l, M_HEADS * M_DV), s.reshape((nb,) + S_SHAPE), m.reshape(nb, 8, LANES)


R_ROWS = 40


def _route_kernel(x_ref, att_ref, hm_ref, wo_ref, g_ref, wr_ref, br_ref,
                  x1_ref, hp_ref, ri_ref, rg_ref, cnt_ref, cnt_scr):
    i = pl.program_id(0)
    tm = x_ref.shape[0]

    @pl.when(i == 0)
    def _():
        cnt_scr[...] = jnp.zeros_like(cnt_scr)

    mix = jnp.concatenate([att_ref[...], hm_ref[...]], axis=1)
    x1 = x_ref[...] + _dot(mix, wo_ref[...])
    x1_ref[...] = x1
    hn = x1 * lax.rsqrt(jnp.mean(x1 * x1, axis=-1, keepdims=True) + EPS) * g_ref[...]
    hb = hn.astype(BF16)
    half = D_MODEL // 2
    hi = lax.bitcast_convert_type(hb[:, :half].astype(F32), U32)
    lo = lax.bitcast_convert_type(hb[:, half:].astype(F32), U32)
    hp_ref[...] = (hi & jnp.uint32(0xFFFF0000)) | (lo >> 16)

    logits = _dot_nt(wr_ref[...], hb) + br_ref[:, 0:1]
    e_log = logits[0:N_EXPERTS, :]
    g_log = logits[N_EXPERTS:N_EXPERTS + N_GROUPS, :]
    gmax = jnp.max(g_log, axis=0, keepdims=True)
    gsum = jnp.sum(jnp.exp(g_log - gmax), axis=0, keepdims=True)
    gi = lax.broadcasted_iota(I32, g_log.shape, 0)
    g_idx = jnp.min(jnp.where(g_log == gmax, gi, N_GROUPS), axis=0, keepdims=True)
    e_sel = jnp.zeros((E_PER_GROUP, tm), F32)
    for gg in range(N_GROUPS):
        e_sel = jnp.where(g_idx == gg, e_log[gg * E_PER_GROUP:(gg + 1) * E_PER_GROUP, :], e_sel)
    ei = lax.broadcasted_iota(I32, e_sel.shape, 0)
    m1 = jnp.max(e_sel, axis=0, keepdims=True)
    i1 = jnp.min(jnp.where(e_sel == m1, ei, E_PER_GROUP), axis=0, keepdims=True)
    e2 = jnp.where(ei == i1, -jnp.inf, e_sel)
    m2 = jnp.max(e2, axis=0, keepdims=True)
    i2 = jnp.min(jnp.where(e2 == m2, ei, E_PER_GROUP), axis=0, keepdims=True)
    ex = jnp.exp(m2 - m1)
    gp = 1.0 / gsum
    p1 = 1.0 / (1.0 + ex)
    gate1 = gp * p1
    gate2 = gp * (ex * p1)
    id1 = g_idx * E_PER_GROUP + i1
    id2 = g_idx * E_PER_GROUP + i2

    xi = lax.broadcasted_iota(I32, (N_EXPERTS, tm), 0)
    oh1 = xi == id1
    oh2 = xi == id2
    e_cnt = (oh1 | oh2).astype(F32)
    rr = lax.broadcasted_iota(I32, (tm, tm), 0)
    cc = lax.broadcasted_iota(I32, (tm, tm), 1)
    upper = (rr < cc).astype(BF16)
    pref = _dot(e_cnt.astype(BF16), upper) + cnt_scr[:, 0:1]
    rank1 = jnp.sum(jnp.where(oh1, pref, 0.0), axis=0, keepdims=True)
    rank2 = jnp.sum(jnp.where(oh2, pref, 0.0), axis=0, keepdims=True)
    cnt_new = cnt_scr[...] + jnp.sum(e_cnt, axis=1, keepdims=True)
    cnt_scr[...] = cnt_new
    cnt_ref[...] = cnt_new.astype(I32)

    zi = jnp.zeros((1, tm), I32)
    ri_ref[...] = jnp.concatenate([id1, id2, rank1.astype(I32), rank2.astype(I32), zi, zi, zi, zi], axis=0)
    zf = jnp.zeros((1, tm), F32)
    rg_ref[...] = jnp.concatenate([gate1, gate2, zf, zf, zf, zf, zf, zf], axis=0)


def _route(x2d, att, hm, pw, tm):
    t = x2d.shape[0]
    row = lambda i: (i, 0)
    col = lambda i: (0, i)
    full = lambda i: (0, 0)
    return pl.pallas_call(
        _route_kernel,
        grid=(t // tm,),
        in_specs=[
            pl.BlockSpec((tm, D_MODEL), row),
            pl.BlockSpec((tm, D_MODEL // 2), row),
            pl.BlockSpec((tm, D_MODEL // 2), row),
            pl.BlockSpec((D_MODEL, D_MODEL), full),
            pl.BlockSpec((1, D_MODEL), full),
            pl.BlockSpec((R_ROWS, D_MODEL), full),
            pl.BlockSpec((R_ROWS, LANES), full),
        ],
        out_specs=[
            pl.BlockSpec((tm, D_MODEL), row),
            pl.BlockSpec((tm, D_MODEL // 2), row),
            pl.BlockSpec((8, tm), col),
            pl.BlockSpec((8, tm), col),
            pl.BlockSpec((N_EXPERTS, LANES), full),
        ],
        out_shape=[
            jax.ShapeDtypeStruct((t, D_MODEL), F32),
            jax.ShapeDtypeStruct((t, D_MODEL // 2), U32),
            jax.ShapeDtypeStruct((8, t), I32),
            jax.ShapeDtypeStruct((8, t), F32),
            jax.ShapeDtypeStruct((N_EXPERTS, LANES), I32),
        ],
        scratch_shapes=[pltpu.VMEM((N_EXPERTS, LANES), F32)],
        compiler_params=_cparams(("arbitrary",)),
        name="route",
    )(x2d, att, hm, pw["w_out"], pw["g_ffn"], pw["w_r"], pw["b_r"])


SUBLANES = 8


N_ZERO_BLOCKS = 2 * N_EXPERTS


def _dispatch_kernel(zb_ref, dest_ref, hp_ref, xs_ref, zero_scr, hbuf, sem, zsem, lsem, *, tm, bm):
    @pl.when(pl.program_id(0) == 0)
    def _():
        zero_scr[...] = jnp.zeros_like(zero_scr)

        def zcopy(k):
            return pltpu.make_async_copy(zero_scr, xs_ref.at[pl.ds(pl.multiple_of(zb_ref[k] * bm, bm), bm)], zsem)

        def zstart(k, carry):
            @pl.when(zb_ref[k] >= 0)
            def _():
                zcopy(k).start()
            return carry

        def zwait(k, carry):
            @pl.when(zb_ref[k] >= 0)
            def _():
                zcopy(k).wait()
            return carry

        lax.fori_loop(0, N_ZERO_BLOCKS, zstart, 0)
        lax.fori_loop(0, N_ZERO_BLOCKS, zwait, 0)

    i = pl.program_id(0)
    n = pl.num_programs(0)
    slabs = tm // SUBLANES

    def load(tile, s):
        return pltpu.make_async_copy(hp_ref.at[pl.ds(pl.multiple_of(tile * slabs, slabs), slabs)], hbuf.at[s],
                                     lsem.at[s])

    def scatter_done(s):
        return pltpu.make_async_copy(xs_ref.at[pl.ds(0, 2 * tm)], xs_ref.at[pl.ds(0, 2 * tm)], sem.at[s])

    @pl.when(i == 0)
    def _():
        load(0, 0).start()

        @pl.when(n > 1)
        def _():
            load(1, 1).start()

    cur = i % 3
    load(i, cur).wait()

    def body(g, carry):
        for u in range(SUBLANES):
            t = g * SUBLANES + u
            src = hbuf.at[cur, g, pl.ds(u, 1), :]
            pltpu.make_async_copy(src, xs_ref.at[pl.ds(dest_ref[t], 1)], sem.at[i % 2]).start(priority=0)
            pltpu.make_async_copy(src, xs_ref.at[pl.ds(dest_ref[tm + t], 1)], sem.at[i % 2]).start(priority=1)
        return carry

    lax.fori_loop(0, slabs, body, 0)

    @pl.when(i > 0)
    def _():
        scatter_done((i + 1) % 2).wait()

    @pl.when(i + 2 < n)
    def _():
        load(i + 2, (i + 2) % 3).start()

    @pl.when(i == n - 1)
    def _():
        scatter_done(i % 2).wait()


def _tile_slots(dest, tm):
    t = dest.shape[1]
    return dest[0:2].reshape(2, t // tm, tm).transpose(1, 0, 2).reshape(2 * t)


def _dispatch(zero_blocks, dest, hp, n_slots, tm, bm):
    t = hp.shape[0]
    return pl.pallas_call(
        functools.partial(_dispatch_kernel, tm=tm, bm=bm),
        grid_spec=pltpu.PrefetchScalarGridSpec(
            num_scalar_prefetch=1,
            grid=(t // tm,),
            in_specs=[
                pl.BlockSpec((2 * tm,), lambda i, zb: (i,), memory_space=pltpu.SMEM),
                pl.BlockSpec(memory_space=pl.ANY),
            ],
            out_specs=pl.BlockSpec(memory_space=pl.ANY),
            scratch_shapes=[
                pltpu.VMEM((bm, D_MODEL // 2), U32),
                pltpu.VMEM((3, tm // SUBLANES, SUBLANES, D_MODEL // 2), U32),
                pltpu.SemaphoreType.DMA((2,)),
                pltpu.SemaphoreType.DMA,
                pltpu.SemaphoreType.DMA((3,)),
            ],
        ),
        out_shape=jax.ShapeDtypeStruct((n_slots, D_MODEL // 2), U32),
        compiler_params=_cparams(("arbitrary",)),
        name="dispatch",
    )(zero_blocks, _tile_slots(dest, tm), hp.reshape(t // SUBLANES, SUBLANES, D_MODEL // 2))


def _expert_kernel(be_ref, nu_ref, xs_ref, w1_ref, w3_ref, w2_ref, ys_ref, w1_scr, w3_scr, w2_scr):
    i = pl.program_id(0)
    new_expert = jnp.logical_or(i == 0, be_ref[i] != be_ref[jnp.maximum(i - 1, 0)])

    @pl.when(jnp.logical_and(new_expert, i < nu_ref[0]))
    def _():
        w1_scr[...] = w1_ref[0].astype(BF16)
        w3_scr[...] = w3_ref[0].astype(BF16)
        w2_scr[...] = w2_ref[0].astype(BF16)

    @pl.when(i < nu_ref[0])
    def _():
        xw = xs_ref[...]
        xa = lax.bitcast_convert_type(xw & jnp.uint32(0xFFFF0000), F32).astype(BF16)
        xb = lax.bitcast_convert_type(xw << 16, F32).astype(BF16)
        half = D_MODEL // 2
        h1 = _dot(xa, w1_scr[:half, :]) + _dot(xb, w1_scr[half:, :])
        h3 = _dot(xa, w3_scr[:half, :]) + _dot(xb, w3_scr[half:, :])
        a = (h1 * jax.nn.sigmoid(h1)) * h3
        ys_ref[...] = _dot(a.astype(BF16), w2_scr[...])

    @pl.when(pl.program_id(0) >= nu_ref[0])
    def _():
        ys_ref[...] = jnp.zeros_like(ys_ref)


def _experts(blk_e, n_used, xs, pw, bm):
    n_slots = xs.shape[0]
    nblk = n_slots // bm
    blk = lambda i, be, nu: (jnp.minimum(i, nu[0] - 1), 0)
    oblk = lambda i, be, nu: (i, 0)
    wsel = lambda i, be, nu: (be[i], 0, 0)
    return pl.pallas_call(
        _expert_kernel,
        grid_spec=pltpu.PrefetchScalarGridSpec(
            num_scalar_prefetch=2,
            grid=(nblk,),
            in_specs=[
                pl.BlockSpec((bm, D_MODEL // 2), blk),
                pl.BlockSpec((1, D_MODEL, EXPERT_FF), wsel),
                pl.BlockSpec((1, D_MODEL, EXPERT_FF), wsel),
                pl.BlockSpec((1, EXPERT_FF, D_MODEL), wsel),
            ],
            out_specs=pl.BlockSpec((bm, D_MODEL), oblk),
            scratch_shapes=[
                pltpu.VMEM((D_MODEL, EXPERT_FF), BF16),
                pltpu.VMEM((D_MODEL, EXPERT_FF), BF16),
                pltpu.VMEM((EXPERT_FF, D_MODEL), BF16),
            ],
        ),
        out_shape=jax.ShapeDtypeStruct((n_slots, D_MODEL), F32),
        compiler_params=_cparams(("arbitrary",)),
        name="experts",
    )(blk_e, n_used, xs, pw["w1"], pw["w3"], pw["w2"])


def _combine_kernel(dest_ref, dnext_ref, x1_ref, rg_ref, ys_ref, y_ref, r0_scr, r1_scr, sem, *, tm):
    i = pl.program_id(0)
    slot = i % 2

    def issue(slots_ref, s):
        def body(g, carry):
            for u in range(SUBLANES):
                t = g * SUBLANES + u
                pltpu.make_async_copy(ys_ref.at[pl.ds(slots_ref[t], 1)], r0_scr.at[s, g, pl.ds(u, 1), :],
                                      sem.at[s]).start(priority=0)
                pltpu.make_async_copy(ys_ref.at[pl.ds(slots_ref[tm + t], 1)], r1_scr.at[s, g, pl.ds(u, 1), :],
                                      sem.at[s]).start(priority=1)
            return carry

        lax.fori_loop(0, tm // SUBLANES, body, 0)

    @pl.when(i == 0)
    def _():
        issue(dest_ref, 0)

    @pl.when(i + 1 < pl.num_programs(0))
    def _():
        issue(dnext_ref, 1 - slot)

    gt = jnp.concatenate([rg_ref[...], jnp.zeros((LANES - 8, tm), F32)], axis=0).T
    slab = ys_ref.at[pl.ds(0, tm)]
    pltpu.make_async_copy(slab, slab, sem.at[slot]).wait()
    pltpu.make_async_copy(slab, slab, sem.at[slot]).wait()
    r0 = r0_scr[slot].reshape(tm, D_MODEL)
    r1 = r1_scr[slot].reshape(tm, D_MODEL)
    y_ref[...] = x1_ref[...] + gt[:, 0:1] * r0 + gt[:, 1:2] * r1


def _combine(dest, x1, rg, ys, tm):
    t = x1.shape[0]
    nt = t // tm
    row = lambda i: (i, 0)
    col = lambda i: (0, i)
    slots = _tile_slots(dest, tm)
    return pl.pallas_call(
        functools.partial(_combine_kernel, tm=tm),
        grid=(nt,),
        in_specs=[
            pl.BlockSpec((2 * tm,), lambda i: (i,), memory_space=pltpu.SMEM),
            pl.BlockSpec((2 * tm,), lambda i: (jnp.minimum(i + 1, nt - 1),), memory_space=pltpu.SMEM),
            pl.BlockSpec((tm, D_MODEL), row),
            pl.BlockSpec((8, tm), col),
            pl.BlockSpec(memory_space=pl.ANY),
        ],
        out_specs=pl.BlockSpec((tm, D_MODEL), row),
        out_shape=jax.ShapeDtypeStruct((t, D_MODEL), F32),
        scratch_shapes=[
            pltpu.VMEM((2, tm // SUBLANES, SUBLANES, D_MODEL), F32),
            pltpu.VMEM((2, tm // SUBLANES, SUBLANES, D_MODEL), F32),
            pltpu.SemaphoreType.DMA((2,)),
        ],
        compiler_params=_cparams(("arbitrary",)),
        name="combine",
    )(slots, slots, x1, rg, ys)


def _prep_weights(g_attn, w_in, g_cq, w_uq, g_ckv, w_ukv, g_q, g_k, b_igate, b_fgate, g_mh,
                  w_out, g_ffn, w_group, b_group, w_erouter, b_erouter, w1, w3, w2):
    def cols(a, b):
        return w_in[:, a:b]

    o_cq, o_ckv, o_kr = 0, 256, 384
    o_mq, o_mk, o_mv, o_mi, o_mf, o_mo = 416, 672, 928, 1440, 1444, 1448
    hr = MLA_ROPE // 2
    misc = jnp.concatenate([
        cols(o_kr, o_kr + MLA_ROPE), -cols(o_kr + hr, o_kr + MLA_ROPE), cols(o_kr, o_kr + hr),
        cols(o_mi, o_mi + M_HEADS), cols(o_mf, o_mf + M_HEADS),
        jnp.zeros((D_MODEL, LANES - 2 * MLA_ROPE - 2 * M_HEADS), F32)], axis=1)
    w_p = jnp.concatenate([
        cols(o_cq, o_cq + 256), cols(o_ckv, o_ckv + 128), cols(o_mq, o_mq + 256), cols(o_mk, o_mk + 256),
        cols(o_mv, o_mv + 512), cols(o_mo, o_mo + 512), misc], axis=1).astype(BF16)
    nope, r1, r2 = w_uq[..., :MLA_NOPE], w_uq[..., MLA_NOPE:MLA_NOPE + hr], w_uq[..., MLA_NOPE + hr:]
    zq = lambda w: jnp.zeros((MLA_Q_LORA, MLA_HEADS, w), F32)
    w_qa = jnp.concatenate([nope, r1, r2, zq(HEAD_PAD - MLA_QK)], axis=-1).reshape(MLA_Q_LORA, MLA_HEADS * HEAD_PAD)
    w_qb = jnp.concatenate([zq(MLA_NOPE), -r2, r1, zq(HEAD_PAD - MLA_QK)], axis=-1).reshape(MLA_Q_LORA,
                                                                                           MLA_HEADS * HEAD_PAD)
    w_q = jnp.concatenate([w_qa, w_qb], axis=1).astype(BF16)
    pad_g = jnp.zeros((HEAD_PAD - MLA_QK,), F32)
    w_k = jnp.concatenate([w_ukv[..., :MLA_NOPE], jnp.zeros((MLA_KV_LORA, MLA_HEADS, HEAD_PAD - MLA_NOPE), F32)],
                          axis=-1).reshape(MLA_KV_LORA, MLA_HEADS * HEAD_PAD).T.astype(BF16)
    w_v = jnp.concatenate([w_ukv[..., MLA_NOPE:], jnp.zeros((MLA_KV_LORA, MLA_HEADS, LANES - MLA_V), F32)],
                          axis=-1).reshape(MLA_KV_LORA, MLA_HEADS * LANES).astype(BF16)
    gate_bias = jnp.concatenate([jnp.zeros((_L_IG,), F32), b_igate, b_fgate,
                                 jnp.zeros((LANES - _L_FG - M_HEADS,), F32)])[None]
    w_r = jnp.concatenate([w_erouter.T, w_group.T, jnp.zeros((R_ROWS - N_EXPERTS - N_GROUPS, D_MODEL), F32)],
                          axis=0).astype(BF16)
    b_r = jnp.concatenate([b_erouter, b_group, jnp.zeros((R_ROWS - N_EXPERTS - N_GROUPS,), F32)])
    return {
        "w_in": w_p, "g_attn": g_attn[None], "g_cq": g_cq[None], "w_q": w_q, "g_ckv": g_ckv[None],
        "g_q": jnp.concatenate([g_q, pad_g])[None], "g_k": jnp.broadcast_to(jnp.concatenate([g_k, pad_g])[:, None], (HEAD_PAD, LANES)),
        "gate_bias": gate_bias, "w_k": w_k, "w_v": w_v, "g_mh": g_mh,
        "w_out": w_out.astype(BF16), "g_ffn": g_ffn[None], "w_r": w_r,
        "b_r": jnp.broadcast_to(b_r[:, None], (R_ROWS, LANES)),
        "w1": w1, "w3": w3, "w2": w2,
        "score_bound": (MLA_QK * MLA_SCALE * LOG2E * 1.01 * jnp.max(jnp.abs(g_q)) * jnp.max(jnp.abs(g_k))).reshape(1),
    }


def _rope_table(pos):
    half = MLA_ROPE // 2
    inv = ROPE_BASE ** (-np.arange(half, dtype=np.float64) / half)
    ang = np.asarray(pos, np.float64)[:, None] * inv[None, :]
    cos = np.cos(ang)
    sin = np.sin(ang)
    c2 = np.concatenate([cos, cos], axis=1)
    s2 = np.concatenate([sin, sin], axis=1)
    return jnp.asarray(np.concatenate([c2, s2, c2, s2], axis=1), F32)


def _pick(n, pref):
    return pref if n % pref == 0 else n


def _slots_kernel(ri_ref, ps_ref, d_ref):
    tm = ri_ref.shape[1]
    xi = lax.broadcasted_iota(I32, (N_EXPERTS, tm), 0)
    ps = ps_ref[:, 0:1]
    rows = []
    for k in range(2):
        start = jnp.sum(jnp.where(xi == ri_ref[k:k + 1, :], ps, 0), axis=0, keepdims=True)
        rows.append(start + ri_ref[2 + k:3 + k, :])
    d_ref[...] = jnp.concatenate(rows + [jnp.zeros((6, tm), I32)], axis=0)


def _slots(ri, pstart_b, tm):
    t = ri.shape[1]
    return pl.pallas_call(
        _slots_kernel,
        grid=(t // tm,),
        in_specs=[pl.BlockSpec((8, tm), lambda i: (0, i)), pl.BlockSpec((N_EXPERTS, LANES), lambda i: (0, 0))],
        out_specs=pl.BlockSpec((8, tm), lambda i: (0, i)),
        out_shape=jax.ShapeDtypeStruct((8, t), I32),
        compiler_params=_cparams(("arbitrary",)),
        name="slots",
    )(ri, pstart_b)


def _moe_layer(x2d, att, hm, pw, *, tm_route, bm, tm_disp, tm_comb):
    t = x2d.shape[0]
    x1, hp, ri, rg, cnt = _route(x2d, att, hm, pw, tm_route)
    counts = cnt[:, 0]
    padded = (counts + bm - 1) // bm * bm
    pend = jnp.cumsum(padded)
    pstart = pend - padded
    dest = _slots(ri, jnp.broadcast_to(pstart[:, None], (N_EXPERTS, LANES)), _pick(t, 2048))
    n_slots = (2 * t // bm + N_EXPERTS) * bm
    nblk = n_slots // bm
    n_used = (pend[-1] // bm).astype(I32)
    experts = jnp.arange(N_EXPERTS, dtype=I32)
    blk_first = jnp.arange(nblk, dtype=I32) * bm
    blk_e = jnp.sum((pend[None, :] <= blk_first[:, None]).astype(I32), axis=1)
    blk_e = jnp.minimum(blk_e, jnp.max(jnp.where(counts > 0, experts, 0)))
    zero_blocks = jnp.concatenate([jnp.where(counts > 0, pend // bm - 1, -1),
                                   jnp.where(n_used + experts < nblk, n_used + experts, -1)]).astype(I32)
    xs = _dispatch(zero_blocks, dest, hp, n_slots, tm_disp, bm)
    ys = _experts(blk_e, n_used[None], xs, pw, bm)
    return _combine(dest, x1, rg, ys, tm_comb)


def _state_pack(c, n):
    ct = jnp.swapaxes(c, -1, -2)
    s = jnp.concatenate([ct, jnp.broadcast_to(n[..., None], ct.shape[:-1] + (S_W - M_DV,))], axis=-1)
    return s.reshape((s.shape[0],) + S_SHAPE)


def _state_unpack(s, m):
    s = s.reshape(s.shape[0], M_HEADS, M_DK, S_W)
    return jnp.swapaxes(s[..., :M_DV], -1, -2), s[..., M_DV], m[:, :M_HEADS, 0]


def kernel(x_prompt, x_sample, cache_ckv, cache_krope, state_mlstm_c, state_mlstm_n, state_mlstm_m,
           meta_tokens, g_attn, w_in, g_cq, w_uq, g_ckv, w_ukv, g_q, g_k, b_igate, b_fgate, g_mh,
           w_out, g_ffn, w_group, b_group, w_erouter, b_erouter, w1, w3, w2):
    bp, seq = x_prompt.shape[:2]
    bs, dec = x_sample.shape[:2]
    past = cache_ckv.shape[2]
    layer = 0
    pw = _prep_weights(g_attn[layer], w_in[layer], g_cq[layer], w_uq[layer], g_ckv[layer], w_ukv[layer],
                       g_q[layer], g_k[layer], b_igate[layer], b_fgate[layer], g_mh[layer], w_out[layer],
                       g_ffn[layer], w_group[layer], b_group[layer], w_erouter[layer], b_erouter[layer],
                       w1[layer], w3[layer], w2[layer])

    xm = jnp.concatenate([meta_tokens, jnp.zeros((META_PAD - N_META, D_MODEL), F32)], axis=0)
    tab_m = _rope_table(np.arange(META_PAD) - N_META)
    _, ckv_m, misc_m, mq_m, mk_m, mv_m, og_m = _project(xm, tab_m, pw, META_PAD)
    zero_s = jnp.zeros((1,) + S_SHAPE, F32)
    zero_m = jnp.zeros((1, 8, LANES), F32)
    _, s_meta, m_meta = _mlstm(mq_m, mk_m, mv_m, og_m, misc_m, zero_s, zero_m, pw["g_mh"],
                               nb=1, l=META_PAD, n_valid=N_META)

    tp = bp * seq
    xp2 = x_prompt.reshape(tp, D_MODEL)
    tm_p = _pick(seq, 1024)
    q_p, ckv_p, misc_p, mq_p, mk_p, mv_p, og_p = _project(xp2, _rope_table(np.arange(seq)), pw, tm_p)
    tq = _pick(seq, 1024)
    att_p = _attention(q_p, ckv_p, misc_p, ckv_m[None], misc_m[None], pw, nb=bp, lq=seq, lf=seq, tq=tq, tk=256,
                       causal=True, n_tail=N_META)
    hm_p, s_p, m_p = _mlstm(mq_p, mk_p, mv_p, og_p, misc_p, s_meta, m_meta, pw["g_mh"],
                            nb=bp, l=seq)
    y_p = _moe_layer(xp2, att_p, hm_p, pw, tm_route=tm_p, bm=1024, tm_disp=_pick(seq, 1024),
                     tm_comb=_pick(seq, 512))

    ts = bs * dec
    xs2 = x_sample.reshape(ts, D_MODEL)
    q_s, ckv_s, misc_s, mq_s, mk_s, mv_s, og_s = _project(xs2, _rope_table(past + np.arange(dec)), pw, dec)
    kr_cache = jnp.concatenate([cache_krope[layer], jnp.zeros((bs, past, LANES - MLA_ROPE), F32)], axis=-1)
    n_tail = dec + N_META
    assert n_tail <= META_PAD

    def tail_rows(own, meta):
        w = own.shape[-1]
        return jnp.concatenate([own.reshape(bs, dec, w), jnp.broadcast_to(meta[None, :N_META], (bs, N_META, w)),
                                jnp.zeros((bs, META_PAD - n_tail, w), F32)], axis=1)

    att_s = _attention(q_s, cache_ckv[layer].reshape(bs * past, MLA_KV_LORA), kr_cache.reshape(bs * past, LANES),
                       tail_rows(ckv_s, ckv_m), tail_rows(misc_s, misc_m), pw, nb=bs, lq=dec, lf=past, tq=dec,
                       tk=_pick(past, 256), causal=False, n_tail=n_tail)
    s0 = _state_pack(state_mlstm_c[layer], state_mlstm_n[layer])
    m0 = jnp.concatenate([jnp.broadcast_to(state_mlstm_m[layer][:, :, None], (bs, M_HEADS, LANES)),
                          jnp.zeros((bs, 8 - M_HEADS, LANES), F32)], axis=1)
    hm_s, s_s, m_s = _mlstm(mq_s, mk_s, mv_s, og_s, misc_s, s0, m0, pw["g_mh"], nb=bs, l=dec)
    y_s = _moe_layer(xs2, att_s, hm_s, pw, tm_route=_pick(ts, 512), bm=128, tm_disp=_pick(ts, 512),
                     tm_comb=_pick(ts, 512))

    m_ckv = ckv_m[:N_META]
    m_kr = misc_m[:N_META, :MLA_ROPE]
    new_ckv_p = jnp.concatenate([jnp.broadcast_to(m_ckv[None], (bp, N_META, MLA_KV_LORA)),
                                 ckv_p.reshape(bp, seq, MLA_KV_LORA)], axis=1)[None]
    new_kr_p = jnp.concatenate([jnp.broadcast_to(m_kr[None], (bp, N_META, MLA_ROPE)),
                                misc_p[:, :MLA_ROPE].reshape(bp, seq, MLA_ROPE)], axis=1)[None]
    c_p, n_p, mm_p = _state_unpack(s_p, m_p)
    c_s, n_s, mm_s = _state_unpack(s_s, m_s)
    return (y_p.reshape(bp, seq, D_MODEL), y_s.reshape(bs, dec, D_MODEL),
            new_ckv_p, new_kr_p, c_p[None], n_p[None], mm_p[None],
            ckv_s.reshape(bs, dec, MLA_KV_LORA)[None], misc_s[:, :MLA_ROPE].reshape(bs, dec, MLA_ROPE)[None],
            c_s[None], n_s[None], mm_s[None])
```
